```python
import math
import jax
import jax.numpy as jnp
from jax import lax
import numpy as np

D_MODEL = 1024
BATCH = 4
SEQ = 4096
DEPTH = 4

HY_WIDTH = D_MODEL // 4
HY_GROUPS = 4
ATT_WIDTH = D_MODEL // 2
ATT_HEADS = 4
ATT_HEAD_DIM = ATT_WIDTH // (2 * ATT_HEADS)
POOL_WINDOWS = (2, 4, 8, 16)
POOL_WIDTH = D_MODEL - HY_WIDTH - ATT_WIDTH
POOL_GROUP = POOL_WIDTH // len(POOL_WINDOWS)
IN_WIDTH = 3 * HY_WIDTH + 3 * ATT_WIDTH + POOL_WIDTH
ROPE_THETA = 500000.0
ROPE_DIM = ATT_HEAD_DIM // 4
Q_BLOCK = 128
HY_EMB = 33
HY_BANDS = (HY_EMB - 1) // 2
HY_FILTER_HIDDEN = 64
HY_DECAY_TARGET = 1e-2
HY_FAST_DECAY = 0.3
HY_SLOW_DECAY = 1.5
HY_SHIFT = 0.0
N_GROUPS = 4
EXPERTS_PER_GROUP = 4
N_EXPERTS = N_GROUPS * EXPERTS_PER_GROUP
TOP_K_IN_GROUP = 2
D_EXPERT = 256
PLE_DIM = 256
LN_EPS = 1e-5
RMS_EPS = 1e-5
DN_ALPHA = (2 * DEPTH) ** 0.25
DN_BETA = (8 * DEPTH) ** -0.25

kernel_name = 'hybrid_hyena_diffattn_pool_hmoe_encoder'


def layer_norm(x, g, b):
    xf = x.astype(jnp.float32)
    mu = jnp.mean(xf, axis=-1, keepdims=True)
    var = jnp.mean(jnp.square(xf - mu), axis=-1, keepdims=True)
    y = (xf - mu) * lax.rsqrt(var + LN_EPS) * g.astype(jnp.float32) + b.astype(jnp.float32)
    return y.astype(x.dtype)


def rms_norm(x, g):
    xf = x.astype(jnp.float32)
    y = xf * lax.rsqrt(jnp.mean(jnp.square(xf), axis=-1, keepdims=True) + RMS_EPS) * g.astype(jnp.float32)
    return y.astype(x.dtype)


def rope_tables(seq_len):
    pos = jnp.arange(seq_len, dtype=jnp.float32)
    inv_freq = jnp.power(ROPE_THETA, -jnp.arange(0, ROPE_DIM, 2, dtype=jnp.float32) / ROPE_DIM)
    ang = pos[:, None] * inv_freq[None, :]
    ang = jnp.concatenate([ang, ang], axis=-1)
    return jnp.cos(ang), jnp.sin(ang)


def apply_partial_rope(x, cos, sin):
    xf = x.astype(jnp.float32)
    xr, xp = xf[..., :ROPE_DIM], xf[..., ROPE_DIM:]
    half = ROPE_DIM // 2
    rot = jnp.concatenate([-xr[..., half:], xr[..., :half]], axis=-1)
    c = cos[None, :, None, None, :]
    s = sin[None, :, None, None, :]
    return jnp.concatenate([xr * c + rot * s, xp], axis=-1).astype(x.dtype)


def short_conv_centred(u, w, b):
    up = jnp.pad(u, ((0, 0), (1, 1), (0, 0)))
    return up[:, :-2] * w[0] + up[:, 1:-1] * w[1] + up[:, 2:] * w[2] + b


def hyena_filters(seq_len, fw1, fb1, freq1, fw2, fb2, freq2, fw3):
    f32 = jnp.float32
    t = jnp.linspace(0.0, 1.0, seq_len, dtype=f32)[:, None]
    wpos = 2.0 * math.pi * jnp.arange(seq_len, dtype=f32)[:, None] / seq_len
    bands = jnp.linspace(1e-4, HY_BANDS - 1, HY_BANDS, dtype=f32)[None, :]
    z = jnp.concatenate([t, jnp.cos(wpos * bands), -jnp.sin(wpos * bands)], axis=-1)
    hdn = jnp.sin(freq1.astype(f32) * (z @ fw1.astype(f32) + fb1.astype(f32)))
    hdn = jnp.sin(freq2.astype(f32) * (hdn @ fw2.astype(f32) + fb2.astype(f32)))
    filt = hdn @ fw3.astype(f32)
    max_decay = math.log(HY_DECAY_TARGET) / HY_FAST_DECAY
    min_decay = math.log(HY_DECAY_TARGET) / HY_SLOW_DECAY
    deltas = jnp.linspace(min_decay, max_decay, HY_WIDTH, dtype=f32)[None, :]
    window = jnp.exp(-t * jnp.abs(deltas)) + HY_SHIFT
    h_fwd = filt[:, :HY_WIDTH] * window
    h_bwd = filt[:, HY_WIDTH:] * window
    k = jnp.concatenate([h_fwd, jnp.zeros((1, HY_WIDTH), f32), h_bwd[:0:-1]], axis=0)
    return k * lax.rsqrt(jnp.sum(jnp.square(k), axis=0, keepdims=True) + 1e-6)


def hyena_mixer(u, conv_w, conv_b, fw1, fb1, freq1, fw2, fb2, freq2, fw3, bias_d):
    B, L, _ = u.shape
    uc = short_conv_centred(u, conv_w, conv_b)
    x0, x1, v = jnp.split(uc, 3, axis=-1)
    v = (v * x1).astype(jnp.float32)
    k = hyena_filters(L, fw1, fb1, freq1, fw2, fb2, freq2, fw3)
    vf = jnp.fft.rfft(v, n=2 * L, axis=1)
    kf = jnp.fft.rfft(k, axis=0)
    y = jnp.fft.irfft(vf * kf[None], n=2 * L, axis=1)[:, :L]
    y = y + v * bias_d.astype(jnp.float32)
    return (y * x0.astype(jnp.float32)).astype(u.dtype)


def diff_attention(q, k, v, lq1, lk1, lq2, lk2, subln_g, lam_init, cos, sin):
    B, S, _ = q.shape
    q = q.reshape(B, S, ATT_HEADS, 2, ATT_HEAD_DIM)
    k = k.reshape(B, S, ATT_HEADS, 2, ATT_HEAD_DIM)
    v = v.reshape(B, S, ATT_HEADS, 2 * ATT_HEAD_DIM)
    q = apply_partial_rope(q, cos, sin) * (ATT_HEAD_DIM ** -0.5)
    k = apply_partial_rope(k, cos, sin)
    f32 = jnp.float32
    lam = (jnp.exp(jnp.sum(lq1.astype(f32) * lk1.astype(f32)))
           - jnp.exp(jnp.sum(lq2.astype(f32) * lk2.astype(f32))) + lam_init)
    nb = S // Q_BLOCK
    qb = q.reshape(B, nb, Q_BLOCK, ATT_HEADS, 2, ATT_HEAD_DIM).transpose(1, 0, 2, 3, 4, 5)

    def one_block(qblk):
        s = jnp.einsum('bqhcd,bkhcd->bhcqk', qblk, k, preferred_element_type=f32)
        pr = jax.nn.softmax(s, axis=-1)
        a = pr[:, :, 0] - lam * pr[:, :, 1]
        return jnp.einsum('bhqk,bkhe->bqhe', a.astype(v.dtype), v)

    o = lax.map(one_block, qb)
    o = o.transpose(1, 0, 2, 3, 4).reshape(B, S, ATT_HEADS, 2 * ATT_HEAD_DIM)
    o = rms_norm(o, subln_g) * (1.0 - lam_init)
    return o.reshape(B, S, ATT_HEADS * 2 * ATT_HEAD_DIM)


def pool_mixer(u, w, b, scale):
    B, S, _ = u.shape
    G = len(POOL_WINDOWS)
    f32 = jnp.float32
    uf = u.astype(f32).reshape(B, S, G, POOL_GROUP)
    csum = jnp.concatenate([jnp.zeros((B, 1, G, POOL_GROUP), f32), jnp.cumsum(uf, axis=1)], axis=1)
    pos = jnp.arange(S)
    pooled = []
    for g, win in enumerate(POOL_WINDOWS):
        lo = jnp.clip(pos - win // 2, 0, S - 1)
        hi = jnp.clip(pos + win // 2 - 1, 0, S - 1)
        cg = csum[:, :, g]
        cnt = (hi - lo + 1).astype(f32)[None, :, None]
        pooled.append((cg[:, hi + 1] - cg[:, lo]) / cnt)
    y = jnp.stack(pooled, axis=2) - uf
    y = jnp.einsum('bsgc,gcd->bsgd', y, w.astype(f32)) + b.astype(f32)
    return (y.reshape(B, S, POOL_WIDTH) * scale.astype(f32)).astype(u.dtype)


def hier_moe(h, wgc, bgc, wgf, bgf, w1, w3, w2):
    B, S, D = h.shape
    f32 = jnp.float32
    t = h.reshape(B * S, D)
    T = t.shape[0]
    pg = jax.nn.softmax(jnp.dot(t, wgc, preferred_element_type=f32) + bgc.astype(f32), axis=-1)
    gw, gi = lax.top_k(pg, 1)
    fl = (jnp.dot(t, wgf, preferred_element_type=f32) + bgf.astype(f32)).reshape(T, N_GROUPS, EXPERTS_PER_GROUP)
    idx = jnp.broadcast_to(gi[:, :, None], (T, 1, EXPERTS_PER_GROUP))
    fs = jnp.take_along_axis(fl, idx, axis=1)[:, 0]
    tv, ti = lax.top_k(jax.nn.softmax(fs, axis=-1), TOP_K_IN_GROUP)
    tv = tv / jnp.sum(tv, axis=-1, keepdims=True)
    eid = gi * EXPERTS_PER_GROUP + ti
    wts = gw * tv
    gates = jnp.sum(jax.nn.one_hot(eid, N_EXPERTS, dtype=f32) * wts[..., None], axis=1)
    a = jnp.einsum('td,edf->tef', t, w1)
    c = jnp.einsum('td,edf->tef', t, w3)
    act = jax.nn.silu(a) * c * gates[:, :, None].astype(t.dtype)
    y = jnp.einsum('tef,efd->td', act, w2)
    return y.reshape(B, S, D)


def setup_inputs(seed: int = 0) -> dict:
    key = jax.random.key(seed)
    ks = iter(jax.random.split(key, 48))
    f32 = jnp.float32

    def nrm(shape, scale):
        return scale * jax.random.normal(next(ks), shape, f32)

    def gain(shape):
        return 1.0 + 0.05 * jax.random.normal(next(ks), shape, f32)

    L = DEPTH
    return {
        'x': nrm((BATCH, SEQ, D_MODEL), 1.0),
        'p': nrm((DEPTH, BATCH, SEQ, PLE_DIM), 1.0),
        'ln0_g': gain((D_MODEL,)),
        'ln0_b': nrm((D_MODEL,), 0.02),
        'w_in': nrm((L, D_MODEL, IN_WIDTH), D_MODEL ** -0.5),
        'hy_conv_w': nrm((L, 3, 3 * HY_WIDTH), 3 ** -0.5),
        'hy_conv_b': nrm((L, 3 * HY_WIDTH), 0.02),
        'hy_fw1': nrm((L, HY_EMB, HY_FILTER_HIDDEN), HY_EMB ** -0.5),
        'hy_fb1': nrm((L, HY_FILTER_HIDDEN), 0.02),
        'hy_freq1': gain((L, HY_FILTER_HIDDEN)),
        'hy_fw2': nrm((L, HY_FILTER_HIDDEN, HY_FILTER_HIDDEN), HY_FILTER_HIDDEN ** -0.5),
        'hy_fb2': nrm((L, HY_FILTER_HIDDEN), 0.02),
        'hy_freq2': gain((L, HY_FILTER_HIDDEN)),
        'hy_fw3': nrm((L, HY_FILTER_HIDDEN, 2 * HY_WIDTH), HY_FILTER_HIDDEN ** -0.5),
        'hy_bias': nrm((L, HY_WIDTH), 1.0),
        'att_lq1': nrm((L, ATT_HEAD_DIM), 0.1),
        'att_lk1': nrm((L, ATT_HEAD_DIM), 0.1),
        'att_lq2': nrm((L, ATT_HEAD_DIM), 0.1),
        'att_lk2': nrm((L, ATT_HEAD_DIM), 0.1),
        'att_subln_g': gain((L, 2 * ATT_HEAD_DIM)),
        'pool_w': nrm((L, len(POOL_WINDOWS), POOL_GROUP, POOL_GROUP), POOL_GROUP ** -0.5),
        'pool_b': nrm((L, len(POOL_WINDOWS), POOL_GROUP), 0.02),
        'pool_scale': gain((L, POOL_WIDTH)),
        'w_out': nrm((L, D_MODEL, D_MODEL), DN_BETA * D_MODEL ** -0.5),
        'ln1_g': gain((L, D_MODEL)),
        'ln1_b': nrm((L, D_MODEL), 0.02),
        'moe_wgc': nrm((L, D_MODEL, N_GROUPS), D_MODEL ** -0.5),
        'moe_bgc': nrm((L, N_GROUPS), 0.01),
        'moe_wgf': nrm((L, D_MODEL, N_EXPERTS), D_MODEL ** -0.5),
        'moe_bgf': nrm((L, N_EXPERTS), 0.01),
        'moe_w1': nrm((L, N_EXPERTS, D_MODEL, D_EXPERT), D_MODEL ** -0.5),
        'moe_w3': nrm((L, N_EXPERTS, D_MODEL, D_EXPERT), D_MODEL ** -0.5),
        'moe_w2': nrm((L, N_EXPERTS, D_EXPERT, D_MODEL), DN_BETA * D_EXPERT ** -0.5),
        'ple_wg': nrm((L, D_MODEL, D_MODEL), D_MODEL ** -0.5),
        'ple_bg': nrm((L, D_MODEL), 0.02),
        'ple_wp': nrm((L, PLE_DIM, D_MODEL), DN_BETA * PLE_DIM ** -0.5),
        'ln2_g': gain((L, D_MODEL)),
        'ln2_b': nrm((L, D_MODEL), 0.02),
    }


def reference(x, p, ln0_g, ln0_b, w_in, hy_conv_w, hy_conv_b, hy_fw1, hy_fb1, hy_freq1,
              hy_fw2, hy_fb2, hy_freq2, hy_fw3, hy_bias, att_lq1, att_lk1, att_lq2, att_lk2,
              att_subln_g, pool_w, pool_b, pool_scale, w_out, ln1_g, ln1_b, moe_wgc, moe_bgc,
              moe_wgf, moe_bgf, moe_w1, moe_w3, moe_w2, ple_wg, ple_bg, ple_wp, ln2_g, ln2_b):
    S = x.shape[1]
    h = layer_norm(x, ln0_g, ln0_b)
    cos, sin = rope_tables(S)
    o_q = 3 * HY_WIDTH
    o_k = o_q + ATT_WIDTH
    o_v = o_k + ATT_WIDTH
    o_p = o_v + ATT_WIDTH
    for i in range(DEPTH):
        lam_init = 0.8 - 0.6 * math.exp(-0.3 * i)
        u = h @ w_in[i]
        y_hy = hyena_mixer(u[..., :o_q], hy_conv_w[i], hy_conv_b[i], hy_fw1[i], hy_fb1[i],
                           hy_freq1[i], hy_fw2[i], hy_fb2[i], hy_freq2[i], hy_fw3[i], hy_bias[i])
        y_att = diff_attention(u[..., o_q:o_k], u[..., o_k:o_v], u[..., o_v:o_p],
                               att_lq1[i], att_lk1[i], att_lq2[i], att_lk2[i], att_subln_g[i],
                               lam_init, cos, sin)
        y_pool = pool_mixer(u[..., o_p:], pool_w[i], pool_b[i], pool_scale[i])
        mix = jnp.concatenate([y_hy, y_att, y_pool], axis=-1) @ w_out[i]
        h = layer_norm(DN_ALPHA * h + mix, ln1_g[i], ln1_b[i])
        y_moe = hier_moe(h, moe_wgc[i], moe_bgc[i], moe_wgf[i], moe_bgf[i],
                         moe_w1[i], moe_w3[i], moe_w2[i])
        y_ple = jax.nn.sigmoid(h @ ple_wg[i] + ple_bg[i]) * (p[i] @ ple_wp[i])
        h = layer_norm(DN_ALPHA * h + y_moe + y_ple, ln2_g[i], ln2_b[i])
    return h
```

```python
import functools
import math

import numpy as np
import jax
import jax.numpy as jnp
from jax import lax
from jax.experimental import pallas as pl
from jax.experimental.pallas import tpu as pltpu

F32 = jnp.float32
BF16 = jnp.bfloat16

D_MODEL = 1024
DEPTH = 4
HY_WIDTH = 256
ATT_WIDTH = 512
ATT_HEADS = 4
ATT_HEAD_DIM = 64
POOL_WINDOWS = (2, 4, 8, 16)
POOL_WIDTH = 256
POOL_GROUP = 64
IN_WIDTH = 3 * HY_WIDTH + 3 * ATT_WIDTH + POOL_WIDTH
ROPE_THETA = 500000.0
ROPE_DIM = ATT_HEAD_DIM // 4
HY_EMB = 33
HY_BANDS = (HY_EMB - 1) // 2
HY_FILTER_HIDDEN = 64
HY_DECAY_TARGET = 1e-2
HY_FAST_DECAY = 0.3
HY_SLOW_DECAY = 1.5
N_GROUPS = 4
EXPERTS_PER_GROUP = 4
N_EXPERTS = 16
D_EXPERT = 256
PLE_DIM = 256
LN_EPS = 1e-5
RMS_EPS = 1e-5
DN_ALPHA = (2 * DEPTH) ** 0.25

LANES = 128
FFT_N2 = 128
GATE_COARSE_LANE = 0
GATE_FINE_LANE = N_GROUPS
NEG_BIG = -1e30
LOG2E = 1.4426950408889634
VMEM_LIMIT = 56 * 1024 * 1024


def _cparams(sem):
    return pltpu.CompilerParams(dimension_semantics=sem, vmem_limit_bytes=VMEM_LIMIT)


def _split(x):
    hi = x.astype(BF16)
    lo = (x - hi.astype(F32)).astype(BF16)
    return hi, lo


def _dot3(ah, al, bh, bl):
    d = functools.partial(jnp.dot, preferred_element_type=F32)
    return d(ah, bh) + (d(ah, bl) + d(al, bh))


def _ln_rows(x, g, b):
    mu = jnp.mean(x, axis=-1, keepdims=True)
    xc = x - mu
    var = jnp.mean(xc * xc, axis=-1, keepdims=True)
    return xc * lax.rsqrt(var + LN_EPS) * g + b


def _ln_kernel(x_ref, g_ref, b_ref, o_ref):
    o_ref[...] = _ln_rows(x_ref[...], g_ref[...], b_ref[...])


def _layer_norm(x2, g, b, tm=512):
    T, D = x2.shape
    return pl.pallas_call(
        _ln_kernel,
        grid=(T // tm,),
        in_specs=[pl.BlockSpec((tm, D), lambda i: (i, 0)),
                  pl.BlockSpec((1, D), lambda i: (0, 0)),
                  pl.BlockSpec((1, D), lambda i: (0, 0))],
        out_specs=pl.BlockSpec((tm, D), lambda i: (i, 0)),
        out_shape=jax.ShapeDtypeStruct((T, D), F32),
        compiler_params=_cparams(("parallel",)),
        name="ln0",
    )(x2, g.reshape(1, D), b.reshape(1, D))


def _rope(x, cos_f, sin_a, sin_b):
    half = ROPE_DIM // 2
    return x * cos_f + pltpu.roll(x, LANES - half, 1) * sin_a + pltpu.roll(x, half, 1) * sin_b


def _inproj_kernel(h_ref, w_ref, cos_ref, sa_ref, sb_ref, uh_ref, qkv_ref, up_ref):
    hb = h_ref[...].astype(BF16)
    mm = lambda c0, c1: jnp.dot(hb, w_ref[:, c0:c1], preferred_element_type=F32)
    o_q = 3 * HY_WIDTH
    o_k = o_q + ATT_WIDTH
    o_v = o_k + ATT_WIDTH
    o_p = o_v + ATT_WIDTH
    uh_ref[...] = mm(0, o_q)
    up_ref[...] = mm(o_p, IN_WIDTH)
    cos_f, sin_a, sin_b = cos_ref[...], sa_ref[...], sb_ref[...]
    q = mm(o_q, o_k)
    k = mm(o_k, o_v)
    qscale = ATT_HEAD_DIM ** -0.5 * LOG2E
    for hd in range(ATT_HEADS):
        sl = slice(hd * LANES, (hd + 1) * LANES)
        qkv_ref[:, sl] = (_rope(q[:, sl], cos_f, sin_a, sin_b) * qscale).astype(BF16)
        qkv_ref[:, ATT_WIDTH + hd * LANES:ATT_WIDTH + (hd + 1) * LANES] = \
            _rope(k[:, sl], cos_f, sin_a, sin_b).astype(BF16)
    qkv_ref[:, 2 * ATT_WIDTH:] = mm(o_v, o_p).astype(BF16)


def _in_proj(h2, w_bf, tables, seq, tm=512):
    T, D = h2.shape
    N = w_bf.shape[1]
    spt = seq // tm
    tspec = pl.BlockSpec((tm, LANES), lambda i: (i % spt, 0))
    row = lambda c: pl.BlockSpec((tm, c), lambda i: (i, 0))
    return pl.pallas_call(
        _inproj_kernel,
        grid=(T // tm,),
        in_specs=[row(D), pl.BlockSpec((D, N), lambda i: (0, 0)), tspec, tspec, tspec],
        out_specs=[row(3 * HY_WIDTH), row(3 * ATT_WIDTH), row(POOL_WIDTH)],
        out_shape=[jax.ShapeDtypeStruct((T, 3 * HY_WIDTH), F32),
                   jax.ShapeDtypeStruct((T, 3 * ATT_WIDTH), BF16),
                   jax.ShapeDtypeStruct((T, POOL_WIDTH), F32)],
        compiler_params=_cparams(("parallel",)),
        name="in_proj",
    )(h2, w_bf, *tables)


def _hy_filter_kernel(band_ref, fw1_ref, fb1_ref, fr1_ref, fw2_ref, fb2_ref, fr2_ref, fw3_ref,
                      dl_ref, k_ref, ssq_ref, *, seq, tr):
    j = pl.program_id(1)
    n = j * tr + lax.broadcasted_iota(jnp.int32, (tr, 1), 0)
    pos = jnp.where(n < seq, n, 2 * seq - n).astype(F32)
    t = pos * (1.0 / (seq - 1))
    wpos = (2.0 * math.pi / seq) * pos
    lane = lax.broadcasted_iota(jnp.int32, (tr, LANES), 1)
    arg = wpos * band_ref[...]
    z = jnp.where(lane == 0, t,
                  jnp.where(lane <= HY_BANDS, jnp.cos(arg),
                            jnp.where(lane < HY_EMB, -jnp.sin(arg), 0.0)))
    h1 = _dot3(*_split(z), *_split(fw1_ref[...])) + fb1_ref[...]
    h1 = jnp.sin(fr1_ref[...] * h1)
    h2 = _dot3(*_split(h1), *_split(fw2_ref[...])) + fb2_ref[...]
    h2 = jnp.sin(fr2_ref[...] * h2)
    filt = _dot3(*_split(h2), *_split(fw3_ref[...]))
    window = jnp.exp(-t * jnp.abs(dl_ref[...]))
    kk = jnp.where(n < seq, filt[:, :HY_WIDTH], filt[:, HY_WIDTH:]) * window
    kk = jnp.where(n == seq, 0.0, kk)
    k_ref[...] = kk

    @pl.when(j == 0)
    def _():
        ssq_ref[...] = jnp.zeros_like(ssq_ref)

    ssq_ref[...] += jnp.sum(kk * kk, axis=0, keepdims=True)


def _hy_filter(seq, fw1, fb1, freq1, fw2, fb2, freq2, fw3, tr=1024):
    L = fw1.shape[0]
    n = 2 * seq
    H = HY_FILTER_HIDDEN
    bands = np.linspace(1e-4, HY_BANDS - 1, HY_BANDS)
    bandv = np.zeros((1, LANES), np.float32)
    bandv[0, 1:1 + HY_BANDS] = bands
    bandv[0, 1 + HY_BANDS:HY_EMB] = bands
    fw1p = jnp.zeros((L, LANES, H), F32).at[:, :HY_EMB].set(fw1)
    max_decay = math.log(HY_DECAY_TARGET) / HY_FAST_DECAY
    min_decay = math.log(HY_DECAY_TARGET) / HY_SLOW_DECAY
    deltas = np.linspace(min_decay, max_decay, HY_WIDTH).astype(np.float32).reshape(1, HY_WIDTH)
    vec = lambda a: a.reshape(L, 1, a.shape[-1])
    lspec = lambda r, c: pl.BlockSpec((None, r, c), lambda l, j: (l, 0, 0))
    return pl.pallas_call(
        functools.partial(_hy_filter_kernel, seq=seq, tr=tr),
        grid=(L, n // tr),
        in_specs=[pl.BlockSpec((1, LANES), lambda l, j: (0, 0)),
                  lspec(LANES, H), lspec(1, H), lspec(1, H),
                  lspec(H, H), lspec(1, H), lspec(1, H),
                  lspec(H, 2 * HY_WIDTH),
                  pl.BlockSpec((1, HY_WIDTH), lambda l, j: (0, 0))],
        out_specs=[pl.BlockSpec((None, tr, HY_WIDTH), lambda l, j: (l, j, 0)),
                   pl.BlockSpec((None, 1, HY_WIDTH), lambda l, j: (l, 0, 0))],
        out_shape=[jax.ShapeDtypeStruct((L, n, HY_WIDTH), F32),
                   jax.ShapeDtypeStruct((L, 1, HY_WIDTH), F32)],
        compiler_params=_cparams(("parallel", "arbitrary")),
        name="hy_filter",
    )(jnp.asarray(bandv), fw1p, vec(fb1), vec(freq1), fw2, vec(fb2), vec(freq2), fw3,
      jnp.asarray(deltas))


def _k1_rows(n1):
    k1 = n1 // 2 + 1
    return k1, -(-k1 // 8) * 8


def _bf_pair(m):
    m32 = jnp.asarray(np.asarray(m, np.float32))
    return _split(m32)


def _stage_a_kernel(x_ref, frh_ref, frl_ref, fih_ref, fil_ref, ar_ref, ai_ref):
    xh, xl = _split(x_ref[...])
    ar_ref[...] = _dot3(frh_ref[...], frl_ref[...], xh, xl)
    ai_ref[...] = _dot3(fih_ref[...], fil_ref[...], xh, xl)


def _stage_a(x3, n1_total, tl=4096):
    Bt, n1u, LN = x3.shape
    k1n, k1p = _k1_rows(n1_total)
    ang = 2.0 * np.pi * np.outer(np.arange(k1p), np.arange(n1u)) / n1_total
    valid = (np.arange(k1p) < k1n)[:, None]
    frh, frl = _bf_pair(np.cos(ang) * valid)
    fih, fil = _bf_pair(-np.sin(ang) * valid)
    cspec = pl.BlockSpec((k1p, n1u), lambda b, j: (0, 0))
    ospec = pl.BlockSpec((None, k1p, tl), lambda b, j: (b, 0, j))
    return pl.pallas_call(
        _stage_a_kernel,
        grid=(Bt, LN // tl),
        in_specs=[pl.BlockSpec((None, n1u, tl), lambda b, j: (b, 0, j)), cspec, cspec, cspec, cspec],
        out_specs=[ospec, ospec],
        out_shape=[jax.ShapeDtypeStruct((Bt, k1p, LN), F32)] * 2,
        compiler_params=_cparams(("parallel", "parallel")),
        name="hy_stage_a",
    )(x3, frh, frl, fih, fil)


def _stage_b_consts(n1_total):
    n = n1_total * FFT_N2
    k1n, k1p = _k1_rows(n1_total)
    ang2 = 2.0 * np.pi * np.outer(np.arange(FFT_N2), np.arange(FFT_N2)) / FFT_N2
    c2, s2 = np.cos(ang2), np.sin(ang2)
    mf = np.block([[c2, s2], [-s2, c2]])
    mi = np.block([[c2, -s2], [s2, c2]])
    angt = 2.0 * np.pi * np.outer(np.arange(k1p), np.arange(FFT_N2)) / n
    twc = jnp.asarray(np.cos(angt).astype(np.float32)).reshape(k1p, FFT_N2, 1)
    tws = jnp.asarray(np.sin(angt).astype(np.float32)).reshape(k1p, FFT_N2, 1)
    return _bf_pair(mf), _bf_pair(mi), twc, tws


def _fwd_b(ar, ai, c, s, mfh, mfl):
    tr_ = ar * c + ai * s
    ti_ = ai * c - ar * s
    xh, xl = _split(jnp.concatenate([tr_, ti_], axis=0))
    z = _dot3(mfh, mfl, xh, xl)
    return z[:FFT_N2], z[FFT_N2:]


def _stage_bk_kernel(ar_ref, ai_ref, c_ref, s_ref, mfh_ref, mfl_ref, ssq_ref, kr_ref, ki_ref,
                     *, k1n, inv_n):
    k1 = pl.program_id(1)

    @pl.when(k1 < k1n)
    def _():
        zr, zi = _fwd_b(ar_ref[...], ai_ref[...], c_ref[...], s_ref[...], mfh_ref[...], mfl_ref[...])
        scale = lax.rsqrt(ssq_ref[...] + 1e-6) * inv_n
        kr_ref[...] = zr * scale
        ki_ref[...] = zi * scale

    @pl.when(k1 >= k1n)
    def _():
        kr_ref[...] = jnp.zeros_like(kr_ref)
        ki_ref[...] = jnp.zeros_like(ki_ref)


def _stage_b_filter(ar4, ai4, ssq, consts, n1_total):
    L, k1p, _, C = ar4.shape
    k1n, _ = _k1_rows(n1_total)
    (mfh, mfl), _, twc, tws = consts
    blk = pl.BlockSpec((None, None, FFT_N2, C), lambda l, k: (l, k, 0, 0))
    tw = pl.BlockSpec((None, FFT_N2, 1), lambda l, k: (k, 0, 0))
    mat = pl.BlockSpec((2 * FFT_N2, 2 * FFT_N2), lambda l, k: (0, 0))
    return pl.pallas_call(
        functools.partial(_stage_bk_kernel, k1n=k1n, inv_n=1.0 / (n1_total * FFT_N2)),
        grid=(L, k1p),
        in_specs=[blk, blk, tw, tw, mat, mat, pl.BlockSpec((None, 1, C), lambda l, k: (l, 0, 0))],
        out_specs=[blk, blk],
        out_shape=[jax.ShapeDtypeStruct(ar4.shape, F32)] * 2,
        compiler_params=_cparams(("parallel", "parallel")),
        name="hy_stage_b_filter",
    )(ar4, ai4, twc, tws, mfh, mfl, ssq)


def _stage_b_kernel(ar_ref, ai_ref, kr_ref, ki_ref, c_ref, s_ref, mfh_ref, mfl_ref, mih_ref, mil_ref,
                    br_ref, bi_ref, *, k1n):
    k1 = pl.program_id(1)

    @pl.when(k1 < k1n)
    def _():
        c = c_ref[...]
        s = s_ref[...]
        zr, zi = _fwd_b(ar_ref[...], ai_ref[...], c, s, mfh_ref[...], mfl_ref[...])
        kr = kr_ref[...]
        ki = ki_ref[...]
        pr = zr * kr - zi * ki
        pi = zr * ki + zi * kr
        ph, plo = _split(jnp.concatenate([pr, pi], axis=0))
        y = _dot3(mih_ref[...], mil_ref[...], ph, plo)
        yr = y[:FFT_N2]
        yi = y[FFT_N2:]
        br_ref[...] = yr * c - yi * s
        bi_ref[...] = yi * c + yr * s

    @pl.when(k1 >= k1n)
    def _():
        br_ref[...] = jnp.zeros_like(br_ref)
        bi_ref[...] = jnp.zeros_like(bi_ref)


def _stage_b(ar4, ai4, kr3, ki3, consts, n1_total):
    B, k1p, _, C = ar4.shape
    k1n, _ = _k1_rows(n1_total)
    (mfh, mfl), (mih, mil), twc, tws = consts
    blk = pl.BlockSpec((None, None, FFT_N2, C), lambda k, b: (b, k, 0, 0))
    kblk = pl.BlockSpec((None, FFT_N2, C), lambda k, b: (k, 0, 0))
    tw = pl.BlockSpec((None, FFT_N2, 1), lambda k, b: (k, 0, 0))
    mat = pl.BlockSpec((2 * FFT_N2, 2 * FFT_N2), lambda k, b: (0, 0))
    return pl.pallas_call(
        functools.partial(_stage_b_kernel, k1n=k1n),
        grid=(k1p, B),
        in_specs=[blk, blk, kblk, kblk, tw, tw, mat, mat, mat, mat],
        out_specs=[blk, blk],
        out_shape=[jax.ShapeDtypeStruct(ar4.shape, F32)] * 2,
        compiler_params=_cparams(("parallel", "parallel")),
        name="hy_stage_b",
    )(ar4, ai4, kr3, ki3, twc, tws, mfh, mfl, mih, mil)


def _stage_c_kernel(br_ref, bi_ref, gch_ref, gcl_ref, gsh_ref, gsl_ref, vv_ref, x0_ref, bias_ref, o_ref):
    brh, brl = _split(br_ref[...])
    bih, bil = _split(bi_ref[...])
    y = _dot3(gch_ref[...], gcl_ref[...], brh, brl) + _dot3(gsh_ref[...], gsl_ref[...], bih, bil)
    vv = vv_ref[...]
    o_ref[...] = (y + vv * bias_ref[...]) * x0_ref[...]


def _stage_c(br3, bi3, vv3, x03, bias_t, n1_total, tl=4096):
    B, k1p, LN = br3.shape
    n1u = vv3.shape[1]
    k1n, _ = _k1_rows(n1_total)
    k1 = np.arange(k1p)
    w = np.where((k1 == 0) | (k1 == n1_total // 2), 1.0, 2.0) * (k1 < k1n)
    ang = 2.0 * np.pi * np.outer(np.arange(n1u), k1) / n1_total
    gch, gcl = _bf_pair(np.cos(ang) * w[None, :])
    gsh, gsl = _bf_pair(-np.sin(ang) * w[None, :])
    bspec = pl.BlockSpec((None, k1p, tl), lambda b, j: (b, 0, j))
    gspec = pl.BlockSpec((n1u, k1p), lambda b, j: (0, 0))
    xspec = pl.BlockSpec((None, n1u, tl), lambda b, j: (b, 0, j))
    return pl.pallas_call(
        _stage_c_kernel,
        grid=(B, LN // tl),
        in_specs=[bspec, bspec, gspec, gspec, gspec, gspec, xspec, xspec,
                  pl.BlockSpec((1, tl), lambda b, j: (0, j))],
        out_specs=xspec,
        out_shape=jax.ShapeDtypeStruct(vv3.shape, F32),
        compiler_params=_cparams(("parallel", "parallel")),
        name="hy_stage_c",
    )(br3, bi3, gch, gcl, gsh, gsl, vv3, x03, bias_t)


def _conv3(x, w_ref, b_ref):
    S = x.shape[0]
    row = lax.broadcasted_iota(jnp.int32, x.shape, 0)
    prev = jnp.where(row == 0, 0.0, pltpu.roll(x, 1, 0))
    nxt = jnp.where(row == S - 1, 0.0, pltpu.roll(x, S - 1, 0))
    return prev * w_ref[0:1, :] + x * w_ref[1:2, :] + nxt * w_ref[2:3, :] + b_ref[...]


def _hy_gate_kernel(x0_ref, x1_ref, v_ref, w0_ref, w1_ref, w2_ref, b0_ref, b1_ref, b2_ref,
                    vv_ref, x0c_ref):
    x0c_ref[...] = _conv3(x0_ref[...], w0_ref, b0_ref)
    vv_ref[...] = _conv3(v_ref[...], w2_ref, b2_ref) * _conv3(x1_ref[...], w1_ref, b1_ref)


def _hy_gate(u3, conv_w, conv_b):
    B, S, _ = u3.shape
    nct = HY_WIDTH // LANES
    conv_b2 = conv_b.reshape(1, 3 * HY_WIDTH)
    uspec = lambda part: pl.BlockSpec((None, S, LANES), lambda b, c: (b, 0, part * nct + c))
    wspec = lambda part: pl.BlockSpec((3, LANES), lambda b, c: (0, part * nct + c))
    bspec = lambda part: pl.BlockSpec((1, LANES), lambda b, c: (0, part * nct + c))
    ospec = pl.BlockSpec((None, S, LANES), lambda b, c: (b, 0, c))
    return pl.pallas_call(
        _hy_gate_kernel,
        grid=(B, nct),
        in_specs=[uspec(0), uspec(1), uspec(2), wspec(0), wspec(1), wspec(2),
                  bspec(0), bspec(1), bspec(2)],
        out_specs=[ospec, ospec],
        out_shape=[jax.ShapeDtypeStruct((B, S, HY_WIDTH), F32)] * 2,
        compiler_params=_cparams(("parallel", "parallel")),
        name="hy_gate",
    )(u3, u3, u3, conv_w, conv_w, conv_w, conv_b2, conv_b2, conv_b2)


def _attn_kernel(q_ref, k_ref, v_ref, lq1_ref, lk1_ref, lq2_ref, lk2_ref, g_ref, o_ref, *, lam_init):
    lam = (jnp.exp(jnp.sum(lq1_ref[...] * lk1_ref[...], keepdims=True))
           - jnp.exp(jnp.sum(lq2_ref[...] * lk2_ref[...], keepdims=True)) + lam_init)
    q = q_ref[...]
    kr = k_ref[...]
    lane = lax.broadcasted_iota(jnp.int32, q.shape, 1)
    probs = []
    for c in range(2):
        sel = (lane < ATT_HEAD_DIM) if c == 0 else (lane >= ATT_HEAD_DIM)
        qc = jnp.where(sel, q, jnp.zeros_like(q))
        s = lax.dot_general(qc, kr, (((1,), (1,)), ((), ())), preferred_element_type=F32)
        m = jnp.max(s, axis=-1, keepdims=True)
        p = jnp.exp2(s - m)
        l = jnp.sum(p, axis=-1, keepdims=True)
        probs.append((p, l))
    (p1, l1), (p2, l2) = probs
    a = p1 * (1.0 / l1) - p2 * (lam / l2)
    o = jnp.dot(a.astype(BF16), v_ref[...], preferred_element_type=F32)
    o = o * lax.rsqrt(jnp.mean(o * o, axis=-1, keepdims=True) + RMS_EPS) * g_ref[...]
    o_ref[...] = o * (1.0 - lam_init)


def _rope_tables(seq):
    pos = np.arange(seq, dtype=np.float64)
    inv_freq = np.power(ROPE_THETA, -np.arange(0, ROPE_DIM, 2, dtype=np.float64) / ROPE_DIM)
    ang = pos[:, None] * inv_freq[None, :]
    half = ROPE_DIM // 2
    cos_f = np.ones((seq, LANES), np.float32)
    sin_a = np.zeros((seq, LANES), np.float32)
    sin_b = np.zeros((seq, LANES), np.float32)
    for base in range(0, LANES, ATT_HEAD_DIM):
        cos_f[:, base:base + half] = np.cos(ang)
        cos_f[:, base + half:base + ROPE_DIM] = np.cos(ang)
        sin_a[:, base:base + half] = -np.sin(ang)
        sin_b[:, base + half:base + ROPE_DIM] = np.sin(ang)
    return jnp.asarray(cos_f), jnp.asarray(sin_a), jnp.asarray(sin_b)


def _diff_attention(qkv3, lq1, lk1, lq2, lk2, subln_g, lam_init, tq=256):
    B, S, _ = qkv3.shape
    nb = ATT_WIDTH // LANES
    qspec = pl.BlockSpec((None, tq, LANES), lambda b, h, i: (b, i, h))
    kspec = pl.BlockSpec((None, S, LANES), lambda b, h, i: (b, 0, nb + h))
    vspec = pl.BlockSpec((None, S, LANES), lambda b, h, i: (b, 0, 2 * nb + h))
    vec = pl.BlockSpec((1, ATT_HEAD_DIM), lambda b, h, i: (0, 0))
    r1 = lambda a: a.reshape(1, -1)
    return pl.pallas_call(
        functools.partial(_attn_kernel, lam_init=lam_init),
        grid=(B, ATT_HEADS, S // tq),
        in_specs=[qspec, kspec, vspec, vec, vec, vec, vec,
                  pl.BlockSpec((1, LANES), lambda b, h, i: (0, 0))],
        out_specs=pl.BlockSpec((None, tq, LANES), lambda b, h, i: (b, i, h)),
        out_shape=jax.ShapeDtypeStruct((B, S, ATT_WIDTH), F32),
        compiler_params=_cparams(("parallel", "parallel", "arbitrary")),
        name="diff_attn",
    )(qkv3, qkv3, qkv3, r1(lq1), r1(lk1), r1(lq2), r1(lk2), r1(subln_g))


def _pool_kernel(u_ref, w_ref, b_ref, sc_ref, o_ref):
    ct = pl.program_id(1)
    x = u_ref[...]
    S = x.shape[0]
    row = lax.broadcasted_iota(jnp.int32, x.shape, 0)
    lane = lax.broadcasted_iota(jnp.int32, x.shape, 1)

    def shifted(d):
        if d == 0:
            return x
        r = pltpu.roll(x, (-d) % S, 0)
        return jnp.where((row + d >= 0) & (row + d < S), r, 0.0)

    sums = {}
    acc = shifted(-1) + x
    sums[2] = acc
    lo, hi = -1, 0
    for w in POOL_WINDOWS[1:]:
        for d in list(range(-(w // 2), lo)) + list(range(hi + 1, w // 2)):
            acc = acc + shifted(d)
        lo, hi = -(w // 2), w // 2 - 1
        sums[w] = acc
    grp = 2 * ct + (lane >= POOL_GROUP).astype(jnp.int32)
    win_sum = sums[POOL_WINDOWS[-1]]
    half = jnp.full(x.shape, POOL_WINDOWS[-1] // 2, jnp.int32)
    for g in range(len(POOL_WINDOWS) - 2, -1, -1):
        win_sum = jnp.where(grp == g, sums[POOL_WINDOWS[g]], win_sum)
        half = jnp.where(grp == g, POOL_WINDOWS[g] // 2, half)
    cnt = jnp.minimum(row + half - 1, S - 1) - jnp.maximum(row - half, 0) + 1
    y = win_sum / cnt.astype(F32) - x
    y = _dot3(*_split(y), *_split(w_ref[...])) + b_ref[...]
    o_ref[...] = y * sc_ref[...]


def _pool_mixer(u3, w, b, scale):
    B, S, _ = u3.shape
    nct = POOL_WIDTH // LANES
    gpt = LANES // POOL_GROUP
    wbd = jnp.zeros((nct, LANES, LANES), F32)
    for g in range(len(POOL_WINDOWS)):
        t, o = divmod(g, gpt)
        wbd = wbd.at[t, o * POOL_GROUP:(o + 1) * POOL_GROUP, o * POOL_GROUP:(o + 1) * POOL_GROUP].set(w[g])
    vspec = pl.BlockSpec((1, LANES), lambda bb, c: (0, c))
    return pl.pallas_call(
        _pool_kernel,
        grid=(B, nct),
        in_specs=[pl.BlockSpec((None, S, LANES), lambda bb, c: (bb, 0, c)),
                  pl.BlockSpec((None, LANES, LANES), lambda bb, c: (c, 0, 0)),
                  vspec, vspec],
        out_specs=pl.BlockSpec((None, S, LANES), lambda bb, c: (bb, 0, c)),
        out_shape=jax.ShapeDtypeStruct((B, S, POOL_WIDTH), F32),
        compiler_params=_cparams(("parallel", "parallel")),
        name="pool_mixer",
    )(u3, wbd, b.reshape(1, POOL_WIDTH), scale.reshape(1, POOL_WIDTH))


def _outproj_kernel(h_ref, yh_ref, ya_ref, yp_ref, wh_ref, wa_ref, wp_ref, g_ref, b_ref,
                    wgh_ref, wgl_ref, bg_ref, o_ref, gate_ref):
    d = functools.partial(jnp.dot, preferred_element_type=F32)
    mix = (d(yh_ref[...].astype(BF16), wh_ref[...]) + d(ya_ref[...].astype(BF16), wa_ref[...])
           + d(yp_ref[...].astype(BF16), wp_ref[...]))
    h1 = _ln_rows(DN_ALPHA * h_ref[...] + mix, g_ref[...], b_ref[...])
    o_ref[...] = h1

    logit = _dot3(*_split(h1), wgh_ref[...], wgl_ref[...]) + bg_ref[...]
    lni = lax.broadcasted_iota(jnp.int32, logit.shape, 1)
    ln = lni.astype(F32)
    grp = lax.shift_right_arithmetic(lni - GATE_FINE_LANE, 2).astype(F32)
    first = lambda mask: jnp.min(jnp.where(mask, ln, float(LANES)), axis=-1, keepdims=True)
    cmask = lni < N_GROUPS
    lc = jnp.where(cmask, logit, NEG_BIG)
    mc = jnp.max(lc, axis=-1, keepdims=True)
    gw = 1.0 / jnp.sum(jnp.where(cmask, jnp.exp(lc - mc), 0.0), axis=-1, keepdims=True)
    gi = first(cmask & (lc == mc))
    fmask = (lni >= GATE_FINE_LANE) & (lni < GATE_FINE_LANE + N_EXPERTS) & (grp == gi)
    lf = jnp.where(fmask, logit, NEG_BIG)
    m1 = jnp.max(lf, axis=-1, keepdims=True)
    i1 = first(fmask & (lf == m1))
    rest = fmask & (ln != i1)
    lf2 = jnp.where(rest, logit, NEG_BIG)
    m2 = jnp.max(lf2, axis=-1, keepdims=True)
    i2 = first(rest & (lf2 == m2))
    e2 = jnp.exp(m2 - m1)
    w1 = gw / (1.0 + e2)
    w2 = gw * e2 / (1.0 + e2)
    gate_ref[...] = jnp.where(ln == i1, w1, 0.0) + jnp.where(ln == i2, w2, 0.0)


def _out_proj_ln_gate(h2, yh, ya, yp, w_out_bf, g, b, wgc, bgc, wgf, bgf, tm=512):
    T, D = h2.shape
    wg = jnp.zeros((D, LANES), F32).at[:, :N_GROUPS].set(wgc)
    wg = wg.at[:, GATE_FINE_LANE:GATE_FINE_LANE + N_EXPERTS].set(wgf)
    bg = jnp.zeros((1, LANES), F32).at[0, :N_GROUPS].set(bgc)
    bg = bg.at[0, GATE_FINE_LANE:GATE_FINE_LANE + N_EXPERTS].set(bgf)
    wgh, wgl = _split(wg)
    o1, o2 = HY_WIDTH, HY_WIDTH + ATT_WIDTH
    row = lambda c: pl.BlockSpec((tm, c), lambda i: (i, 0))
    full = lambda r, c: pl.BlockSpec((r, c), lambda i: (0, 0))
    return pl.pallas_call(
        _outproj_kernel,
        grid=(T // tm,),
        in_specs=[row(D), row(HY_WIDTH), row(ATT_WIDTH), row(POOL_WIDTH),
                  full(HY_WIDTH, D), full(ATT_WIDTH, D), full(POOL_WIDTH, D),
                  full(1, D), full(1, D), full(D, LANES), full(D, LANES), full(1, LANES)],
        out_specs=[row(D), row(LANES)],
        out_shape=[jax.ShapeDtypeStruct((T, D), F32), jax.ShapeDtypeStruct((T, LANES), F32)],
        compiler_params=_cparams(("parallel",)),
        name="out_proj_ln_gate",
    )(h2, yh, ya, yp, w_out_bf[:o1], w_out_bf[o1:o2], w_out_bf[o2:], g.reshape(1, D), b.reshape(1, D),
      wgh, wgl, bg)


def _moe_kernel(h_ref, gate_ref, w1_ref, w3_ref, w2_ref, p_ref, pwg_ref, pbg_ref, pwp_ref,
                g_ref, b_ref, o_ref, acc_scr, hb_scr):
    e = pl.program_id(1)
    d = functools.partial(jnp.dot, preferred_element_type=F32)

    @pl.when(e == 0)
    def _():
        hb_scr[...] = h_ref[...].astype(BF16)
        acc_scr[...] = jnp.zeros_like(acc_scr)

    hb = hb_scr[...]
    a = d(hb, w1_ref[...])
    c = d(hb, w3_ref[...])
    gates = gate_ref[...]
    ln = lax.broadcasted_iota(jnp.int32, gates.shape, 1)
    ge = jnp.sum(jnp.where(ln == GATE_FINE_LANE + e, gates, 0.0), axis=-1, keepdims=True)
    act = a * jax.nn.sigmoid(a) * c * ge
    acc_scr[...] += d(act.astype(BF16), w2_ref[...])

    @pl.when(e == pl.num_programs(1) - 1)
    def _():
        z = d(hb, pwg_ref[...]) + pbg_ref[...]
        y_ple = jax.nn.sigmoid(z) * d(p_ref[...].astype(BF16), pwp_ref[...])
        r = DN_ALPHA * h_ref[...] + acc_scr[...] + y_ple
        o_ref[...] = _ln_rows(r, g_ref[...], b_ref[...])


def _moe_ple_ln(h2, gates, w1_bf, w3_bf, w2_bf, p2, pwg_bf, pbg, pwp_bf, g, b, tm=512):
    T, D = h2.shape
    E = w1_bf.shape[0]
    row = lambda c: pl.BlockSpec((tm, c), lambda i, e: (i, 0))
    full = lambda r, c: pl.BlockSpec((r, c), lambda i, e: (0, 0))
    return pl.pallas_call(
        _moe_kernel,
        grid=(T // tm, E),
        in_specs=[row(D), row(LANES),
                  pl.BlockSpec((None, D, D_EXPERT), lambda i, e: (e, 0, 0)),
                  pl.BlockSpec((None, D, D_EXPERT), lambda i, e: (e, 0, 0)),
                  pl.BlockSpec((None, D_EXPERT, D), lambda i, e: (e, 0, 0)),
                  row(PLE_DIM), full(D, D), full(1, D), full(PLE_DIM, D), full(1, D), full(1, D)],
        out_specs=row(D),
        out_shape=jax.ShapeDtypeStruct((T, D), F32),
        scratch_shapes=[pltpu.VMEM((tm, D), F32), pltpu.VMEM((tm, D), BF16)],
        compiler_params=_cparams(("parallel", "arbitrary")),
        name="moe_ple_ln",
    )(h2, gates, w1_bf, w3_bf, w2_bf, p2, pwg_bf, pbg.reshape(1, D), pwp_bf, g.reshape(1, D), b.reshape(1, D))


def _hyena_mixer(u3, kr, ki, consts, conv_w, conv_b, bias_d, n1_total):
    B, S, _ = u3.shape
    C = HY_WIDTH
    n1u = S // FFT_N2
    vv, x0c = _hy_gate(u3, conv_w, conv_b)
    view = lambda a: a.reshape(B, n1u, FFT_N2 * C)
    ar, ai = _stage_a(view(vv), n1_total)
    k1p = ar.shape[1]
    to4 = lambda a: a.reshape(B, k1p, FFT_N2, C)
    br, bi = _stage_b(to4(ar), to4(ai), kr, ki, consts, n1_total)
    to3 = lambda a: a.reshape(B, k1p, FFT_N2 * C)
    bias_t = jnp.tile(bias_d.reshape(1, C), (1, FFT_N2))
    y = _stage_c(to3(br), to3(bi), view(vv), view(x0c), bias_t, n1_total)
    return y.reshape(B * S, C)


def kernel(x, p, ln0_g, ln0_b, w_in, hy_conv_w, hy_conv_b, hy_fw1, hy_fb1, hy_freq1, hy_fw2, hy_fb2, hy_freq2, hy_fw3, hy_bias, att_lq1, att_lk1, att_lq2, att_lk2, att_subln_g, pool_w, pool_b, pool_scale, w_out, ln1_g, ln1_b, moe_wgc, moe_bgc, moe_wgf, moe_bgf, moe_w1, moe_w3, moe_w2, ple_wg, ple_bg, ple_wp, ln2_g, ln2_b):
    B, S, D = x.shape
    L = w_in.shape[0]
    T = B * S
    n1_total = 2 * S // FFT_N2
    C = HY_WIDTH

    kfilt, ssq = _hy_filter(S, hy_fw1, hy_fb1, hy_freq1, hy_fw2, hy_fb2, hy_freq2, hy_fw3)
    kar, kai = _stage_a(kfilt.reshape(L, n1_total, FFT_N2 * C), n1_total)
    k1p = kar.shape[1]
    consts = _stage_b_consts(n1_total)
    kr_all, ki_all = _stage_b_filter(kar.reshape(L, k1p, FFT_N2, C), kai.reshape(L, k1p, FFT_N2, C),
                                     ssq, consts, n1_total)
    tables = _rope_tables(S)

    h = _layer_norm(x.reshape(T, D), ln0_g, ln0_b)
    for i in range(L):
        lam_init = 0.8 - 0.6 * math.exp(-0.3 * i)
        uh, qkv, up = _in_proj(h, w_in[i].astype(BF16), tables, S)
        y_hy = _hyena_mixer(uh.reshape(B, S, 3 * C), kr_all[i], ki_all[i], consts,
                            hy_conv_w[i], hy_conv_b[i], hy_bias[i], n1_total)
        y_att = _diff_attention(qkv.reshape(B, S, 3 * ATT_WIDTH), att_lq1[i], att_lk1[i], att_lq2[i],
                                att_lk2[i], att_subln_g[i], lam_init).reshape(T, ATT_WIDTH)
        y_pool = _pool_mixer(up.reshape(B, S, POOL_WIDTH), pool_w[i], pool_b[i],
                             pool_scale[i]).reshape(T, POOL_WIDTH)
        h, gates = _out_proj_ln_gate(h, y_hy, y_att, y_pool, w_out[i].astype(BF16), ln1_g[i], ln1_b[i],
                                     moe_wgc[i], moe_bgc[i], moe_wgf[i], moe_bgf[i])
        h = _moe_ple_ln(h, gates, moe_w1[i].astype(BF16), moe_w3[i].astype(BF16), moe_w2[i].astype(BF16),
                        p[i].reshape(T, PLE_DIM), ple_wg[i].astype(BF16), ple_bg[i], ple_wp[i].astype(BF16),
                        ln2_g[i], ln2_b[i])
    return h.reshape(B, S, D)
```

```python
import functools
import math

import numpy as np
import jax
import jax.numpy as jnp
from jax import lax
from jax.experimental import pallas as pl
from jax.experimental.pallas import tpu as pltpu

F32 = jnp.float32
BF16 = jnp.bfloat16

D_MODEL = 1024
DEPTH = 4
HY_WIDTH = 256
ATT_WIDTH = 512
ATT_HEADS = 4
ATT_HEAD_DIM = 64
POOL_WINDOWS = (2, 4, 8, 16)
POOL_WIDTH = 256
POOL_GROUP = 64
IN_WIDTH = 3 * HY_WIDTH + 3 * ATT_WIDTH + POOL_WIDTH
ROPE_THETA = 500000.0
ROPE_DIM = ATT_HEAD_DIM // 4
HY_EMB = 33
HY_BANDS = (HY_EMB - 1) // 2
HY_FILTER_HIDDEN = 64
HY_DECAY_TARGET = 1e-2
HY_FAST_DECAY = 0.3
HY_SLOW_DECAY = 1.5
N_GROUPS = 4
EXPERTS_PER_GROUP = 4
N_EXPERTS = 16
D_EXPERT = 256
PLE_DIM = 256
LN_EPS = 1e-5
RMS_EPS = 1e-5
DN_ALPHA = (2 * DEPTH) ** 0.25

LANES = 128
FFT_N2 = 128
GATE_COARSE_LANE = 0
GATE_FINE_LANE = N_GROUPS
NEG_BIG = -1e30
LOG2E = 1.4426950408889634
VMEM_LIMIT = 56 * 1024 * 1024


def _cparams(sem):
    return pltpu.CompilerParams(dimension_semantics=sem, vmem_limit_bytes=VMEM_LIMIT)


def _split(x):
    hi = x.astype(BF16)
    lo = (x - hi.astype(F32)).astype(BF16)
    return hi, lo


def _dot3(ah, al, bh, bl):
    d = functools.partial(jnp.dot, preferred_element_type=F32)
    return d(ah, bh) + (d(ah, bl) + d(al, bh))


def _ln_rows(x, g, b):
    mu = jnp.mean(x, axis=-1, keepdims=True)
    xc = x - mu
    var = jnp.mean(xc * xc, axis=-1, keepdims=True)
    return xc * lax.rsqrt(var + LN_EPS) * g + b


def _ln_kernel(x_ref, g_ref, b_ref, o_ref):
    o_ref[...] = _ln_rows(x_ref[...], g_ref[...], b_ref[...])


def _layer_norm(x2, g, b, tm=512):
    T, D = x2.shape
    return pl.pallas_call(
        _ln_kernel,
        grid=(T // tm,),
        in_specs=[pl.BlockSpec((tm, D), lambda i: (i, 0)),
                  pl.BlockSpec((1, D), lambda i: (0, 0)),
                  pl.BlockSpec((1, D), lambda i: (0, 0))],
        out_specs=pl.BlockSpec((tm, D), lambda i: (i, 0)),
        out_shape=jax.ShapeDtypeStruct((T, D), F32),
        compiler_params=_cparams(("parallel",)),
        name="ln0",
    )(x2, g.reshape(1, D), b.reshape(1, D))


def _rope(x, cos_f, sin_a, sin_b):
    half = ROPE_DIM // 2
    return x * cos_f + pltpu.roll(x, LANES - half, 1) * sin_a + pltpu.roll(x, half, 1) * sin_b


def _inproj_kernel(h_ref, w_ref, cos_ref, sa_ref, sb_ref, uh_ref, qkv_ref, up_ref):
    hb = h_ref[...].astype(BF16)
    mm = lambda c0, c1: jnp.dot(hb, w_ref[:, c0:c1], preferred_element_type=F32)
    o_q = 3 * HY_WIDTH
    o_k = o_q + ATT_WIDTH
    o_v = o_k + ATT_WIDTH
    o_p = o_v + ATT_WIDTH
    uh_ref[...] = mm(0, o_q)
    up_ref[...] = mm(o_p, IN_WIDTH)
    cos_f, sin_a, sin_b = cos_ref[...], sa_ref[...], sb_ref[...]
    q = mm(o_q, o_k)
    k = mm(o_k, o_v)
    qscale = ATT_HEAD_DIM ** -0.5 * LOG2E
    for hd in range(ATT_HEADS):
        sl = slice(hd * LANES, (hd + 1) * LANES)
        qkv_ref[:, sl] = (_rope(q[:, sl], cos_f, sin_a, sin_b) * qscale).astype(BF16)
        qkv_ref[:, ATT_WIDTH + hd * LANES:ATT_WIDTH + (hd + 1) * LANES] = \
            _rope(k[:, sl], cos_f, sin_a, sin_b).astype(BF16)
    qkv_ref[:, 2 * ATT_WIDTH:] = mm(o_v, o_p).astype(BF16)


def _in_proj(h2, w_bf, tables, seq, tm=512):
    T, D = h2.shape
    N = w_bf.shape[1]
    spt = seq // tm
    tspec = pl.BlockSpec((tm, LANES), lambda i: (i % spt, 0))
    row = lambda c: pl.BlockSpec((tm, c), lambda i: (i, 0))
    return pl.pallas_call(
        _inproj_kernel,
        grid=(T // tm,),
        in_specs=[row(D), pl.BlockSpec((D, N), lambda i: (0, 0)), tspec, tspec, tspec],
        out_specs=[row(3 * HY_WIDTH), row(3 * ATT_WIDTH), row(POOL_WIDTH)],
        out_shape=[jax.ShapeDtypeStruct((T, 3 * HY_WIDTH), F32),
                   jax.ShapeDtypeStruct((T, 3 * ATT_WIDTH), BF16),
                   jax.ShapeDtypeStruct((T, POOL_WIDTH), F32)],
        compiler_params=_cparams(("parallel",)),
        name="in_proj",
    )(h2, w_bf, *tables)


def _hy_filter_kernel(band_ref, fw1_ref, fb1_ref, fr1_ref, fw2_ref, fb2_ref, fr2_ref, fw3_ref,
                      dl_ref, k_ref, ssq_ref, *, seq, tr):
    j = pl.program_id(1)
    n = j * tr + lax.broadcasted_iota(jnp.int32, (tr, 1), 0)
    pos = jnp.where(n < seq, n, 2 * seq - n).astype(F32)
    t = pos * (1.0 / (seq - 1))
    wpos = (2.0 * math.pi / seq) * pos
    lane = lax.broadcasted_iota(jnp.int32, (tr, LANES), 1)
    arg = wpos * band_ref[...]
    z = jnp.where(lane == 0, t,
                  jnp.where(lane <= HY_BANDS, jnp.cos(arg),
                            jnp.where(lane < HY_EMB, -jnp.sin(arg), 0.0)))
    h1 = _dot3(*_split(z), *_split(fw1_ref[...])) + fb1_ref[...]
    h1 = jnp.sin(fr1_ref[...] * h1)
    h2 = _dot3(*_split(h1), *_split(fw2_ref[...])) + fb2_ref[...]
    h2 = jnp.sin(fr2_ref[...] * h2)
    filt = _dot3(*_split(h2), *_split(fw3_ref[...]))
    window = jnp.exp(-t * jnp.abs(dl_ref[...]))
    kk = jnp.where(n < seq, filt[:, :HY_WIDTH], filt[:, HY_WIDTH:]) * window
    kk = jnp.where(n == seq, 0.0, kk)
    k_ref[...] = kk

    @pl.when(j == 0)
    def _():
        ssq_ref[...] = jnp.zeros_like(ssq_ref)

    ssq_ref[...] += jnp.sum(kk * kk, axis=0, keepdims=True)


def _hy_filter(seq, fw1, fb1, freq1, fw2, fb2, freq2, fw3, tr=1024):
    L = fw1.shape[0]
    n = 2 * seq
    H = HY_FILTER_HIDDEN
    bands = np.linspace(1e-4, HY_BANDS - 1, HY_BANDS)
    bandv = np.zeros((1, LANES), np.float32)
    bandv[0, 1:1 + HY_BANDS] = bands
    bandv[0, 1 + HY_BANDS:HY_EMB] = bands
    fw1p = jnp.zeros((L, LANES, H), F32).at[:, :HY_EMB].set(fw1)
    max_decay = math.log(HY_DECAY_TARGET) / HY_FAST_DECAY
    min_decay = math.log(HY_DECAY_TARGET) / HY_SLOW_DECAY
    deltas = np.linspace(min_decay, max_decay, HY_WIDTH).astype(np.float32).reshape(1, HY_WIDTH)
    vec = lambda a: a.reshape(L, 1, a.shape[-1])
    lspec = lambda r, c: pl.BlockSpec((None, r, c), lambda l, j: (l, 0, 0))
    return pl.pallas_call(
        functools.partial(_hy_filter_kernel, seq=seq, tr=tr),
        grid=(L, n // tr),
        in_specs=[pl.BlockSpec((1, LANES), lambda l, j: (0, 0)),
                  lspec(LANES, H), lspec(1, H), lspec(1, H),
                  lspec(H, H), lspec(1, H), lspec(1, H),
                  lspec(H, 2 * HY_WIDTH),
                  pl.BlockSpec((1, HY_WIDTH), lambda l, j: (0, 0))],
        out_specs=[pl.BlockSpec((None, tr, HY_WIDTH), lambda l, j: (l, j, 0)),
                   pl.BlockSpec((None, 1, HY_WIDTH), lambda l, j: (l, 0, 0))],
        out_shape=[jax.ShapeDtypeStruct((L, n, HY_WIDTH), F32),
                   jax.ShapeDtypeStruct((L, 1, HY_WIDTH), F32)],
        compiler_params=_cparams(("parallel", "arbitrary")),
        name="hy_filter",
    )(jnp.asarray(bandv), fw1p, vec(fb1), vec(freq1), fw2, vec(fb2), vec(freq2), fw3,
      jnp.asarray(deltas))


def _k1_rows(n1):
    k1 = n1 // 2 + 1
    return k1, -(-k1 // 8) * 8


def _bf_pair(m):
    m32 = jnp.asarray(np.asarray(m, np.float32))
    return _split(m32)


def _stage_a_kernel(x_ref, frh_ref, frl_ref, fih_ref, fil_ref, ar_ref, ai_ref):
    xh, xl = _split(x_ref[...])
    ar_ref[...] = _dot3(frh_ref[...], frl_ref[...], xh, xl)
    ai_ref[...] = _dot3(fih_ref[...], fil_ref[...], xh, xl)


def _stage_a(x3, n1_total, tl=4096):
    Bt, n1u, LN = x3.shape
    k1n, k1p = _k1_rows(n1_total)
    ang = 2.0 * np.pi * np.outer(np.arange(k1p), np.arange(n1u)) / n1_total
    valid = (np.arange(k1p) < k1n)[:, None]
    frh, frl = _bf_pair(np.cos(ang) * valid)
    fih, fil = _bf_pair(-np.sin(ang) * valid)
    cspec = pl.BlockSpec((k1p, n1u), lambda b, j: (0, 0))
    ospec = pl.BlockSpec((None, k1p, tl), lambda b, j: (b, 0, j))
    return pl.pallas_call(
        _stage_a_kernel,
        grid=(Bt, LN // tl),
        in_specs=[pl.BlockSpec((None, n1u, tl), lambda b, j: (b, 0, j)), cspec, cspec, cspec, cspec],
        out_specs=[ospec, ospec],
        out_shape=[jax.ShapeDtypeStruct((Bt, k1p, LN), F32)] * 2,
        compiler_params=_cparams(("parallel", "parallel")),
        name="hy_stage_a",
    )(x3, frh, frl, fih, fil)


def _stage_b_consts(n1_total):
    n = n1_total * FFT_N2
    k1n, k1p = _k1_rows(n1_total)
    ang2 = 2.0 * np.pi * np.outer(np.arange(FFT_N2), np.arange(FFT_N2)) / FFT_N2
    c2, s2 = np.cos(ang2), np.sin(ang2)
    mf = np.block([[c2, s2], [-s2, c2]])
    mi = np.block([[c2, -s2], [s2, c2]])
    angt = 2.0 * np.pi * np.outer(np.arange(k1p), np.arange(FFT_N2)) / n
    twc = jnp.asarray(np.cos(angt).astype(np.float32)).reshape(k1p, FFT_N2, 1)
    tws = jnp.asarray(np.sin(angt).astype(np.float32)).reshape(k1p, FFT_N2, 1)
    return _bf_pair(mf), _bf_pair(mi), twc, tws


def _fwd_b(ar, ai, c, s, mfh, mfl):
    tr_ = ar * c + ai * s
    ti_ = ai * c - ar * s
    xh, xl = _split(jnp.concatenate([tr_, ti_], axis=0))
    z = _dot3(mfh, mfl, xh, xl)
    return z[:FFT_N2], z[FFT_N2:]


def _stage_bk_kernel(ar_ref, ai_ref, c_ref, s_ref, mfh_ref, mfl_ref, ssq_ref, kr_ref, ki_ref,
                     *, k1n, inv_n):
    k1 = pl.program_id(1)

    @pl.when(k1 < k1n)
    def _():
        zr, zi = _fwd_b(ar_ref[...], ai_ref[...], c_ref[...], s_ref[...], mfh_ref[...], mfl_ref[...])
        scale = lax.rsqrt(ssq_ref[...] + 1e-6) * inv_n
        kr_ref[...] = zr * scale
        ki_ref[...] = zi * scale

    @pl.when(k1 >= k1n)
    def _():
        kr_ref[...] = jnp.zeros_like(kr_ref)
        ki_ref[...] = jnp.zeros_like(ki_ref)


def _stage_b_filter(ar4, ai4, ssq, consts, n1_total):
    L, k1p, _, C = ar4.shape
    k1n, _ = _k1_rows(n1_total)
    (mfh, mfl), _, twc, tws = consts
    blk = pl.BlockSpec((None, None, FFT_N2, C), lambda l, k: (l, k, 0, 0))
    tw = pl.BlockSpec((None, FFT_N2, 1), lambda l, k: (k, 0, 0))
    mat = pl.BlockSpec((2 * FFT_N2, 2 * FFT_N2), lambda l, k: (0, 0))
    return pl.pallas_call(
        functools.partial(_stage_bk_kernel, k1n=k1n, inv_n=1.0 / (n1_total * FFT_N2)),
        grid=(L, k1p),
        in_specs=[blk, blk, tw, tw, mat, mat, pl.BlockSpec((None, 1, C), lambda l, k: (l, 0, 0))],
        out_specs=[blk, blk],
        out_shape=[jax.ShapeDtypeStruct(ar4.shape, F32)] * 2,
        compiler_params=_cparams(("parallel", "parallel")),
        name="hy_stage_b_filter",
    )(ar4, ai4, twc, tws, mfh, mfl, ssq)


SUBLANES = 8


def _hy_conv_kernel(vv_ref, x0_ref, kr_ref, ki_ref, tw_ref, kah_ref, kal_ref, kch_ref, kcl_ref,
                    mfh_ref, mfl_ref, mih_ref, mil_ref, bias_ref, o_ref, a_scr, *, n1u, k1n, k1p):
    groups = FFT_N2 // SUBLANES

    def rows(n1, g):
        return pl.ds(pl.multiple_of(n1 * FFT_N2 + g * SUBLANES, SUBLANES), SUBLANES)

    def stage_a(g, carry):
        xg = jnp.concatenate([vv_ref[rows(n1, g), :] for n1 in range(n1u)], axis=0)
        ag = _dot3(kah_ref[...], kal_ref[...], *_split(xg))
        gs = pl.ds(pl.multiple_of(g * SUBLANES, SUBLANES), SUBLANES)
        for part in range(2):
            for k1 in range(k1p):
                r = (part * k1p + k1) * SUBLANES
                a_scr[part, k1, gs, :] = ag[r:r + SUBLANES]
        return carry

    lax.fori_loop(0, groups, stage_a, 0)

    def stage_b(k1, carry):
        tw = tw_ref[k1]
        c = tw[:, 0:1]
        s = tw[:, 1:2]
        zr, zi = _fwd_b(a_scr[0, k1], a_scr[1, k1], c, s, mfh_ref[...], mfl_ref[...])
        kr = kr_ref[k1]
        ki = ki_ref[k1]
        pr = zr * kr - zi * ki
        pi = zr * ki + zi * kr
        y = _dot3(mih_ref[...], mil_ref[...], *_split(jnp.concatenate([pr, pi], axis=0)))
        yr = y[:FFT_N2]
        yi = y[FFT_N2:]
        a_scr[0, k1] = yr * c - yi * s
        a_scr[1, k1] = yi * c + yr * s
        return carry

    lax.fori_loop(0, k1n, stage_b, 0)

    def stage_c(g, carry):
        gs = pl.ds(pl.multiple_of(g * SUBLANES, SUBLANES), SUBLANES)
        bg = jnp.concatenate([a_scr[part, k1, gs, :] for part in range(2) for k1 in range(k1p)], axis=0)
        yg = _dot3(kch_ref[...], kcl_ref[...], *_split(bg))
        for n1 in range(n1u):
            r = rows(n1, g)
            o_ref[r, :] = (yg[n1 * SUBLANES:(n1 + 1) * SUBLANES] + vv_ref[r, :] * bias_ref[...]) * x0_ref[r, :]
        return carry

    lax.fori_loop(0, groups, stage_c, 0)


def _hy_conv(vv, x0c, kr3, ki3, consts, bias_d, n1_total):
    B, S, C = vv.shape
    n1u = S // FFT_N2
    k1n, k1p = _k1_rows(n1_total)
    (mfh, mfl), (mih, mil), twc, tws = consts
    tw = jnp.concatenate([twc, tws], axis=-1)
    k1 = np.arange(k1p)
    valid = (k1 < k1n)[:, None]
    eye = np.eye(SUBLANES)
    ang_a = 2.0 * np.pi * np.outer(k1, np.arange(n1u)) / n1_total
    ka = np.concatenate([np.kron(np.cos(ang_a) * valid, eye), np.kron(-np.sin(ang_a) * valid, eye)], axis=0)
    w = np.where((k1 == 0) | (k1 == n1_total // 2), 1.0, 2.0) * (k1 < k1n)
    ang_c = 2.0 * np.pi * np.outer(np.arange(n1u), k1) / n1_total
    kc = np.concatenate([np.kron(np.cos(ang_c) * w[None, :], eye), np.kron(-np.sin(ang_c) * w[None, :], eye)], axis=1)
    kah, kal = _bf_pair(ka)
    kch, kcl = _bf_pair(kc)
    nct = C // LANES
    xspec = pl.BlockSpec((None, S, LANES), lambda c, b: (b, 0, c))
    kspec = pl.BlockSpec((k1p, FFT_N2, LANES), lambda c, b: (0, 0, c))
    full = lambda a: pl.BlockSpec(a.shape, lambda c, b: (0,) * a.ndim)
    return pl.pallas_call(
        functools.partial(_hy_conv_kernel, n1u=n1u, k1n=k1n, k1p=k1p),
        grid=(nct, B),
        in_specs=[xspec, xspec, kspec, kspec, full(tw), full(kah), full(kal), full(kch), full(kcl),
                  full(mfh), full(mfl), full(mih), full(mil),
                  pl.BlockSpec((1, LANES), lambda c, b: (0, c))],
        out_specs=xspec,
        out_shape=jax.ShapeDtypeStruct((B, S, C), F32),
        scratch_shapes=[pltpu.VMEM((2, k1p, FFT_N2, LANES), F32)],
        compiler_params=_cparams(("parallel", "parallel")),
        name="hy_conv",
    )(vv, x0c, kr3, ki3, tw, kah, kal, kch, kcl, mfh, mfl, mih, mil, bias_d.reshape(1, C))


def _conv3(x, w_ref, b_ref):
    S = x.shape[0]
    row = lax.broadcasted_iota(jnp.int32, x.shape, 0)
    prev = jnp.where(row == 0, 0.0, pltpu.roll(x, 1, 0))
    nxt = jnp.where(row == S - 1, 0.0, pltpu.roll(x, S - 1, 0))
    return prev * w_ref[0:1, :] + x * w_ref[1:2, :] + nxt * w_ref[2:3, :] + b_ref[...]


def _hy_gate_kernel(x0_ref, x1_ref, v_ref, w0_ref, w1_ref, w2_ref, b0_ref, b1_ref, b2_ref,
                    vv_ref, x0c_ref):
    x0c_ref[...] = _conv3(x0_ref[...], w0_ref, b0_ref)
    vv_ref[...] = _conv3(v_ref[...], w2_ref, b2_ref) * _conv3(x1_ref[...], w1_ref, b1_ref)


def _hy_gate(u3, conv_w, conv_b):
    B, S, _ = u3.shape
    nct = HY_WIDTH // LANES
    conv_b2 = conv_b.reshape(1, 3 * HY_WIDTH)
    uspec = lambda part: pl.BlockSpec((None, S, LANES), lambda b, c: (b, 0, part * nct + c))
    wspec = lambda part: pl.BlockSpec((3, LANES), lambda b, c: (0, part * nct + c))
    bspec = lambda part: pl.BlockSpec((1, LANES), lambda b, c: (0, part * nct + c))
    ospec = pl.BlockSpec((None, S, LANES), lambda b, c: (b, 0, c))
    return pl.pallas_call(
        _hy_gate_kernel,
        grid=(B, nct),
        in_specs=[uspec(0), uspec(1), uspec(2), wspec(0), wspec(1), wspec(2),
                  bspec(0), bspec(1), bspec(2)],
        out_specs=[ospec, ospec],
        out_shape=[jax.ShapeDtypeStruct((B, S, HY_WIDTH), F32)] * 2,
        compiler_params=_cparams(("parallel", "parallel")),
        name="hy_gate",
    )(u3, u3, u3, conv_w, conv_w, conv_w, conv_b2, conv_b2, conv_b2)


def _attn_kernel(q_ref, k_ref, v_ref, lq1_ref, lk1_ref, lq2_ref, lk2_ref, g_ref, o_ref,
                 s_scr, a_scr, inv_scr, *, lam_init, tq):
    S = q_ref.shape[0]
    nblk = S // tq
    lam = (jnp.exp(jnp.sum(lq1_ref[...] * lk1_ref[...], keepdims=True))
           - jnp.exp(jnp.sum(lq2_ref[...] * lk2_ref[...], keepdims=True)) + lam_init)
    lane = lax.broadcasted_iota(jnp.int32, (tq, LANES), 1)

    def blk(i):
        return pl.ds(pl.multiple_of(i * tq, tq), tq)

    def scores(i, slot):
        q = q_ref[blk(i), :]
        for c in range(2):
            sel = (lane < ATT_HEAD_DIM) if c == 0 else (lane >= ATT_HEAD_DIM)
            qc = jnp.where(sel, q, jnp.zeros_like(q))
            s_scr[slot, c] = lax.dot_general(qc, k_ref[...], (((1,), (1,)), ((), ())),
                                             preferred_element_type=F32)

    def softmax(slot):
        pl_ = []
        for c in range(2):
            s = s_scr[slot, c]
            m = jnp.max(s, axis=-1, keepdims=True)
            p = jnp.exp2(s - m)
            pl_.append((p.astype(BF16), jnp.sum(p, axis=-1, keepdims=True)))
        (p1, l1), (p2, l2) = pl_
        a_scr[slot] = p1 - (lam * l1 / l2).astype(BF16) * p2
        inv_scr[slot] = 1.0 / l1

    def values(i, slot):
        o = jnp.dot(a_scr[slot], v_ref[...], preferred_element_type=F32) * inv_scr[slot]
        o = o * lax.rsqrt(jnp.mean(o * o, axis=-1, keepdims=True) + RMS_EPS) * g_ref[...]
        o_ref[blk(i), :] = o * (1.0 - lam_init)

    scores(0, 0)
    softmax(0)
    scores(1, 1)

    def pair(j, carry):
        t = 2 * j
        values(t - 2, 0)
        softmax(1)
        scores(t, 0)
        values(t - 1, 1)
        softmax(0)
        scores(t + 1, 1)
        return carry

    lax.fori_loop(1, nblk // 2, pair, 0)
    values(nblk - 2, 0)
    softmax(1)
    values(nblk - 1, 1)


def _rope_tables(seq):
    pos = np.arange(seq, dtype=np.float64)
    inv_freq = np.power(ROPE_THETA, -np.arange(0, ROPE_DIM, 2, dtype=np.float64) / ROPE_DIM)
    ang = pos[:, None] * inv_freq[None, :]
    half = ROPE_DIM // 2
    cos_f = np.ones((seq, LANES), np.float32)
    sin_a = np.zeros((seq, LANES), np.float32)
    sin_b = np.zeros((seq, LANES), np.float32)
    for base in range(0, LANES, ATT_HEAD_DIM):
        cos_f[:, base:base + half] = np.cos(ang)
        cos_f[:, base + half:base + ROPE_DIM] = np.cos(ang)
        sin_a[:, base:base + half] = -np.sin(ang)
        sin_b[:, base + half:base + ROPE_DIM] = np.sin(ang)
    return jnp.asarray(cos_f), jnp.asarray(sin_a), jnp.asarray(sin_b)


def _diff_attention(qkv3, lq1, lk1, lq2, lk2, subln_g, lam_init, tq=256):
    B, S, _ = qkv3.shape
    nb = ATT_WIDTH // LANES
    assert S % (2 * tq) == 0
    spec = lambda part: pl.BlockSpec((None, S, LANES), lambda b, h: (b, 0, part * nb + h))
    vec = pl.BlockSpec((1, ATT_HEAD_DIM), lambda b, h: (0, 0))
    r1 = lambda a: a.reshape(1, -1)
    return pl.pallas_call(
        functools.partial(_attn_kernel, lam_init=lam_init, tq=tq),
        grid=(B, ATT_HEADS),
        in_specs=[spec(0), spec(1), spec(2), vec, vec, vec, vec,
                  pl.BlockSpec((1, LANES), lambda b, h: (0, 0))],
        out_specs=pl.BlockSpec((None, S, LANES), lambda b, h: (b, 0, h)),
        out_shape=jax.ShapeDtypeStruct((B, S, ATT_WIDTH), F32),
        scratch_shapes=[pltpu.VMEM((2, 2, tq, S), F32), pltpu.VMEM((2, tq, S), BF16),
                        pltpu.VMEM((2, tq, 1), F32)],
        compiler_params=_cparams(("parallel", "parallel")),
        name="diff_attn",
    )(qkv3, qkv3, qkv3, r1(lq1), r1(lk1), r1(lq2), r1(lk2), r1(subln_g))


def _pool_kernel(u_ref, w_ref, b_ref, sc_ref, o_ref):
    ct = pl.program_id(1)
    x = u_ref[...]
    S = x.shape[0]
    row = lax.broadcasted_iota(jnp.int32, x.shape, 0)
    lane = lax.broadcasted_iota(jnp.int32, x.shape, 1)

    def shifted(d):
        if d == 0:
            return x
        r = pltpu.roll(x, (-d) % S, 0)
        return jnp.where((row + d >= 0) & (row + d < S), r, 0.0)

    sums = {}
    acc = shifted(-1) + x
    sums[2] = acc
    lo, hi = -1, 0
    for w in POOL_WINDOWS[1:]:
        for d in list(range(-(w // 2), lo)) + list(range(hi + 1, w // 2)):
            acc = acc + shifted(d)
        lo, hi = -(w // 2), w // 2 - 1
        sums[w] = acc
    grp = 2 * ct + (lane >= POOL_GROUP).astype(jnp.int32)
    win_sum = sums[POOL_WINDOWS[-1]]
    half = jnp.full(x.shape, POOL_WINDOWS[-1] // 2, jnp.int32)
    for g in range(len(POOL_WINDOWS) - 2, -1, -1):
        win_sum = jnp.where(grp == g, sums[POOL_WINDOWS[g]], win_sum)
        half = jnp.where(grp == g, POOL_WINDOWS[g] // 2, half)
    cnt = jnp.minimum(row + half - 1, S - 1) - jnp.maximum(row - half, 0) + 1
    y = win_sum / cnt.astype(F32) - x
    y = _dot3(*_split(y), *_split(w_ref[...])) + b_ref[...]
    o_ref[...] = y * sc_ref[...]


def _pool_mixer(u3, w, b, scale):
    B, S, _ = u3.shape
    nct = POOL_WIDTH // LANES
    gpt = LANES // POOL_GROUP
    wbd = jnp.zeros((nct, LANES, LANES), F32)
    for g in range(len(POOL_WINDOWS)):
        t, o = divmod(g, gpt)
        wbd = wbd.at[t, o * POOL_GROUP:(o + 1) * POOL_GROUP, o * POOL_GROUP:(o + 1) * POOL_GROUP].set(w[g])
    vspec = pl.BlockSpec((1, LANES), lambda bb, c: (0, c))
    return pl.pallas_call(
        _pool_kernel,
        grid=(B, nct),
        in_specs=[pl.BlockSpec((None, S, LANES), lambda bb, c: (bb, 0, c)),
                  pl.BlockSpec((None, LANES, LANES), lambda bb, c: (c, 0, 0)),
                  vspec, vspec],
        out_specs=pl.BlockSpec((None, S, LANES), lambda bb, c: (bb, 0, c)),
        out_shape=jax.ShapeDtypeStruct((B, S, POOL_WIDTH), F32),
        compiler_params=_cparams(("parallel", "parallel")),
        name="pool_mixer",
    )(u3, wbd, b.reshape(1, POOL_WIDTH), scale.reshape(1, POOL_WIDTH))


def _outproj_kernel(h_ref, yh_ref, ya_ref, yp_ref, wh_ref, wa_ref, wp_ref, g_ref, b_ref,
                    wgh_ref, wgl_ref, bg_ref, o_ref, gate_ref):
    d = functools.partial(jnp.dot, preferred_element_type=F32)
    mix = (d(yh_ref[...].astype(BF16), wh_ref[...]) + d(ya_ref[...].astype(BF16), wa_ref[...])
           + d(yp_ref[...].astype(BF16), wp_ref[...]))
    h1 = _ln_rows(DN_ALPHA * h_ref[...] + mix, g_ref[...], b_ref[...])
    o_ref[...] = h1

    logit = _dot3(*_split(h1), wgh_ref[...], wgl_ref[...]) + bg_ref[...]
    lni = lax.broadcasted_iota(jnp.int32, logit.shape, 1)
    ln = lni.astype(F32)
    grp = lax.shift_right_arithmetic(lni - GATE_FINE_LANE, 2).astype(F32)
    first = lambda mask: jnp.min(jnp.where(mask, ln, float(LANES)), axis=-1, keepdims=True)
    cmask = lni < N_GROUPS
    lc = jnp.where(cmask, logit, NEG_BIG)
    mc = jnp.max(lc, axis=-1, keepdims=True)
    gw = 1.0 / jnp.sum(jnp.where(cmask, jnp.exp(lc - mc), 0.0), axis=-1, keepdims=True)
    gi = first(cmask & (lc == mc))
    fmask = (lni >= GATE_FINE_LANE) & (lni < GATE_FINE_LANE + N_EXPERTS) & (grp == gi)
    lf = jnp.where(fmask, logit, NEG_BIG)
    m1 = jnp.max(lf, axis=-1, keepdims=True)
    i1 = first(fmask & (lf == m1))
    rest = fmask & (ln != i1)
    lf2 = jnp.where(rest, logit, NEG_BIG)
    m2 = jnp.max(lf2, axis=-1, keepdims=True)
    i2 = first(rest & (lf2 == m2))
    e2 = jnp.exp(m2 - m1)
    w1 = gw / (1.0 + e2)
    w2 = gw * e2 / (1.0 + e2)
    gate_ref[...] = jnp.where(ln == i1, w1, 0.0) + jnp.where(ln == i2, w2, 0.0)


def _out_proj_ln_gate(h2, yh, ya, yp, w_out_bf, g, b, wgc, bgc, wgf, bgf, tm=512):
    T, D = h2.shape
    wg = jnp.zeros((D, LANES), F32).at[:, :N_GROUPS].set(wgc)
    wg = wg.at[:, GATE_FINE_LANE:GATE_FINE_LANE + N_EXPERTS].set(wgf)
    bg = jnp.zeros((1, LANES), F32).at[0, :N_GROUPS].set(bgc)
    bg = bg.at[0, GATE_FINE_LANE:GATE_FINE_LANE + N_EXPERTS].set(bgf)
    wgh, wgl = _split(wg)
    o1, o2 = HY_WIDTH, HY_WIDTH + ATT_WIDTH
    row = lambda c: pl.BlockSpec((tm, c), lambda i: (i, 0))
    full = lambda r, c: pl.BlockSpec((r, c), lambda i: (0, 0))
    return pl.pallas_call(
        _outproj_kernel,
        grid=(T // tm,),
        in_specs=[row(D), row(HY_WIDTH), row(ATT_WIDTH), row(POOL_WIDTH),
                  full(HY_WIDTH, D), full(ATT_WIDTH, D), full(POOL_WIDTH, D),
                  full(1, D), full(1, D), full(D, LANES), full(D, LANES), full(1, LANES)],
        out_specs=[row(D), row(LANES)],
        out_shape=[jax.ShapeDtypeStruct((T, D), F32), jax.ShapeDtypeStruct((T, LANES), F32)],
        compiler_params=_cparams(("parallel",)),
        name="out_proj_ln_gate",
    )(h2, yh, ya, yp, w_out_bf[:o1], w_out_bf[o1:o2], w_out_bf[o2:], g.reshape(1, D), b.reshape(1, D),
      wgh, wgl, bg)


def _moe_kernel(h_ref, gate_ref, w1_ref, w3_ref, w2_ref, p_ref, pwg_ref, pbg_ref, pwp_ref,
                g_ref, b_ref, o_ref, acc_scr, hb_scr):
    e = pl.program_id(1)
    d = functools.partial(jnp.dot, preferred_element_type=F32)

    @pl.when(e == 0)
    def _():
        hb_scr[...] = h_ref[...].astype(BF16)
        acc_scr[...] = jnp.zeros_like(acc_scr)

    hb = hb_scr[...]
    a = d(hb, w1_ref[...])
    c = d(hb, w3_ref[...])
    gates = gate_ref[...]
    ln = lax.broadcasted_iota(jnp.int32, gates.shape, 1)
    ge = jnp.sum(jnp.where(ln == GATE_FINE_LANE + e, gates, 0.0), axis=-1, keepdims=True)
    act = a * jax.nn.sigmoid(a) * c * ge
    acc_scr[...] += d(act.astype(BF16), w2_ref[...])

    @pl.when(e == pl.num_programs(1) - 1)
    def _():
        z = d(hb, pwg_ref[...]) + pbg_ref[...]
        y_ple = jax.nn.sigmoid(z) * d(p_ref[...].astype(BF16), pwp_ref[...])
        r = DN_ALPHA * h_ref[...] + acc_scr[...] + y_ple
        o_ref[...] = _ln_rows(r, g_ref[...], b_ref[...])


def _moe_ple_ln(h2, gates, w1_bf, w3_bf, w2_bf, p2, pwg_bf, pbg, pwp_bf, g, b, tm=512):
    T, D = h2.shape
    E = w1_bf.shape[0]
    row = lambda c: pl.BlockSpec((tm, c), lambda i, e: (i, 0))
    full = lambda r, c: pl.BlockSpec((r, c), lambda i, e: (0, 0))
    return pl.pallas_call(
        _moe_kernel,
        grid=(T // tm, E),
        in_specs=[row(D), row(LANES),
                  pl.BlockSpec((None, D, D_EXPERT), lambda i, e: (e, 0, 0)),
                  pl.BlockSpec((None, D, D_EXPERT), lambda i, e: (e, 0, 0)),
                  pl.BlockSpec((None, D_EXPERT, D), lambda i, e: (e, 0, 0)),
                  row(PLE_DIM), full(D, D), full(1, D), full(PLE_DIM, D), full(1, D), full(1, D)],
        out_specs=row(D),
        out_shape=jax.ShapeDtypeStruct((T, D), F32),
        scratch_shapes=[pltpu.VMEM((tm, D), F32), pltpu.VMEM((tm, D), BF16)],
        compiler_params=_cparams(("parallel", "arbitrary")),
        name="moe_ple_ln",
    )(h2, gates, w1_bf, w3_bf, w2_bf, p2, pwg_bf, pbg.reshape(1, D), pwp_bf, g.reshape(1, D), b.reshape(1, D))


def _hyena_mixer(u3, kr, ki, consts, conv_w, conv_b, bias_d, n1_total):
    B, S, _ = u3.shape
    vv, x0c = _hy_gate(u3, conv_w, conv_b)
    y = _hy_conv(vv, x0c, kr, ki, consts, bias_d, n1_total)
    return y.reshape(B * S, HY_WIDTH)


def kernel(x, p, ln0_g, ln0_b, w_in, hy_conv_w, hy_conv_b, hy_fw1, hy_fb1, hy_freq1, hy_fw2, hy_fb2, hy_freq2, hy_fw3, hy_bias, att_lq1, att_lk1, att_lq2, att_lk2, att_subln_g, pool_w, pool_b, pool_scale, w_out, ln1_g, ln1_b, moe_wgc, moe_bgc, moe_wgf, moe_bgf, moe_w1, moe_w3, moe_w2, ple_wg, ple_bg, ple_wp, ln2_g, ln2_b):
    B, S, D = x.shape
    L = w_in.shape[0]
    T = B * S
    n1_total = 2 * S // FFT_N2
    C = HY_WIDTH

    kfilt, ssq = _hy_filter(S, hy_fw1, hy_fb1, hy_freq1, hy_fw2, hy_fb2, hy_freq2, hy_fw3)
    kar, kai = _stage_a(kfilt.reshape(L, n1_total, FFT_N2 * C), n1_total)
    k1p = kar.shape[1]
    consts = _stage_b_consts(n1_total)
    kr_all, ki_all = _stage_b_filter(kar.reshape(L, k1p, FFT_N2, C), kai.reshape(L, k1p, FFT_N2, C),
                                     ssq, consts, n1_total)
    tables = _rope_tables(S)

    h = _layer_norm(x.reshape(T, D), ln0_g, ln0_b)
    for i in range(L):
        lam_init = 0.8 - 0.6 * math.exp(-0.3 * i)
        uh, qkv, up = _in_proj(h, w_in[i].astype(BF16), tables, S)
        y_hy = _hyena_mixer(uh.reshape(B, S, 3 * C), kr_all[i], ki_all[i], consts,
                            hy_conv_w[i], hy_conv_b[i], hy_bias[i], n1_total)
        y_att = _diff_attention(qkv.reshape(B, S, 3 * ATT_WIDTH), att_lq1[i], att_lk1[i], att_lq2[i],
                                att_lk2[i], att_subln_g[i], lam_init).reshape(T, ATT_WIDTH)
        y_pool = _pool_mixer(up.reshape(B, S, POOL_WIDTH), pool_w[i], pool_b[i],
                             pool_scale[i]).reshape(T, POOL_WIDTH)
        h, gates = _out_proj_ln_gate(h, y_hy, y_att, y_pool, w_out[i].astype(BF16), ln1_g[i], ln1_b[i],
                                     moe_wgc[i], moe_bgc[i], moe_wgf[i], moe_bgf[i])
        h = _moe_ple_ln(h, gates, moe_w1[i].astype(BF16), moe_w3[i].astype(BF16), moe_w2[i].astype(BF16),
                        p[i].reshape(T, PLE_DIM), ple_wg[i].astype(BF16), ple_bg[i], ple_wp[i].astype(BF16),
                        ln2_g[i], ln2_b[i])
    return h.reshape(B, S, D)
```

```python
import functools
import math

import numpy as np
import jax
import jax.numpy as jnp
from jax import lax
from jax.experimental import pallas as pl
from jax.experimental.pallas import tpu as pltpu

F32 = jnp.float32
BF16 = jnp.bfloat16

D_MODEL = 1024
DEPTH = 4
HY_WIDTH = 256
ATT_WIDTH = 512
ATT_HEADS = 4
ATT_HEAD_DIM = 64
POOL_WINDOWS = (2, 4, 8, 16)
POOL_WIDTH = 256
POOL_GROUP = 64
IN_WIDTH = 3 * HY_WIDTH + 3 * ATT_WIDTH + POOL_WIDTH
ROPE_THETA = 500000.0
ROPE_DIM = ATT_HEAD_DIM // 4
HY_EMB = 33
HY_BANDS = (HY_EMB - 1) // 2
HY_FILTER_HIDDEN = 64
HY_DECAY_TARGET = 1e-2
HY_FAST_DECAY = 0.3
HY_SLOW_DECAY = 1.5
N_GROUPS = 4
EXPERTS_PER_GROUP = 4
N_EXPERTS = 16
D_EXPERT = 256
PLE_DIM = 256
LN_EPS = 1e-5
RMS_EPS = 1e-5
DN_ALPHA = (2 * DEPTH) ** 0.25

LANES = 128
FFT_N2 = 128
GATE_COARSE_LANE = 0
GATE_FINE_LANE = N_GROUPS
NEG_BIG = -1e30
LOG2E = 1.4426950408889634
VMEM_LIMIT = 56 * 1024 * 1024


def _cparams(sem):
    return pltpu.CompilerParams(dimension_semantics=sem, vmem_limit_bytes=VMEM_LIMIT)


def _split(x):
    hi = x.astype(BF16)
    lo = (x - hi.astype(F32)).astype(BF16)
    return hi, lo


def _dot3(ah, al, bh, bl):
    d = functools.partial(jnp.dot, preferred_element_type=F32)
    return d(ah, bh) + (d(ah, bl) + d(al, bh))


def _ln_rows(x, g, b):
    mu = jnp.mean(x, axis=-1, keepdims=True)
    xc = x - mu
    var = jnp.mean(xc * xc, axis=-1, keepdims=True)
    return xc * lax.rsqrt(var + LN_EPS) * g + b


def _ln_kernel(x_ref, g_ref, b_ref, o_ref):
    o_ref[...] = _ln_rows(x_ref[...], g_ref[...], b_ref[...])


def _layer_norm(x2, g, b, tm=512):
    T, D = x2.shape
    return pl.pallas_call(
        _ln_kernel,
        grid=(T // tm,),
        in_specs=[pl.BlockSpec((tm, D), lambda i: (i, 0)),
                  pl.BlockSpec((1, D), lambda i: (0, 0)),
                  pl.BlockSpec((1, D), lambda i: (0, 0))],
        out_specs=pl.BlockSpec((tm, D), lambda i: (i, 0)),
        out_shape=jax.ShapeDtypeStruct((T, D), F32),
        compiler_params=_cparams(("parallel",)),
        name="ln0",
    )(x2, g.reshape(1, D), b.reshape(1, D))


def _rope(x, cos_f, sin_a, sin_b):
    half = ROPE_DIM // 2
    return x * cos_f + pltpu.roll(x, LANES - half, 1) * sin_a + pltpu.roll(x, half, 1) * sin_b


def _inproj_kernel(h_ref, w_ref, cos_ref, sa_ref, sb_ref, uh_ref, qkv_ref, up_ref):
    hb = h_ref[...].astype(BF16)
    mm = lambda c0, c1: jnp.dot(hb, w_ref[:, c0:c1], preferred_element_type=F32)
    o_q = 3 * HY_WIDTH
    o_k = o_q + ATT_WIDTH
    o_v = o_k + ATT_WIDTH
    o_p = o_v + ATT_WIDTH
    uh_ref[...] = mm(0, o_q)
    up_ref[...] = mm(o_p, IN_WIDTH)
    cos_f, sin_a, sin_b = cos_ref[...], sa_ref[...], sb_ref[...]
    q = mm(o_q, o_k)
    k = mm(o_k, o_v)
    qscale = ATT_HEAD_DIM ** -0.5 * LOG2E
    for hd in range(ATT_HEADS):
        sl = slice(hd * LANES, (hd + 1) * LANES)
        qkv_ref[:, sl] = (_rope(q[:, sl], cos_f, sin_a, sin_b) * qscale).astype(BF16)
        qkv_ref[:, ATT_WIDTH + hd * LANES:ATT_WIDTH + (hd + 1) * LANES] = \
            _rope(k[:, sl], cos_f, sin_a, sin_b).astype(BF16)
    qkv_ref[:, 2 * ATT_WIDTH:] = mm(o_v, o_p).astype(BF16)


def _in_proj(h2, w_bf, tables, seq, tm=512):
    T, D = h2.shape
    N = w_bf.shape[1]
    spt = seq // tm
    tspec = pl.BlockSpec((tm, LANES), lambda i: (i % spt, 0))
    row = lambda c: pl.BlockSpec((tm, c), lambda i: (i, 0))
    return pl.pallas_call(
        _inproj_kernel,
        grid=(T // tm,),
        in_specs=[row(D), pl.BlockSpec((D, N), lambda i: (0, 0)), tspec, tspec, tspec],
        out_specs=[row(3 * HY_WIDTH), row(3 * ATT_WIDTH), row(POOL_WIDTH)],
        out_shape=[jax.ShapeDtypeStruct((T, 3 * HY_WIDTH), F32),
                   jax.ShapeDtypeStruct((T, 3 * ATT_WIDTH), BF16),
                   jax.ShapeDtypeStruct((T, POOL_WIDTH), F32)],
        compiler_params=_cparams(("parallel",)),
        name="in_proj",
    )(h2, w_bf, *tables)


def _hy_filter_kernel(band_ref, fw1_ref, fb1_ref, fr1_ref, fw2_ref, fb2_ref, fr2_ref, fw3_ref,
                      dl_ref, k_ref, ssq_ref, *, seq, tr):
    j = pl.program_id(1)
    n = j * tr + lax.broadcasted_iota(jnp.int32, (tr, 1), 0)
    pos = jnp.where(n < seq, n, 2 * seq - n).astype(F32)
    t = pos * (1.0 / (seq - 1))
    wpos = (2.0 * math.pi / seq) * pos
    lane = lax.broadcasted_iota(jnp.int32, (tr, LANES), 1)
    arg = wpos * band_ref[...]
    z = jnp.where(lane == 0, t,
                  jnp.where(lane <= HY_BANDS, jnp.cos(arg),
                            jnp.where(lane < HY_EMB, -jnp.sin(arg), 0.0)))
    h1 = _dot3(*_split(z), *_split(fw1_ref[...])) + fb1_ref[...]
    h1 = jnp.sin(fr1_ref[...] * h1)
    h2 = _dot3(*_split(h1), *_split(fw2_ref[...])) + fb2_ref[...]
    h2 = jnp.sin(fr2_ref[...] * h2)
    filt = _dot3(*_split(h2), *_split(fw3_ref[...]))
    window = jnp.exp(-t * jnp.abs(dl_ref[...]))
    kk = jnp.where(n < seq, filt[:, :HY_WIDTH], filt[:, HY_WIDTH:]) * window
    kk = jnp.where(n == seq, 0.0, kk)
    k_ref[...] = kk

    @pl.when(j == 0)
    def _():
        ssq_ref[...] = jnp.zeros_like(ssq_ref)

    ssq_ref[...] += jnp.sum(kk * kk, axis=0, keepdims=True)


def _hy_filter(seq, fw1, fb1, freq1, fw2, fb2, freq2, fw3, tr=1024):
    L = fw1.shape[0]
    n = 2 * seq
    H = HY_FILTER_HIDDEN
    bands = np.linspace(1e-4, HY_BANDS - 1, HY_BANDS)
    bandv = np.zeros((1, LANES), np.float32)
    bandv[0, 1:1 + HY_BANDS] = bands
    bandv[0, 1 + HY_BANDS:HY_EMB] = bands
    fw1p = jnp.zeros((L, LANES, H), F32).at[:, :HY_EMB].set(fw1)
    max_decay = math.log(HY_DECAY_TARGET) / HY_FAST_DECAY
    min_decay = math.log(HY_DECAY_TARGET) / HY_SLOW_DECAY
    deltas = np.linspace(min_decay, max_decay, HY_WIDTH).astype(np.float32).reshape(1, HY_WIDTH)
    vec = lambda a: a.reshape(L, 1, a.shape[-1])
    lspec = lambda r, c: pl.BlockSpec((None, r, c), lambda l, j: (l, 0, 0))
    return pl.pallas_call(
        functools.partial(_hy_filter_kernel, seq=seq, tr=tr),
        grid=(L, n // tr),
        in_specs=[pl.BlockSpec((1, LANES), lambda l, j: (0, 0)),
                  lspec(LANES, H), lspec(1, H), lspec(1, H),
                  lspec(H, H), lspec(1, H), lspec(1, H),
                  lspec(H, 2 * HY_WIDTH),
                  pl.BlockSpec((1, HY_WIDTH), lambda l, j: (0, 0))],
        out_specs=[pl.BlockSpec((None, tr, HY_WIDTH), lambda l, j: (l, j, 0)),
                   pl.BlockSpec((None, 1, HY_WIDTH), lambda l, j: (l, 0, 0))],
        out_shape=[jax.ShapeDtypeStruct((L, n, HY_WIDTH), F32),
                   jax.ShapeDtypeStruct((L, 1, HY_WIDTH), F32)],
        compiler_params=_cparams(("parallel", "arbitrary")),
        name="hy_filter",
    )(jnp.asarray(bandv), fw1p, vec(fb1), vec(freq1), fw2, vec(fb2), vec(freq2), fw3,
      jnp.asarray(deltas))


def _k1_rows(n1):
    k1 = n1 // 2 + 1
    return k1, -(-k1 // 8) * 8


def _bf_pair(m):
    m32 = jnp.asarray(np.asarray(m, np.float32))
    return _split(m32)


def _stage_a_kernel(x_ref, frh_ref, frl_ref, fih_ref, fil_ref, ar_ref, ai_ref):
    xh, xl = _split(x_ref[...])
    ar_ref[...] = _dot3(frh_ref[...], frl_ref[...], xh, xl)
    ai_ref[...] = _dot3(fih_ref[...], fil_ref[...], xh, xl)


def _stage_a(x3, n1_total, tl=4096):
    Bt, n1u, LN = x3.shape
    k1n, k1p = _k1_rows(n1_total)
    ang = 2.0 * np.pi * np.outer(np.arange(k1p), np.arange(n1u)) / n1_total
    valid = (np.arange(k1p) < k1n)[:, None]
    frh, frl = _bf_pair(np.cos(ang) * valid)
    fih, fil = _bf_pair(-np.sin(ang) * valid)
    cspec = pl.BlockSpec((k1p, n1u), lambda b, j: (0, 0))
    ospec = pl.BlockSpec((None, k1p, tl), lambda b, j: (b, 0, j))
    return pl.pallas_call(
        _stage_a_kernel,
        grid=(Bt, LN // tl),
        in_specs=[pl.BlockSpec((None, n1u, tl), lambda b, j: (b, 0, j)), cspec, cspec, cspec, cspec],
        out_specs=[ospec, ospec],
        out_shape=[jax.ShapeDtypeStruct((Bt, k1p, LN), F32)] * 2,
        compiler_params=_cparams(("parallel", "parallel")),
        name="hy_stage_a",
    )(x3, frh, frl, fih, fil)


def _stage_b_consts(n1_total):
    n = n1_total * FFT_N2
    k1n, k1p = _k1_rows(n1_total)
    ang2 = 2.0 * np.pi * np.outer(np.arange(FFT_N2), np.arange(FFT_N2)) / FFT_N2
    c2, s2 = np.cos(ang2), np.sin(ang2)
    mf = np.block([[c2, s2], [-s2, c2]])
    mi = np.block([[c2, -s2], [s2, c2]])
    angt = 2.0 * np.pi * np.outer(np.arange(k1p), np.arange(FFT_N2)) / n
    twc = jnp.asarray(np.cos(angt).astype(np.float32)).reshape(k1p, FFT_N2, 1)
    tws = jnp.asarray(np.sin(angt).astype(np.float32)).reshape(k1p, FFT_N2, 1)
    return _bf_pair(mf), _bf_pair(mi), twc, tws


def _fwd_b(ar, ai, c, s, mfh, mfl):
    tr_ = ar * c + ai * s
    ti_ = ai * c - ar * s
    xh, xl = _split(jnp.concatenate([tr_, ti_], axis=0))
    z = _dot3(mfh, mfl, xh, xl)
    return z[:FFT_N2], z[FFT_N2:]


def _stage_bk_kernel(ar_ref, ai_ref, c_ref, s_ref, mfh_ref, mfl_ref, ssq_ref, kr_ref, ki_ref,
                     *, k1n, inv_n):
    k1 = pl.program_id(1)

    @pl.when(k1 < k1n)
    def _():
        zr, zi = _fwd_b(ar_ref[...], ai_ref[...], c_ref[...], s_ref[...], mfh_ref[...], mfl_ref[...])
        scale = lax.rsqrt(ssq_ref[...] + 1e-6) * inv_n
        kr_ref[...] = zr * scale
        ki_ref[...] = zi * scale

    @pl.when(k1 >= k1n)
    def _():
        kr_ref[...] = jnp.zeros_like(kr_ref)
        ki_ref[...] = jnp.zeros_like(ki_ref)


def _stage_b_filter(ar4, ai4, ssq, consts, n1_total):
    L, k1p, _, C = ar4.shape
    k1n, _ = _k1_rows(n1_total)
    (mfh, mfl), _, twc, tws = consts
    blk = pl.BlockSpec((None, None, FFT_N2, C), lambda l, k: (l, k, 0, 0))
    tw = pl.BlockSpec((None, FFT_N2, 1), lambda l, k: (k, 0, 0))
    mat = pl.BlockSpec((2 * FFT_N2, 2 * FFT_N2), lambda l, k: (0, 0))
    return pl.pallas_call(
        functools.partial(_stage_bk_kernel, k1n=k1n, inv_n=1.0 / (n1_total * FFT_N2)),
        grid=(L, k1p),
        in_specs=[blk, blk, tw, tw, mat, mat, pl.BlockSpec((None, 1, C), lambda l, k: (l, 0, 0))],
        out_specs=[blk, blk],
        out_shape=[jax.ShapeDtypeStruct(ar4.shape, F32)] * 2,
        compiler_params=_cparams(("parallel", "parallel")),
        name="hy_stage_b_filter",
    )(ar4, ai4, twc, tws, mfh, mfl, ssq)


SUBLANES = 8


def _hy_conv_kernel(vv_ref, x0_ref, kr_ref, ki_ref, tw_ref, kah_ref, kal_ref, kch_ref, kcl_ref,
                    mfh_ref, mfl_ref, mih_ref, mil_ref, bias_ref, o_ref, a_scr, *, n1u, k1n, k1p):
    groups = FFT_N2 // SUBLANES

    def rows(n1, g):
        return pl.ds(pl.multiple_of(n1 * FFT_N2 + g * SUBLANES, SUBLANES), SUBLANES)

    def stage_a(g, carry):
        xg = jnp.concatenate([vv_ref[rows(n1, g), :] for n1 in range(n1u)], axis=0)
        ag = _dot3(kah_ref[...], kal_ref[...], *_split(xg))
        gs = pl.ds(pl.multiple_of(g * SUBLANES, SUBLANES), SUBLANES)
        for part in range(2):
            for k1 in range(k1p):
                r = (part * k1p + k1) * SUBLANES
                a_scr[part, k1, gs, :] = ag[r:r + SUBLANES]
        return carry

    lax.fori_loop(0, groups, stage_a, 0)

    def stage_b(k1, carry):
        tw = tw_ref[k1]
        c = tw[:, 0:1]
        s = tw[:, 1:2]
        zr, zi = _fwd_b(a_scr[0, k1], a_scr[1, k1], c, s, mfh_ref[...], mfl_ref[...])
        kr = kr_ref[k1]
        ki = ki_ref[k1]
        pr = zr * kr - zi * ki
        pi = zr * ki + zi * kr
        y = _dot3(mih_ref[...], mil_ref[...], *_split(jnp.concatenate([pr, pi], axis=0)))
        yr = y[:FFT_N2]
        yi = y[FFT_N2:]
        a_scr[0, k1] = yr * c - yi * s
        a_scr[1, k1] = yi * c + yr * s
        return carry

    lax.fori_loop(0, k1n, stage_b, 0)

    def stage_c(g, carry):
        gs = pl.ds(pl.multiple_of(g * SUBLANES, SUBLANES), SUBLANES)
        bg = jnp.concatenate([a_scr[part, k1, gs, :] for part in range(2) for k1 in range(k1p)], axis=0)
        yg = _dot3(kch_ref[...], kcl_ref[...], *_split(bg))
        for n1 in range(n1u):
            r = rows(n1, g)
            o_ref[r, :] = (yg[n1 * SUBLANES:(n1 + 1) * SUBLANES] + vv_ref[r, :] * bias_ref[...]) * x0_ref[r, :]
        return carry

    lax.fori_loop(0, groups, stage_c, 0)


def _hy_conv(vv, x0c, kr3, ki3, consts, bias_d, n1_total):
    B, S, C = vv.shape
    n1u = S // FFT_N2
    k1n, k1p = _k1_rows(n1_total)
    (mfh, mfl), (mih, mil), twc, tws = consts
    tw = jnp.concatenate([twc, tws], axis=-1)
    k1 = np.arange(k1p)
    valid = (k1 < k1n)[:, None]
    eye = np.eye(SUBLANES)
    ang_a = 2.0 * np.pi * np.outer(k1, np.arange(n1u)) / n1_total
    ka = np.concatenate([np.kron(np.cos(ang_a) * valid, eye), np.kron(-np.sin(ang_a) * valid, eye)], axis=0)
    w = np.where((k1 == 0) | (k1 == n1_total // 2), 1.0, 2.0) * (k1 < k1n)
    ang_c = 2.0 * np.pi * np.outer(np.arange(n1u), k1) / n1_total
    kc = np.concatenate([np.kron(np.cos(ang_c) * w[None, :], eye), np.kron(-np.sin(ang_c) * w[None, :], eye)], axis=1)
    kah, kal = _bf_pair(ka)
    kch, kcl = _bf_pair(kc)
    nct = C // LANES
    xspec = pl.BlockSpec((None, S, LANES), lambda c, b: (b, 0, c))
    kspec = pl.BlockSpec((k1p, FFT_N2, LANES), lambda c, b: (0, 0, c))
    full = lambda a: pl.BlockSpec(a.shape, lambda c, b: (0,) * a.ndim)
    return pl.pallas_call(
        functools.partial(_hy_conv_kernel, n1u=n1u, k1n=k1n, k1p=k1p),
        grid=(nct, B),
        in_specs=[xspec, xspec, kspec, kspec, full(tw), full(kah), full(kal), full(kch), full(kcl),
                  full(mfh), full(mfl), full(mih), full(mil),
                  pl.BlockSpec((1, LANES), lambda c, b: (0, c))],
        out_specs=xspec,
        out_shape=jax.ShapeDtypeStruct((B, S, C), F32),
        scratch_shapes=[pltpu.VMEM((2, k1p, FFT_N2, LANES), F32)],
        compiler_params=_cparams(("parallel", "parallel")),
        name="hy_conv",
    )(vv, x0c, kr3, ki3, tw, kah, kal, kch, kcl, mfh, mfl, mih, mil, bias_d.reshape(1, C))


def _conv3(x, w_ref, b_ref):
    S = x.shape[0]
    row = lax.broadcasted_iota(jnp.int32, x.shape, 0)
    prev = jnp.where(row == 0, 0.0, pltpu.roll(x, 1, 0))
    nxt = jnp.where(row == S - 1, 0.0, pltpu.roll(x, S - 1, 0))
    return prev * w_ref[0:1, :] + x * w_ref[1:2, :] + nxt * w_ref[2:3, :] + b_ref[...]


def _hy_gate_kernel(x0_ref, x1_ref, v_ref, w0_ref, w1_ref, w2_ref, b0_ref, b1_ref, b2_ref,
                    vv_ref, x0c_ref):
    x0c_ref[...] = _conv3(x0_ref[...], w0_ref, b0_ref)
    vv_ref[...] = _conv3(v_ref[...], w2_ref, b2_ref) * _conv3(x1_ref[...], w1_ref, b1_ref)


def _hy_gate(u3, conv_w, conv_b):
    B, S, _ = u3.shape
    nct = HY_WIDTH // LANES
    conv_b2 = conv_b.reshape(1, 3 * HY_WIDTH)
    uspec = lambda part: pl.BlockSpec((None, S, LANES), lambda b, c: (b, 0, part * nct + c))
    wspec = lambda part: pl.BlockSpec((3, LANES), lambda b, c: (0, part * nct + c))
    bspec = lambda part: pl.BlockSpec((1, LANES), lambda b, c: (0, part * nct + c))
    ospec = pl.BlockSpec((None, S, LANES), lambda b, c: (b, 0, c))
    return pl.pallas_call(
        _hy_gate_kernel,
        grid=(B, nct),
        in_specs=[uspec(0), uspec(1), uspec(2), wspec(0), wspec(1), wspec(2),
                  bspec(0), bspec(1), bspec(2)],
        out_specs=[ospec, ospec],
        out_shape=[jax.ShapeDtypeStruct((B, S, HY_WIDTH), F32)] * 2,
        compiler_params=_cparams(("parallel", "parallel")),
        name="hy_gate",
    )(u3, u3, u3, conv_w, conv_w, conv_w, conv_b2, conv_b2, conv_b2)


def _attn_kernel(q_ref, k_ref, v_ref, lq1_ref, lk1_ref, lq2_ref, lk2_ref, g_ref, o_ref,
                 s_scr, a_scr, inv_scr, *, lam_init, tq):
    S = q_ref.shape[0]
    nblk = S // tq
    lam = (jnp.exp(jnp.sum(lq1_ref[...] * lk1_ref[...], keepdims=True))
           - jnp.exp(jnp.sum(lq2_ref[...] * lk2_ref[...], keepdims=True)) + lam_init)
    lane = lax.broadcasted_iota(jnp.int32, (tq, LANES), 1)

    def blk(i):
        return pl.ds(pl.multiple_of(i * tq, tq), tq)

    def scores(i, slot):
        q = q_ref[blk(i), :]
        for c in range(2):
            sel = (lane < ATT_HEAD_DIM) if c == 0 else (lane >= ATT_HEAD_DIM)
            qc = jnp.where(sel, q, jnp.zeros_like(q))
            s_scr[slot, c] = lax.dot_general(qc, k_ref[...], (((1,), (1,)), ((), ())),
                                             preferred_element_type=F32)

    def softmax(slot):
        pl_ = []
        for c in range(2):
            s = s_scr[slot, c]
            m = jnp.max(s, axis=-1, keepdims=True)
            p = jnp.exp2(s - m)
            pl_.append((p.astype(BF16), jnp.sum(p, axis=-1, keepdims=True)))
        (p1, l1), (p2, l2) = pl_
        a_scr[slot] = p1 - (lam * l1 / l2).astype(BF16) * p2
        inv_scr[slot] = 1.0 / l1

    def values(i, slot):
        o = jnp.dot(a_scr[slot], v_ref[...], preferred_element_type=F32) * inv_scr[slot]
        o = o * lax.rsqrt(jnp.mean(o * o, axis=-1, keepdims=True) + RMS_EPS) * g_ref[...]
        o_ref[blk(i), :] = o * (1.0 - lam_init)

    scores(0, 0)
    softmax(0)
    scores(1, 1)

    def pair(j, carry):
        t = 2 * j
        values(t - 2, 0)
        softmax(1)
        scores(t, 0)
        values(t - 1, 1)
        softmax(0)
        scores(t + 1, 1)
        return carry

    lax.fori_loop(1, nblk // 2, pair, 0)
    values(nblk - 2, 0)
    softmax(1)
    values(nblk - 1, 1)


def _rope_tables(seq):
    pos = np.arange(seq, dtype=np.float64)
    inv_freq = np.power(ROPE_THETA, -np.arange(0, ROPE_DIM, 2, dtype=np.float64) / ROPE_DIM)
    ang = pos[:, None] * inv_freq[None, :]
    half = ROPE_DIM // 2
    cos_f = np.ones((seq, LANES), np.float32)
    sin_a = np.zeros((seq, LANES), np.float32)
    sin_b = np.zeros((seq, LANES), np.float32)
    for base in range(0, LANES, ATT_HEAD_DIM):
        cos_f[:, base:base + half] = np.cos(ang)
        cos_f[:, base + half:base + ROPE_DIM] = np.cos(ang)
        sin_a[:, base:base + half] = -np.sin(ang)
        sin_b[:, base + half:base + ROPE_DIM] = np.sin(ang)
    return jnp.asarray(cos_f), jnp.asarray(sin_a), jnp.asarray(sin_b)


def _diff_attention(qkv3, lq1, lk1, lq2, lk2, subln_g, lam_init, tq=256):
    B, S, _ = qkv3.shape
    nb = ATT_WIDTH // LANES
    assert S % (2 * tq) == 0
    spec = lambda part: pl.BlockSpec((None, S, LANES), lambda b, h: (b, 0, part * nb + h))
    vec = pl.BlockSpec((1, ATT_HEAD_DIM), lambda b, h: (0, 0))
    r1 = lambda a: a.reshape(1, -1)
    return pl.pallas_call(
        functools.partial(_attn_kernel, lam_init=lam_init, tq=tq),
        grid=(B, ATT_HEADS),
        in_specs=[spec(0), spec(1), spec(2), vec, vec, vec, vec,
                  pl.BlockSpec((1, LANES), lambda b, h: (0, 0))],
        out_specs=pl.BlockSpec((None, S, LANES), lambda b, h: (b, 0, h)),
        out_shape=jax.ShapeDtypeStruct((B, S, ATT_WIDTH), F32),
        scratch_shapes=[pltpu.VMEM((2, 2, tq, S), F32), pltpu.VMEM((2, tq, S), BF16),
                        pltpu.VMEM((2, tq, 1), F32)],
        compiler_params=_cparams(("parallel", "parallel")),
        name="diff_attn",
    )(qkv3, qkv3, qkv3, r1(lq1), r1(lk1), r1(lq2), r1(lk2), r1(subln_g))


def _pool_kernel(u_ref, w_ref, b_ref, sc_ref, o_ref):
    ct = pl.program_id(1)
    x = u_ref[...]
    S = x.shape[0]
    row = lax.broadcasted_iota(jnp.int32, x.shape, 0)
    lane = lax.broadcasted_iota(jnp.int32, x.shape, 1)

    def shifted(d):
        if d == 0:
            return x
        r = pltpu.roll(x, (-d) % S, 0)
        return jnp.where((row + d >= 0) & (row + d < S), r, 0.0)

    sums = {}
    acc = shifted(-1) + x
    sums[2] = acc
    lo, hi = -1, 0
    for w in POOL_WINDOWS[1:]:
        for d in list(range(-(w // 2), lo)) + list(range(hi + 1, w // 2)):
            acc = acc + shifted(d)
        lo, hi = -(w // 2), w // 2 - 1
        sums[w] = acc
    grp = 2 * ct + (lane >= POOL_GROUP).astype(jnp.int32)
    win_sum = sums[POOL_WINDOWS[-1]]
    half = jnp.full(x.shape, POOL_WINDOWS[-1] // 2, jnp.int32)
    for g in range(len(POOL_WINDOWS) - 2, -1, -1):
        win_sum = jnp.where(grp == g, sums[POOL_WINDOWS[g]], win_sum)
        half = jnp.where(grp == g, POOL_WINDOWS[g] // 2, half)
    cnt = jnp.minimum(row + half - 1, S - 1) - jnp.maximum(row - half, 0) + 1
    y = win_sum / cnt.astype(F32) - x
    y = _dot3(*_split(y), *_split(w_ref[...])) + b_ref[...]
    o_ref[...] = y * sc_ref[...]


def _pool_mixer(u3, w, b, scale):
    B, S, _ = u3.shape
    nct = POOL_WIDTH // LANES
    gpt = LANES // POOL_GROUP
    wbd = jnp.zeros((nct, LANES, LANES), F32)
    for g in range(len(POOL_WINDOWS)):
        t, o = divmod(g, gpt)
        wbd = wbd.at[t, o * POOL_GROUP:(o + 1) * POOL_GROUP, o * POOL_GROUP:(o + 1) * POOL_GROUP].set(w[g])
    vspec = pl.BlockSpec((1, LANES), lambda bb, c: (0, c))
    return pl.pallas_call(
        _pool_kernel,
        grid=(B, nct),
        in_specs=[pl.BlockSpec((None, S, LANES), lambda bb, c: (bb, 0, c)),
                  pl.BlockSpec((None, LANES, LANES), lambda bb, c: (c, 0, 0)),
                  vspec, vspec],
        out_specs=pl.BlockSpec((None, S, LANES), lambda bb, c: (bb, 0, c)),
        out_shape=jax.ShapeDtypeStruct((B, S, POOL_WIDTH), F32),
        compiler_params=_cparams(("parallel", "parallel")),
        name="pool_mixer",
    )(u3, wbd, b.reshape(1, POOL_WIDTH), scale.reshape(1, POOL_WIDTH))


PAIRS_PER_GROUP = 6
N_CLASSES = N_GROUPS * PAIRS_PER_GROUP
ROUTE_W_LO, ROUTE_W_HI, ROUTE_CLS, ROUTE_RANK = 0, 1, 2, 3
ROW_WIDTH = D_MODEL + LANES


def _class_experts(c):
    g, pidx = divmod(c, PAIRS_PER_GROUP)
    pairs = [(a, b) for a in range(EXPERTS_PER_GROUP) for b in range(a + 1, EXPERTS_PER_GROUP)]
    lo, hi = pairs[pidx]
    return g * EXPERTS_PER_GROUP + lo, g * EXPERTS_PER_GROUP + hi


def _outproj_kernel(h_ref, yh_ref, ya_ref, yp_ref, wh_ref, wa_ref, wp_ref, g_ref, b_ref,
                    wgh_ref, wgl_ref, bg_ref, tri_ref, hx_ref, cnt_ref, carry_scr):
    d = functools.partial(jnp.dot, preferred_element_type=F32)
    mix = (d(yh_ref[...].astype(BF16), wh_ref[...]) + d(ya_ref[...].astype(BF16), wa_ref[...])
           + d(yp_ref[...].astype(BF16), wp_ref[...]))
    h1 = _ln_rows(DN_ALPHA * h_ref[...] + mix, g_ref[...], b_ref[...])
    hx_ref[:, :D_MODEL] = h1

    @pl.when(pl.program_id(0) == 0)
    def _():
        carry_scr[...] = jnp.zeros_like(carry_scr)

    logit = _dot3(*_split(h1), wgh_ref[...], wgl_ref[...]) + bg_ref[...]
    lni = lax.broadcasted_iota(jnp.int32, logit.shape, 1)
    ln = lni.astype(F32)
    grp = lax.shift_right_arithmetic(lni - GATE_FINE_LANE, 2).astype(F32)
    first = lambda mask: jnp.min(jnp.where(mask, ln, float(LANES)), axis=-1, keepdims=True)
    cmask = lni < N_GROUPS
    lc = jnp.where(cmask, logit, NEG_BIG)
    mc = jnp.max(lc, axis=-1, keepdims=True)
    gw = 1.0 / jnp.sum(jnp.where(cmask, jnp.exp(lc - mc), 0.0), axis=-1, keepdims=True)
    gi = first(cmask & (lc == mc))
    fmask = (lni >= GATE_FINE_LANE) & (lni < GATE_FINE_LANE + N_EXPERTS) & (grp == gi)
    lf = jnp.where(fmask, logit, NEG_BIG)
    m1 = jnp.max(lf, axis=-1, keepdims=True)
    i1 = first(fmask & (lf == m1))
    rest = fmask & (ln != i1)
    lf2 = jnp.where(rest, logit, NEG_BIG)
    m2 = jnp.max(lf2, axis=-1, keepdims=True)
    i2 = first(rest & (lf2 == m2))
    e2 = jnp.exp(m2 - m1)
    w1 = gw / (1.0 + e2)
    w2 = gw * e2 / (1.0 + e2)

    j1 = i1 - GATE_FINE_LANE - EXPERTS_PER_GROUP * gi
    j2 = i2 - GATE_FINE_LANE - EXPERTS_PER_GROUP * gi
    lo = jnp.minimum(j1, j2)
    hi = jnp.maximum(j1, j2)
    w_lo = jnp.where(j1 < j2, w1, w2)
    w_hi = jnp.where(j1 < j2, w2, w1)
    base = jnp.where(lo == 0.0, 0.0, jnp.where(lo == 1.0, 3.0, 5.0))
    cls = gi * PAIRS_PER_GROUP + base + hi - lo - 1.0
    onehot = ln == cls
    before = jnp.dot(tri_ref[...], onehot.astype(BF16), preferred_element_type=F32)
    carry = carry_scr[...]
    rank = jnp.sum(jnp.where(onehot, before + carry, 0.0), axis=-1, keepdims=True)
    carry = carry + jnp.sum(onehot.astype(F32), axis=0, keepdims=True)
    carry_scr[...] = carry
    cnt_ref[...] = carry
    hx_ref[:, D_MODEL:] = jnp.where(lni == ROUTE_W_LO, w_lo,
                                    jnp.where(lni == ROUTE_W_HI, w_hi,
                                              jnp.where(lni == ROUTE_CLS, cls,
                                                        jnp.where(lni == ROUTE_RANK, rank, 0.0))))


def _out_proj_ln_route(h2, yh, ya, yp, w_out_bf, g, b, wgc, bgc, wgf, bgf, tm=512):
    T, D = h2.shape
    tri = jnp.asarray(np.tril(np.ones((tm, tm), np.float32), -1)).astype(BF16)
    wg = jnp.zeros((D, LANES), F32).at[:, :N_GROUPS].set(wgc)
    wg = wg.at[:, GATE_FINE_LANE:GATE_FINE_LANE + N_EXPERTS].set(wgf)
    bg = jnp.zeros((1, LANES), F32).at[0, :N_GROUPS].set(bgc)
    bg = bg.at[0, GATE_FINE_LANE:GATE_FINE_LANE + N_EXPERTS].set(bgf)
    wgh, wgl = _split(wg)
    o1, o2 = HY_WIDTH, HY_WIDTH + ATT_WIDTH
    row = lambda c: pl.BlockSpec((tm, c), lambda i: (i, 0))
    full = lambda r, c: pl.BlockSpec((r, c), lambda i: (0, 0))
    return pl.pallas_call(
        _outproj_kernel,
        grid=(T // tm,),
        in_specs=[row(D), row(HY_WIDTH), row(ATT_WIDTH), row(POOL_WIDTH),
                  full(HY_WIDTH, D), full(ATT_WIDTH, D), full(POOL_WIDTH, D),
                  full(1, D), full(1, D), full(D, LANES), full(D, LANES), full(1, LANES), full(tm, tm)],
        out_specs=[row(ROW_WIDTH), full(1, LANES)],
        out_shape=[jax.ShapeDtypeStruct((T, ROW_WIDTH), F32), jax.ShapeDtypeStruct((1, LANES), F32)],
        scratch_shapes=[pltpu.VMEM((1, LANES), F32)],
        compiler_params=_cparams(("arbitrary",)),
        name="out_proj_ln_route",
    )(h2, yh, ya, yp, w_out_bf[:o1], w_out_bf[o1:o2], w_out_bf[o2:], g.reshape(1, D), b.reshape(1, D),
      wgh, wgl, bg, tri)


MOE_TILE = 256
DMA_BATCH = 256


def _moe_rows_padded(T):
    return T + N_CLASSES * MOE_TILE


def _start_rows(n, start_copy):
    def start(r, carry):
        start_copy(2 * r, 0)
        start_copy(2 * r + 1, 1)
        return carry

    lax.fori_loop(0, n // 2, start, 0, unroll=4)


def _wait_rows(n, wait_copy):
    def wait(r, carry):
        wait_copy(0)
        wait_copy(1)
        return carry

    lax.fori_loop(0, n // 2, wait, 0, unroll=4)


def _dispatch_kernel(cls_ref, rank_ref, cnt_ref, hx_ref, xs_ref, dest_ref, tlo_ref, thi_ref, tval_ref,
                     off_scr, zero_scr, sem, *, tm, n_tiles):
    i = pl.program_id(0)

    @pl.when(i == 0)
    def _():
        zero_scr[...] = jnp.zeros_like(zero_scr)
        off = jnp.int32(0)
        tile = jnp.int32(0)
        fills = []
        for c in range(N_CLASSES):
            n = cnt_ref[c]
            nt = lax.shift_right_logical(n + (MOE_TILE - 1), MOE_TILE.bit_length() - 1)
            off_scr[c] = off
            e_lo, e_hi = _class_experts(c)

            def mark(k, carry, tile=tile, e_lo=e_lo, e_hi=e_hi):
                tlo_ref[tile + k] = e_lo
                thi_ref[tile + k] = e_hi
                tval_ref[tile + k] = 1
                return carry

            lax.fori_loop(0, nt, mark, 0)
            last = off + (nt - 1) * MOE_TILE
            fill = pltpu.make_async_copy(zero_scr, xs_ref.at[pl.ds(pl.multiple_of(last, MOE_TILE), MOE_TILE)],
                                         sem.at[2])
            fills.append((nt > 0, fill))

            @pl.when(nt > 0)
            def _(fill=fill):
                fill.start()

            off = off + nt * MOE_TILE
            tile = tile + nt

        def tail_fill(k):
            rows = pl.ds(pl.multiple_of(k * MOE_TILE, MOE_TILE), MOE_TILE)
            return pltpu.make_async_copy(zero_scr, xs_ref.at[rows], sem.at[2])

        def unused(k, carry):
            tlo_ref[k] = 0
            thi_ref[k] = 0
            tval_ref[k] = 0
            tail_fill(k).start()
            return carry

        lax.fori_loop(tile, n_tiles, unused, 0)
        for used, fill in fills:
            @pl.when(used)
            def _(fill=fill):
                fill.wait()

        def unused_wait(k, carry):
            tail_fill(k).wait()
            return carry

        lax.fori_loop(tile, n_tiles, unused_wait, 0)

    base = i * tm

    def start_copy(r, prio):
        t = base + r
        d = off_scr[cls_ref[t]] + rank_ref[t]
        dest_ref[t] = d
        pltpu.make_async_copy(hx_ref.at[r], xs_ref.at[d], sem.at[prio]).start(priority=prio)

    def wait_copy(prio):
        pltpu.make_async_copy(hx_ref.at[0], xs_ref.at[0], sem.at[prio]).wait()

    for b0 in range(0, tm, DMA_BATCH):
        _start_rows(DMA_BATCH, lambda r, prio: start_copy(b0 + r, prio))
        _wait_rows(DMA_BATCH, wait_copy)


def _moe_dispatch(hx, cls_i, rank_i, cnt_i, tm=512):
    T = hx.shape[0]
    rows = _moe_rows_padded(T)
    n_tiles = rows // MOE_TILE
    smem = pl.BlockSpec(memory_space=pltpu.SMEM)
    return pl.pallas_call(
        functools.partial(_dispatch_kernel, tm=tm, n_tiles=n_tiles),
        grid_spec=pltpu.PrefetchScalarGridSpec(
            num_scalar_prefetch=3,
            grid=(T // tm,),
            in_specs=[pl.BlockSpec((tm, ROW_WIDTH), lambda i, *_: (i, 0))],
            out_specs=[pl.BlockSpec(memory_space=pl.ANY), smem, smem, smem, smem],
            scratch_shapes=[pltpu.SMEM((N_CLASSES,), jnp.int32), pltpu.VMEM((MOE_TILE, ROW_WIDTH), F32),
                            pltpu.SemaphoreType.DMA((3,))],
        ),
        out_shape=[jax.ShapeDtypeStruct((rows, ROW_WIDTH), F32), jax.ShapeDtypeStruct((T,), jnp.int32),
                   jax.ShapeDtypeStruct((n_tiles,), jnp.int32), jax.ShapeDtypeStruct((n_tiles,), jnp.int32),
                   jax.ShapeDtypeStruct((n_tiles,), jnp.int32)],
        compiler_params=_cparams(("arbitrary",)),
        name="moe_dispatch",
    )(cls_i, rank_i, cnt_i, hx)


def _experts_kernel(tlo_ref, thi_ref, tval_ref, xs_ref, w1l_ref, w3l_ref, w2l_ref, w1h_ref, w3h_ref, w2h_ref,
                    ys_ref):
    j = pl.program_id(0)
    d = functools.partial(jnp.dot, preferred_element_type=F32)

    @pl.when(tval_ref[j] == 1)
    def _():
        xb = xs_ref[:, :D_MODEL].astype(BF16)
        rec = xs_ref[:, D_MODEL:]

        def expert(w1_ref, w3_ref, w2_ref, lane):
            a = d(xb, w1_ref[...])
            c = d(xb, w3_ref[...])
            act = a * jax.nn.sigmoid(a) * c * rec[:, lane:lane + 1]
            return d(act.astype(BF16), w2_ref[...])

        ys_ref[...] = expert(w1l_ref, w3l_ref, w2l_ref, ROUTE_W_LO) + expert(w1h_ref, w3h_ref, w2h_ref, ROUTE_W_HI)

    @pl.when(tval_ref[j] == 0)
    def _():
        ys_ref[...] = jnp.zeros_like(ys_ref)


def _moe_experts(xs, tlo, thi, tval, w1_bf, w3_bf, w2_bf):
    rows = xs.shape[0]
    D = D_MODEL
    wspec = lambda shape, which: pl.BlockSpec(
        (None,) + shape, (lambda j, tlo, thi, tval: (tlo[j], 0, 0)) if which == 0
        else (lambda j, tlo, thi, tval: (thi[j], 0, 0)))
    up = (D, D_EXPERT)
    down = (D_EXPERT, D)
    return pl.pallas_call(
        _experts_kernel,
        grid_spec=pltpu.PrefetchScalarGridSpec(
            num_scalar_prefetch=3,
            grid=(rows // MOE_TILE,),
            in_specs=[pl.BlockSpec((MOE_TILE, ROW_WIDTH), lambda j, *_: (j, 0)),
                      wspec(up, 0), wspec(up, 0), wspec(down, 0), wspec(up, 1), wspec(up, 1), wspec(down, 1)],
            out_specs=pl.BlockSpec((MOE_TILE, D), lambda j, *_: (j, 0)),
        ),
        out_shape=jax.ShapeDtypeStruct((rows, D), F32),
        compiler_params=_cparams(("arbitrary",)),
        name="moe_experts",
    )(tlo, thi, tval, xs, w1_bf, w3_bf, w2_bf, w1_bf, w3_bf, w2_bf)


def _combine_kernel(dest_ref, hx_ref, ys_ref, p_ref, pwg_ref, pbg_ref, pwp_ref, g_ref, b_ref, o_ref,
                    y_scr, sem, *, tm):
    base = pl.program_id(0) * tm
    d = functools.partial(jnp.dot, preferred_element_type=F32)

    def start_copy(r, prio):
        pltpu.make_async_copy(ys_ref.at[dest_ref[base + r]], y_scr.at[r], sem.at[prio]).start(priority=prio)

    def wait_copy(prio):
        pltpu.make_async_copy(ys_ref.at[0], y_scr.at[0], sem.at[prio]).wait()

    h1 = hx_ref[...]
    for b0 in range(0, tm, DMA_BATCH):
        _start_rows(DMA_BATCH, lambda r, prio: start_copy(b0 + r, prio))
        if b0 == 0:
            z = d(h1.astype(BF16), pwg_ref[...]) + pbg_ref[...]
            y_ple = jax.nn.sigmoid(z) * d(p_ref[...].astype(BF16), pwp_ref[...])
        _wait_rows(DMA_BATCH, wait_copy)
    r_ = DN_ALPHA * h1 + y_scr[...] + y_ple
    o_ref[...] = _ln_rows(r_, g_ref[...], b_ref[...])


def _moe_combine_ple_ln(hx, ys, dest, p2, pwg_bf, pbg, pwp_bf, g, b, tm=512):
    T = hx.shape[0]
    D = D_MODEL
    row = lambda c: pl.BlockSpec((tm, c), lambda i, *_: (i, 0))
    full = lambda r, c: pl.BlockSpec((r, c), lambda i, *_: (0, 0))
    return pl.pallas_call(
        functools.partial(_combine_kernel, tm=tm),
        grid_spec=pltpu.PrefetchScalarGridSpec(
            num_scalar_prefetch=1,
            grid=(T // tm,),
            in_specs=[row(D), pl.BlockSpec(memory_space=pl.ANY), row(PLE_DIM),
                      full(D, D), full(1, D), full(PLE_DIM, D), full(1, D), full(1, D)],
            out_specs=row(D),
            scratch_shapes=[pltpu.VMEM((tm, D), F32), pltpu.SemaphoreType.DMA((2,))],
        ),
        out_shape=jax.ShapeDtypeStruct((T, D), F32),
        compiler_params=_cparams(("arbitrary",)),
        name="moe_combine_ple_ln",
    )(dest, hx, ys, p2, pwg_bf, pbg.reshape(1, D), pwp_bf, g.reshape(1, D), b.reshape(1, D))


def _hyena_mixer(u3, kr, ki, consts, conv_w, conv_b, bias_d, n1_total):
    B, S, _ = u3.shape
    vv, x0c = _hy_gate(u3, conv_w, conv_b)
    y = _hy_conv(vv, x0c, kr, ki, consts, bias_d, n1_total)
    return y.reshape(B * S, HY_WIDTH)


def kernel(x, p, ln0_g, ln0_b, w_in, hy_conv_w, hy_conv_b, hy_fw1, hy_fb1, hy_freq1, hy_fw2, hy_fb2, hy_freq2, hy_fw3, hy_bias, att_lq1, att_lk1, att_lq2, att_lk2, att_subln_g, pool_w, pool_b, pool_scale, w_out, ln1_g, ln1_b, moe_wgc, moe_bgc, moe_wgf, moe_bgf, moe_w1, moe_w3, moe_w2, ple_wg, ple_bg, ple_wp, ln2_g, ln2_b):
    B, S, D = x.shape
    L = w_in.shape[0]
    T = B * S
    n1_total = 2 * S // FFT_N2
    C = HY_WIDTH

    kfilt, ssq = _hy_filter(S, hy_fw1, hy_fb1, hy_freq1, hy_fw2, hy_fb2, hy_freq2, hy_fw3)
    kar, kai = _stage_a(kfilt.reshape(L, n1_total, FFT_N2 * C), n1_total)
    k1p = kar.shape[1]
    consts = _stage_b_consts(n1_total)
    kr_all, ki_all = _stage_b_filter(kar.reshape(L, k1p, FFT_N2, C), kai.reshape(L, k1p, FFT_N2, C),
                                     ssq, consts, n1_total)
    tables = _rope_tables(S)

    h = _layer_norm(x.reshape(T, D), ln0_g, ln0_b)
    for i in range(L):
        lam_init = 0.8 - 0.6 * math.exp(-0.3 * i)
        uh, qkv, up = _in_proj(h, w_in[i].astype(BF16), tables, S)
        y_hy = _hyena_mixer(uh.reshape(B, S, 3 * C), kr_all[i], ki_all[i], consts,
                            hy_conv_w[i], hy_conv_b[i], hy_bias[i], n1_total)
        y_att = _diff_attention(qkv.reshape(B, S, 3 * ATT_WIDTH), att_lq1[i], att_lk1[i], att_lq2[i],
                                att_lk2[i], att_subln_g[i], lam_init).reshape(T, ATT_WIDTH)
        y_pool = _pool_mixer(up.reshape(B, S, POOL_WIDTH), pool_w[i], pool_b[i],
                             pool_scale[i]).reshape(T, POOL_WIDTH)
        hx, counts = _out_proj_ln_route(h, y_hy, y_att, y_pool, w_out[i].astype(BF16), ln1_g[i], ln1_b[i],
                                        moe_wgc[i], moe_bgc[i], moe_wgf[i], moe_bgf[i])
        cls_i = hx[:, D + ROUTE_CLS].astype(jnp.int32)
        rank_i = hx[:, D + ROUTE_RANK].astype(jnp.int32)
        xs, dest, tlo, thi, tval = _moe_dispatch(hx, cls_i, rank_i, counts[0].astype(jnp.int32))
        ys = _moe_experts(xs, tlo, thi, tval, moe_w1[i].astype(BF16), moe_w3[i].astype(BF16),
                          moe_w2[i].astype(BF16))
        h = _moe_combine_ple_ln(hx, ys, dest, p[i].reshape(T, PLE_DIM), ple_wg[i].astype(BF16), ple_bg[i],
                                ple_wp[i].astype(BF16), ln2_g[i], ln2_b[i])
    return h.reshape(B, S, D)
```

```python
import functools
import math

import numpy as np
import jax
import jax.numpy as jnp
from jax import lax
from jax.experimental import pallas as pl
from jax.experimental.pallas import tpu as pltpu

F32 = jnp.float32
BF16 = jnp.bfloat16

D_MODEL = 1024
DEPTH = 4
HY_WIDTH = 256
ATT_WIDTH = 512
ATT_HEADS = 4
ATT_HEAD_DIM = 64
POOL_WINDOWS = (2, 4, 8, 16)
POOL_WIDTH = 256
POOL_GROUP = 64
IN_WIDTH = 3 * HY_WIDTH + 3 * ATT_WIDTH + POOL_WIDTH
ROPE_THETA = 500000.0
ROPE_DIM = ATT_HEAD_DIM // 4
HY_EMB = 33
HY_BANDS = (HY_EMB - 1) // 2
HY_FILTER_HIDDEN = 64
HY_DECAY_TARGET = 1e-2
HY_FAST_DECAY = 0.3
HY_SLOW_DECAY = 1.5
N_GROUPS = 4
EXPERTS_PER_GROUP = 4
N_EXPERTS = 16
D_EXPERT = 256
PLE_DIM = 256
LN_EPS = 1e-5
RMS_EPS = 1e-5
DN_ALPHA = (2 * DEPTH) ** 0.25

LANES = 128
FFT_N2 = 128
GATE_COARSE_LANE = 0
GATE_FINE_LANE = N_GROUPS
NEG_BIG = -1e30
LOG2E = 1.4426950408889634
VMEM_LIMIT = 56 * 1024 * 1024


def _cparams(sem):
    return pltpu.CompilerParams(dimension_semantics=sem, vmem_limit_bytes=VMEM_LIMIT)


def _split(x):
    hi = x.astype(BF16)
    lo = (x - hi.astype(F32)).astype(BF16)
    return hi, lo


def _dot3(ah, al, bh, bl):
    d = functools.partial(jnp.dot, preferred_element_type=F32)
    return d(ah, bh) + (d(ah, bl) + d(al, bh))


def _ln_rows(x, g, b):
    mu = jnp.mean(x, axis=-1, keepdims=True)
    xc = x - mu
    var = jnp.mean(xc * xc, axis=-1, keepdims=True)
    return xc * lax.rsqrt(var + LN_EPS) * g + b


def _ln_kernel(x_ref, g_ref, b_ref, o_ref):
    o_ref[...] = _ln_rows(x_ref[...], g_ref[...], b_ref[...])


def _layer_norm(x2, g, b, tm=512):
    T, D = x2.shape
    return pl.pallas_call(
        _ln_kernel,
        grid=(T // tm,),
        in_specs=[pl.BlockSpec((tm, D), lambda i: (i, 0)),
                  pl.BlockSpec((1, D), lambda i: (0, 0)),
                  pl.BlockSpec((1, D), lambda i: (0, 0))],
        out_specs=pl.BlockSpec((tm, D), lambda i: (i, 0)),
        out_shape=jax.ShapeDtypeStruct((T, D), F32),
        compiler_params=_cparams(("parallel",)),
        name="ln0",
    )(x2, g.reshape(1, D), b.reshape(1, D))


def _rope(x, cos_f, sin_a, sin_b):
    half = ROPE_DIM // 2
    return x * cos_f + pltpu.roll(x, LANES - half, 1) * sin_a + pltpu.roll(x, half, 1) * sin_b


def _inproj_kernel(h_ref, w_ref, cos_ref, sa_ref, sb_ref, uh_ref, qkv_ref, up_ref):
    hb = h_ref[...].astype(BF16)
    mm = lambda c0, c1: jnp.dot(hb, w_ref[:, c0:c1], preferred_element_type=F32)
    o_q = 3 * HY_WIDTH
    o_k = o_q + ATT_WIDTH
    o_v = o_k + ATT_WIDTH
    o_p = o_v + ATT_WIDTH
    uh_ref[...] = mm(0, o_q)
    up_ref[...] = mm(o_p, IN_WIDTH)
    cos_f, sin_a, sin_b = cos_ref[...], sa_ref[...], sb_ref[...]
    q = mm(o_q, o_k)
    k = mm(o_k, o_v)
    qscale = ATT_HEAD_DIM ** -0.5 * LOG2E
    for hd in range(ATT_HEADS):
        sl = slice(hd * LANES, (hd + 1) * LANES)
        qkv_ref[:, sl] = (_rope(q[:, sl], cos_f, sin_a, sin_b) * qscale).astype(BF16)
        qkv_ref[:, ATT_WIDTH + hd * LANES:ATT_WIDTH + (hd + 1) * LANES] = \
            _rope(k[:, sl], cos_f, sin_a, sin_b).astype(BF16)
    qkv_ref[:, 2 * ATT_WIDTH:] = mm(o_v, o_p).astype(BF16)


def _in_proj(h2, w_bf, tables, seq, tm=512):
    T, D = h2.shape
    N = w_bf.shape[1]
    spt = seq // tm
    tspec = pl.BlockSpec((tm, LANES), lambda i: (i % spt, 0))
    row = lambda c: pl.BlockSpec((tm, c), lambda i: (i, 0))
    return pl.pallas_call(
        _inproj_kernel,
        grid=(T // tm,),
        in_specs=[row(D), pl.BlockSpec((D, N), lambda i: (0, 0)), tspec, tspec, tspec],
        out_specs=[row(3 * HY_WIDTH), row(3 * ATT_WIDTH), row(POOL_WIDTH)],
        out_shape=[jax.ShapeDtypeStruct((T, 3 * HY_WIDTH), F32),
                   jax.ShapeDtypeStruct((T, 3 * ATT_WIDTH), BF16),
                   jax.ShapeDtypeStruct((T, POOL_WIDTH), F32)],
        compiler_params=_cparams(("parallel",)),
        name="in_proj",
    )(h2, w_bf, *tables)


def _hy_filter_kernel(band_ref, fw1_ref, fb1_ref, fr1_ref, fw2_ref, fb2_ref, fr2_ref, fw3_ref,
                      dl_ref, k_ref, ssq_ref, *, seq, tr):
    j = pl.program_id(1)
    n = j * tr + lax.broadcasted_iota(jnp.int32, (tr, 1), 0)
    pos = jnp.where(n < seq, n, 2 * seq - n).astype(F32)
    t = pos * (1.0 / (seq - 1))
    wpos = (2.0 * math.pi / seq) * pos
    lane = lax.broadcasted_iota(jnp.int32, (tr, LANES), 1)
    arg = wpos * band_ref[...]
    z = jnp.where(lane == 0, t,
                  jnp.where(lane <= HY_BANDS, jnp.cos(arg),
                            jnp.where(lane < HY_EMB, -jnp.sin(arg), 0.0)))
    h1 = _dot3(*_split(z), *_split(fw1_ref[...])) + fb1_ref[...]
    h1 = jnp.sin(fr1_ref[...] * h1)
    h2 = _dot3(*_split(h1), *_split(fw2_ref[...])) + fb2_ref[...]
    h2 = jnp.sin(fr2_ref[...] * h2)
    filt = _dot3(*_split(h2), *_split(fw3_ref[...]))
    window = jnp.exp(-t * jnp.abs(dl_ref[...]))
    kk = jnp.where(n < seq, filt[:, :HY_WIDTH], filt[:, HY_WIDTH:]) * window
    kk = jnp.where(n == seq, 0.0, kk)
    k_ref[...] = kk

    @pl.when(j == 0)
    def _():
        ssq_ref[...] = jnp.zeros_like(ssq_ref)

    ssq_ref[...] += jnp.sum(kk * kk, axis=0, keepdims=True)


def _hy_filter(seq, fw1, fb1, freq1, fw2, fb2, freq2, fw3, tr=1024):
    L = fw1.shape[0]
    n = 2 * seq
    H = HY_FILTER_HIDDEN
    bands = np.linspace(1e-4, HY_BANDS - 1, HY_BANDS)
    bandv = np.zeros((1, LANES), np.float32)
    bandv[0, 1:1 + HY_BANDS] = bands
    bandv[0, 1 + HY_BANDS:HY_EMB] = bands
    fw1p = jnp.zeros((L, LANES, H), F32).at[:, :HY_EMB].set(fw1)
    max_decay = math.log(HY_DECAY_TARGET) / HY_FAST_DECAY
    min_decay = math.log(HY_DECAY_TARGET) / HY_SLOW_DECAY
    deltas = np.linspace(min_decay, max_decay, HY_WIDTH).astype(np.float32).reshape(1, HY_WIDTH)
    vec = lambda a: a.reshape(L, 1, a.shape[-1])
    lspec = lambda r, c: pl.BlockSpec((None, r, c), lambda l, j: (l, 0, 0))
    return pl.pallas_call(
        functools.partial(_hy_filter_kernel, seq=seq, tr=tr),
        grid=(L, n // tr),
        in_specs=[pl.BlockSpec((1, LANES), lambda l, j: (0, 0)),
                  lspec(LANES, H), lspec(1, H), lspec(1, H),
                  lspec(H, H), lspec(1, H), lspec(1, H),
                  lspec(H, 2 * HY_WIDTH),
                  pl.BlockSpec((1, HY_WIDTH), lambda l, j: (0, 0))],
        out_specs=[pl.BlockSpec((None, tr, HY_WIDTH), lambda l, j: (l, j, 0)),
                   pl.BlockSpec((None, 1, HY_WIDTH), lambda l, j: (l, 0, 0))],
        out_shape=[jax.ShapeDtypeStruct((L, n, HY_WIDTH), F32),
                   jax.ShapeDtypeStruct((L, 1, HY_WIDTH), F32)],
        compiler_params=_cparams(("parallel", "arbitrary")),
        name="hy_filter",
    )(jnp.asarray(bandv), fw1p, vec(fb1), vec(freq1), fw2, vec(fb2), vec(freq2), fw3,
      jnp.asarray(deltas))


def _k1_rows(n1):
    k1 = n1 // 2 + 1
    return k1, -(-k1 // 8) * 8


def _bf_pair(m):
    m32 = jnp.asarray(np.asarray(m, np.float32))
    return _split(m32)


def _stage_a_kernel(x_ref, frh_ref, frl_ref, fih_ref, fil_ref, ar_ref, ai_ref):
    xh, xl = _split(x_ref[...])
    ar_ref[...] = _dot3(frh_ref[...], frl_ref[...], xh, xl)
    ai_ref[...] = _dot3(fih_ref[...], fil_ref[...], xh, xl)


def _stage_a(x3, n1_total, tl=4096):
    Bt, n1u, LN = x3.shape
    k1n, k1p = _k1_rows(n1_total)
    ang = 2.0 * np.pi * np.outer(np.arange(k1p), np.arange(n1u)) / n1_total
    valid = (np.arange(k1p) < k1n)[:, None]
    frh, frl = _bf_pair(np.cos(ang) * valid)
    fih, fil = _bf_pair(-np.sin(ang) * valid)
    cspec = pl.BlockSpec((k1p, n1u), lambda b, j: (0, 0))
    ospec = pl.BlockSpec((None, k1p, tl), lambda b, j: (b, 0, j))
    return pl.pallas_call(
        _stage_a_kernel,
        grid=(Bt, LN // tl),
        in_specs=[pl.BlockSpec((None, n1u, tl), lambda b, j: (b, 0, j)), cspec, cspec, cspec, cspec],
        out_specs=[ospec, ospec],
        out_shape=[jax.ShapeDtypeStruct((Bt, k1p, LN), F32)] * 2,
        compiler_params=_cparams(("parallel", "parallel")),
        name="hy_stage_a",
    )(x3, frh, frl, fih, fil)


def _stage_b_consts(n1_total):
    n = n1_total * FFT_N2
    k1n, k1p = _k1_rows(n1_total)
    ang2 = 2.0 * np.pi * np.outer(np.arange(FFT_N2), np.arange(FFT_N2)) / FFT_N2
    c2, s2 = np.cos(ang2), np.sin(ang2)
    mf = np.block([[c2, s2], [-s2, c2]])
    mi = np.block([[c2, -s2], [s2, c2]])
    angt = 2.0 * np.pi * np.outer(np.arange(k1p), np.arange(FFT_N2)) / n
    twc = jnp.asarray(np.cos(angt).astype(np.float32)).reshape(k1p, FFT_N2, 1)
    tws = jnp.asarray(np.sin(angt).astype(np.float32)).reshape(k1p, FFT_N2, 1)
    return _bf_pair(mf), _bf_pair(mi), twc, tws


def _fwd_b(ar, ai, c, s, mfh, mfl):
    tr_ = ar * c + ai * s
    ti_ = ai * c - ar * s
    xh, xl = _split(jnp.concatenate([tr_, ti_], axis=0))
    z = _dot3(mfh, mfl, xh, xl)
    return z[:FFT_N2], z[FFT_N2:]


def _stage_bk_kernel(ar_ref, ai_ref, c_ref, s_ref, mfh_ref, mfl_ref, ssq_ref, kr_ref, ki_ref,
                     *, k1n, inv_n):
    k1 = pl.program_id(1)

    @pl.when(k1 < k1n)
    def _():
        zr, zi = _fwd_b(ar_ref[...], ai_ref[...], c_ref[...], s_ref[...], mfh_ref[...], mfl_ref[...])
        scale = lax.rsqrt(ssq_ref[...] + 1e-6) * inv_n
        kr_ref[...] = zr * scale
        ki_ref[...] = zi * scale

    @pl.when(k1 >= k1n)
    def _():
        kr_ref[...] = jnp.zeros_like(kr_ref)
        ki_ref[...] = jnp.zeros_like(ki_ref)


def _stage_b_filter(ar4, ai4, ssq, consts, n1_total):
    L, k1p, _, C = ar4.shape
    k1n, _ = _k1_rows(n1_total)
    (mfh, mfl), _, twc, tws = consts
    blk = pl.BlockSpec((None, None, FFT_N2, C), lambda l, k: (l, k, 0, 0))
    tw = pl.BlockSpec((None, FFT_N2, 1), lambda l, k: (k, 0, 0))
    mat = pl.BlockSpec((2 * FFT_N2, 2 * FFT_N2), lambda l, k: (0, 0))
    return pl.pallas_call(
        functools.partial(_stage_bk_kernel, k1n=k1n, inv_n=1.0 / (n1_total * FFT_N2)),
        grid=(L, k1p),
        in_specs=[blk, blk, tw, tw, mat, mat, pl.BlockSpec((None, 1, C), lambda l, k: (l, 0, 0))],
        out_specs=[blk, blk],
        out_shape=[jax.ShapeDtypeStruct(ar4.shape, F32)] * 2,
        compiler_params=_cparams(("parallel", "parallel")),
        name="hy_stage_b_filter",
    )(ar4, ai4, twc, tws, mfh, mfl, ssq)


SUBLANES = 8


def _hy_conv_kernel(vv_ref, x0_ref, kr_ref, ki_ref, tw_ref, kah_ref, kal_ref, kch_ref, kcl_ref,
                    mfh_ref, mfl_ref, mih_ref, mil_ref, bias_ref, o_ref, a_scr, *, n1u, k1n, k1p):
    groups = FFT_N2 // SUBLANES

    def rows(n1, g):
        return pl.ds(pl.multiple_of(n1 * FFT_N2 + g * SUBLANES, SUBLANES), SUBLANES)

    def stage_a(g, carry):
        xg = jnp.concatenate([vv_ref[rows(n1, g), :] for n1 in range(n1u)], axis=0)
        ag = _dot3(kah_ref[...], kal_ref[...], *_split(xg))
        gs = pl.ds(pl.multiple_of(g * SUBLANES, SUBLANES), SUBLANES)
        for part in range(2):
            for k1 in range(k1p):
                r = (part * k1p + k1) * SUBLANES
                a_scr[part, k1, gs, :] = ag[r:r + SUBLANES]
        return carry

    lax.fori_loop(0, groups, stage_a, 0)

    def stage_b(k1, carry):
        tw = tw_ref[k1]
        c = tw[:, 0:1]
        s = tw[:, 1:2]
        zr, zi = _fwd_b(a_scr[0, k1], a_scr[1, k1], c, s, mfh_ref[...], mfl_ref[...])
        kr = kr_ref[k1]
        ki = ki_ref[k1]
        pr = zr * kr - zi * ki
        pi = zr * ki + zi * kr
        y = _dot3(mih_ref[...], mil_ref[...], *_split(jnp.concatenate([pr, pi], axis=0)))
        yr = y[:FFT_N2]
        yi = y[FFT_N2:]
        a_scr[0, k1] = yr * c - yi * s
        a_scr[1, k1] = yi * c + yr * s
        return carry

    lax.fori_loop(0, k1n, stage_b, 0, unroll=3 if k1n % 3 == 0 else 1)

    def stage_c(g, carry):
        gs = pl.ds(pl.multiple_of(g * SUBLANES, SUBLANES), SUBLANES)
        bg = jnp.concatenate([a_scr[part, k1, gs, :] for part in range(2) for k1 in range(k1p)], axis=0)
        yg = _dot3(kch_ref[...], kcl_ref[...], *_split(bg))
        for n1 in range(n1u):
            r = rows(n1, g)
            o_ref[r, :] = (yg[n1 * SUBLANES:(n1 + 1) * SUBLANES] + vv_ref[r, :] * bias_ref[...]) * x0_ref[r, :]
        return carry

    lax.fori_loop(0, groups, stage_c, 0)


def _hy_conv(vv, x0c, kr3, ki3, consts, bias_d, n1_total):
    B, S, C = vv.shape
    n1u = S // FFT_N2
    k1n, k1p = _k1_rows(n1_total)
    (mfh, mfl), (mih, mil), twc, tws = consts
    tw = jnp.concatenate([twc, tws], axis=-1)
    k1 = np.arange(k1p)
    valid = (k1 < k1n)[:, None]
    eye = np.eye(SUBLANES)
    ang_a = 2.0 * np.pi * np.outer(k1, np.arange(n1u)) / n1_total
    ka = np.concatenate([np.kron(np.cos(ang_a) * valid, eye), np.kron(-np.sin(ang_a) * valid, eye)], axis=0)
    w = np.where((k1 == 0) | (k1 == n1_total // 2), 1.0, 2.0) * (k1 < k1n)
    ang_c = 2.0 * np.pi * np.outer(np.arange(n1u), k1) / n1_total
    kc = np.concatenate([np.kron(np.cos(ang_c) * w[None, :], eye), np.kron(-np.sin(ang_c) * w[None, :], eye)], axis=1)
    kah, kal = _bf_pair(ka)
    kch, kcl = _bf_pair(kc)
    nct = C // LANES
    xspec = pl.BlockSpec((None, S, LANES), lambda c, b: (b, 0, c))
    kspec = pl.BlockSpec((k1p, FFT_N2, LANES), lambda c, b: (0, 0, c))
    full = lambda a: pl.BlockSpec(a.shape, lambda c, b: (0,) * a.ndim)
    return pl.pallas_call(
        functools.partial(_hy_conv_kernel, n1u=n1u, k1n=k1n, k1p=k1p),
        grid=(nct, B),
        in_specs=[xspec, xspec, kspec, kspec, full(tw), full(kah), full(kal), full(kch), full(kcl),
                  full(mfh), full(mfl), full(mih), full(mil),
                  pl.BlockSpec((1, LANES), lambda c, b: (0, c))],
        out_specs=xspec,
        out_shape=jax.ShapeDtypeStruct((B, S, C), F32),
        scratch_shapes=[pltpu.VMEM((2, k1p, FFT_N2, LANES), F32)],
        compiler_params=_cparams(("parallel", "parallel")),
        name="hy_conv",
    )(vv, x0c, kr3, ki3, tw, kah, kal, kch, kcl, mfh, mfl, mih, mil, bias_d.reshape(1, C))


def _conv3(x, w_ref, b_ref):
    S = x.shape[0]
    row = lax.broadcasted_iota(jnp.int32, x.shape, 0)
    prev = jnp.where(row == 0, 0.0, pltpu.roll(x, 1, 0))
    nxt = jnp.where(row == S - 1, 0.0, pltpu.roll(x, S - 1, 0))
    return prev * w_ref[0:1, :] + x * w_ref[1:2, :] + nxt * w_ref[2:3, :] + b_ref[...]


def _hy_gate_kernel(x0_ref, x1_ref, v_ref, w0_ref, w1_ref, w2_ref, b0_ref, b1_ref, b2_ref,
                    vv_ref, x0c_ref):
    x0c_ref[...] = _conv3(x0_ref[...], w0_ref, b0_ref)
    vv_ref[...] = _conv3(v_ref[...], w2_ref, b2_ref) * _conv3(x1_ref[...], w1_ref, b1_ref)


def _hy_gate(u3, conv_w, conv_b):
    B, S, _ = u3.shape
    nct = HY_WIDTH // LANES
    conv_b2 = conv_b.reshape(1, 3 * HY_WIDTH)
    uspec = lambda part: pl.BlockSpec((None, S, LANES), lambda b, c: (b, 0, part * nct + c))
    wspec = lambda part: pl.BlockSpec((3, LANES), lambda b, c: (0, part * nct + c))
    bspec = lambda part: pl.BlockSpec((1, LANES), lambda b, c: (0, part * nct + c))
    ospec = pl.BlockSpec((None, S, LANES), lambda b, c: (b, 0, c))
    return pl.pallas_call(
        _hy_gate_kernel,
        grid=(B, nct),
        in_specs=[uspec(0), uspec(1), uspec(2), wspec(0), wspec(1), wspec(2),
                  bspec(0), bspec(1), bspec(2)],
        out_specs=[ospec, ospec],
        out_shape=[jax.ShapeDtypeStruct((B, S, HY_WIDTH), F32)] * 2,
        compiler_params=_cparams(("parallel", "parallel")),
        name="hy_gate",
    )(u3, u3, u3, conv_w, conv_w, conv_w, conv_b2, conv_b2, conv_b2)


def _attn_kernel(q_ref, k_ref, v_ref, lq1_ref, lk1_ref, lq2_ref, lk2_ref, g_ref, o_ref,
                 s_scr, p_scr, vx_scr, *, lam_init, tq):
    S = q_ref.shape[0]
    nblk = S // tq
    lam = (jnp.exp(jnp.sum(lq1_ref[...] * lk1_ref[...], keepdims=True))
           - jnp.exp(jnp.sum(lq2_ref[...] * lk2_ref[...], keepdims=True)) + lam_init)
    lane = lax.broadcasted_iota(jnp.int32, (tq, LANES), 1)

    def blk(i):
        return pl.ds(pl.multiple_of(i * tq, tq), tq)

    def scores(i, slot):
        q = q_ref[blk(i), :]
        for c in range(2):
            sel = (lane < ATT_HEAD_DIM) if c == 0 else (lane >= ATT_HEAD_DIM)
            qc = jnp.where(sel, q, jnp.zeros_like(q))
            s_scr[slot, c] = lax.dot_general(qc, k_ref[...], (((1,), (1,)), ((), ())),
                                             preferred_element_type=F32)

    vx_scr[:, :LANES] = v_ref[...]
    vx_scr[:, LANES:] = (lax.broadcasted_iota(jnp.int32, (S, LANES), 1) == 0).astype(BF16)

    def softmax(slot):
        for c in range(2):
            s = s_scr[slot, c]
            m = jnp.max(s, axis=-1, keepdims=True)
            p_scr[slot, c] = jnp.exp2(s - m).astype(BF16)

    def values(i, slot):
        oe1 = jnp.dot(p_scr[slot, 0], vx_scr[...], preferred_element_type=F32)
        oe2 = jnp.dot(p_scr[slot, 1], vx_scr[...], preferred_element_type=F32)
        o = oe1[:, :LANES] / oe1[:, LANES:LANES + 1] - oe2[:, :LANES] * (lam / oe2[:, LANES:LANES + 1])
        o = o * lax.rsqrt(jnp.mean(o * o, axis=-1, keepdims=True) + RMS_EPS) * g_ref[...]
        o_ref[blk(i), :] = o * (1.0 - lam_init)

    scores(0, 0)
    softmax(0)
    scores(1, 1)

    def pair(j, carry):
        t = 2 * j
        values(t - 2, 0)
        softmax(1)
        scores(t, 0)
        values(t - 1, 1)
        softmax(0)
        scores(t + 1, 1)
        return carry

    lax.fori_loop(1, nblk // 2, pair, 0)
    values(nblk - 2, 0)
    softmax(1)
    values(nblk - 1, 1)


def _rope_tables(seq):
    pos = np.arange(seq, dtype=np.float64)
    inv_freq = np.power(ROPE_THETA, -np.arange(0, ROPE_DIM, 2, dtype=np.float64) / ROPE_DIM)
    ang = pos[:, None] * inv_freq[None, :]
    half = ROPE_DIM // 2
    cos_f = np.ones((seq, LANES), np.float32)
    sin_a = np.zeros((seq, LANES), np.float32)
    sin_b = np.zeros((seq, LANES), np.float32)
    for base in range(0, LANES, ATT_HEAD_DIM):
        cos_f[:, base:base + half] = np.cos(ang)
        cos_f[:, base + half:base + ROPE_DIM] = np.cos(ang)
        sin_a[:, base:base + half] = -np.sin(ang)
        sin_b[:, base + half:base + ROPE_DIM] = np.sin(ang)
    return jnp.asarray(cos_f), jnp.asarray(sin_a), jnp.asarray(sin_b)


def _diff_attention(qkv3, lq1, lk1, lq2, lk2, subln_g, lam_init, tq=256):
    B, S, _ = qkv3.shape
    nb = ATT_WIDTH // LANES
    assert S % (2 * tq) == 0
    spec = lambda part: pl.BlockSpec((None, S, LANES), lambda b, h: (b, 0, part * nb + h))
    vec = pl.BlockSpec((1, ATT_HEAD_DIM), lambda b, h: (0, 0))
    r1 = lambda a: a.reshape(1, -1)
    return pl.pallas_call(
        functools.partial(_attn_kernel, lam_init=lam_init, tq=tq),
        grid=(B, ATT_HEADS),
        in_specs=[spec(0), spec(1), spec(2), vec, vec, vec, vec,
                  pl.BlockSpec((1, LANES), lambda b, h: (0, 0))],
        out_specs=pl.BlockSpec((None, S, LANES), lambda b, h: (b, 0, h)),
        out_shape=jax.ShapeDtypeStruct((B, S, ATT_WIDTH), F32),
        scratch_shapes=[pltpu.VMEM((2, 2, tq, S), F32), pltpu.VMEM((2, 2, tq, S), BF16),
                        pltpu.VMEM((S, 2 * LANES), BF16)],
        compiler_params=_cparams(("parallel", "parallel")),
        name="diff_attn",
    )(qkv3, qkv3, qkv3, r1(lq1), r1(lk1), r1(lq2), r1(lk2), r1(subln_g))


def _pool_kernel(u_ref, w_ref, b_ref, sc_ref, o_ref):
    ct = pl.program_id(1)
    x = u_ref[...]
    S = x.shape[0]
    row = lax.broadcasted_iota(jnp.int32, x.shape, 0)
    lane = lax.broadcasted_iota(jnp.int32, x.shape, 1)

    def shifted(d):
        if d == 0:
            return x
        r = pltpu.roll(x, (-d) % S, 0)
        return jnp.where((row + d >= 0) & (row + d < S), r, 0.0)

    sums = {}
    acc = shifted(-1) + x
    sums[2] = acc
    lo, hi = -1, 0
    for w in POOL_WINDOWS[1:]:
        for d in list(range(-(w // 2), lo)) + list(range(hi + 1, w // 2)):
            acc = acc + shifted(d)
        lo, hi = -(w // 2), w // 2 - 1
        sums[w] = acc
    grp = 2 * ct + (lane >= POOL_GROUP).astype(jnp.int32)
    win_sum = sums[POOL_WINDOWS[-1]]
    half = jnp.full(x.shape, POOL_WINDOWS[-1] // 2, jnp.int32)
    for g in range(len(POOL_WINDOWS) - 2, -1, -1):
        win_sum = jnp.where(grp == g, sums[POOL_WINDOWS[g]], win_sum)
        half = jnp.where(grp == g, POOL_WINDOWS[g] // 2, half)
    cnt = jnp.minimum(row + half - 1, S - 1) - jnp.maximum(row - half, 0) + 1
    y = win_sum / cnt.astype(F32) - x
    y = _dot3(*_split(y), *_split(w_ref[...])) + b_ref[...]
    o_ref[...] = y * sc_ref[...]


def _pool_mixer(u3, w, b, scale):
    B, S, _ = u3.shape
    nct = POOL_WIDTH // LANES
    gpt = LANES // POOL_GROUP
    wbd = jnp.zeros((nct, LANES, LANES), F32)
    for g in range(len(POOL_WINDOWS)):
        t, o = divmod(g, gpt)
        wbd = wbd.at[t, o * POOL_GROUP:(o + 1) * POOL_GROUP, o * POOL_GROUP:(o + 1) * POOL_GROUP].set(w[g])
    vspec = pl.BlockSpec((1, LANES), lambda bb, c: (0, c))
    return pl.pallas_call(
        _pool_kernel,
        grid=(B, nct),
        in_specs=[pl.BlockSpec((None, S, LANES), lambda bb, c: (bb, 0, c)),
                  pl.BlockSpec((None, LANES, LANES), lambda bb, c: (c, 0, 0)),
                  vspec, vspec],
        out_specs=pl.BlockSpec((None, S, LANES), lambda bb, c: (bb, 0, c)),
        out_shape=jax.ShapeDtypeStruct((B, S, POOL_WIDTH), F32),
        compiler_params=_cparams(("parallel", "parallel")),
        name="pool_mixer",
    )(u3, wbd, b.reshape(1, POOL_WIDTH), scale.reshape(1, POOL_WIDTH))


PAIRS_PER_GROUP = 6
N_CLASSES = N_GROUPS * PAIRS_PER_GROUP
ROUTE_W_LO, ROUTE_W_HI, ROUTE_CLS, ROUTE_RANK = 0, 1, 2, 3
TOKEN_ROWS = D_MODEL // LANES


def _store_token_major(ref, x, first_row, unit_rows):
    n = x.shape[0]
    for j in range(TOKEN_ROWS):
        ref[pl.ds(first_row + j, n, stride=unit_rows), :] = x[:, j * LANES:(j + 1) * LANES]


def _load_token_major(ref, n, first_row, unit_rows):
    return jnp.concatenate([ref[pl.ds(first_row + j, n, stride=unit_rows), :] for j in range(TOKEN_ROWS)],
                           axis=1)


def _class_experts(c):
    g, pidx = divmod(c, PAIRS_PER_GROUP)
    pairs = [(a, b) for a in range(EXPERTS_PER_GROUP) for b in range(a + 1, EXPERTS_PER_GROUP)]
    lo, hi = pairs[pidx]
    return g * EXPERTS_PER_GROUP + lo, g * EXPERTS_PER_GROUP + hi


def _outproj_kernel(h_ref, yh_ref, ya_ref, yp_ref, wh_ref, wa_ref, wp_ref, g_ref, b_ref,
                    wgh_ref, wgl_ref, bg_ref, tri_ref, h1_ref, h1t_ref, route_ref, cnt_ref, carry_scr):
    d = functools.partial(jnp.dot, preferred_element_type=F32)
    mix = (d(yh_ref[...].astype(BF16), wh_ref[...]) + d(ya_ref[...].astype(BF16), wa_ref[...])
           + d(yp_ref[...].astype(BF16), wp_ref[...]))
    h1 = _ln_rows(DN_ALPHA * h_ref[...] + mix, g_ref[...], b_ref[...])
    h1_ref[...] = h1
    _store_token_major(h1t_ref, h1, 0, TOKEN_ROWS)

    @pl.when(pl.program_id(0) == 0)
    def _():
        carry_scr[...] = jnp.zeros_like(carry_scr)

    logit = _dot3(*_split(h1), wgh_ref[...], wgl_ref[...]) + bg_ref[...]
    lni = lax.broadcasted_iota(jnp.int32, logit.shape, 1)
    ln = lni.astype(F32)
    grp = lax.shift_right_arithmetic(lni - GATE_FINE_LANE, 2).astype(F32)
    first = lambda mask: jnp.min(jnp.where(mask, ln, float(LANES)), axis=-1, keepdims=True)
    cmask = lni < N_GROUPS
    lc = jnp.where(cmask, logit, NEG_BIG)
    mc = jnp.max(lc, axis=-1, keepdims=True)
    gw = 1.0 / jnp.sum(jnp.where(cmask, jnp.exp(lc - mc), 0.0), axis=-1, keepdims=True)
    gi = first(cmask & (lc == mc))
    fmask = (lni >= GATE_FINE_LANE) & (lni < GATE_FINE_LANE + N_EXPERTS) & (grp == gi)
    lf = jnp.where(fmask, logit, NEG_BIG)
    m1 = jnp.max(lf, axis=-1, keepdims=True)
    i1 = first(fmask & (lf == m1))
    rest = fmask & (ln != i1)
    lf2 = jnp.where(rest, logit, NEG_BIG)
    m2 = jnp.max(lf2, axis=-1, keepdims=True)
    i2 = first(rest & (lf2 == m2))
    e2 = jnp.exp(m2 - m1)
    w1 = gw / (1.0 + e2)
    w2 = gw * e2 / (1.0 + e2)

    j1 = i1 - GATE_FINE_LANE - EXPERTS_PER_GROUP * gi
    j2 = i2 - GATE_FINE_LANE - EXPERTS_PER_GROUP * gi
    lo = jnp.minimum(j1, j2)
    hi = jnp.maximum(j1, j2)
    w_lo = jnp.where(j1 < j2, w1, w2)
    w_hi = jnp.where(j1 < j2, w2, w1)
    base = jnp.where(lo == 0.0, 0.0, jnp.where(lo == 1.0, 3.0, 5.0))
    cls = gi * PAIRS_PER_GROUP + base + hi - lo - 1.0
    onehot = ln == cls
    before = jnp.dot(tri_ref[...], onehot.astype(BF16), preferred_element_type=F32)
    carry = carry_scr[...]
    rank = jnp.sum(jnp.where(onehot, before + carry, 0.0), axis=-1, keepdims=True)
    carry = carry + jnp.sum(onehot.astype(F32), axis=0, keepdims=True)
    carry_scr[...] = carry
    cnt_ref[...] = carry
    route_ref[...] = jnp.where(lni == ROUTE_W_LO, w_lo,
                               jnp.where(lni == ROUTE_W_HI, w_hi,
                                         jnp.where(lni == ROUTE_CLS, cls,
                                                   jnp.where(lni == ROUTE_RANK, rank, 0.0))))


def _out_proj_ln_route(h2, yh, ya, yp, w_out_bf, g, b, wgc, bgc, wgf, bgf, tm=512):
    T, D = h2.shape
    tri = jnp.asarray(np.tril(np.ones((tm, tm), np.float32), -1)).astype(BF16)
    wg = jnp.zeros((D, LANES), F32).at[:, :N_GROUPS].set(wgc)
    wg = wg.at[:, GATE_FINE_LANE:GATE_FINE_LANE + N_EXPERTS].set(wgf)
    bg = jnp.zeros((1, LANES), F32).at[0, :N_GROUPS].set(bgc)
    bg = bg.at[0, GATE_FINE_LANE:GATE_FINE_LANE + N_EXPERTS].set(bgf)
    wgh, wgl = _split(wg)
    o1, o2 = HY_WIDTH, HY_WIDTH + ATT_WIDTH
    row = lambda c: pl.BlockSpec((tm, c), lambda i: (i, 0))
    full = lambda r, c: pl.BlockSpec((r, c), lambda i: (0, 0))
    return pl.pallas_call(
        _outproj_kernel,
        grid=(T // tm,),
        in_specs=[row(D), row(HY_WIDTH), row(ATT_WIDTH), row(POOL_WIDTH),
                  full(HY_WIDTH, D), full(ATT_WIDTH, D), full(POOL_WIDTH, D),
                  full(1, D), full(1, D), full(D, LANES), full(D, LANES), full(1, LANES), full(tm, tm)],
        out_specs=[row(D), pl.BlockSpec((tm * TOKEN_ROWS, LANES), lambda i: (i, 0)), row(LANES), full(1, LANES)],
        out_shape=[jax.ShapeDtypeStruct((T, D), F32), jax.ShapeDtypeStruct((T * TOKEN_ROWS, LANES), F32),
                   jax.ShapeDtypeStruct((T, LANES), F32), jax.ShapeDtypeStruct((1, LANES), F32)],
        scratch_shapes=[pltpu.VMEM((1, LANES), F32)],
        compiler_params=_cparams(("arbitrary",)),
        name="out_proj_ln_route",
    )(h2, yh, ya, yp, w_out_bf[:o1], w_out_bf[o1:o2], w_out_bf[o2:], g.reshape(1, D), b.reshape(1, D),
      wgh, wgl, bg, tri)


MOE_TILE = 256
DMA_BATCH = 256


def _moe_rows_padded(T):
    return T + N_CLASSES * MOE_TILE


def _token_rows(ref, t, unit_rows):
    return ref.at[pl.ds(pl.multiple_of(t * unit_rows, unit_rows), unit_rows)]


DMA_GROUP = 8


def _start_tokens(n, slot_of, start_copy):
    def group(g, carry):
        first = g * DMA_GROUP
        slots = [slot_of(first + k) for k in range(DMA_GROUP)]
        for k in range(DMA_GROUP):
            start_copy(first + k, slots[k], k % 2)
        return carry

    lax.fori_loop(0, n // DMA_GROUP, group, 0)


def _wait_tokens(n, wait_copy):
    def wait(r, carry):
        wait_copy(0)
        wait_copy(1)
        return carry

    lax.fori_loop(0, n // 2, wait, 0, unroll=4)


def _dispatch_kernel(cls_ref, rank_ref, cnt_ref, h1t_ref, xs_ref, dest_ref, tlo_ref, thi_ref, tval_ref,
                     off_scr, zero_scr, sem, *, tm, n_tiles):
    i = pl.program_id(0)
    base = i * tm
    tile_rows = MOE_TILE * TOKEN_ROWS

    def fill_copy(tile):
        rows = pl.ds(pl.multiple_of(tile * tile_rows, tile_rows), tile_rows)
        return pltpu.make_async_copy(zero_scr, xs_ref.at[rows], sem.at[2])

    @pl.when(i == 0)
    def _():
        zero_scr[...] = jnp.zeros_like(zero_scr)
        off = jnp.int32(0)
        tile = jnp.int32(0)
        fills = []
        for c in range(N_CLASSES):
            n = cnt_ref[c]
            nt = lax.shift_right_logical(n + (MOE_TILE - 1), MOE_TILE.bit_length() - 1)
            off_scr[c] = off
            e_lo, e_hi = _class_experts(c)

            def mark(k, carry, tile=tile, e_lo=e_lo, e_hi=e_hi):
                tlo_ref[tile + k] = e_lo
                thi_ref[tile + k] = e_hi
                tval_ref[tile + k] = 1
                return carry

            lax.fori_loop(0, nt, mark, 0)
            fill = fill_copy(tile + nt - 1)
            fills.append((nt > 0, fill))

            @pl.when(nt > 0)
            def _(fill=fill):
                fill.start()

            off = off + nt * MOE_TILE
            tile = tile + nt

        def unused(k, carry):
            tlo_ref[k] = 0
            thi_ref[k] = 0
            tval_ref[k] = 0
            fill_copy(k).start()
            return carry

        lax.fori_loop(tile, n_tiles, unused, 0)
        for used, fill in fills:
            @pl.when(used)
            def _(fill=fill):
                fill.wait()

        def unused_wait(k, carry):
            fill_copy(k).wait()
            return carry

        lax.fori_loop(tile, n_tiles, unused_wait, 0)

    def slot(r, carry):
        t = base + r
        dest_ref[t] = off_scr[cls_ref[t]] + rank_ref[t]
        return carry

    lax.fori_loop(0, tm, slot, 0, unroll=8)

    def start_copy(t, d, prio):
        pltpu.make_async_copy(_token_rows(h1t_ref, t, TOKEN_ROWS), _token_rows(xs_ref, d, TOKEN_ROWS),
                              sem.at[prio]).start(priority=prio)

    def wait_copy(prio):
        pltpu.make_async_copy(_token_rows(h1t_ref, 0, TOKEN_ROWS), _token_rows(xs_ref, 0, TOKEN_ROWS),
                              sem.at[prio]).wait()

    _start_tokens(tm, lambda r: dest_ref[base + r], lambda r, d, prio: start_copy(base + r, d, prio))

    @pl.when(i > 0)
    def _():
        _wait_tokens(tm, wait_copy)

    @pl.when(i == pl.num_programs(0) - 1)
    def _():
        _wait_tokens(tm, wait_copy)


def _moe_dispatch(h1t, cls_i, rank_i, cnt_i, tm=512):
    T = h1t.shape[0] // TOKEN_ROWS
    slots = _moe_rows_padded(T)
    n_tiles = slots // MOE_TILE
    smem = pl.BlockSpec(memory_space=pltpu.SMEM)
    return pl.pallas_call(
        functools.partial(_dispatch_kernel, tm=tm, n_tiles=n_tiles),
        grid_spec=pltpu.PrefetchScalarGridSpec(
            num_scalar_prefetch=3,
            grid=(T // tm,),
            in_specs=[pl.BlockSpec(memory_space=pl.ANY)],
            out_specs=[pl.BlockSpec(memory_space=pl.ANY), smem, smem, smem, smem],
            scratch_shapes=[pltpu.SMEM((N_CLASSES,), jnp.int32),
                            pltpu.VMEM((MOE_TILE * TOKEN_ROWS, LANES), F32),
                            pltpu.SemaphoreType.DMA((3,))],
        ),
        out_shape=[jax.ShapeDtypeStruct((slots * TOKEN_ROWS, LANES), F32), jax.ShapeDtypeStruct((T,), jnp.int32),
                   jax.ShapeDtypeStruct((n_tiles,), jnp.int32), jax.ShapeDtypeStruct((n_tiles,), jnp.int32),
                   jax.ShapeDtypeStruct((n_tiles,), jnp.int32)],
        compiler_params=_cparams(("arbitrary",)),
        name="moe_dispatch",
    )(cls_i, rank_i, cnt_i, h1t)


def _experts_kernel(tlo_ref, thi_ref, tval_ref, xs_ref, w1l_ref, w3l_ref, w2l_ref, w1h_ref, w3h_ref, w2h_ref,
                    ys_ref):
    j = pl.program_id(0)
    d = functools.partial(jnp.dot, preferred_element_type=F32)

    @pl.when(tval_ref[j] == 1)
    def _():
        xb = _load_token_major(xs_ref, MOE_TILE, 0, TOKEN_ROWS).astype(BF16)

        def expert(w1_ref, w3_ref, w2_ref):
            a = d(xb, w1_ref[...])
            c = d(xb, w3_ref[...])
            return d((a * jax.nn.sigmoid(a) * c).astype(BF16), w2_ref[...])

        _store_token_major(ys_ref, expert(w1l_ref, w3l_ref, w2l_ref), 0, 2 * TOKEN_ROWS)
        _store_token_major(ys_ref, expert(w1h_ref, w3h_ref, w2h_ref), TOKEN_ROWS, 2 * TOKEN_ROWS)

    @pl.when(tval_ref[j] == 0)
    def _():
        ys_ref[...] = jnp.zeros_like(ys_ref)


def _moe_experts(xs, tlo, thi, tval, w1_bf, w3_bf, w2_bf):
    slots = xs.shape[0] // TOKEN_ROWS
    D = D_MODEL
    wspec = lambda shape, which: pl.BlockSpec(
        (None,) + shape, (lambda j, tlo, thi, tval: (tlo[j], 0, 0)) if which == 0
        else (lambda j, tlo, thi, tval: (thi[j], 0, 0)))
    up = (D, D_EXPERT)
    down = (D_EXPERT, D)
    return pl.pallas_call(
        _experts_kernel,
        grid_spec=pltpu.PrefetchScalarGridSpec(
            num_scalar_prefetch=3,
            grid=(slots // MOE_TILE,),
            in_specs=[pl.BlockSpec((MOE_TILE * TOKEN_ROWS, LANES), lambda j, *_: (j, 0)),
                      wspec(up, 0), wspec(up, 0), wspec(down, 0), wspec(up, 1), wspec(up, 1), wspec(down, 1)],
            out_specs=pl.BlockSpec((MOE_TILE * 2 * TOKEN_ROWS, LANES), lambda j, *_: (j, 0)),
        ),
        out_shape=jax.ShapeDtypeStruct((slots * 2 * TOKEN_ROWS, LANES), F32),
        compiler_params=_cparams(("arbitrary",)),
        name="moe_experts",
    )(tlo, thi, tval, xs, w1_bf, w3_bf, w2_bf, w1_bf, w3_bf, w2_bf)


def _combine_kernel(dest_ref, h1_ref, route_ref, ys_ref, p_ref, pwg_ref, pbg_ref, pwp_ref, g_ref, b_ref,
                    o_ref, y_scr, sem, *, tm):
    base = pl.program_id(0) * tm
    d = functools.partial(jnp.dot, preferred_element_type=F32)
    unit = 2 * TOKEN_ROWS

    def start_copy(r, d, prio):
        pltpu.make_async_copy(_token_rows(ys_ref, d, unit), _token_rows(y_scr, r, unit),
                              sem.at[prio]).start(priority=prio)

    def wait_copy(prio):
        pltpu.make_async_copy(_token_rows(ys_ref, 0, unit), _token_rows(y_scr, 0, unit), sem.at[prio]).wait()

    h1 = h1_ref[...]
    for b0 in range(0, tm, DMA_BATCH):
        _start_tokens(DMA_BATCH, lambda r: dest_ref[base + b0 + r],
                      lambda r, d, prio: start_copy(b0 + r, d, prio))
        if b0 == 0:
            z = d(h1.astype(BF16), pwg_ref[...]) + pbg_ref[...]
            y_ple = jax.nn.sigmoid(z) * d(p_ref[...].astype(BF16), pwp_ref[...])
        _wait_tokens(DMA_BATCH, wait_copy)
    rec = route_ref[...]
    y_moe = (rec[:, ROUTE_W_LO:ROUTE_W_LO + 1] * _load_token_major(y_scr, tm, 0, unit)
             + rec[:, ROUTE_W_HI:ROUTE_W_HI + 1] * _load_token_major(y_scr, tm, TOKEN_ROWS, unit))
    o_ref[...] = _ln_rows(DN_ALPHA * h1 + y_moe + y_ple, g_ref[...], b_ref[...])


def _moe_combine_ple_ln(h1, route, ys, dest, p2, pwg_bf, pbg, pwp_bf, g, b, tm=512):
    T, D = h1.shape
    row = lambda c: pl.BlockSpec((tm, c), lambda i, *_: (i, 0))
    full = lambda r, c: pl.BlockSpec((r, c), lambda i, *_: (0, 0))
    return pl.pallas_call(
        functools.partial(_combine_kernel, tm=tm),
        grid_spec=pltpu.PrefetchScalarGridSpec(
            num_scalar_prefetch=1,
            grid=(T // tm,),
            in_specs=[row(D), row(LANES), pl.BlockSpec(memory_space=pl.ANY), row(PLE_DIM),
                      full(D, D), full(1, D), full(PLE_DIM, D), full(1, D), full(1, D)],
            out_specs=row(D),
            scratch_shapes=[pltpu.VMEM((tm * 2 * TOKEN_ROWS, LANES), F32), pltpu.SemaphoreType.DMA((2,))],
        ),
        out_shape=jax.ShapeDtypeStruct((T, D), F32),
        compiler_params=_cparams(("arbitrary",)),
        name="moe_combine_ple_ln",
    )(dest, h1, route, ys, p2, pwg_bf, pbg.reshape(1, D), pwp_bf, g.reshape(1, D), b.reshape(1, D))


def _hyena_mixer(u3, kr, ki, consts, conv_w, conv_b, bias_d, n1_total):
    B, S, _ = u3.shape
    vv, x0c = _hy_gate(u3, conv_w, conv_b)
    y = _hy_conv(vv, x0c, kr, ki, consts, bias_d, n1_total)
    return y.reshape(B * S, HY_WIDTH)


def kernel(x, p, ln0_g, ln0_b, w_in, hy_conv_w, hy_conv_b, hy_fw1, hy_fb1, hy_freq1, hy_fw2, hy_fb2, hy_freq2, hy_fw3, hy_bias, att_lq1, att_lk1, att_lq2, att_lk2, att_subln_g, pool_w, pool_b, pool_scale, w_out, ln1_g, ln1_b, moe_wgc, moe_bgc, moe_wgf, moe_bgf, moe_w1, moe_w3, moe_w2, ple_wg, ple_bg, ple_wp, ln2_g, ln2_b):
    B, S, D = x.shape
    L = w_in.shape[0]
    T = B * S
    n1_total = 2 * S // FFT_N2
    C = HY_WIDTH

    kfilt, ssq = _hy_filter(S, hy_fw1, hy_fb1, hy_freq1, hy_fw2, hy_fb2, hy_freq2, hy_fw3)
    kar, kai = _stage_a(kfilt.reshape(L, n1_total, FFT_N2 * C), n1_total)
    k1p = kar.shape[1]
    consts = _stage_b_consts(n1_total)
    kr_all, ki_all = _stage_b_filter(kar.reshape(L, k1p, FFT_N2, C), kai.reshape(L, k1p, FFT_N2, C),
                                     ssq, consts, n1_total)
    tables = _rope_tables(S)

    h = _layer_norm(x.reshape(T, D), ln0_g, ln0_b)
    for i in range(L):
        lam_init = 0.8 - 0.6 * math.exp(-0.3 * i)
        uh, qkv, up = _in_proj(h, w_in[i].astype(BF16), tables, S)
        y_hy = _hyena_mixer(uh.reshape(B, S, 3 * C), kr_all[i], ki_all[i], consts,
                            hy_conv_w[i], hy_conv_b[i], hy_bias[i], n1_total)
        y_att = _diff_attention(qkv.reshape(B, S, 3 * ATT_WIDTH), att_lq1[i], att_lk1[i], att_lq2[i],
                                att_lk2[i], att_subln_g[i], lam_init).reshape(T, ATT_WIDTH)
        y_pool = _pool_mixer(up.reshape(B, S, POOL_WIDTH), pool_w[i], pool_b[i],
                             pool_scale[i]).reshape(T, POOL_WIDTH)
        h1, h1t, route, counts = _out_proj_ln_route(h, y_hy, y_att, y_pool, w_out[i].astype(BF16), ln1_g[i],
                                                    ln1_b[i], moe_wgc[i], moe_bgc[i], moe_wgf[i], moe_bgf[i])
        cls_i = route[:, ROUTE_CLS].astype(jnp.int32)
        rank_i = route[:, ROUTE_RANK].astype(jnp.int32)
        xs, dest, tlo, thi, tval = _moe_dispatch(h1t, cls_i, rank_i, counts[0].astype(jnp.int32))
        ys = _moe_experts(xs, tlo, thi, tval, moe_w1[i].astype(BF16), moe_w3[i].astype(BF16),
                          moe_w2[i].astype(BF16))
        h = _moe_combine_ple_ln(h1, route, ys, dest, p[i].reshape(T, PLE_DIM), ple_wg[i].astype(BF16),
                                ple_bg[i], ple_wp[i].astype(BF16), ln2_g[i], ln2_b[i])
    return h.reshape(B, S, D)
```

```python
import functools
import math

import numpy as np
import jax
import jax.numpy as jnp
from jax import lax
from jax.experimental import pallas as pl
from jax.experimental.pallas import tpu as pltpu

F32 = jnp.float32
BF16 = jnp.bfloat16

D_MODEL = 1024
DEPTH = 4
HY_WIDTH = 256
ATT_WIDTH = 512
ATT_HEADS = 4
ATT_HEAD_DIM = 64
POOL_WINDOWS = (2, 4, 8, 16)
POOL_WIDTH = 256
POOL_GROUP = 64
IN_WIDTH = 3 * HY_WIDTH + 3 * ATT_WIDTH + POOL_WIDTH
ROPE_THETA = 500000.0
ROPE_DIM = ATT_HEAD_DIM // 4
HY_EMB = 33
HY_BANDS = (HY_EMB - 1) // 2
HY_FILTER_HIDDEN = 64
HY_DECAY_TARGET = 1e-2
HY_FAST_DECAY = 0.3
HY_SLOW_DECAY = 1.5
N_GROUPS = 4
EXPERTS_PER_GROUP = 4
N_EXPERTS = 16
D_EXPERT = 256
PLE_DIM = 256
LN_EPS = 1e-5
RMS_EPS = 1e-5
DN_ALPHA = (2 * DEPTH) ** 0.25

LANES = 128
FFT_N2 = 128
GATE_COARSE_LANE = 0
GATE_FINE_LANE = N_GROUPS
NEG_BIG = -1e30
LOG2E = 1.4426950408889634
VMEM_LIMIT = 56 * 1024 * 1024


def _cparams(sem):
    return pltpu.CompilerParams(dimension_semantics=sem, vmem_limit_bytes=VMEM_LIMIT)


def _split(x):
    hi = x.astype(BF16)
    lo = (x - hi.astype(F32)).astype(BF16)
    return hi, lo


def _dot3(ah, al, bh, bl):
    d = functools.partial(jnp.dot, preferred_element_type=F32)
    return d(ah, bh) + (d(ah, bl) + d(al, bh))


def _ln_rows(x, g, b):
    mu = jnp.mean(x, axis=-1, keepdims=True)
    xc = x - mu
    var = jnp.mean(xc * xc, axis=-1, keepdims=True)
    return xc * lax.rsqrt(var + LN_EPS) * g + b


def _ln_kernel(x_ref, g_ref, b_ref, o_ref):
    o_ref[...] = _ln_rows(x_ref[...], g_ref[...], b_ref[...])


def _layer_norm(x2, g, b, tm=512):
    T, D = x2.shape
    return pl.pallas_call(
        _ln_kernel,
        grid=(T // tm,),
        in_specs=[pl.BlockSpec((tm, D), lambda i: (i, 0)),
                  pl.BlockSpec((1, D), lambda i: (0, 0)),
                  pl.BlockSpec((1, D), lambda i: (0, 0))],
        out_specs=pl.BlockSpec((tm, D), lambda i: (i, 0)),
        out_shape=jax.ShapeDtypeStruct((T, D), F32),
        compiler_params=_cparams(("parallel",)),
        name="ln0",
    )(x2, g.reshape(1, D), b.reshape(1, D))


def _rope(x, cos_f, sin_a, sin_b):
    half = ROPE_DIM // 2
    return x * cos_f + pltpu.roll(x, LANES - half, 1) * sin_a + pltpu.roll(x, half, 1) * sin_b


def _inproj_kernel(h_ref, w_ref, cos_ref, sa_ref, sb_ref, uh_ref, qkv_ref, up_ref):
    hb = h_ref[...].astype(BF16)
    mm = lambda c0, c1: jnp.dot(hb, w_ref[:, c0:c1], preferred_element_type=F32)
    o_q = 3 * HY_WIDTH
    o_k = o_q + ATT_WIDTH
    o_v = o_k + ATT_WIDTH
    o_p = o_v + ATT_WIDTH
    uh_ref[...] = mm(0, o_q)
    up_ref[...] = mm(o_p, IN_WIDTH)
    cos_f, sin_a, sin_b = cos_ref[...], sa_ref[...], sb_ref[...]
    q = mm(o_q, o_k)
    k = mm(o_k, o_v)
    qscale = ATT_HEAD_DIM ** -0.5 * LOG2E
    for hd in range(ATT_HEADS):
        sl = slice(hd * LANES, (hd + 1) * LANES)
        qkv_ref[:, sl] = (_rope(q[:, sl], cos_f, sin_a, sin_b) * qscale).astype(BF16)
        qkv_ref[:, ATT_WIDTH + hd * LANES:ATT_WIDTH + (hd + 1) * LANES] = \
            _rope(k[:, sl], cos_f, sin_a, sin_b).astype(BF16)
    qkv_ref[:, 2 * ATT_WIDTH:] = mm(o_v, o_p).astype(BF16)


def _in_proj(h2, w_bf, tables, seq, tm=512):
    T, D = h2.shape
    N = w_bf.shape[1]
    spt = seq // tm
    tspec = pl.BlockSpec((tm, LANES), lambda i: (i % spt, 0))
    row = lambda c: pl.BlockSpec((tm, c), lambda i: (i, 0))
    return pl.pallas_call(
        _inproj_kernel,
        grid=(T // tm,),
        in_specs=[row(D), pl.BlockSpec((D, N), lambda i: (0, 0)), tspec, tspec, tspec],
        out_specs=[row(3 * HY_WIDTH), row(3 * ATT_WIDTH), row(POOL_WIDTH)],
        out_shape=[jax.ShapeDtypeStruct((T, 3 * HY_WIDTH), F32),
                   jax.ShapeDtypeStruct((T, 3 * ATT_WIDTH), BF16),
                   jax.ShapeDtypeStruct((T, POOL_WIDTH), F32)],
        compiler_params=_cparams(("parallel",)),
        name="in_proj",
    )(h2, w_bf, *tables)


def _hy_filter_kernel(band_ref, fw1_ref, fb1_ref, fr1_ref, fw2_ref, fb2_ref, fr2_ref, fw3_ref,
                      dl_ref, k_ref, ssq_ref, *, seq, tr):
    j = pl.program_id(1)
    n = j * tr + lax.broadcasted_iota(jnp.int32, (tr, 1), 0)
    pos = jnp.where(n < seq, n, 2 * seq - n).astype(F32)
    t = pos * (1.0 / (seq - 1))
    wpos = (2.0 * math.pi / seq) * pos
    lane = lax.broadcasted_iota(jnp.int32, (tr, LANES), 1)
    arg = wpos * band_ref[...]
    z = jnp.where(lane == 0, t,
                  jnp.where(lane <= HY_BANDS, jnp.cos(arg),
                            jnp.where(lane < HY_EMB, -jnp.sin(arg), 0.0)))
    h1 = _dot3(*_split(z), *_split(fw1_ref[...])) + fb1_ref[...]
    h1 = jnp.sin(fr1_ref[...] * h1)
    h2 = _dot3(*_split(h1), *_split(fw2_ref[...])) + fb2_ref[...]
    h2 = jnp.sin(fr2_ref[...] * h2)
    filt = _dot3(*_split(h2), *_split(fw3_ref[...]))
    window = jnp.exp(-t * jnp.abs(dl_ref[...]))
    kk = jnp.where(n < seq, filt[:, :HY_WIDTH], filt[:, HY_WIDTH:]) * window
    kk = jnp.where(n == seq, 0.0, kk)
    k_ref[...] = kk

    @pl.when(j == 0)
    def _():
        ssq_ref[...] = jnp.zeros_like(ssq_ref)

    ssq_ref[...] += jnp.sum(kk * kk, axis=0, keepdims=True)


def _hy_filter(seq, fw1, fb1, freq1, fw2, fb2, freq2, fw3, tr=1024):
    L = fw1.shape[0]
    n = 2 * seq
    H = HY_FILTER_HIDDEN
    bands = np.linspace(1e-4, HY_BANDS - 1, HY_BANDS)
    bandv = np.zeros((1, LANES), np.float32)
    bandv[0, 1:1 + HY_BANDS] = bands
    bandv[0, 1 + HY_BANDS:HY_EMB] = bands
    fw1p = jnp.zeros((L, LANES, H), F32).at[:, :HY_EMB].set(fw1)
    max_decay = math.log(HY_DECAY_TARGET) / HY_FAST_DECAY
    min_decay = math.log(HY_DECAY_TARGET) / HY_SLOW_DECAY
    deltas = np.linspace(min_decay, max_decay, HY_WIDTH).astype(np.float32).reshape(1, HY_WIDTH)
    vec = lambda a: a.reshape(L, 1, a.shape[-1])
    lspec = lambda r, c: pl.BlockSpec((None, r, c), lambda l, j: (l, 0, 0))
    return pl.pallas_call(
        functools.partial(_hy_filter_kernel, seq=seq, tr=tr),
        grid=(L, n // tr),
        in_specs=[pl.BlockSpec((1, LANES), lambda l, j: (0, 0)),
                  lspec(LANES, H), lspec(1, H), lspec(1, H),
                  lspec(H, H), lspec(1, H), lspec(1, H),
                  lspec(H, 2 * HY_WIDTH),
                  pl.BlockSpec((1, HY_WIDTH), lambda l, j: (0, 0))],
        out_specs=[pl.BlockSpec((None, tr, HY_WIDTH), lambda l, j: (l, j, 0)),
                   pl.BlockSpec((None, 1, HY_WIDTH), lambda l, j: (l, 0, 0))],
        out_shape=[jax.ShapeDtypeStruct((L, n, HY_WIDTH), F32),
                   jax.ShapeDtypeStruct((L, 1, HY_WIDTH), F32)],
        compiler_params=_cparams(("parallel", "arbitrary")),
        name="hy_filter",
    )(jnp.asarray(bandv), fw1p, vec(fb1), vec(freq1), fw2, vec(fb2), vec(freq2), fw3,
      jnp.asarray(deltas))


def _k1_rows(n1):
    k1 = n1 // 2 + 1
    return k1, -(-k1 // 8) * 8


def _bf_pair(m):
    m32 = jnp.asarray(np.asarray(m, np.float32))
    return _split(m32)


def _stage_a_kernel(x_ref, frh_ref, frl_ref, fih_ref, fil_ref, ar_ref, ai_ref):
    xh, xl = _split(x_ref[...])
    ar_ref[...] = _dot3(frh_ref[...], frl_ref[...], xh, xl)
    ai_ref[...] = _dot3(fih_ref[...], fil_ref[...], xh, xl)


def _stage_a(x3, n1_total, tl=4096):
    Bt, n1u, LN = x3.shape
    k1n, k1p = _k1_rows(n1_total)
    ang = 2.0 * np.pi * np.outer(np.arange(k1p), np.arange(n1u)) / n1_total
    valid = (np.arange(k1p) < k1n)[:, None]
    frh, frl = _bf_pair(np.cos(ang) * valid)
    fih, fil = _bf_pair(-np.sin(ang) * valid)
    cspec = pl.BlockSpec((k1p, n1u), lambda b, j: (0, 0))
    ospec = pl.BlockSpec((None, k1p, tl), lambda b, j: (b, 0, j))
    return pl.pallas_call(
        _stage_a_kernel,
        grid=(Bt, LN // tl),
        in_specs=[pl.BlockSpec((None, n1u, tl), lambda b, j: (b, 0, j)), cspec, cspec, cspec, cspec],
        out_specs=[ospec, ospec],
        out_shape=[jax.ShapeDtypeStruct((Bt, k1p, LN), F32)] * 2,
        compiler_params=_cparams(("parallel", "parallel")),
        name="hy_stage_a",
    )(x3, frh, frl, fih, fil)


def _stage_b_consts(n1_total):
    n = n1_total * FFT_N2
    k1n, k1p = _k1_rows(n1_total)
    ang2 = 2.0 * np.pi * np.outer(np.arange(FFT_N2), np.arange(FFT_N2)) / FFT_N2
    c2, s2 = np.cos(ang2), np.sin(ang2)
    mf = np.block([[c2, s2], [-s2, c2]])
    mi = np.block([[c2, -s2], [s2, c2]])
    angt = 2.0 * np.pi * np.outer(np.arange(k1p), np.arange(FFT_N2)) / n
    twc = jnp.asarray(np.cos(angt).astype(np.float32)).reshape(k1p, FFT_N2, 1)
    tws = jnp.asarray(np.sin(angt).astype(np.float32)).reshape(k1p, FFT_N2, 1)
    return _bf_pair(mf), _bf_pair(mi), twc, tws


def _fwd_b(ar, ai, c, s, mfh, mfl):
    tr_ = ar * c + ai * s
    ti_ = ai * c - ar * s
    xh, xl = _split(jnp.concatenate([tr_, ti_], axis=0))
    z = _dot3(mfh, mfl, xh, xl)
    return z[:FFT_N2], z[FFT_N2:]


def _stage_bk_kernel(ar_ref, ai_ref, c_ref, s_ref, mfh_ref, mfl_ref, ssq_ref, kr_ref, ki_ref,
                     *, k1n, inv_n):
    k1 = pl.program_id(1)

    @pl.when(k1 < k1n)
    def _():
        zr, zi = _fwd_b(ar_ref[...], ai_ref[...], c_ref[...], s_ref[...], mfh_ref[...], mfl_ref[...])
        scale = lax.rsqrt(ssq_ref[...] + 1e-6) * inv_n
        kr_ref[...] = zr * scale
        ki_ref[...] = zi * scale

    @pl.when(k1 >= k1n)
    def _():
        kr_ref[...] = jnp.zeros_like(kr_ref)
        ki_ref[...] = jnp.zeros_like(ki_ref)


def _stage_b_filter(ar4, ai4, ssq, consts, n1_total):
    L, k1p, _, C = ar4.shape
    k1n, _ = _k1_rows(n1_total)
    (mfh, mfl), _, twc, tws = consts
    blk = pl.BlockSpec((None, None, FFT_N2, C), lambda l, k: (l, k, 0, 0))
    tw = pl.BlockSpec((None, FFT_N2, 1), lambda l, k: (k, 0, 0))
    mat = pl.BlockSpec((2 * FFT_N2, 2 * FFT_N2), lambda l, k: (0, 0))
    return pl.pallas_call(
        functools.partial(_stage_bk_kernel, k1n=k1n, inv_n=1.0 / (n1_total * FFT_N2)),
        grid=(L, k1p),
        in_specs=[blk, blk, tw, tw, mat, mat, pl.BlockSpec((None, 1, C), lambda l, k: (l, 0, 0))],
        out_specs=[blk, blk],
        out_shape=[jax.ShapeDtypeStruct(ar4.shape, F32)] * 2,
        compiler_params=_cparams(("parallel", "parallel")),
        name="hy_stage_b_filter",
    )(ar4, ai4, twc, tws, mfh, mfl, ssq)


SUBLANES = 8


def _hy_conv_kernel(vv_ref, x0_ref, kr_ref, ki_ref, tw_ref, kah_ref, kal_ref, kch_ref, kcl_ref,
                    mfh_ref, mfl_ref, mih_ref, mil_ref, bias_ref, o_ref, a_scr, *, n1u, k1n, k1p):
    groups = FFT_N2 // SUBLANES

    def rows(n1, g):
        return pl.ds(pl.multiple_of(n1 * FFT_N2 + g * SUBLANES, SUBLANES), SUBLANES)

    def stage_a(g, carry):
        xg = jnp.concatenate([vv_ref[rows(n1, g), :] for n1 in range(n1u)], axis=0)
        ag = _dot3(kah_ref[...], kal_ref[...], *_split(xg))
        gs = pl.ds(pl.multiple_of(g * SUBLANES, SUBLANES), SUBLANES)
        for part in range(2):
            for k1 in range(k1p):
                r = (part * k1p + k1) * SUBLANES
                a_scr[part, k1, gs, :] = ag[r:r + SUBLANES]
        return carry

    lax.fori_loop(0, groups, stage_a, 0)

    def stage_b(k1, carry):
        tw = tw_ref[k1]
        c = tw[:, 0:1]
        s = tw[:, 1:2]
        zr, zi = _fwd_b(a_scr[0, k1], a_scr[1, k1], c, s, mfh_ref[...], mfl_ref[...])
        kr = kr_ref[k1]
        ki = ki_ref[k1]
        pr = zr * kr - zi * ki
        pi = zr * ki + zi * kr
        y = _dot3(mih_ref[...], mil_ref[...], *_split(jnp.concatenate([pr, pi], axis=0)))
        yr = y[:FFT_N2]
        yi = y[FFT_N2:]
        a_scr[0, k1] = yr * c - yi * s
        a_scr[1, k1] = yi * c + yr * s
        return carry

    lax.fori_loop(0, k1n, stage_b, 0, unroll=3 if k1n % 3 == 0 else 1)

    def stage_c(g, carry):
        gs = pl.ds(pl.multiple_of(g * SUBLANES, SUBLANES), SUBLANES)
        bg = jnp.concatenate([a_scr[part, k1, gs, :] for part in range(2) for k1 in range(k1p)], axis=0)
        yg = _dot3(kch_ref[...], kcl_ref[...], *_split(bg))
        for n1 in range(n1u):
            r = rows(n1, g)
            o_ref[r, :] = (yg[n1 * SUBLANES:(n1 + 1) * SUBLANES] + vv_ref[r, :] * bias_ref[...]) * x0_ref[r, :]
        return carry

    lax.fori_loop(0, groups, stage_c, 0)


def _hy_conv(vv, x0c, kr3, ki3, consts, bias_d, n1_total):
    B, S, C = vv.shape
    n1u = S // FFT_N2
    k1n, k1p = _k1_rows(n1_total)
    (mfh, mfl), (mih, mil), twc, tws = consts
    tw = jnp.concatenate([twc, tws], axis=-1)
    k1 = np.arange(k1p)
    valid = (k1 < k1n)[:, None]
    eye = np.eye(SUBLANES)
    ang_a = 2.0 * np.pi * np.outer(k1, np.arange(n1u)) / n1_total
    ka = np.concatenate([np.kron(np.cos(ang_a) * valid, eye), np.kron(-np.sin(ang_a) * valid, eye)], axis=0)
    w = np.where((k1 == 0) | (k1 == n1_total // 2), 1.0, 2.0) * (k1 < k1n)
    ang_c = 2.0 * np.pi * np.outer(np.arange(n1u), k1) / n1_total
    kc = np.concatenate([np.kron(np.cos(ang_c) * w[None, :], eye), np.kron(-np.sin(ang_c) * w[None, :], eye)], axis=1)
    kah, kal = _bf_pair(ka)
    kch, kcl = _bf_pair(kc)
    nct = C // LANES
    xspec = pl.BlockSpec((None, S, LANES), lambda c, b: (b, 0, c))
    kspec = pl.BlockSpec((k1p, FFT_N2, LANES), lambda c, b: (0, 0, c))
    full = lambda a: pl.BlockSpec(a.shape, lambda c, b: (0,) * a.ndim)
    return pl.pallas_call(
        functools.partial(_hy_conv_kernel, n1u=n1u, k1n=k1n, k1p=k1p),
        grid=(nct, B),
        in_specs=[xspec, xspec, kspec, kspec, full(tw), full(kah), full(kal), full(kch), full(kcl),
                  full(mfh), full(mfl), full(mih), full(mil),
                  pl.BlockSpec((1, LANES), lambda c, b: (0, c))],
        out_specs=xspec,
        out_shape=jax.ShapeDtypeStruct((B, S, C), F32),
        scratch_shapes=[pltpu.VMEM((2, k1p, FFT_N2, LANES), F32)],
        compiler_params=_cparams(("parallel", "parallel")),
        name="hy_conv",
    )(vv, x0c, kr3, ki3, tw, kah, kal, kch, kcl, mfh, mfl, mih, mil, bias_d.reshape(1, C))


def _conv3(x, w_ref, b_ref):
    S = x.shape[0]
    row = lax.broadcasted_iota(jnp.int32, x.shape, 0)
    prev = jnp.where(row == 0, 0.0, pltpu.roll(x, 1, 0))
    nxt = jnp.where(row == S - 1, 0.0, pltpu.roll(x, S - 1, 0))
    return prev * w_ref[0:1, :] + x * w_ref[1:2, :] + nxt * w_ref[2:3, :] + b_ref[...]


def _hy_gate_kernel(x0_ref, x1_ref, v_ref, w0_ref, w1_ref, w2_ref, b0_ref, b1_ref, b2_ref,
                    vv_ref, x0c_ref):
    x0c_ref[...] = _conv3(x0_ref[...], w0_ref, b0_ref)
    vv_ref[...] = _conv3(v_ref[...], w2_ref, b2_ref) * _conv3(x1_ref[...], w1_ref, b1_ref)


def _hy_gate(u3, conv_w, conv_b):
    B, S, _ = u3.shape
    nct = HY_WIDTH // LANES
    conv_b2 = conv_b.reshape(1, 3 * HY_WIDTH)
    uspec = lambda part: pl.BlockSpec((None, S, LANES), lambda b, c: (b, 0, part * nct + c))
    wspec = lambda part: pl.BlockSpec((3, LANES), lambda b, c: (0, part * nct + c))
    bspec = lambda part: pl.BlockSpec((1, LANES), lambda b, c: (0, part * nct + c))
    ospec = pl.BlockSpec((None, S, LANES), lambda b, c: (b, 0, c))
    return pl.pallas_call(
        _hy_gate_kernel,
        grid=(B, nct),
        in_specs=[uspec(0), uspec(1), uspec(2), wspec(0), wspec(1), wspec(2),
                  bspec(0), bspec(1), bspec(2)],
        out_specs=[ospec, ospec],
        out_shape=[jax.ShapeDtypeStruct((B, S, HY_WIDTH), F32)] * 2,
        compiler_params=_cparams(("parallel", "parallel")),
        name="hy_gate",
    )(u3, u3, u3, conv_w, conv_w, conv_w, conv_b2, conv_b2, conv_b2)


def _attn_kernel(q_ref, k_ref, v_ref, lq1_ref, lk1_ref, lq2_ref, lk2_ref, g_ref, o_ref,
                 s_scr, p_scr, vx_scr, *, lam_init, tq):
    S = q_ref.shape[0]
    nblk = S // tq
    lam = (jnp.exp(jnp.sum(lq1_ref[...] * lk1_ref[...], keepdims=True))
           - jnp.exp(jnp.sum(lq2_ref[...] * lk2_ref[...], keepdims=True)) + lam_init)
    lane = lax.broadcasted_iota(jnp.int32, (tq, LANES), 1)

    def blk(i):
        return pl.ds(pl.multiple_of(i * tq, tq), tq)

    def scores(i, slot):
        q = q_ref[blk(i), :]
        for c in range(2):
            sel = (lane < ATT_HEAD_DIM) if c == 0 else (lane >= ATT_HEAD_DIM)
            qc = jnp.where(sel, q, jnp.zeros_like(q))
            s_scr[slot, c] = lax.dot_general(qc, k_ref[...], (((1,), (1,)), ((), ())),
                                             preferred_element_type=F32)

    vx_scr[:, :LANES] = v_ref[...]
    vx_scr[:, LANES:] = (lax.broadcasted_iota(jnp.int32, (S, LANES), 1) == 0).astype(BF16)

    def softmax(slot):
        for c in range(2):
            s = s_scr[slot, c]
            m = jnp.max(s, axis=-1, keepdims=True)
            p_scr[slot, c] = jnp.exp2(s - m).astype(BF16)

    def values(i, slot):
        oe1 = jnp.dot(p_scr[slot, 0], vx_scr[...], preferred_element_type=F32)
        oe2 = jnp.dot(p_scr[slot, 1], vx_scr[...], preferred_element_type=F32)
        o = oe1[:, :LANES] / oe1[:, LANES:LANES + 1] - oe2[:, :LANES] * (lam / oe2[:, LANES:LANES + 1])
        o = o * lax.rsqrt(jnp.mean(o * o, axis=-1, keepdims=True) + RMS_EPS) * g_ref[...]
        o_ref[blk(i), :] = o * (1.0 - lam_init)

    scores(0, 0)
    softmax(0)
    scores(1, 1)

    def pair(j, carry):
        t = 2 * j
        values(t - 2, 0)
        softmax(1)
        scores(t, 0)
        values(t - 1, 1)
        softmax(0)
        scores(t + 1, 1)
        return carry

    lax.fori_loop(1, nblk // 2, pair, 0)
    values(nblk - 2, 0)
    softmax(1)
    values(nblk - 1, 1)


def _rope_tables(seq):
    pos = np.arange(seq, dtype=np.float64)
    inv_freq = np.power(ROPE_THETA, -np.arange(0, ROPE_DIM, 2, dtype=np.float64) / ROPE_DIM)
    ang = pos[:, None] * inv_freq[None, :]
    half = ROPE_DIM // 2
    cos_f = np.ones((seq, LANES), np.float32)
    sin_a = np.zeros((seq, LANES), np.float32)
    sin_b = np.zeros((seq, LANES), np.float32)
    for base in range(0, LANES, ATT_HEAD_DIM):
        cos_f[:, base:base + half] = np.cos(ang)
        cos_f[:, base + half:base + ROPE_DIM] = np.cos(ang)
        sin_a[:, base:base + half] = -np.sin(ang)
        sin_b[:, base + half:base + ROPE_DIM] = np.sin(ang)
    return jnp.asarray(cos_f), jnp.asarray(sin_a), jnp.asarray(sin_b)


def _diff_attention(qkv3, lq1, lk1, lq2, lk2, subln_g, lam_init, tq=256):
    B, S, _ = qkv3.shape
    nb = ATT_WIDTH // LANES
    assert S % (2 * tq) == 0
    spec = lambda part: pl.BlockSpec((None, S, LANES), lambda b, h: (b, 0, part * nb + h))
    vec = pl.BlockSpec((1, ATT_HEAD_DIM), lambda b, h: (0, 0))
    r1 = lambda a: a.reshape(1, -1)
    return pl.pallas_call(
        functools.partial(_attn_kernel, lam_init=lam_init, tq=tq),
        grid=(B, ATT_HEADS),
        in_specs=[spec(0), spec(1), spec(2), vec, vec, vec, vec,
                  pl.BlockSpec((1, LANES), lambda b, h: (0, 0))],
        out_specs=pl.BlockSpec((None, S, LANES), lambda b, h: (b, 0, h)),
        out_shape=jax.ShapeDtypeStruct((B, S, ATT_WIDTH), F32),
        scratch_shapes=[pltpu.VMEM((2, 2, tq, S), F32), pltpu.VMEM((2, 2, tq, S), BF16),
                        pltpu.VMEM((S, 2 * LANES), BF16)],
        compiler_params=_cparams(("parallel", "parallel")),
        name="diff_attn",
    )(qkv3, qkv3, qkv3, r1(lq1), r1(lk1), r1(lq2), r1(lk2), r1(subln_g))


def _pool_kernel(u_ref, w_ref, b_ref, sc_ref, o_ref):
    ct = pl.program_id(1)
    x = u_ref[...]
    S = x.shape[0]
    row = lax.broadcasted_iota(jnp.int32, x.shape, 0)
    lane = lax.broadcasted_iota(jnp.int32, x.shape, 1)

    def shifted(d):
        if d == 0:
            return x
        r = pltpu.roll(x, (-d) % S, 0)
        return jnp.where((row + d >= 0) & (row + d < S), r, 0.0)

    sums = {}
    acc = shifted(-1) + x
    sums[2] = acc
    lo, hi = -1, 0
    for w in POOL_WINDOWS[1:]:
        for d in list(range(-(w // 2), lo)) + list(range(hi + 1, w // 2)):
            acc = acc + shifted(d)
        lo, hi = -(w // 2), w // 2 - 1
        sums[w] = acc
    grp = 2 * ct + (lane >= POOL_GROUP).astype(jnp.int32)
    win_sum = sums[POOL_WINDOWS[-1]]
    half = jnp.full(x.shape, POOL_WINDOWS[-1] // 2, jnp.int32)
    for g in range(len(POOL_WINDOWS) - 2, -1, -1):
        win_sum = jnp.where(grp == g, sums[POOL_WINDOWS[g]], win_sum)
        half = jnp.where(grp == g, POOL_WINDOWS[g] // 2, half)
    cnt = jnp.minimum(row + half - 1, S - 1) - jnp.maximum(row - half, 0) + 1
    y = win_sum / cnt.astype(F32) - x
    y = _dot3(*_split(y), *_split(w_ref[...])) + b_ref[...]
    o_ref[...] = y * sc_ref[...]


def _pool_mixer(u3, w, b, scale):
    B, S, _ = u3.shape
    nct = POOL_WIDTH // LANES
    gpt = LANES // POOL_GROUP
    wbd = jnp.zeros((nct, LANES, LANES), F32)
    for g in range(len(POOL_WINDOWS)):
        t, o = divmod(g, gpt)
        wbd = wbd.at[t, o * POOL_GROUP:(o + 1) * POOL_GROUP, o * POOL_GROUP:(o + 1) * POOL_GROUP].set(w[g])
    vspec = pl.BlockSpec((1, LANES), lambda bb, c: (0, c))
    return pl.pallas_call(
        _pool_kernel,
        grid=(B, nct),
        in_specs=[pl.BlockSpec((None, S, LANES), lambda bb, c: (bb, 0, c)),
                  pl.BlockSpec((None, LANES, LANES), lambda bb, c: (c, 0, 0)),
                  vspec, vspec],
        out_specs=pl.BlockSpec((None, S, LANES), lambda bb, c: (bb, 0, c)),
        out_shape=jax.ShapeDtypeStruct((B, S, POOL_WIDTH), F32),
        compiler_params=_cparams(("parallel", "parallel")),
        name="pool_mixer",
    )(u3, wbd, b.reshape(1, POOL_WIDTH), scale.reshape(1, POOL_WIDTH))


PAIRS_PER_GROUP = 6
N_CLASSES = N_GROUPS * PAIRS_PER_GROUP
ROUTE_W_LO, ROUTE_W_HI, ROUTE_CLS, ROUTE_RANK = 0, 1, 2, 3
TOKEN_ROWS = D_MODEL // LANES


def _store_token_major(ref, x, first_row, unit_rows):
    n = x.shape[0]
    for j in range(TOKEN_ROWS):
        ref[pl.ds(first_row + j, n, stride=unit_rows), :] = x[:, j * LANES:(j + 1) * LANES]


def _load_token_major(ref, n, first_row, unit_rows):
    return jnp.concatenate([ref[pl.ds(first_row + j, n, stride=unit_rows), :] for j in range(TOKEN_ROWS)],
                           axis=1)


def _class_experts(c):
    g, pidx = divmod(c, PAIRS_PER_GROUP)
    pairs = [(a, b) for a in range(EXPERTS_PER_GROUP) for b in range(a + 1, EXPERTS_PER_GROUP)]
    lo, hi = pairs[pidx]
    return g * EXPERTS_PER_GROUP + lo, g * EXPERTS_PER_GROUP + hi


def _outproj_kernel(h_ref, yh_ref, ya_ref, yp_ref, wh_ref, wa_ref, wp_ref, g_ref, b_ref,
                    wgh_ref, wgl_ref, bg_ref, tri_ref, h1_ref, h1t_ref, route_ref, cnt_ref, carry_scr):
    d = functools.partial(jnp.dot, preferred_element_type=F32)
    mix = (d(yh_ref[...].astype(BF16), wh_ref[...]) + d(ya_ref[...].astype(BF16), wa_ref[...])
           + d(yp_ref[...].astype(BF16), wp_ref[...]))
    h1 = _ln_rows(DN_ALPHA * h_ref[...] + mix, g_ref[...], b_ref[...])
    h1_ref[...] = h1
    _store_token_major(h1t_ref, h1, 0, TOKEN_ROWS)

    @pl.when(pl.program_id(0) == 0)
    def _():
        carry_scr[...] = jnp.zeros_like(carry_scr)

    logit = _dot3(*_split(h1), wgh_ref[...], wgl_ref[...]) + bg_ref[...]
    lni = lax.broadcasted_iota(jnp.int32, logit.shape, 1)
    ln = lni.astype(F32)
    grp = lax.shift_right_arithmetic(lni - GATE_FINE_LANE, 2).astype(F32)
    first = lambda mask: jnp.min(jnp.where(mask, ln, float(LANES)), axis=-1, keepdims=True)
    cmask = lni < N_GROUPS
    lc = jnp.where(cmask, logit, NEG_BIG)
    mc = jnp.max(lc, axis=-1, keepdims=True)
    gw = 1.0 / jnp.sum(jnp.where(cmask, jnp.exp(lc - mc), 0.0), axis=-1, keepdims=True)
    gi = first(cmask & (lc == mc))
    fmask = (lni >= GATE_FINE_LANE) & (lni < GATE_FINE_LANE + N_EXPERTS) & (grp == gi)
    lf = jnp.where(fmask, logit, NEG_BIG)
    m1 = jnp.max(lf, axis=-1, keepdims=True)
    i1 = first(fmask & (lf == m1))
    rest = fmask & (ln != i1)
    lf2 = jnp.where(rest, logit, NEG_BIG)
    m2 = jnp.max(lf2, axis=-1, keepdims=True)
    i2 = first(rest & (lf2 == m2))
    e2 = jnp.exp(m2 - m1)
    w1 = gw / (1.0 + e2)
    w2 = gw * e2 / (1.0 + e2)

    j1 = i1 - GATE_FINE_LANE - EXPERTS_PER_GROUP * gi
    j2 = i2 - GATE_FINE_LANE - EXPERTS_PER_GROUP * gi
    lo = jnp.minimum(j1, j2)
    hi = jnp.maximum(j1, j2)
    w_lo = jnp.where(j1 < j2, w1, w2)
    w_hi = jnp.where(j1 < j2, w2, w1)
    base = jnp.where(lo == 0.0, 0.0, jnp.where(lo == 1.0, 3.0, 5.0))
    cls = gi * PAIRS_PER_GROUP + base + hi - lo - 1.0
    onehot = ln == cls
    before = jnp.dot(tri_ref[...], onehot.astype(BF16), preferred_element_type=F32)
    carry = carry_scr[...]
    rank = jnp.sum(jnp.where(onehot, before + carry, 0.0), axis=-1, keepdims=True)
    carry = carry + jnp.sum(onehot.astype(F32), axis=0, keepdims=True)
    carry_scr[...] = carry
    cnt_ref[...] = carry
    route_ref[...] = jnp.where(lni == ROUTE_W_LO, w_lo,
                               jnp.where(lni == ROUTE_W_HI, w_hi,
                                         jnp.where(lni == ROUTE_CLS, cls,
                                                   jnp.where(lni == ROUTE_RANK, rank, 0.0))))


def _out_proj_ln_route(h2, yh, ya, yp, w_out_bf, g, b, wgc, bgc, wgf, bgf, tm=512):
    T, D = h2.shape
    tri = jnp.asarray(np.tril(np.ones((tm, tm), np.float32), -1)).astype(BF16)
    wg = jnp.zeros((D, LANES), F32).at[:, :N_GROUPS].set(wgc)
    wg = wg.at[:, GATE_FINE_LANE:GATE_FINE_LANE + N_EXPERTS].set(wgf)
    bg = jnp.zeros((1, LANES), F32).at[0, :N_GROUPS].set(bgc)
    bg = bg.at[0, GATE_FINE_LANE:GATE_FINE_LANE + N_EXPERTS].set(bgf)
    wgh, wgl = _split(wg)
    o1, o2 = HY_WIDTH, HY_WIDTH + ATT_WIDTH
    row = lambda c: pl.BlockSpec((tm, c), lambda i: (i, 0))
    full = lambda r, c: pl.BlockSpec((r, c), lambda i: (0, 0))
    return pl.pallas_call(
        _outproj_kernel,
        grid=(T // tm,),
        in_specs=[row(D), row(HY_WIDTH), row(ATT_WIDTH), row(POOL_WIDTH),
                  full(HY_WIDTH, D), full(ATT_WIDTH, D), full(POOL_WIDTH, D),
                  full(1, D), full(1, D), full(D, LANES), full(D, LANES), full(1, LANES), full(tm, tm)],
        out_specs=[row(D), pl.BlockSpec((tm * TOKEN_ROWS, LANES), lambda i: (i, 0)), row(LANES), full(1, LANES)],
        out_shape=[jax.ShapeDtypeStruct((T, D), F32), jax.ShapeDtypeStruct((T * TOKEN_ROWS, LANES), F32),
                   jax.ShapeDtypeStruct((T, LANES), F32), jax.ShapeDtypeStruct((1, LANES), F32)],
        scratch_shapes=[pltpu.VMEM((1, LANES), F32)],
        compiler_params=_cparams(("arbitrary",)),
        name="out_proj_ln_route",
    )(h2, yh, ya, yp, w_out_bf[:o1], w_out_bf[o1:o2], w_out_bf[o2:], g.reshape(1, D), b.reshape(1, D),
      wgh, wgl, bg, tri)


MOE_TILE = 256
DMA_BATCH = 256


def _moe_rows_padded(T):
    return T + N_CLASSES * MOE_TILE


def _token_rows(ref, t, unit_rows):
    return ref.at[pl.ds(pl.multiple_of(t * unit_rows, unit_rows), unit_rows)]


DMA_GROUP = 8


def _start_tokens(n, slot_of, start_copy):
    def group(g, carry):
        first = g * DMA_GROUP
        slots = [slot_of(first + k) for k in range(DMA_GROUP)]
        for k in range(DMA_GROUP):
            start_copy(first + k, slots[k], k % 2)
        return carry

    lax.fori_loop(0, n // DMA_GROUP, group, 0)


def _wait_tokens(n, wait_copy):
    def wait(r, carry):
        wait_copy(0)
        wait_copy(1)
        return carry

    lax.fori_loop(0, n // 2, wait, 0, unroll=4)


def _dispatch_kernel(cls_ref, rank_ref, cnt_ref, h1t_ref, xs_ref, dest_ref, tlo_ref, thi_ref, tval_ref,
                     off_scr, zero_scr, sem, *, tm, n_tiles):
    i = pl.program_id(0)
    base = i * tm
    tile_rows = MOE_TILE * TOKEN_ROWS

    def fill_copy(tile):
        rows = pl.ds(pl.multiple_of(tile * tile_rows, tile_rows), tile_rows)
        return pltpu.make_async_copy(zero_scr, xs_ref.at[rows], sem.at[2])

    @pl.when(i == 0)
    def _():
        zero_scr[...] = jnp.zeros_like(zero_scr)
        off = jnp.int32(0)
        tile = jnp.int32(0)
        fills = []
        for c in range(N_CLASSES):
            n = cnt_ref[c]
            nt = lax.shift_right_logical(n + (MOE_TILE - 1), MOE_TILE.bit_length() - 1)
            off_scr[c] = off
            e_lo, e_hi = _class_experts(c)

            def mark(k, carry, tile=tile, e_lo=e_lo, e_hi=e_hi):
                tlo_ref[tile + k] = e_lo
                thi_ref[tile + k] = e_hi
                tval_ref[tile + k] = 1
                return carry

            lax.fori_loop(0, nt, mark, 0)
            fill = fill_copy(tile + nt - 1)
            fills.append((nt > 0, fill))

            @pl.when(nt > 0)
            def _(fill=fill):
                fill.start()

            off = off + nt * MOE_TILE
            tile = tile + nt

        def unused(k, carry):
            tlo_ref[k] = 0
            thi_ref[k] = 0
            tval_ref[k] = 0
            fill_copy(k).start()
            return carry

        lax.fori_loop(tile, n_tiles, unused, 0)
        for used, fill in fills:
            @pl.when(used)
            def _(fill=fill):
                fill.wait()

        def unused_wait(k, carry):
            fill_copy(k).wait()
            return carry

        lax.fori_loop(tile, n_tiles, unused_wait, 0)

    def slot(r, carry):
        t = base + r
        dest_ref[t] = off_scr[cls_ref[t]] + rank_ref[t]
        return carry

    lax.fori_loop(0, tm, slot, 0, unroll=8)

    def start_copy(r, d, prio):
        pltpu.make_async_copy(_token_rows(h1t_ref, r, TOKEN_ROWS), _token_rows(xs_ref, d, TOKEN_ROWS),
                              sem.at[prio]).start(priority=prio)

    def wait_copy(prio):
        pltpu.make_async_copy(_token_rows(h1t_ref, 0, TOKEN_ROWS), _token_rows(xs_ref, 0, TOKEN_ROWS),
                              sem.at[prio]).wait()

    _start_tokens(tm, lambda r: dest_ref[base + r], start_copy)
    _wait_tokens(tm, wait_copy)


def _moe_dispatch(h1t, cls_i, rank_i, cnt_i, tm=512):
    T = h1t.shape[0] // TOKEN_ROWS
    slots = _moe_rows_padded(T)
    n_tiles = slots // MOE_TILE
    smem = pl.BlockSpec(memory_space=pltpu.SMEM)
    return pl.pallas_call(
        functools.partial(_dispatch_kernel, tm=tm, n_tiles=n_tiles),
        grid_spec=pltpu.PrefetchScalarGridSpec(
            num_scalar_prefetch=3,
            grid=(T // tm,),
            in_specs=[pl.BlockSpec((tm * TOKEN_ROWS, LANES), lambda i, *_: (i, 0))],
            out_specs=[pl.BlockSpec(memory_space=pl.ANY), smem, smem, smem, smem],
            scratch_shapes=[pltpu.SMEM((N_CLASSES,), jnp.int32),
                            pltpu.VMEM((MOE_TILE * TOKEN_ROWS, LANES), F32),
                            pltpu.SemaphoreType.DMA((3,))],
        ),
        out_shape=[jax.ShapeDtypeStruct((slots * TOKEN_ROWS, LANES), F32), jax.ShapeDtypeStruct((T,), jnp.int32),
                   jax.ShapeDtypeStruct((n_tiles,), jnp.int32), jax.ShapeDtypeStruct((n_tiles,), jnp.int32),
                   jax.ShapeDtypeStruct((n_tiles,), jnp.int32)],
        compiler_params=_cparams(("arbitrary",)),
        name="moe_dispatch",
    )(cls_i, rank_i, cnt_i, h1t)


def _experts_kernel(tlo_ref, thi_ref, tval_ref, xs_ref, w1l_ref, w3l_ref, w2l_ref, w1h_ref, w3h_ref, w2h_ref,
                    ys_ref):
    j = pl.program_id(0)
    d = functools.partial(jnp.dot, preferred_element_type=F32)

    @pl.when(tval_ref[j] == 1)
    def _():
        xb = _load_token_major(xs_ref, MOE_TILE, 0, TOKEN_ROWS).astype(BF16)

        def expert(w1_ref, w3_ref, w2_ref):
            a = d(xb, w1_ref[...])
            c = d(xb, w3_ref[...])
            return d((a * jax.nn.sigmoid(a) * c).astype(BF16), w2_ref[...])

        _store_token_major(ys_ref, expert(w1l_ref, w3l_ref, w2l_ref), 0, 2 * TOKEN_ROWS)
        _store_token_major(ys_ref, expert(w1h_ref, w3h_ref, w2h_ref), TOKEN_ROWS, 2 * TOKEN_ROWS)

    @pl.when(tval_ref[j] == 0)
    def _():
        ys_ref[...] = jnp.zeros_like(ys_ref)


def _moe_experts(xs, tlo, thi, tval, w1_bf, w3_bf, w2_bf):
    slots = xs.shape[0] // TOKEN_ROWS
    D = D_MODEL
    wspec = lambda shape, which: pl.BlockSpec(
        (None,) + shape, (lambda j, tlo, thi, tval: (tlo[j], 0, 0)) if which == 0
        else (lambda j, tlo, thi, tval: (thi[j], 0, 0)))
    up = (D, D_EXPERT)
    down = (D_EXPERT, D)
    return pl.pallas_call(
        _experts_kernel,
        grid_spec=pltpu.PrefetchScalarGridSpec(
            num_scalar_prefetch=3,
            grid=(slots // MOE_TILE,),
            in_specs=[pl.BlockSpec((MOE_TILE * TOKEN_ROWS, LANES), lambda j, *_: (j, 0)),
                      wspec(up, 0), wspec(up, 0), wspec(down, 0), wspec(up, 1), wspec(up, 1), wspec(down, 1)],
            out_specs=pl.BlockSpec((MOE_TILE * 2 * TOKEN_ROWS, LANES), lambda j, *_: (j, 0)),
        ),
        out_shape=jax.ShapeDtypeStruct((slots * 2 * TOKEN_ROWS, LANES), F32),
        compiler_params=_cparams(("arbitrary",)),
        name="moe_experts",
    )(tlo, thi, tval, xs, w1_bf, w3_bf, w2_bf, w1_bf, w3_bf, w2_bf)


def _combine_kernel(dest_ref, h1_ref, route_ref, ys_ref, p_ref, pwg_ref, pbg_ref, pwp_ref, g_ref, b_ref,
                    o_ref, y_scr, ym_scr, sem, *, tm):
    i = pl.program_id(0)
    d = functools.partial(jnp.dot, preferred_element_type=F32)
    unit = 2 * TOKEN_ROWS

    def gather(step, buf):
        def start_copy(r, d, prio):
            pltpu.make_async_copy(_token_rows(ys_ref, d, unit), _token_rows(y_scr.at[buf], r, unit),
                                  sem.at[2 * buf + prio]).start(priority=prio)
        _start_tokens(tm, lambda r: dest_ref[step * tm + r], start_copy)

    def wait_copy(buf, prio):
        pltpu.make_async_copy(_token_rows(ys_ref, 0, unit), _token_rows(y_scr.at[buf], 0, unit),
                              sem.at[2 * buf + prio]).wait()

    @pl.when(i == 0)
    def _():
        gather(0, 0)

    for buf in range(2):
        @pl.when((i + 1 < pl.num_programs(0)) & ((i + 1) % 2 == buf))
        def _(buf=buf):
            gather(i + 1, buf)

    h1 = h1_ref[...]
    z = d(h1.astype(BF16), pwg_ref[...]) + pbg_ref[...]
    y_ple = jax.nn.sigmoid(z) * d(p_ref[...].astype(BF16), pwp_ref[...])
    rec = route_ref[...]
    for buf in range(2):
        @pl.when(i % 2 == buf)
        def _(buf=buf):
            _wait_tokens(tm, functools.partial(wait_copy, buf))
            ym_scr[...] = (rec[:, ROUTE_W_LO:ROUTE_W_LO + 1] * _load_token_major(y_scr.at[buf], tm, 0, unit)
                           + rec[:, ROUTE_W_HI:ROUTE_W_HI + 1]
                           * _load_token_major(y_scr.at[buf], tm, TOKEN_ROWS, unit))
    y_moe = ym_scr[...]
    o_ref[...] = _ln_rows(DN_ALPHA * h1 + y_moe + y_ple, g_ref[...], b_ref[...])


def _moe_combine_ple_ln(h1, route, ys, dest, p2, pwg_bf, pbg, pwp_bf, g, b, tm=512):
    T, D = h1.shape
    row = lambda c: pl.BlockSpec((tm, c), lambda i, *_: (i, 0))
    full = lambda r, c: pl.BlockSpec((r, c), lambda i, *_: (0, 0))
    return pl.pallas_call(
        functools.partial(_combine_kernel, tm=tm),
        grid_spec=pltpu.PrefetchScalarGridSpec(
            num_scalar_prefetch=1,
            grid=(T // tm,),
            in_specs=[row(D), row(LANES), pl.BlockSpec(memory_space=pl.ANY), row(PLE_DIM),
                      full(D, D), full(1, D), full(PLE_DIM, D), full(1, D), full(1, D)],
            out_specs=row(D),
            scratch_shapes=[pltpu.VMEM((2, tm * 2 * TOKEN_ROWS, LANES), F32), pltpu.VMEM((tm, D), F32),
                            pltpu.SemaphoreType.DMA((4,))],
        ),
        out_shape=jax.ShapeDtypeStruct((T, D), F32),
        compiler_params=_cparams(("arbitrary",)),
        name="moe_combine_ple_ln",
    )(dest, h1, route, ys, p2, pwg_bf, pbg.reshape(1, D), pwp_bf, g.reshape(1, D), b.reshape(1, D))


def _hyena_mixer(u3, kr, ki, consts, conv_w, conv_b, bias_d, n1_total):
    B, S, _ = u3.shape
    vv, x0c = _hy_gate(u3, conv_w, conv_b)
    y = _hy_conv(vv, x0c, kr, ki, consts, bias_d, n1_total)
    return y.reshape(B * S, HY_WIDTH)


def kernel(x, p, ln0_g, ln0_b, w_in, hy_conv_w, hy_conv_b, hy_fw1, hy_fb1, hy_freq1, hy_fw2, hy_fb2, hy_freq2, hy_fw3, hy_bias, att_lq1, att_lk1, att_lq2, att_lk2, att_subln_g, pool_w, pool_b, pool_scale, w_out, ln1_g, ln1_b, moe_wgc, moe_bgc, moe_wgf, moe_bgf, moe_w1, moe_w3, moe_w2, ple_wg, ple_bg, ple_wp, ln2_g, ln2_b):
    B, S, D = x.shape
    L = w_in.shape[0]
    T = B * S
    n1_total = 2 * S // FFT_N2
    C = HY_WIDTH

    kfilt, ssq = _hy_filter(S, hy_fw1, hy_fb1, hy_freq1, hy_fw2, hy_fb2, hy_freq2, hy_fw3)
    kar, kai = _stage_a(kfilt.reshape(L, n1_total, FFT_N2 * C), n1_total)
    k1p = kar.shape[1]
    consts = _stage_b_consts(n1_total)
    kr_all, ki_all = _stage_b_filter(kar.reshape(L, k1p, FFT_N2, C), kai.reshape(L, k1p, FFT_N2, C),
                                     ssq, consts, n1_total)
    tables = _rope_tables(S)

    h = _layer_norm(x.reshape(T, D), ln0_g, ln0_b)
    for i in range(L):
        lam_init = 0.8 - 0.6 * math.exp(-0.3 * i)
        uh, qkv, up = _in_proj(h, w_in[i].astype(BF16), tables, S)
        y_hy = _hyena_mixer(uh.reshape(B, S, 3 * C), kr_all[i], ki_all[i], consts,
                            hy_conv_w[i], hy_conv_b[i], hy_bias[i], n1_total)
        y_att = _diff_attention(qkv.reshape(B, S, 3 * ATT_WIDTH), att_lq1[i], att_lk1[i], att_lq2[i],
                                att_lk2[i], att_subln_g[i], lam_init).reshape(T, ATT_WIDTH)
        y_pool = _pool_mixer(up.reshape(B, S, POOL_WIDTH), pool_w[i], pool_b[i],
                             pool_scale[i]).reshape(T, POOL_WIDTH)
        h1, h1t, route, counts = _out_proj_ln_route(h, y_hy, y_att, y_pool, w_out[i].astype(BF16), ln1_g[i],
                                                    ln1_b[i], moe_wgc[i], moe_bgc[i], moe_wgf[i], moe_bgf[i])
        cls_i = route[:, ROUTE_CLS].astype(jnp.int32)
        rank_i = route[:, ROUTE_RANK].astype(jnp.int32)
        xs, dest, tlo, thi, tval = _moe_dispatch(h1t, cls_i, rank_i, counts[0].astype(jnp.int32))
        ys = _moe_experts(xs, tlo, thi, tval, moe_w1[i].astype(BF16), moe_w3[i].astype(BF16),
                          moe_w2[i].astype(BF16))
        h = _moe_combine_ple_ln(h1, route, ys, dest, p[i].reshape(T, PLE_DIM), ple_wg[i].astype(BF16),
                                ple_bg[i], ple_wp[i].astype(BF16), ln2_g[i], ln2_b[i])
    return h.reshape(B, S, D)
```

```python
import functools
import math

import numpy as np
import jax
import jax.numpy as jnp
from jax import lax
from jax.experimental import pallas as pl
from jax.experimental.pallas import tpu as pltpu

F32 = jnp.float32
BF16 = jnp.bfloat16

D_MODEL = 1024
DEPTH = 4
HY_WIDTH = 256
ATT_WIDTH = 512
ATT_HEADS = 4
ATT_HEAD_DIM = 64
POOL_WINDOWS = (2, 4, 8, 16)
POOL_WIDTH = 256
POOL_GROUP = 64
IN_WIDTH = 3 * HY_WIDTH + 3 * ATT_WIDTH + POOL_WIDTH
ROPE_THETA = 500000.0
ROPE_DIM = ATT_HEAD_DIM // 4
HY_EMB = 33
HY_BANDS = (HY_EMB - 1) // 2
HY_FILTER_HIDDEN = 64
HY_DECAY_TARGET = 1e-2
HY_FAST_DECAY = 0.3
HY_SLOW_DECAY = 1.5
N_GROUPS = 4
EXPERTS_PER_GROUP = 4
N_EXPERTS = 16
D_EXPERT = 256
PLE_DIM = 256
LN_EPS = 1e-5
RMS_EPS = 1e-5
DN_ALPHA = (2 * DEPTH) ** 0.25

LANES = 128
FFT_N2 = 128
GATE_COARSE_LANE = 0
GATE_FINE_LANE = N_GROUPS
NEG_BIG = -1e30
LOG2E = 1.4426950408889634
VMEM_LIMIT = 56 * 1024 * 1024


def _cparams(sem):
    return pltpu.CompilerParams(dimension_semantics=sem, vmem_limit_bytes=VMEM_LIMIT)


def _split(x):
    hi = x.astype(BF16)
    lo = (x - hi.astype(F32)).astype(BF16)
    return hi, lo


def _dot3(ah, al, bh, bl):
    d = functools.partial(jnp.dot, preferred_element_type=F32)
    return d(ah, bh) + (d(ah, bl) + d(al, bh))


def _ln_rows(x, g, b):
    mu = jnp.mean(x, axis=-1, keepdims=True)
    xc = x - mu
    var = jnp.mean(xc * xc, axis=-1, keepdims=True)
    return xc * lax.rsqrt(var + LN_EPS) * g + b


def _ln_kernel(x_ref, g_ref, b_ref, o_ref):
    o_ref[...] = _ln_rows(x_ref[...], g_ref[...], b_ref[...])


def _layer_norm(x2, g, b, tm=512):
    T, D = x2.shape
    return pl.pallas_call(
        _ln_kernel,
        grid=(T // tm,),
        in_specs=[pl.BlockSpec((tm, D), lambda i: (i, 0)),
                  pl.BlockSpec((1, D), lambda i: (0, 0)),
                  pl.BlockSpec((1, D), lambda i: (0, 0))],
        out_specs=pl.BlockSpec((tm, D), lambda i: (i, 0)),
        out_shape=jax.ShapeDtypeStruct((T, D), F32),
        compiler_params=_cparams(("parallel",)),
        name="ln0",
    )(x2, g.reshape(1, D), b.reshape(1, D))


def _rope(x, cos_f, sin_a, sin_b):
    half = ROPE_DIM // 2
    return x * cos_f + pltpu.roll(x, LANES - half, 1) * sin_a + pltpu.roll(x, half, 1) * sin_b


def _inproj_kernel(h_ref, w_ref, cos_ref, sa_ref, sb_ref, uh_ref, qkv_ref, up_ref):
    hb = h_ref[...].astype(BF16)
    mm = lambda c0, c1: jnp.dot(hb, w_ref[:, c0:c1], preferred_element_type=F32)
    o_q = 3 * HY_WIDTH
    o_k = o_q + ATT_WIDTH
    o_v = o_k + ATT_WIDTH
    o_p = o_v + ATT_WIDTH
    uh_ref[...] = mm(0, o_q)
    up_ref[...] = mm(o_p, IN_WIDTH)
    cos_f, sin_a, sin_b = cos_ref[...], sa_ref[...], sb_ref[...]
    q = mm(o_q, o_k)
    k = mm(o_k, o_v)
    qscale = ATT_HEAD_DIM ** -0.5 * LOG2E
    for hd in range(ATT_HEADS):
        sl = slice(hd * LANES, (hd + 1) * LANES)
        qkv_ref[:, sl] = (_rope(q[:, sl], cos_f, sin_a, sin_b) * qscale).astype(BF16)
        qkv_ref[:, ATT_WIDTH + hd * LANES:ATT_WIDTH + (hd + 1) * LANES] = \
            _rope(k[:, sl], cos_f, sin_a, sin_b).astype(BF16)
    qkv_ref[:, 2 * ATT_WIDTH:] = mm(o_v, o_p).astype(BF16)


def _in_proj(h2, w_bf, tables, seq, tm=512):
    T, D = h2.shape
    N = w_bf.shape[1]
    spt = seq // tm
    tspec = pl.BlockSpec((tm, LANES), lambda i: (i % spt, 0))
    row = lambda c: pl.BlockSpec((tm, c), lambda i: (i, 0))
    return pl.pallas_call(
        _inproj_kernel,
        grid=(T // tm,),
        in_specs=[row(D), pl.BlockSpec((D, N), lambda i: (0, 0)), tspec, tspec, tspec],
        out_specs=[row(3 * HY_WIDTH), row(3 * ATT_WIDTH), row(POOL_WIDTH)],
        out_shape=[jax.ShapeDtypeStruct((T, 3 * HY_WIDTH), F32),
                   jax.ShapeDtypeStruct((T, 3 * ATT_WIDTH), BF16),
                   jax.ShapeDtypeStruct((T, POOL_WIDTH), F32)],
        compiler_params=_cparams(("parallel",)),
        name="in_proj",
    )(h2, w_bf, *tables)


def _hy_filter_kernel(band_ref, phase_ref, fw1_ref, fb1_ref, fr1_ref, fw2_ref, fb2_ref, fr2_ref, fw3_ref,
                      dl_ref, k_ref, ssq_ref, z_scr, *, seq, tr):
    j = pl.program_id(0)
    l = pl.program_id(1)
    n = j * tr + lax.broadcasted_iota(jnp.int32, (tr, 1), 0)
    pos = jnp.where(n < seq, n, 2 * seq - n).astype(F32)
    t = pos * (1.0 / (seq - 1))

    @pl.when(l == 0)
    def _():
        wpos = (2.0 * math.pi / seq) * pos
        lane = lax.broadcasted_iota(jnp.int32, (tr, LANES), 1)
        z_scr[...] = jnp.where(lane == 0, t,
                               jnp.where(lane < HY_EMB, jnp.sin(wpos * band_ref[...] + phase_ref[...]), 0.0))

    @pl.when((l == 0) & (j == 0))
    def _():
        ssq_ref[...] = jnp.zeros_like(ssq_ref)

    z = z_scr[...]
    h1 = _dot3(*_split(z), *_split(fw1_ref[...])) + fb1_ref[...]
    h1 = jnp.sin(fr1_ref[...] * h1)
    h2 = _dot3(*_split(h1), *_split(fw2_ref[...])) + fb2_ref[...]
    h2 = jnp.sin(fr2_ref[...] * h2)
    filt = _dot3(*_split(h2), *_split(fw3_ref[...]))
    window = jnp.exp(-t * jnp.abs(dl_ref[...]))
    kk = jnp.where(n < seq, filt[:, :HY_WIDTH], filt[:, HY_WIDTH:]) * window
    kk = jnp.where(n == seq, 0.0, kk)
    k_ref[...] = kk
    ssq_ref[l] += jnp.sum(kk * kk, axis=0, keepdims=True)


def _hy_filter(seq, fw1, fb1, freq1, fw2, fb2, freq2, fw3, tr=1024):
    L = fw1.shape[0]
    n = 2 * seq
    H = HY_FILTER_HIDDEN
    bands = np.linspace(1e-4, HY_BANDS - 1, HY_BANDS)
    bandv = np.zeros((1, LANES), np.float32)
    bandv[0, 1:1 + HY_BANDS] = bands
    bandv[0, 1 + HY_BANDS:HY_EMB] = bands
    phase = np.zeros((1, LANES), np.float32)
    phase[0, 1:1 + HY_BANDS] = 0.5 * math.pi
    phase[0, 1 + HY_BANDS:HY_EMB] = math.pi
    fw1p = jnp.zeros((L, LANES, H), F32).at[:, :HY_EMB].set(fw1)
    max_decay = math.log(HY_DECAY_TARGET) / HY_FAST_DECAY
    min_decay = math.log(HY_DECAY_TARGET) / HY_SLOW_DECAY
    deltas = np.linspace(min_decay, max_decay, HY_WIDTH).astype(np.float32).reshape(1, HY_WIDTH)
    vec = lambda a: a.reshape(L, 1, a.shape[-1])
    lspec = lambda r, c: pl.BlockSpec((None, r, c), lambda j, l: (l, 0, 0))
    lanes = pl.BlockSpec((1, LANES), lambda j, l: (0, 0))
    return pl.pallas_call(
        functools.partial(_hy_filter_kernel, seq=seq, tr=tr),
        grid=(n // tr, L),
        in_specs=[lanes, lanes,
                  lspec(LANES, H), lspec(1, H), lspec(1, H),
                  lspec(H, H), lspec(1, H), lspec(1, H),
                  lspec(H, 2 * HY_WIDTH),
                  pl.BlockSpec((1, HY_WIDTH), lambda j, l: (0, 0))],
        out_specs=[pl.BlockSpec((None, tr, HY_WIDTH), lambda j, l: (l, j, 0)),
                   pl.BlockSpec((L, 1, HY_WIDTH), lambda j, l: (0, 0, 0))],
        out_shape=[jax.ShapeDtypeStruct((L, n, HY_WIDTH), F32),
                   jax.ShapeDtypeStruct((L, 1, HY_WIDTH), F32)],
        scratch_shapes=[pltpu.VMEM((tr, LANES), F32)],
        compiler_params=_cparams(("arbitrary", "arbitrary")),
        name="hy_filter",
    )(jnp.asarray(bandv), jnp.asarray(phase), fw1p, vec(fb1), vec(freq1), fw2, vec(fb2), vec(freq2), fw3,
      jnp.asarray(deltas))


def _k1_rows(n1):
    k1 = n1 // 2 + 1
    return k1, -(-k1 // 8) * 8


def _bf_pair(m):
    m32 = jnp.asarray(np.asarray(m, np.float32))
    return _split(m32)


def _stage_a_kernel(x_ref, frh_ref, frl_ref, fih_ref, fil_ref, ar_ref, ai_ref):
    xh, xl = _split(x_ref[...])
    ar_ref[...] = _dot3(frh_ref[...], frl_ref[...], xh, xl)
    ai_ref[...] = _dot3(fih_ref[...], fil_ref[...], xh, xl)


def _stage_a(x3, n1_total, tl=4096):
    Bt, n1u, LN = x3.shape
    k1n, k1p = _k1_rows(n1_total)
    ang = 2.0 * np.pi * np.outer(np.arange(k1p), np.arange(n1u)) / n1_total
    valid = (np.arange(k1p) < k1n)[:, None]
    frh, frl = _bf_pair(np.cos(ang) * valid)
    fih, fil = _bf_pair(-np.sin(ang) * valid)
    cspec = pl.BlockSpec((k1p, n1u), lambda b, j: (0, 0))
    ospec = pl.BlockSpec((None, k1p, tl), lambda b, j: (b, 0, j))
    return pl.pallas_call(
        _stage_a_kernel,
        grid=(Bt, LN // tl),
        in_specs=[pl.BlockSpec((None, n1u, tl), lambda b, j: (b, 0, j)), cspec, cspec, cspec, cspec],
        out_specs=[ospec, ospec],
        out_shape=[jax.ShapeDtypeStruct((Bt, k1p, LN), F32)] * 2,
        compiler_params=_cparams(("parallel", "parallel")),
        name="hy_stage_a",
    )(x3, frh, frl, fih, fil)


def _stage_b_consts(n1_total):
    n = n1_total * FFT_N2
    k1n, k1p = _k1_rows(n1_total)
    ang2 = 2.0 * np.pi * np.outer(np.arange(FFT_N2), np.arange(FFT_N2)) / FFT_N2
    c2, s2 = np.cos(ang2), np.sin(ang2)
    mf = np.block([[c2, s2], [-s2, c2]])
    mi = np.block([[c2, -s2], [s2, c2]])
    angt = 2.0 * np.pi * np.outer(np.arange(k1p), np.arange(FFT_N2)) / n
    twc = jnp.asarray(np.cos(angt).astype(np.float32)).reshape(k1p, FFT_N2, 1)
    tws = jnp.asarray(np.sin(angt).astype(np.float32)).reshape(k1p, FFT_N2, 1)
    return _bf_pair(mf), _bf_pair(mi), twc, tws


def _fwd_b(ar, ai, c, s, mfh, mfl):
    tr_ = ar * c + ai * s
    ti_ = ai * c - ar * s
    xh, xl = _split(jnp.concatenate([tr_, ti_], axis=0))
    z = _dot3(mfh, mfl, xh, xl)
    return z[:FFT_N2], z[FFT_N2:]


def _stage_bk_kernel(ar_ref, ai_ref, c_ref, s_ref, mfh_ref, mfl_ref, ssq_ref, kr_ref, ki_ref,
                     *, k1n, inv_n):
    scale = lax.rsqrt(ssq_ref[...] + 1e-6) * inv_n
    for kk in range(ar_ref.shape[0]):
        k1 = pl.program_id(1) * ar_ref.shape[0] + kk

        @pl.when(k1 < k1n)
        def _(kk=kk):
            zr, zi = _fwd_b(ar_ref[kk], ai_ref[kk], c_ref[kk], s_ref[kk], mfh_ref[...], mfl_ref[...])
            kr_ref[kk] = zr * scale
            ki_ref[kk] = zi * scale

        @pl.when(k1 >= k1n)
        def _(kk=kk):
            kr_ref[kk] = jnp.zeros((FFT_N2, kr_ref.shape[-1]), F32)
            ki_ref[kk] = jnp.zeros((FFT_N2, ki_ref.shape[-1]), F32)


def _stage_b_filter(ar4, ai4, ssq, consts, n1_total):
    L, k1p, _, C = ar4.shape
    k1n, _ = _k1_rows(n1_total)
    (mfh, mfl), _, twc, tws = consts
    kb = SUBLANES
    blk = pl.BlockSpec((None, kb, FFT_N2, C), lambda l, k: (l, k, 0, 0))
    tw = pl.BlockSpec((kb, FFT_N2, 1), lambda l, k: (k, 0, 0))
    mat = pl.BlockSpec((2 * FFT_N2, 2 * FFT_N2), lambda l, k: (0, 0))
    return pl.pallas_call(
        functools.partial(_stage_bk_kernel, k1n=k1n, inv_n=1.0 / (n1_total * FFT_N2)),
        grid=(L, k1p // kb),
        in_specs=[blk, blk, tw, tw, mat, mat, pl.BlockSpec((None, 1, C), lambda l, k: (l, 0, 0))],
        out_specs=[blk, blk],
        out_shape=[jax.ShapeDtypeStruct(ar4.shape, F32)] * 2,
        compiler_params=_cparams(("parallel", "parallel")),
        name="hy_stage_b_filter",
    )(ar4, ai4, twc, tws, mfh, mfl, ssq)


SUBLANES = 8


def _hy_conv_kernel(vv_ref, x0_ref, kr_ref, ki_ref, tw_ref, kah_ref, kal_ref, kch_ref, kcl_ref,
                    mfh_ref, mfl_ref, mih_ref, mil_ref, bias_ref, o_ref, a_scr, *, n1u, k1n, k1p):
    groups = FFT_N2 // SUBLANES

    def rows(n1, g):
        return pl.ds(pl.multiple_of(n1 * FFT_N2 + g * SUBLANES, SUBLANES), SUBLANES)

    def stage_a(g, carry):
        xg = jnp.concatenate([vv_ref[rows(n1, g), :] for n1 in range(n1u)], axis=0)
        ag = _dot3(kah_ref[...], kal_ref[...], *_split(xg))
        gs = pl.ds(pl.multiple_of(g * SUBLANES, SUBLANES), SUBLANES)
        for part in range(2):
            for k1 in range(k1p):
                r = (part * k1p + k1) * SUBLANES
                a_scr[part, k1, gs, :] = ag[r:r + SUBLANES]
        return carry

    lax.fori_loop(0, groups, stage_a, 0, unroll=2)

    def stage_b(k1, carry):
        tw = tw_ref[k1]
        c = tw[:, 0:1]
        s = tw[:, 1:2]
        zr, zi = _fwd_b(a_scr[0, k1], a_scr[1, k1], c, s, mfh_ref[...], mfl_ref[...])
        kr = kr_ref[k1]
        ki = ki_ref[k1]
        pr = zr * kr - zi * ki
        pi = zr * ki + zi * kr
        y = _dot3(mih_ref[...], mil_ref[...], *_split(jnp.concatenate([pr, pi], axis=0)))
        yr = y[:FFT_N2]
        yi = y[FFT_N2:]
        a_scr[0, k1] = yr * c - yi * s
        a_scr[1, k1] = yi * c + yr * s
        return carry

    lax.fori_loop(0, k1n, stage_b, 0, unroll=3 if k1n % 3 == 0 else 1)

    def stage_c(g, carry):
        gs = pl.ds(pl.multiple_of(g * SUBLANES, SUBLANES), SUBLANES)
        bg = jnp.concatenate([a_scr[part, k1, gs, :] for part in range(2) for k1 in range(k1p)], axis=0)
        yg = _dot3(kch_ref[...], kcl_ref[...], *_split(bg))
        for n1 in range(n1u):
            r = rows(n1, g)
            o_ref[r, :] = (yg[n1 * SUBLANES:(n1 + 1) * SUBLANES] + vv_ref[r, :] * bias_ref[...]) * x0_ref[r, :]
        return carry

    lax.fori_loop(0, groups, stage_c, 0, unroll=2)


def _hy_conv(vv, x0c, kr3, ki3, consts, bias_d, n1_total):
    B, S, C = vv.shape
    n1u = S // FFT_N2
    k1n, k1p = _k1_rows(n1_total)
    (mfh, mfl), (mih, mil), twc, tws = consts
    tw = jnp.concatenate([twc, tws], axis=-1)
    k1 = np.arange(k1p)
    valid = (k1 < k1n)[:, None]
    eye = np.eye(SUBLANES)
    ang_a = 2.0 * np.pi * np.outer(k1, np.arange(n1u)) / n1_total
    ka = np.concatenate([np.kron(np.cos(ang_a) * valid, eye), np.kron(-np.sin(ang_a) * valid, eye)], axis=0)
    w = np.where((k1 == 0) | (k1 == n1_total // 2), 1.0, 2.0) * (k1 < k1n)
    ang_c = 2.0 * np.pi * np.outer(np.arange(n1u), k1) / n1_total
    kc = np.concatenate([np.kron(np.cos(ang_c) * w[None, :], eye), np.kron(-np.sin(ang_c) * w[None, :], eye)], axis=1)
    kah, kal = _bf_pair(ka)
    kch, kcl = _bf_pair(kc)
    nct = C // LANES
    xspec = pl.BlockSpec((None, S, LANES), lambda c, b: (b, 0, c))
    kspec = pl.BlockSpec((k1p, FFT_N2, LANES), lambda c, b: (0, 0, c))
    full = lambda a: pl.BlockSpec(a.shape, lambda c, b: (0,) * a.ndim)
    return pl.pallas_call(
        functools.partial(_hy_conv_kernel, n1u=n1u, k1n=k1n, k1p=k1p),
        grid=(nct, B),
        in_specs=[xspec, xspec, kspec, kspec, full(tw), full(kah), full(kal), full(kch), full(kcl),
                  full(mfh), full(mfl), full(mih), full(mil),
                  pl.BlockSpec((1, LANES), lambda c, b: (0, c))],
        out_specs=xspec,
        out_shape=jax.ShapeDtypeStruct((B, S, C), F32),
        scratch_shapes=[pltpu.VMEM((2, k1p, FFT_N2, LANES), F32)],
        compiler_params=_cparams(("parallel", "parallel")),
        name="hy_conv",
    )(vv, x0c, kr3, ki3, tw, kah, kal, kch, kcl, mfh, mfl, mih, mil, bias_d.reshape(1, C))


def _conv3(x, w_ref, b_ref):
    S = x.shape[0]
    row = lax.broadcasted_iota(jnp.int32, x.shape, 0)
    prev = jnp.where(row == 0, 0.0, pltpu.roll(x, 1, 0))
    nxt = jnp.where(row == S - 1, 0.0, pltpu.roll(x, S - 1, 0))
    return prev * w_ref[0:1, :] + x * w_ref[1:2, :] + nxt * w_ref[2:3, :] + b_ref[...]


def _hy_gate_kernel(x0_ref, x1_ref, v_ref, w0_ref, w1_ref, w2_ref, b0_ref, b1_ref, b2_ref,
                    vv_ref, x0c_ref):
    x0c_ref[...] = _conv3(x0_ref[...], w0_ref, b0_ref)
    vv_ref[...] = _conv3(v_ref[...], w2_ref, b2_ref) * _conv3(x1_ref[...], w1_ref, b1_ref)


def _hy_gate(u3, conv_w, conv_b):
    B, S, _ = u3.shape
    nct = HY_WIDTH // LANES
    conv_b2 = conv_b.reshape(1, 3 * HY_WIDTH)
    uspec = lambda part: pl.BlockSpec((None, S, LANES), lambda b, c: (b, 0, part * nct + c))
    wspec = lambda part: pl.BlockSpec((3, LANES), lambda b, c: (0, part * nct + c))
    bspec = lambda part: pl.BlockSpec((1, LANES), lambda b, c: (0, part * nct + c))
    ospec = pl.BlockSpec((None, S, LANES), lambda b, c: (b, 0, c))
    return pl.pallas_call(
        _hy_gate_kernel,
        grid=(B, nct),
        in_specs=[uspec(0), uspec(1), uspec(2), wspec(0), wspec(1), wspec(2),
                  bspec(0), bspec(1), bspec(2)],
        out_specs=[ospec, ospec],
        out_shape=[jax.ShapeDtypeStruct((B, S, HY_WIDTH), F32)] * 2,
        compiler_params=_cparams(("parallel", "parallel")),
        name="hy_gate",
    )(u3, u3, u3, conv_w, conv_w, conv_w, conv_b2, conv_b2, conv_b2)


def _attn_kernel(q_ref, k_ref, v_ref, lq1_ref, lk1_ref, lq2_ref, lk2_ref, g_ref, o_ref,
                 s_scr, p_scr, vx_scr, *, lam_init, tq):
    S = q_ref.shape[0]
    nblk = S // tq
    lam = (jnp.exp(jnp.sum(lq1_ref[...] * lk1_ref[...], keepdims=True))
           - jnp.exp(jnp.sum(lq2_ref[...] * lk2_ref[...], keepdims=True)) + lam_init)
    lane = lax.broadcasted_iota(jnp.int32, (tq, LANES), 1)

    def blk(i):
        return pl.ds(pl.multiple_of(i * tq, tq), tq)

    def scores(i, slot):
        q = q_ref[blk(i), :]
        for c in range(2):
            sel = (lane < ATT_HEAD_DIM) if c == 0 else (lane >= ATT_HEAD_DIM)
            qc = jnp.where(sel, q, jnp.zeros_like(q))
            s_scr[slot, c] = lax.dot_general(qc, k_ref[...], (((1,), (1,)), ((), ())),
                                             preferred_element_type=F32)

    vx_scr[:, :LANES] = v_ref[...]
    vx_scr[:, LANES:] = (lax.broadcasted_iota(jnp.int32, (S, LANES), 1) == 0).astype(BF16)

    def softmax(slot):
        for c in range(2):
            s = s_scr[slot, c]
            m = jnp.max(s, axis=-1, keepdims=True)
            p_scr[slot, c] = jnp.exp2(s - m).astype(BF16)

    def values(i, slot):
        oe1 = jnp.dot(p_scr[slot, 0], vx_scr[...], preferred_element_type=F32)
        oe2 = jnp.dot(p_scr[slot, 1], vx_scr[...], preferred_element_type=F32)
        o = oe1[:, :LANES] / oe1[:, LANES:LANES + 1] - oe2[:, :LANES] * (lam / oe2[:, LANES:LANES + 1])
        o = o * lax.rsqrt(jnp.mean(o * o, axis=-1, keepdims=True) + RMS_EPS) * g_ref[...]
        o_ref[blk(i), :] = o * (1.0 - lam_init)

    scores(0, 0)
    softmax(0)
    scores(1, 1)

    def pair(j, carry):
        t = 2 * j
        values(t - 2, 0)
        softmax(1)
        scores(t, 0)
        values(t - 1, 1)
        softmax(0)
        scores(t + 1, 1)
        return carry

    lax.fori_loop(1, nblk // 2, pair, 0)
    values(nblk - 2, 0)
    softmax(1)
    values(nblk - 1, 1)


def _rope_tables(seq):
    pos = np.arange(seq, dtype=np.float64)
    inv_freq = np.power(ROPE_THETA, -np.arange(0, ROPE_DIM, 2, dtype=np.float64) / ROPE_DIM)
    ang = pos[:, None] * inv_freq[None, :]
    half = ROPE_DIM // 2
    cos_f = np.ones((seq, LANES), np.float32)
    sin_a = np.zeros((seq, LANES), np.float32)
    sin_b = np.zeros((seq, LANES), np.float32)
    for base in range(0, LANES, ATT_HEAD_DIM):
        cos_f[:, base:base + half] = np.cos(ang)
        cos_f[:, base + half:base + ROPE_DIM] = np.cos(ang)
        sin_a[:, base:base + half] = -np.sin(ang)
        sin_b[:, base + half:base + ROPE_DIM] = np.sin(ang)
    return jnp.asarray(cos_f), jnp.asarray(sin_a), jnp.asarray(sin_b)


def _diff_attention(qkv3, lq1, lk1, lq2, lk2, subln_g, lam_init, tq=256):
    B, S, _ = qkv3.shape
    nb = ATT_WIDTH // LANES
    assert S % (2 * tq) == 0
    spec = lambda part: pl.BlockSpec((None, S, LANES), lambda b, h: (b, 0, part * nb + h))
    vec = pl.BlockSpec((1, ATT_HEAD_DIM), lambda b, h: (0, 0))
    r1 = lambda a: a.reshape(1, -1)
    return pl.pallas_call(
        functools.partial(_attn_kernel, lam_init=lam_init, tq=tq),
        grid=(B, ATT_HEADS),
        in_specs=[spec(0), spec(1), spec(2), vec, vec, vec, vec,
                  pl.BlockSpec((1, LANES), lambda b, h: (0, 0))],
        out_specs=pl.BlockSpec((None, S, LANES), lambda b, h: (b, 0, h)),
        out_shape=jax.ShapeDtypeStruct((B, S, ATT_WIDTH), F32),
        scratch_shapes=[pltpu.VMEM((2, 2, tq, S), F32), pltpu.VMEM((2, 2, tq, S), BF16),
                        pltpu.VMEM((S, 2 * LANES), BF16)],
        compiler_params=_cparams(("parallel", "parallel")),
        name="diff_attn",
    )(qkv3, qkv3, qkv3, r1(lq1), r1(lk1), r1(lq2), r1(lk2), r1(subln_g))


def _pool_kernel(u_ref, w_ref, b_ref, sc_ref, o_ref):
    ct = pl.program_id(1)
    x = u_ref[...]
    S = x.shape[0]
    row = lax.broadcasted_iota(jnp.int32, x.shape, 0)
    lane = lax.broadcasted_iota(jnp.int32, x.shape, 1)

    def shifted(d):
        if d == 0:
            return x
        r = pltpu.roll(x, (-d) % S, 0)
        return jnp.where((row + d >= 0) & (row + d < S), r, 0.0)

    sums = {}
    acc = shifted(-1) + x
    sums[2] = acc
    lo, hi = -1, 0
    for w in POOL_WINDOWS[1:]:
        for d in list(range(-(w // 2), lo)) + list(range(hi + 1, w // 2)):
            acc = acc + shifted(d)
        lo, hi = -(w // 2), w // 2 - 1
        sums[w] = acc
    grp = 2 * ct + (lane >= POOL_GROUP).astype(jnp.int32)
    win_sum = sums[POOL_WINDOWS[-1]]
    half = jnp.full(x.shape, POOL_WINDOWS[-1] // 2, jnp.int32)
    for g in range(len(POOL_WINDOWS) - 2, -1, -1):
        win_sum = jnp.where(grp == g, sums[POOL_WINDOWS[g]], win_sum)
        half = jnp.where(grp == g, POOL_WINDOWS[g] // 2, half)
    cnt = jnp.minimum(row + half - 1, S - 1) - jnp.maximum(row - half, 0) + 1
    y = win_sum / cnt.astype(F32) - x
    y = _dot3(*_split(y), *_split(w_ref[...])) + b_ref[...]
    o_ref[...] = y * sc_ref[...]


def _pool_mixer(u3, w, b, scale):
    B, S, _ = u3.shape
    nct = POOL_WIDTH // LANES
    gpt = LANES // POOL_GROUP
    wbd = jnp.zeros((nct, LANES, LANES), F32)
    for g in range(len(POOL_WINDOWS)):
        t, o = divmod(g, gpt)
        wbd = wbd.at[t, o * POOL_GROUP:(o + 1) * POOL_GROUP, o * POOL_GROUP:(o + 1) * POOL_GROUP].set(w[g])
    vspec = pl.BlockSpec((1, LANES), lambda bb, c: (0, c))
    return pl.pallas_call(
        _pool_kernel,
        grid=(B, nct),
        in_specs=[pl.BlockSpec((None, S, LANES), lambda bb, c: (bb, 0, c)),
                  pl.BlockSpec((None, LANES, LANES), lambda bb, c: (c, 0, 0)),
                  vspec, vspec],
        out_specs=pl.BlockSpec((None, S, LANES), lambda bb, c: (bb, 0, c)),
        out_shape=jax.ShapeDtypeStruct((B, S, POOL_WIDTH), F32),
        compiler_params=_cparams(("parallel", "parallel")),
        name="pool_mixer",
    )(u3, wbd, b.reshape(1, POOL_WIDTH), scale.reshape(1, POOL_WIDTH))


PAIRS_PER_GROUP = 6
N_CLASSES = N_GROUPS * PAIRS_PER_GROUP
ROUTE_W_LO, ROUTE_W_HI, ROUTE_CLS, ROUTE_RANK = 0, 1, 2, 3
TOKEN_ROWS = D_MODEL // LANES


def _store_token_major(ref, x, first_row, unit_rows):
    n = x.shape[0]
    for j in range(TOKEN_ROWS):
        ref[pl.ds(first_row + j, n, stride=unit_rows), :] = x[:, j * LANES:(j + 1) * LANES]


def _load_token_major(ref, n, first_row, unit_rows):
    return jnp.concatenate([ref[pl.ds(first_row + j, n, stride=unit_rows), :] for j in range(TOKEN_ROWS)],
                           axis=1)


def _class_experts(c):
    g, pidx = divmod(c, PAIRS_PER_GROUP)
    pairs = [(a, b) for a in range(EXPERTS_PER_GROUP) for b in range(a + 1, EXPERTS_PER_GROUP)]
    lo, hi = pairs[pidx]
    return g * EXPERTS_PER_GROUP + lo, g * EXPERTS_PER_GROUP + hi


def _outproj_kernel(h_ref, yh_ref, ya_ref, yp_ref, wh_ref, wa_ref, wp_ref, g_ref, b_ref,
                    wgh_ref, wgl_ref, bg_ref, tri_ref, h1_ref, h1t_ref, route_ref, cnt_ref, carry_scr):
    d = functools.partial(jnp.dot, preferred_element_type=F32)
    mix = (d(yh_ref[...].astype(BF16), wh_ref[...]) + d(ya_ref[...].astype(BF16), wa_ref[...])
           + d(yp_ref[...].astype(BF16), wp_ref[...]))
    h1 = _ln_rows(DN_ALPHA * h_ref[...] + mix, g_ref[...], b_ref[...])
    h1_ref[...] = h1
    _store_token_major(h1t_ref, h1, 0, TOKEN_ROWS)

    @pl.when(pl.program_id(0) == 0)
    def _():
        carry_scr[...] = jnp.zeros_like(carry_scr)

    logit = _dot3(*_split(h1), wgh_ref[...], wgl_ref[...]) + bg_ref[...]
    lni = lax.broadcasted_iota(jnp.int32, logit.shape, 1)
    ln = lni.astype(F32)
    grp = lax.shift_right_arithmetic(lni - GATE_FINE_LANE, 2).astype(F32)
    first = lambda mask: jnp.min(jnp.where(mask, ln, float(LANES)), axis=-1, keepdims=True)
    cmask = lni < N_GROUPS
    lc = jnp.where(cmask, logit, NEG_BIG)
    mc = jnp.max(lc, axis=-1, keepdims=True)
    gw = 1.0 / jnp.sum(jnp.where(cmask, jnp.exp(lc - mc), 0.0), axis=-1, keepdims=True)
    gi = first(cmask & (lc == mc))
    fmask = (lni >= GATE_FINE_LANE) & (lni < GATE_FINE_LANE + N_EXPERTS) & (grp == gi)
    lf = jnp.where(fmask, logit, NEG_BIG)
    m1 = jnp.max(lf, axis=-1, keepdims=True)
    i1 = first(fmask & (lf == m1))
    rest = fmask & (ln != i1)
    lf2 = jnp.where(rest, logit, NEG_BIG)
    m2 = jnp.max(lf2, axis=-1, keepdims=True)
    i2 = first(rest & (lf2 == m2))
    e2 = jnp.exp(m2 - m1)
    w1 = gw / (1.0 + e2)
    w2 = gw * e2 / (1.0 + e2)

    j1 = i1 - GATE_FINE_LANE - EXPERTS_PER_GROUP * gi
    j2 = i2 - GATE_FINE_LANE - EXPERTS_PER_GROUP * gi
    lo = jnp.minimum(j1, j2)
    hi = jnp.maximum(j1, j2)
    w_lo = jnp.where(j1 < j2, w1, w2)
    w_hi = jnp.where(j1 < j2, w2, w1)
    base = jnp.where(lo == 0.0, 0.0, jnp.where(lo == 1.0, 3.0, 5.0))
    cls = gi * PAIRS_PER_GROUP + base + hi - lo - 1.0
    onehot = ln == cls
    before = jnp.dot(tri_ref[...], onehot.astype(BF16), preferred_element_type=F32)
    carry = carry_scr[...]
    rank = jnp.sum(jnp.where(onehot, before + carry, 0.0), axis=-1, keepdims=True)
    carry = carry + jnp.sum(onehot.astype(F32), axis=0, keepdims=True)
    carry_scr[...] = carry
    cnt_ref[...] = carry
    route_ref[...] = jnp.where(lni == ROUTE_W_LO, w_lo,
                               jnp.where(lni == ROUTE_W_HI, w_hi,
                                         jnp.where(lni == ROUTE_CLS, cls,
                                                   jnp.where(lni == ROUTE_RANK, rank, 0.0))))


def _out_proj_ln_route(h2, yh, ya, yp, w_out_bf, g, b, wgc, bgc, wgf, bgf, tm=512):
    T, D = h2.shape
    tri = jnp.asarray(np.tril(np.ones((tm, tm), np.float32), -1)).astype(BF16)
    wg = jnp.zeros((D, LANES), F32).at[:, :N_GROUPS].set(wgc)
    wg = wg.at[:, GATE_FINE_LANE:GATE_FINE_LANE + N_EXPERTS].set(wgf)
    bg = jnp.zeros((1, LANES), F32).at[0, :N_GROUPS].set(bgc)
    bg = bg.at[0, GATE_FINE_LANE:GATE_FINE_LANE + N_EXPERTS].set(bgf)
    wgh, wgl = _split(wg)
    o1, o2 = HY_WIDTH, HY_WIDTH + ATT_WIDTH
    row = lambda c: pl.BlockSpec((tm, c), lambda i: (i, 0))
    full = lambda r, c: pl.BlockSpec((r, c), lambda i: (0, 0))
    return pl.pallas_call(
        _outproj_kernel,
        grid=(T // tm,),
        in_specs=[row(D), row(HY_WIDTH), row(ATT_WIDTH), row(POOL_WIDTH),
                  full(HY_WIDTH, D), full(ATT_WIDTH, D), full(POOL_WIDTH, D),
                  full(1, D), full(1, D), full(D, LANES), full(D, LANES), full(1, LANES), full(tm, tm)],
        out_specs=[row(D), pl.BlockSpec((tm * TOKEN_ROWS, LANES), lambda i: (i, 0)), row(LANES), full(1, LANES)],
        out_shape=[jax.ShapeDtypeStruct((T, D), F32), jax.ShapeDtypeStruct((T * TOKEN_ROWS, LANES), F32),
                   jax.ShapeDtypeStruct((T, LANES), F32), jax.ShapeDtypeStruct((1, LANES), F32)],
        scratch_shapes=[pltpu.VMEM((1, LANES), F32)],
        compiler_params=_cparams(("arbitrary",)),
        name="out_proj_ln_route",
    )(h2, yh, ya, yp, w_out_bf[:o1], w_out_bf[o1:o2], w_out_bf[o2:], g.reshape(1, D), b.reshape(1, D),
      wgh, wgl, bg, tri)


MOE_TILE = 256
DMA_BATCH = 256


def _moe_rows_padded(T):
    return T + N_CLASSES * MOE_TILE


def _token_rows(ref, t, unit_rows):
    return ref.at[pl.ds(pl.multiple_of(t * unit_rows, unit_rows), unit_rows)]


DMA_GROUP = 8


def _start_tokens(n, slot_of, start_copy):
    def group(g, carry):
        first = g * DMA_GROUP
        slots = [slot_of(first + k) for k in range(DMA_GROUP)]
        for k in range(DMA_GROUP):
            start_copy(first + k, slots[k], k % 2)
        return carry

    lax.fori_loop(0, n // DMA_GROUP, group, 0)


def _wait_tokens(n, src_ref, dst_ref, unit_rows, sems):
    rows = pl.ds(0, (n // 2) * unit_rows)
    for sem in sems:
        pltpu.make_async_copy(src_ref.at[rows], dst_ref.at[rows], sem).wait()


def _dispatch_kernel(cls_ref, rank_ref, cnt_ref, h1t_ref, xs_ref, dest_ref, tlo_ref, thi_ref, tval_ref,
                     off_scr, zero_scr, sem, *, tm, n_tiles):
    i = pl.program_id(0)
    base = i * tm
    tile_rows = MOE_TILE * TOKEN_ROWS

    def fill_copy(tile):
        rows = pl.ds(pl.multiple_of(tile * tile_rows, tile_rows), tile_rows)
        return pltpu.make_async_copy(zero_scr, xs_ref.at[rows], sem.at[2])

    @pl.when(i == 0)
    def _():
        zero_scr[...] = jnp.zeros_like(zero_scr)
        off = jnp.int32(0)
        tile = jnp.int32(0)
        fills = []
        for c in range(N_CLASSES):
            n = cnt_ref[c]
            nt = lax.shift_right_logical(n + (MOE_TILE - 1), MOE_TILE.bit_length() - 1)
            off_scr[c] = off
            e_lo, e_hi = _class_experts(c)

            def mark(k, carry, tile=tile, e_lo=e_lo, e_hi=e_hi):
                tlo_ref[tile + k] = e_lo
                thi_ref[tile + k] = e_hi
                tval_ref[tile + k] = 1
                return carry

            lax.fori_loop(0, nt, mark, 0)
            fill = fill_copy(tile + nt - 1)
            fills.append((nt > 0, fill))

            @pl.when(nt > 0)
            def _(fill=fill):
                fill.start()

            off = off + nt * MOE_TILE
            tile = tile + nt

        def unused(k, carry):
            tlo_ref[k] = 0
            thi_ref[k] = 0
            tval_ref[k] = 0
            fill_copy(k).start()
            return carry

        lax.fori_loop(tile, n_tiles, unused, 0)
        for used, fill in fills:
            @pl.when(used)
            def _(fill=fill):
                fill.wait()

        def unused_wait(k, carry):
            fill_copy(k).wait()
            return carry

        lax.fori_loop(tile, n_tiles, unused_wait, 0)

    def slots_of_step(step):
        def slot(r, carry):
            t = step * tm + r
            dest_ref[t] = off_scr[cls_ref[t]] + rank_ref[t]
            return carry

        lax.fori_loop(0, tm, slot, 0, unroll=8)

    @pl.when(i == 0)
    def _():
        slots_of_step(0)

    def start_copy(r, d, prio):
        pltpu.make_async_copy(_token_rows(h1t_ref, r, TOKEN_ROWS), _token_rows(xs_ref, d, TOKEN_ROWS),
                              sem.at[prio]).start(priority=prio)

    _start_tokens(tm, lambda r: dest_ref[base + r], start_copy)

    @pl.when(i + 1 < pl.num_programs(0))
    def _():
        slots_of_step(i + 1)

    _wait_tokens(tm, h1t_ref, xs_ref, TOKEN_ROWS, [sem.at[0], sem.at[1]])


def _moe_dispatch(h1t, cls_i, rank_i, cnt_i, tm=512):
    T = h1t.shape[0] // TOKEN_ROWS
    slots = _moe_rows_padded(T)
    n_tiles = slots // MOE_TILE
    smem = pl.BlockSpec(memory_space=pltpu.SMEM)
    return pl.pallas_call(
        functools.partial(_dispatch_kernel, tm=tm, n_tiles=n_tiles),
        grid_spec=pltpu.PrefetchScalarGridSpec(
            num_scalar_prefetch=3,
            grid=(T // tm,),
            in_specs=[pl.BlockSpec((tm * TOKEN_ROWS, LANES), lambda i, *_: (i, 0))],
            out_specs=[pl.BlockSpec(memory_space=pl.ANY), smem, smem, smem, smem],
            scratch_shapes=[pltpu.SMEM((N_CLASSES,), jnp.int32),
                            pltpu.VMEM((MOE_TILE * TOKEN_ROWS, LANES), F32),
                            pltpu.SemaphoreType.DMA((3,))],
        ),
        out_shape=[jax.ShapeDtypeStruct((slots * TOKEN_ROWS, LANES), F32), jax.ShapeDtypeStruct((T,), jnp.int32),
                   jax.ShapeDtypeStruct((n_tiles,), jnp.int32), jax.ShapeDtypeStruct((n_tiles,), jnp.int32),
                   jax.ShapeDtypeStruct((n_tiles,), jnp.int32)],
        compiler_params=_cparams(("arbitrary",)),
        name="moe_dispatch",
    )(cls_i, rank_i, cnt_i, h1t)


def _experts_kernel(tlo_ref, thi_ref, tval_ref, xs_ref, w1l_ref, w3l_ref, w2l_ref, w1h_ref, w3h_ref, w2h_ref,
                    ys_ref):
    j = pl.program_id(0)
    d = functools.partial(jnp.dot, preferred_element_type=F32)

    @pl.when(tval_ref[j] == 1)
    def _():
        xb = _load_token_major(xs_ref, MOE_TILE, 0, TOKEN_ROWS).astype(BF16)

        def expert(w1_ref, w3_ref, w2_ref):
            a = d(xb, w1_ref[...])
            c = d(xb, w3_ref[...])
            return d((a * jax.nn.sigmoid(a) * c).astype(BF16), w2_ref[...])

        _store_token_major(ys_ref, expert(w1l_ref, w3l_ref, w2l_ref), 0, 2 * TOKEN_ROWS)
        _store_token_major(ys_ref, expert(w1h_ref, w3h_ref, w2h_ref), TOKEN_ROWS, 2 * TOKEN_ROWS)

    @pl.when(tval_ref[j] == 0)
    def _():
        ys_ref[...] = jnp.zeros_like(ys_ref)


def _moe_experts(xs, tlo, thi, tval, w1_bf, w3_bf, w2_bf):
    slots = xs.shape[0] // TOKEN_ROWS
    D = D_MODEL
    wspec = lambda shape, which: pl.BlockSpec(
        (None,) + shape, (lambda j, tlo, thi, tval: (tlo[j], 0, 0)) if which == 0
        else (lambda j, tlo, thi, tval: (thi[j], 0, 0)))
    up = (D, D_EXPERT)
    down = (D_EXPERT, D)
    return pl.pallas_call(
        _experts_kernel,
        grid_spec=pltpu.PrefetchScalarGridSpec(
            num_scalar_prefetch=3,
            grid=(slots // MOE_TILE,),
            in_specs=[pl.BlockSpec((MOE_TILE * TOKEN_ROWS, LANES), lambda j, *_: (j, 0)),
                      wspec(up, 0), wspec(up, 0), wspec(down, 0), wspec(up, 1), wspec(up, 1), wspec(down, 1)],
            out_specs=pl.BlockSpec((MOE_TILE * 2 * TOKEN_ROWS, LANES), lambda j, *_: (j, 0)),
        ),
        out_shape=jax.ShapeDtypeStruct((slots * 2 * TOKEN_ROWS, LANES), F32),
        compiler_params=_cparams(("arbitrary",)),
        name="moe_experts",
    )(tlo, thi, tval, xs, w1_bf, w3_bf, w2_bf, w1_bf, w3_bf, w2_bf)


def _combine_kernel(dest_ref, h1_ref, route_ref, ys_ref, p_ref, pwg_ref, pbg_ref, pwp_ref, g_ref, b_ref,
                    o_ref, y_scr, ym_scr, sem, *, tm):
    i = pl.program_id(0)
    d = functools.partial(jnp.dot, preferred_element_type=F32)
    unit = 2 * TOKEN_ROWS

    def gather(step, buf):
        def start_copy(r, d, prio):
            pltpu.make_async_copy(_token_rows(ys_ref, d, unit), _token_rows(y_scr.at[buf], r, unit),
                                  sem.at[2 * buf + prio]).start(priority=prio)
        _start_tokens(tm, lambda r: dest_ref[step * tm + r], start_copy)

    @pl.when(i == 0)
    def _():
        gather(0, 0)

    for buf in range(2):
        @pl.when((i + 1 < pl.num_programs(0)) & ((i + 1) % 2 == buf))
        def _(buf=buf):
            gather(i + 1, buf)

    h1 = h1_ref[...]
    z = d(h1.astype(BF16), pwg_ref[...]) + pbg_ref[...]
    y_ple = jax.nn.sigmoid(z) * d(p_ref[...].astype(BF16), pwp_ref[...])
    rec = route_ref[...]
    for buf in range(2):
        @pl.when(i % 2 == buf)
        def _(buf=buf):
            _wait_tokens(tm, ys_ref, y_scr.at[buf], unit, [sem.at[2 * buf], sem.at[2 * buf + 1]])
            ym_scr[...] = (rec[:, ROUTE_W_LO:ROUTE_W_LO + 1] * _load_token_major(y_scr.at[buf], tm, 0, unit)
                           + rec[:, ROUTE_W_HI:ROUTE_W_HI + 1]
                           * _load_token_major(y_scr.at[buf], tm, TOKEN_ROWS, unit))
    y_moe = ym_scr[...]
    o_ref[...] = _ln_rows(DN_ALPHA * h1 + y_moe + y_ple, g_ref[...], b_ref[...])


def _moe_combine_ple_ln(h1, route, ys, dest, p2, pwg_bf, pbg, pwp_bf, g, b, tm=512):
    T, D = h1.shape
    row = lambda c: pl.BlockSpec((tm, c), lambda i, *_: (i, 0))
    full = lambda r, c: pl.BlockSpec((r, c), lambda i, *_: (0, 0))
    return pl.pallas_call(
        functools.partial(_combine_kernel, tm=tm),
        grid_spec=pltpu.PrefetchScalarGridSpec(
            num_scalar_prefetch=1,
            grid=(T // tm,),
            in_specs=[row(D), row(LANES), pl.BlockSpec(memory_space=pl.ANY), row(PLE_DIM),
                      full(D, D), full(1, D), full(PLE_DIM, D), full(1, D), full(1, D)],
            out_specs=row(D),
            scratch_shapes=[pltpu.VMEM((2, tm * 2 * TOKEN_ROWS, LANES), F32), pltpu.VMEM((tm, D), F32),
                            pltpu.SemaphoreType.DMA((4,))],
        ),
        out_shape=jax.ShapeDtypeStruct((T, D), F32),
        compiler_params=_cparams(("arbitrary",)),
        name="moe_combine_ple_ln",
    )(dest, h1, route, ys, p2, pwg_bf, pbg.reshape(1, D), pwp_bf, g.reshape(1, D), b.reshape(1, D))


def _hyena_mixer(u3, kr, ki, consts, conv_w, conv_b, bias_d, n1_total):
    B, S, _ = u3.shape
    vv, x0c = _hy_gate(u3, conv_w, conv_b)
    y = _hy_conv(vv, x0c, kr, ki, consts, bias_d, n1_total)
    return y.reshape(B * S, HY_WIDTH)


def kernel(x, p, ln0_g, ln0_b, w_in, hy_conv_w, hy_conv_b, hy_fw1, hy_fb1, hy_freq1, hy_fw2, hy_fb2, hy_freq2, hy_fw3, hy_bias, att_lq1, att_lk1, att_lq2, att_lk2, att_subln_g, pool_w, pool_b, pool_scale, w_out, ln1_g, ln1_b, moe_wgc, moe_bgc, moe_wgf, moe_bgf, moe_w1, moe_w3, moe_w2, ple_wg, ple_bg, ple_wp, ln2_g, ln2_b):
    B, S, D = x.shape
    L = w_in.shape[0]
    T = B * S
    n1_total = 2 * S // FFT_N2
    C = HY_WIDTH

    kfilt, ssq = _hy_filter(S, hy_fw1, hy_fb1, hy_freq1, hy_fw2, hy_fb2, hy_freq2, hy_fw3)
    kar, kai = _stage_a(kfilt.reshape(L, n1_total, FFT_N2 * C), n1_total)
    k1p = kar.shape[1]
    consts = _stage_b_consts(n1_total)
    kr_all, ki_all = _stage_b_filter(kar.reshape(L, k1p, FFT_N2, C), kai.reshape(L, k1p, FFT_N2, C),
                                     ssq, consts, n1_total)
    tables = _rope_tables(S)

    h = _layer_norm(x.reshape(T, D), ln0_g, ln0_b)
    for i in range(L):
        lam_init = 0.8 - 0.6 * math.exp(-0.3 * i)
        uh, qkv, up = _in_proj(h, w_in[i].astype(BF16), tables, S)
        y_hy = _hyena_mixer(uh.reshape(B, S, 3 * C), kr_all[i], ki_all[i], consts,
                            hy_conv_w[i], hy_conv_b[i], hy_bias[i], n1_total)
        y_att = _diff_attention(qkv.reshape(B, S, 3 * ATT_WIDTH), att_lq1[i], att_lk1[i], att_lq2[i],
                                att_lk2[i], att_subln_g[i], lam_init).reshape(T, ATT_WIDTH)
        y_pool = _pool_mixer(up.reshape(B, S, POOL_WIDTH), pool_w[i], pool_b[i],
                             pool_scale[i]).reshape(T, POOL_WIDTH)
        h1, h1t, route, counts = _out_proj_ln_route(h, y_hy, y_att, y_pool, w_out[i].astype(BF16), ln1_g[i],
                                                    ln1_b[i], moe_wgc[i], moe_bgc[i], moe_wgf[i], moe_bgf[i])
        cls_i = route[:, ROUTE_CLS].astype(jnp.int32)
        rank_i = route[:, ROUTE_RANK].astype(jnp.int32)
        xs, dest, tlo, thi, tval = _moe_dispatch(h1t, cls_i, rank_i, counts[0].astype(jnp.int32))
        ys = _moe_experts(xs, tlo, thi, tval, moe_w1[i].astype(BF16), moe_w3[i].astype(BF16),
                          moe_w2[i].astype(BF16))
        h = _moe_combine_ple_ln(h1, route, ys, dest, p[i].reshape(T, PLE_DIM), ple_wg[i].astype(BF16),
                                ple_bg[i], ple_wp[i].astype(BF16), ln2_g[i], ln2_b[i])
    return h.reshape(B, S, D)
```

```python
import functools
import math

import numpy as np
import jax
import jax.numpy as jnp
from jax import lax
from jax.experimental import pallas as pl
from jax.experimental.pallas import tpu as pltpu

F32 = jnp.float32
BF16 = jnp.bfloat16

D_MODEL = 1024
DEPTH = 4
HY_WIDTH = 256
ATT_WIDTH = 512
ATT_HEADS = 4
ATT_HEAD_DIM = 64
POOL_WINDOWS = (2, 4, 8, 16)
POOL_WIDTH = 256
POOL_GROUP = 64
IN_WIDTH = 3 * HY_WIDTH + 3 * ATT_WIDTH + POOL_WIDTH
ROPE_THETA = 500000.0
ROPE_DIM = ATT_HEAD_DIM // 4
HY_EMB = 33
HY_BANDS = (HY_EMB - 1) // 2
HY_FILTER_HIDDEN = 64
HY_DECAY_TARGET = 1e-2
HY_FAST_DECAY = 0.3
HY_SLOW_DECAY = 1.5
N_GROUPS = 4
EXPERTS_PER_GROUP = 4
N_EXPERTS = 16
D_EXPERT = 256
PLE_DIM = 256
LN_EPS = 1e-5
RMS_EPS = 1e-5
DN_ALPHA = (2 * DEPTH) ** 0.25

LANES = 128
FFT_N2 = 128
GATE_COARSE_LANE = 0
GATE_FINE_LANE = N_GROUPS
NEG_BIG = -1e30
LOG2E = 1.4426950408889634
VMEM_LIMIT = 56 * 1024 * 1024


def _cparams(sem):
    return pltpu.CompilerParams(dimension_semantics=sem, vmem_limit_bytes=VMEM_LIMIT)


def _split(x):
    hi = x.astype(BF16)
    lo = (x - hi.astype(F32)).astype(BF16)
    return hi, lo


def _dot3(ah, al, bh, bl):
    d = functools.partial(jnp.dot, preferred_element_type=F32)
    return d(ah, bh) + (d(ah, bl) + d(al, bh))


def _ln_rows(x, g, b):
    mu = jnp.mean(x, axis=-1, keepdims=True)
    xc = x - mu
    var = jnp.mean(xc * xc, axis=-1, keepdims=True)
    return xc * lax.rsqrt(var + LN_EPS) * g + b


def _ln_kernel(x_ref, g_ref, b_ref, o_ref):
    o_ref[...] = _ln_rows(x_ref[...], g_ref[...], b_ref[...])


def _layer_norm(x2, g, b, tm=512):
    T, D = x2.shape
    return pl.pallas_call(
        _ln_kernel,
        grid=(T // tm,),
        in_specs=[pl.BlockSpec((tm, D), lambda i: (i, 0)),
                  pl.BlockSpec((1, D), lambda i: (0, 0)),
                  pl.BlockSpec((1, D), lambda i: (0, 0))],
        out_specs=pl.BlockSpec((tm, D), lambda i: (i, 0)),
        out_shape=jax.ShapeDtypeStruct((T, D), F32),
        compiler_params=_cparams(("parallel",)),
        name="ln0",
    )(x2, g.reshape(1, D), b.reshape(1, D))


def _rope(x, cos_f, sin_a, sin_b):
    half = ROPE_DIM // 2
    return x * cos_f + pltpu.roll(x, LANES - half, 1) * sin_a + pltpu.roll(x, half, 1) * sin_b


def _inproj_kernel(h_ref, w_ref, cos_ref, sa_ref, sb_ref, uh_ref, qkv_ref, up_ref):
    hb = h_ref[...].astype(BF16)
    mm = lambda c0, c1: jnp.dot(hb, w_ref[:, c0:c1], preferred_element_type=F32)
    o_q = 3 * HY_WIDTH
    o_k = o_q + ATT_WIDTH
    o_v = o_k + ATT_WIDTH
    o_p = o_v + ATT_WIDTH
    uh_ref[...] = mm(0, o_q)
    up_ref[...] = mm(o_p, IN_WIDTH)
    cos_f, sin_a, sin_b = cos_ref[...], sa_ref[...], sb_ref[...]
    q = mm(o_q, o_k)
    k = mm(o_k, o_v)
    qscale = ATT_HEAD_DIM ** -0.5 * LOG2E
    for hd in range(ATT_HEADS):
        sl = slice(hd * LANES, (hd + 1) * LANES)
        qkv_ref[:, sl] = (_rope(q[:, sl], cos_f, sin_a, sin_b) * qscale).astype(BF16)
        qkv_ref[:, ATT_WIDTH + hd * LANES:ATT_WIDTH + (hd + 1) * LANES] = \
            _rope(k[:, sl], cos_f, sin_a, sin_b).astype(BF16)
    qkv_ref[:, 2 * ATT_WIDTH:] = mm(o_v, o_p).astype(BF16)


def _in_proj(h2, w_bf, tables, seq, tm=512):
    T, D = h2.shape
    N = w_bf.shape[1]
    spt = seq // tm
    tspec = pl.BlockSpec((tm, LANES), lambda i: (i % spt, 0))
    row = lambda c: pl.BlockSpec((tm, c), lambda i: (i, 0))
    return pl.pallas_call(
        _inproj_kernel,
        grid=(T // tm,),
        in_specs=[row(D), pl.BlockSpec((D, N), lambda i: (0, 0)), tspec, tspec, tspec],
        out_specs=[row(3 * HY_WIDTH), row(3 * ATT_WIDTH), row(POOL_WIDTH)],
        out_shape=[jax.ShapeDtypeStruct((T, 3 * HY_WIDTH), F32),
                   jax.ShapeDtypeStruct((T, 3 * ATT_WIDTH), BF16),
                   jax.ShapeDtypeStruct((T, POOL_WIDTH), F32)],
        compiler_params=_cparams(("parallel",)),
        name="in_proj",
    )(h2, w_bf, *tables)


def _hy_filter_kernel(band_ref, phase_ref, fw1_ref, fb1_ref, fr1_ref, fw2_ref, fb2_ref, fr2_ref, fw3_ref,
                      dl_ref, k_ref, ssq_ref, z_scr, *, seq, tr):
    j = pl.program_id(0)
    l = pl.program_id(1)
    n = j * tr + lax.broadcasted_iota(jnp.int32, (tr, 1), 0)
    pos = jnp.where(n < seq, n, 2 * seq - n).astype(F32)
    t = pos * (1.0 / (seq - 1))

    @pl.when(l == 0)
    def _():
        wpos = (2.0 * math.pi / seq) * pos
        lane = lax.broadcasted_iota(jnp.int32, (tr, LANES), 1)
        z_scr[...] = jnp.where(lane == 0, t,
                               jnp.where(lane < HY_EMB, jnp.sin(wpos * band_ref[...] + phase_ref[...]), 0.0))

    @pl.when((l == 0) & (j == 0))
    def _():
        ssq_ref[...] = jnp.zeros_like(ssq_ref)

    z = z_scr[...]
    h1 = _dot3(*_split(z), *_split(fw1_ref[...])) + fb1_ref[...]
    h1 = jnp.sin(fr1_ref[...] * h1)
    h2 = _dot3(*_split(h1), *_split(fw2_ref[...])) + fb2_ref[...]
    h2 = jnp.sin(fr2_ref[...] * h2)
    filt = _dot3(*_split(h2), *_split(fw3_ref[...]))
    window = jnp.exp(-t * jnp.abs(dl_ref[...]))
    kk = jnp.where(n < seq, filt[:, :HY_WIDTH], filt[:, HY_WIDTH:]) * window
    kk = jnp.where(n == seq, 0.0, kk)
    k_ref[...] = kk
    ssq_ref[l] += jnp.sum(kk * kk, axis=0, keepdims=True)


def _hy_filter(seq, fw1, fb1, freq1, fw2, fb2, freq2, fw3, tr=1024):
    L = fw1.shape[0]
    n = 2 * seq
    H = HY_FILTER_HIDDEN
    bands = np.linspace(1e-4, HY_BANDS - 1, HY_BANDS)
    bandv = np.zeros((1, LANES), np.float32)
    bandv[0, 1:1 + HY_BANDS] = bands
    bandv[0, 1 + HY_BANDS:HY_EMB] = bands
    phase = np.zeros((1, LANES), np.float32)
    phase[0, 1:1 + HY_BANDS] = 0.5 * math.pi
    phase[0, 1 + HY_BANDS:HY_EMB] = math.pi
    fw1p = jnp.zeros((L, LANES, H), F32).at[:, :HY_EMB].set(fw1)
    max_decay = math.log(HY_DECAY_TARGET) / HY_FAST_DECAY
    min_decay = math.log(HY_DECAY_TARGET) / HY_SLOW_DECAY
    deltas = np.linspace(min_decay, max_decay, HY_WIDTH).astype(np.float32).reshape(1, HY_WIDTH)
    vec = lambda a: a.reshape(L, 1, a.shape[-1])
    lspec = lambda r, c: pl.BlockSpec((None, r, c), lambda j, l: (l, 0, 0))
    lanes = pl.BlockSpec((1, LANES), lambda j, l: (0, 0))
    return pl.pallas_call(
        functools.partial(_hy_filter_kernel, seq=seq, tr=tr),
        grid=(n // tr, L),
        in_specs=[lanes, lanes,
                  lspec(LANES, H), lspec(1, H), lspec(1, H),
                  lspec(H, H), lspec(1, H), lspec(1, H),
                  lspec(H, 2 * HY_WIDTH),
                  pl.BlockSpec((1, HY_WIDTH), lambda j, l: (0, 0))],
        out_specs=[pl.BlockSpec((None, tr, HY_WIDTH), lambda j, l: (l, j, 0)),
                   pl.BlockSpec((L, 1, HY_WIDTH), lambda j, l: (0, 0, 0))],
        out_shape=[jax.ShapeDtypeStruct((L, n, HY_WIDTH), F32),
                   jax.ShapeDtypeStruct((L, 1, HY_WIDTH), F32)],
        scratch_shapes=[pltpu.VMEM((tr, LANES), F32)],
        compiler_params=_cparams(("arbitrary", "arbitrary")),
        name="hy_filter",
    )(jnp.asarray(bandv), jnp.asarray(phase), fw1p, vec(fb1), vec(freq1), fw2, vec(fb2), vec(freq2), fw3,
      jnp.asarray(deltas))


def _k1_rows(n1):
    k1 = n1 // 2 + 1
    return k1, -(-k1 // 8) * 8


def _bf_pair(m):
    m32 = jnp.asarray(np.asarray(m, np.float32))
    return _split(m32)


def _stage_a_kernel(x_ref, frh_ref, frl_ref, fih_ref, fil_ref, ar_ref, ai_ref):
    xh, xl = _split(x_ref[...])
    ar_ref[...] = _dot3(frh_ref[...], frl_ref[...], xh, xl)
    ai_ref[...] = _dot3(fih_ref[...], fil_ref[...], xh, xl)


def _stage_a(x3, n1_total, tl=4096):
    Bt, n1u, LN = x3.shape
    k1n, k1p = _k1_rows(n1_total)
    ang = 2.0 * np.pi * np.outer(np.arange(k1p), np.arange(n1u)) / n1_total
    valid = (np.arange(k1p) < k1n)[:, None]
    frh, frl = _bf_pair(np.cos(ang) * valid)
    fih, fil = _bf_pair(-np.sin(ang) * valid)
    cspec = pl.BlockSpec((k1p, n1u), lambda b, j: (0, 0))
    ospec = pl.BlockSpec((None, k1p, tl), lambda b, j: (b, 0, j))
    return pl.pallas_call(
        _stage_a_kernel,
        grid=(Bt, LN // tl),
        in_specs=[pl.BlockSpec((None, n1u, tl), lambda b, j: (b, 0, j)), cspec, cspec, cspec, cspec],
        out_specs=[ospec, ospec],
        out_shape=[jax.ShapeDtypeStruct((Bt, k1p, LN), F32)] * 2,
        compiler_params=_cparams(("parallel", "parallel")),
        name="hy_stage_a",
    )(x3, frh, frl, fih, fil)


def _stage_b_consts(n1_total):
    n = n1_total * FFT_N2
    k1n, k1p = _k1_rows(n1_total)
    ang2 = 2.0 * np.pi * np.outer(np.arange(FFT_N2), np.arange(FFT_N2)) / FFT_N2
    c2, s2 = np.cos(ang2), np.sin(ang2)
    mf = np.block([[c2, s2], [-s2, c2]])
    mi = np.block([[c2, -s2], [s2, c2]])
    angt = 2.0 * np.pi * np.outer(np.arange(k1p), np.arange(FFT_N2)) / n
    twc = jnp.asarray(np.cos(angt).astype(np.float32)).reshape(k1p, FFT_N2, 1)
    tws = jnp.asarray(np.sin(angt).astype(np.float32)).reshape(k1p, FFT_N2, 1)
    return _bf_pair(mf), _bf_pair(mi), twc, tws


def _fwd_b(ar, ai, c, s, mfh, mfl):
    tr_ = ar * c + ai * s
    ti_ = ai * c - ar * s
    xh, xl = _split(jnp.concatenate([tr_, ti_], axis=0))
    z = _dot3(mfh, mfl, xh, xl)
    return z[:FFT_N2], z[FFT_N2:]


def _stage_bk_kernel(ar_ref, ai_ref, c_ref, s_ref, mfh_ref, mfl_ref, ssq_ref, kr_ref, ki_ref,
                     *, k1n, inv_n):
    scale = lax.rsqrt(ssq_ref[...] + 1e-6) * inv_n
    for kk in range(ar_ref.shape[0]):
        k1 = pl.program_id(1) * ar_ref.shape[0] + kk

        @pl.when(k1 < k1n)
        def _(kk=kk):
            zr, zi = _fwd_b(ar_ref[kk], ai_ref[kk], c_ref[kk], s_ref[kk], mfh_ref[...], mfl_ref[...])
            kr_ref[kk] = zr * scale
            ki_ref[kk] = zi * scale

        @pl.when(k1 >= k1n)
        def _(kk=kk):
            kr_ref[kk] = jnp.zeros((FFT_N2, kr_ref.shape[-1]), F32)
            ki_ref[kk] = jnp.zeros((FFT_N2, ki_ref.shape[-1]), F32)


def _stage_b_filter(ar4, ai4, ssq, consts, n1_total):
    L, k1p, _, C = ar4.shape
    k1n, _ = _k1_rows(n1_total)
    (mfh, mfl), _, twc, tws = consts
    kb = SUBLANES
    blk = pl.BlockSpec((None, kb, FFT_N2, C), lambda l, k: (l, k, 0, 0))
    tw = pl.BlockSpec((kb, FFT_N2, 1), lambda l, k: (k, 0, 0))
    mat = pl.BlockSpec((2 * FFT_N2, 2 * FFT_N2), lambda l, k: (0, 0))
    return pl.pallas_call(
        functools.partial(_stage_bk_kernel, k1n=k1n, inv_n=1.0 / (n1_total * FFT_N2)),
        grid=(L, k1p // kb),
        in_specs=[blk, blk, tw, tw, mat, mat, pl.BlockSpec((None, 1, C), lambda l, k: (l, 0, 0))],
        out_specs=[blk, blk],
        out_shape=[jax.ShapeDtypeStruct(ar4.shape, F32)] * 2,
        compiler_params=_cparams(("parallel", "parallel")),
        name="hy_stage_b_filter",
    )(ar4, ai4, twc, tws, mfh, mfl, ssq)


SUBLANES = 8


def _hy_conv_kernel(vv_ref, x0_ref, kr_ref, ki_ref, tw_ref, kah_ref, kal_ref, kch_ref, kcl_ref,
                    mfh_ref, mfl_ref, mih_ref, mil_ref, bias_ref, o_ref, a_scr, *, n1u, k1n, k1p):
    groups = FFT_N2 // SUBLANES

    def rows(n1, g):
        return pl.ds(pl.multiple_of(n1 * FFT_N2 + g * SUBLANES, SUBLANES), SUBLANES)

    def stage_a(g, carry):
        xg = jnp.concatenate([vv_ref[rows(n1, g), :] for n1 in range(n1u)], axis=0)
        ag = _dot3(kah_ref[...], kal_ref[...], *_split(xg))
        gs = pl.ds(pl.multiple_of(g * SUBLANES, SUBLANES), SUBLANES)
        for part in range(2):
            for k1 in range(k1p):
                r = (part * k1p + k1) * SUBLANES
                a_scr[part, k1, gs, :] = ag[r:r + SUBLANES]
        return carry

    lax.fori_loop(0, groups, stage_a, 0, unroll=2)

    def stage_b(k1, carry):
        tw = tw_ref[k1]
        c = tw[:, 0:1]
        s = tw[:, 1:2]
        zr, zi = _fwd_b(a_scr[0, k1], a_scr[1, k1], c, s, mfh_ref[...], mfl_ref[...])
        kr = kr_ref[k1]
        ki = ki_ref[k1]
        pr = zr * kr - zi * ki
        pi = zr * ki + zi * kr
        y = _dot3(mih_ref[...], mil_ref[...], *_split(jnp.concatenate([pr, pi], axis=0)))
        yr = y[:FFT_N2]
        yi = y[FFT_N2:]
        a_scr[0, k1] = yr * c - yi * s
        a_scr[1, k1] = yi * c + yr * s
        return carry

    lax.fori_loop(0, k1n, stage_b, 0, unroll=11 if k1n % 11 == 0 else 1)

    def stage_c(g, carry):
        gs = pl.ds(pl.multiple_of(g * SUBLANES, SUBLANES), SUBLANES)
        bg = jnp.concatenate([a_scr[part, k1, gs, :] for part in range(2) for k1 in range(k1p)], axis=0)
        yg = _dot3(kch_ref[...], kcl_ref[...], *_split(bg))
        for n1 in range(n1u):
            r = rows(n1, g)
            o_ref[r, :] = (yg[n1 * SUBLANES:(n1 + 1) * SUBLANES] + vv_ref[r, :] * bias_ref[...]) * x0_ref[r, :]
        return carry

    lax.fori_loop(0, groups, stage_c, 0, unroll=2)


def _hy_conv(vv, x0c, kr3, ki3, consts, bias_d, n1_total):
    B, S, C = vv.shape
    n1u = S // FFT_N2
    k1n, k1p = _k1_rows(n1_total)
    (mfh, mfl), (mih, mil), twc, tws = consts
    tw = jnp.concatenate([twc, tws], axis=-1)
    k1 = np.arange(k1p)
    valid = (k1 < k1n)[:, None]
    eye = np.eye(SUBLANES)
    ang_a = 2.0 * np.pi * np.outer(k1, np.arange(n1u)) / n1_total
    ka = np.concatenate([np.kron(np.cos(ang_a) * valid, eye), np.kron(-np.sin(ang_a) * valid, eye)], axis=0)
    w = np.where((k1 == 0) | (k1 == n1_total // 2), 1.0, 2.0) * (k1 < k1n)
    ang_c = 2.0 * np.pi * np.outer(np.arange(n1u), k1) / n1_total
    kc = np.concatenate([np.kron(np.cos(ang_c) * w[None, :], eye), np.kron(-np.sin(ang_c) * w[None, :], eye)], axis=1)
    kah, kal = _bf_pair(ka)
    kch, kcl = _bf_pair(kc)
    nct = C // LANES
    xspec = pl.BlockSpec((None, S, LANES), lambda c, b: (b, 0, c))
    kspec = pl.BlockSpec((k1p, FFT_N2, LANES), lambda c, b: (0, 0, c))
    full = lambda a: pl.BlockSpec(a.shape, lambda c, b: (0,) * a.ndim)
    return pl.pallas_call(
        functools.partial(_hy_conv_kernel, n1u=n1u, k1n=k1n, k1p=k1p),
        grid=(nct, B),
        in_specs=[xspec, xspec, kspec, kspec, full(tw), full(kah), full(kal), full(kch), full(kcl),
                  full(mfh), full(mfl), full(mih), full(mil),
                  pl.BlockSpec((1, LANES), lambda c, b: (0, c))],
        out_specs=xspec,
        out_shape=jax.ShapeDtypeStruct((B, S, C), F32),
        scratch_shapes=[pltpu.VMEM((2, k1p, FFT_N2, LANES), F32)],
        compiler_params=_cparams(("parallel", "parallel")),
        name="hy_conv",
    )(vv, x0c, kr3, ki3, tw, kah, kal, kch, kcl, mfh, mfl, mih, mil, bias_d.reshape(1, C))


def _conv3(x, w_ref, b_ref):
    S = x.shape[0]
    row = lax.broadcasted_iota(jnp.int32, x.shape, 0)
    prev = jnp.where(row == 0, 0.0, pltpu.roll(x, 1, 0))
    nxt = jnp.where(row == S - 1, 0.0, pltpu.roll(x, S - 1, 0))
    return prev * w_ref[0:1, :] + x * w_ref[1:2, :] + nxt * w_ref[2:3, :] + b_ref[...]


def _hy_gate_kernel(x0_ref, x1_ref, v_ref, w0_ref, w1_ref, w2_ref, b0_ref, b1_ref, b2_ref,
                    vv_ref, x0c_ref):
    x0c_ref[...] = _conv3(x0_ref[...], w0_ref, b0_ref)
    vv_ref[...] = _conv3(v_ref[...], w2_ref, b2_ref) * _conv3(x1_ref[...], w1_ref, b1_ref)


def _hy_gate(u3, conv_w, conv_b):
    B, S, _ = u3.shape
    nct = HY_WIDTH // LANES
    conv_b2 = conv_b.reshape(1, 3 * HY_WIDTH)
    uspec = lambda part: pl.BlockSpec((None, S, LANES), lambda b, c: (b, 0, part * nct + c))
    wspec = lambda part: pl.BlockSpec((3, LANES), lambda b, c: (0, part * nct + c))
    bspec = lambda part: pl.BlockSpec((1, LANES), lambda b, c: (0, part * nct + c))
    ospec = pl.BlockSpec((None, S, LANES), lambda b, c: (b, 0, c))
    return pl.pallas_call(
        _hy_gate_kernel,
        grid=(B, nct),
        in_specs=[uspec(0), uspec(1), uspec(2), wspec(0), wspec(1), wspec(2),
                  bspec(0), bspec(1), bspec(2)],
        out_specs=[ospec, ospec],
        out_shape=[jax.ShapeDtypeStruct((B, S, HY_WIDTH), F32)] * 2,
        compiler_params=_cparams(("parallel", "parallel")),
        name="hy_gate",
    )(u3, u3, u3, conv_w, conv_w, conv_w, conv_b2, conv_b2, conv_b2)


def _attn_kernel(q_ref, k_ref, v_ref, lq1_ref, lk1_ref, lq2_ref, lk2_ref, g_ref, o_ref,
                 s_scr, p_scr, vx_scr, kt_scr, *, lam_init, tq):
    S = q_ref.shape[0]
    nblk = S // tq
    lam = (jnp.exp(jnp.sum(lq1_ref[...] * lk1_ref[...], keepdims=True))
           - jnp.exp(jnp.sum(lq2_ref[...] * lk2_ref[...], keepdims=True)) + lam_init)

    def blk(i):
        return pl.ds(pl.multiple_of(i * tq, tq), tq)

    kt = k_ref[...].T
    feat = lax.broadcasted_iota(jnp.int32, kt.shape, 0)
    kt_scr[0] = jnp.where(feat < ATT_HEAD_DIM, kt, jnp.zeros_like(kt))
    kt_scr[1] = jnp.where(feat >= ATT_HEAD_DIM, kt, jnp.zeros_like(kt))

    def scores(i, slot):
        q = q_ref[blk(i), :]
        for c in range(2):
            s_scr[slot, c] = jnp.dot(q, kt_scr[c], preferred_element_type=F32)

    vx_scr[:, :LANES] = v_ref[...]
    vx_scr[:, LANES:] = (lax.broadcasted_iota(jnp.int32, (S, LANES), 1) == 0).astype(BF16)

    def softmax(slot):
        for c in range(2):
            s = s_scr[slot, c]
            m = jnp.max(s, axis=-1, keepdims=True)
            p_scr[slot, c] = jnp.exp2(s - m).astype(BF16)

    def values(i, slot):
        oe1 = jnp.dot(p_scr[slot, 0], vx_scr[...], preferred_element_type=F32)
        oe2 = jnp.dot(p_scr[slot, 1], vx_scr[...], preferred_element_type=F32)
        o = oe1[:, :LANES] / oe1[:, LANES:LANES + 1] - oe2[:, :LANES] * (lam / oe2[:, LANES:LANES + 1])
        o = o * lax.rsqrt(jnp.mean(o * o, axis=-1, keepdims=True) + RMS_EPS) * g_ref[...]
        o_ref[blk(i), :] = o * (1.0 - lam_init)

    scores(0, 0)
    softmax(0)
    scores(1, 1)

    def pair(j, carry):
        t = 2 * j
        values(t - 2, 0)
        softmax(1)
        scores(t, 0)
        values(t - 1, 1)
        softmax(0)
        scores(t + 1, 1)
        return carry

    lax.fori_loop(1, nblk // 2, pair, 0)
    values(nblk - 2, 0)
    softmax(1)
    values(nblk - 1, 1)


def _rope_tables(seq):
    pos = np.arange(seq, dtype=np.float64)
    inv_freq = np.power(ROPE_THETA, -np.arange(0, ROPE_DIM, 2, dtype=np.float64) / ROPE_DIM)
    ang = pos[:, None] * inv_freq[None, :]
    half = ROPE_DIM // 2
    cos_f = np.ones((seq, LANES), np.float32)
    sin_a = np.zeros((seq, LANES), np.float32)
    sin_b = np.zeros((seq, LANES), np.float32)
    for base in range(0, LANES, ATT_HEAD_DIM):
        cos_f[:, base:base + half] = np.cos(ang)
        cos_f[:, base + half:base + ROPE_DIM] = np.cos(ang)
        sin_a[:, base:base + half] = -np.sin(ang)
        sin_b[:, base + half:base + ROPE_DIM] = np.sin(ang)
    return jnp.asarray(cos_f), jnp.asarray(sin_a), jnp.asarray(sin_b)


def _diff_attention(qkv3, lq1, lk1, lq2, lk2, subln_g, lam_init, tq=256):
    B, S, _ = qkv3.shape
    nb = ATT_WIDTH // LANES
    assert S % (2 * tq) == 0
    spec = lambda part: pl.BlockSpec((None, S, LANES), lambda b, h: (b, 0, part * nb + h))
    vec = pl.BlockSpec((1, ATT_HEAD_DIM), lambda b, h: (0, 0))
    r1 = lambda a: a.reshape(1, -1)
    return pl.pallas_call(
        functools.partial(_attn_kernel, lam_init=lam_init, tq=tq),
        grid=(B, ATT_HEADS),
        in_specs=[spec(0), spec(1), spec(2), vec, vec, vec, vec,
                  pl.BlockSpec((1, LANES), lambda b, h: (0, 0))],
        out_specs=pl.BlockSpec((None, S, LANES), lambda b, h: (b, 0, h)),
        out_shape=jax.ShapeDtypeStruct((B, S, ATT_WIDTH), F32),
        scratch_shapes=[pltpu.VMEM((2, 2, tq, S), F32), pltpu.VMEM((2, 2, tq, S), BF16),
                        pltpu.VMEM((S, 2 * LANES), BF16), pltpu.VMEM((2, LANES, S), BF16)],
        compiler_params=_cparams(("parallel", "parallel")),
        name="diff_attn",
    )(qkv3, qkv3, qkv3, r1(lq1), r1(lk1), r1(lq2), r1(lk2), r1(subln_g))


def _pool_kernel(u_ref, w_ref, b_ref, sc_ref, o_ref):
    ct = pl.program_id(1)
    x = u_ref[...]
    S = x.shape[0]
    row = lax.broadcasted_iota(jnp.int32, x.shape, 0)
    lane = lax.broadcasted_iota(jnp.int32, x.shape, 1)

    def shifted(d):
        if d == 0:
            return x
        r = pltpu.roll(x, (-d) % S, 0)
        return jnp.where((row + d >= 0) & (row + d < S), r, 0.0)

    sums = {}
    acc = shifted(-1) + x
    sums[2] = acc
    lo, hi = -1, 0
    for w in POOL_WINDOWS[1:]:
        for d in list(range(-(w // 2), lo)) + list(range(hi + 1, w // 2)):
            acc = acc + shifted(d)
        lo, hi = -(w // 2), w // 2 - 1
        sums[w] = acc
    grp = 2 * ct + (lane >= POOL_GROUP).astype(jnp.int32)
    win_sum = sums[POOL_WINDOWS[-1]]
    half = jnp.full(x.shape, POOL_WINDOWS[-1] // 2, jnp.int32)
    for g in range(len(POOL_WINDOWS) - 2, -1, -1):
        win_sum = jnp.where(grp == g, sums[POOL_WINDOWS[g]], win_sum)
        half = jnp.where(grp == g, POOL_WINDOWS[g] // 2, half)
    cnt = jnp.minimum(row + half - 1, S - 1) - jnp.maximum(row - half, 0) + 1
    y = win_sum / cnt.astype(F32) - x
    y = _dot3(*_split(y), *_split(w_ref[...])) + b_ref[...]
    o_ref[...] = y * sc_ref[...]


def _pool_mixer(u3, w, b, scale):
    B, S, _ = u3.shape
    nct = POOL_WIDTH // LANES
    gpt = LANES // POOL_GROUP
    wbd = jnp.zeros((nct, LANES, LANES), F32)
    for g in range(len(POOL_WINDOWS)):
        t, o = divmod(g, gpt)
        wbd = wbd.at[t, o * POOL_GROUP:(o + 1) * POOL_GROUP, o * POOL_GROUP:(o + 1) * POOL_GROUP].set(w[g])
    vspec = pl.BlockSpec((1, LANES), lambda bb, c: (0, c))
    return pl.pallas_call(
        _pool_kernel,
        grid=(B, nct),
        in_specs=[pl.BlockSpec((None, S, LANES), lambda bb, c: (bb, 0, c)),
                  pl.BlockSpec((None, LANES, LANES), lambda bb, c: (c, 0, 0)),
                  vspec, vspec],
        out_specs=pl.BlockSpec((None, S, LANES), lambda bb, c: (bb, 0, c)),
        out_shape=jax.ShapeDtypeStruct((B, S, POOL_WIDTH), F32),
        compiler_params=_cparams(("parallel", "parallel")),
        name="pool_mixer",
    )(u3, wbd, b.reshape(1, POOL_WIDTH), scale.reshape(1, POOL_WIDTH))


PAIRS_PER_GROUP = 6
N_CLASSES = N_GROUPS * PAIRS_PER_GROUP
ROUTE_W_LO, ROUTE_W_HI, ROUTE_CLS, ROUTE_RANK = 0, 1, 2, 3
TOKEN_ROWS = D_MODEL // LANES


def _store_token_major(ref, x, first_row, unit_rows):
    n = x.shape[0]
    for j in range(TOKEN_ROWS):
        ref[pl.ds(first_row + j, n, stride=unit_rows), :] = x[:, j * LANES:(j + 1) * LANES]


def _load_token_major(ref, n, first_row, unit_rows):
    return jnp.concatenate([ref[pl.ds(first_row + j, n, stride=unit_rows), :] for j in range(TOKEN_ROWS)],
                           axis=1)


def _class_experts(c):
    g, pidx = divmod(c, PAIRS_PER_GROUP)
    pairs = [(a, b) for a in range(EXPERTS_PER_GROUP) for b in range(a + 1, EXPERTS_PER_GROUP)]
    lo, hi = pairs[pidx]
    return g * EXPERTS_PER_GROUP + lo, g * EXPERTS_PER_GROUP + hi


def _outproj_kernel(h_ref, yh_ref, ya_ref, yp_ref, wh_ref, wa_ref, wp_ref, g_ref, b_ref,
                    wgh_ref, wgl_ref, bg_ref, tri_ref, h1_ref, h1t_ref, route_ref, cnt_ref, carry_scr):
    d = functools.partial(jnp.dot, preferred_element_type=F32)
    mix = (d(yh_ref[...].astype(BF16), wh_ref[...]) + d(ya_ref[...].astype(BF16), wa_ref[...])
           + d(yp_ref[...].astype(BF16), wp_ref[...]))
    h1 = _ln_rows(DN_ALPHA * h_ref[...] + mix, g_ref[...], b_ref[...])
    h1_ref[...] = h1
    _store_token_major(h1t_ref, h1, 0, TOKEN_ROWS)

    @pl.when(pl.program_id(0) == 0)
    def _():
        carry_scr[...] = jnp.zeros_like(carry_scr)

    logit = _dot3(*_split(h1), wgh_ref[...], wgl_ref[...]) + bg_ref[...]
    lni = lax.broadcasted_iota(jnp.int32, logit.shape, 1)
    ln = lni.astype(F32)
    grp = lax.shift_right_arithmetic(lni - GATE_FINE_LANE, 2).astype(F32)
    first = lambda mask: jnp.min(jnp.where(mask, ln, float(LANES)), axis=-1, keepdims=True)
    cmask = lni < N_GROUPS
    lc = jnp.where(cmask, logit, NEG_BIG)
    mc = jnp.max(lc, axis=-1, keepdims=True)
    gw = 1.0 / jnp.sum(jnp.where(cmask, jnp.exp(lc - mc), 0.0), axis=-1, keepdims=True)
    gi = first(cmask & (lc == mc))
    fmask = (lni >= GATE_FINE_LANE) & (lni < GATE_FINE_LANE + N_EXPERTS) & (grp == gi)
    lf = jnp.where(fmask, logit, NEG_BIG)
    m1 = jnp.max(lf, axis=-1, keepdims=True)
    i1 = first(fmask & (lf == m1))
    rest = fmask & (ln != i1)
    lf2 = jnp.where(rest, logit, NEG_BIG)
    m2 = jnp.max(lf2, axis=-1, keepdims=True)
    i2 = first(rest & (lf2 == m2))
    e2 = jnp.exp(m2 - m1)
    w1 = gw / (1.0 + e2)
    w2 = gw * e2 / (1.0 + e2)

    j1 = i1 - GATE_FINE_LANE - EXPERTS_PER_GROUP * gi
    j2 = i2 - GATE_FINE_LANE - EXPERTS_PER_GROUP * gi
    lo = jnp.minimum(j1, j2)
    hi = jnp.maximum(j1, j2)
    w_lo = jnp.where(j1 < j2, w1, w2)
    w_hi = jnp.where(j1 < j2, w2, w1)
    base = jnp.where(lo == 0.0, 0.0, jnp.where(lo == 1.0, 3.0, 5.0))
    cls = gi * PAIRS_PER_GROUP + base + hi - lo - 1.0
    onehot = ln == cls
    before = jnp.dot(tri_ref[...], onehot.astype(BF16), preferred_element_type=F32)
    carry = carry_scr[...]
    rank = jnp.sum(jnp.where(onehot, before + carry, 0.0), axis=-1, keepdims=True)
    carry = carry + jnp.sum(onehot.astype(F32), axis=0, keepdims=True)
    carry_scr[...] = carry
    cnt_ref[...] = carry
    route_ref[...] = jnp.where(lni == ROUTE_W_LO, w_lo,
                               jnp.where(lni == ROUTE_W_HI, w_hi,
                                         jnp.where(lni == ROUTE_CLS, cls,
                                                   jnp.where(lni == ROUTE_RANK, rank, 0.0))))


def _out_proj_ln_route(h2, yh, ya, yp, w_out_bf, g, b, wgc, bgc, wgf, bgf, tm=512):
    T, D = h2.shape
    tri = jnp.asarray(np.tril(np.ones((tm, tm), np.float32), -1)).astype(BF16)
    wg = jnp.zeros((D, LANES), F32).at[:, :N_GROUPS].set(wgc)
    wg = wg.at[:, GATE_FINE_LANE:GATE_FINE_LANE + N_EXPERTS].set(wgf)
    bg = jnp.zeros((1, LANES), F32).at[0, :N_GROUPS].set(bgc)
    bg = bg.at[0, GATE_FINE_LANE:GATE_FINE_LANE + N_EXPERTS].set(bgf)
    wgh, wgl = _split(wg)
    o1, o2 = HY_WIDTH, HY_WIDTH + ATT_WIDTH
    row = lambda c: pl.BlockSpec((tm, c), lambda i: (i, 0))
    full = lambda r, c: pl.BlockSpec((r, c), lambda i: (0, 0))
    return pl.pallas_call(
        _outproj_kernel,
        grid=(T // tm,),
        in_specs=[row(D), row(HY_WIDTH), row(ATT_WIDTH), row(POOL_WIDTH),
                  full(HY_WIDTH, D), full(ATT_WIDTH, D), full(POOL_WIDTH, D),
                  full(1, D), full(1, D), full(D, LANES), full(D, LANES), full(1, LANES), full(tm, tm)],
        out_specs=[row(D), pl.BlockSpec((tm * TOKEN_ROWS, LANES), lambda i: (i, 0)), row(LANES), full(1, LANES)],
        out_shape=[jax.ShapeDtypeStruct((T, D), F32), jax.ShapeDtypeStruct((T * TOKEN_ROWS, LANES), F32),
                   jax.ShapeDtypeStruct((T, LANES), F32), jax.ShapeDtypeStruct((1, LANES), F32)],
        scratch_shapes=[pltpu.VMEM((1, LANES), F32)],
        compiler_params=_cparams(("arbitrary",)),
        name="out_proj_ln_route",
    )(h2, yh, ya, yp, w_out_bf[:o1], w_out_bf[o1:o2], w_out_bf[o2:], g.reshape(1, D), b.reshape(1, D),
      wgh, wgl, bg, tri)


MOE_TILE = 256
DMA_BATCH = 256


def _moe_rows_padded(T):
    return T + N_CLASSES * MOE_TILE


def _token_rows(ref, t, unit_rows):
    return ref.at[pl.ds(pl.multiple_of(t * unit_rows, unit_rows), unit_rows)]


DMA_GROUP = 8


def _start_tokens(n, slot_of, start_copy):
    def group(g, carry):
        first = g * DMA_GROUP
        slots = [slot_of(first + k) for k in range(DMA_GROUP)]
        for k in range(DMA_GROUP):
            start_copy(first + k, slots[k], k % 2)
        return carry

    lax.fori_loop(0, n // DMA_GROUP, group, 0)


def _wait_tokens(n, src_ref, dst_ref, unit_rows, sems):
    rows = pl.ds(0, (n // 2) * unit_rows)
    for sem in sems:
        pltpu.make_async_copy(src_ref.at[rows], dst_ref.at[rows], sem).wait()


def _dispatch_kernel(cls_ref, rank_ref, cnt_ref, h1t_ref, xs_ref, dest_ref, tlo_ref, thi_ref, tval_ref,
                     off_scr, zero_scr, sem, *, tm, n_tiles):
    i = pl.program_id(0)
    base = i * tm
    tile_rows = MOE_TILE * TOKEN_ROWS

    def fill_copy(tile):
        rows = pl.ds(pl.multiple_of(tile * tile_rows, tile_rows), tile_rows)
        return pltpu.make_async_copy(zero_scr, xs_ref.at[rows], sem.at[2])

    @pl.when(i == 0)
    def _():
        zero_scr[...] = jnp.zeros_like(zero_scr)
        off = jnp.int32(0)
        tile = jnp.int32(0)
        fills = []
        for c in range(N_CLASSES):
            n = cnt_ref[c]
            nt = lax.shift_right_logical(n + (MOE_TILE - 1), MOE_TILE.bit_length() - 1)
            off_scr[c] = off
            e_lo, e_hi = _class_experts(c)

            def mark(k, carry, tile=tile, e_lo=e_lo, e_hi=e_hi):
                tlo_ref[tile + k] = e_lo
                thi_ref[tile + k] = e_hi
                tval_ref[tile + k] = 1
                return carry

            lax.fori_loop(0, nt, mark, 0)
            fill = fill_copy(tile + nt - 1)
            fills.append((nt > 0, fill))

            @pl.when(nt > 0)
            def _(fill=fill):
                fill.start()

            off = off + nt * MOE_TILE
            tile = tile + nt

        def unused(k, carry):
            tlo_ref[k] = 0
            thi_ref[k] = 0
            tval_ref[k] = 0
            fill_copy(k).start()
            return carry

        lax.fori_loop(tile, n_tiles, unused, 0)
        for used, fill in fills:
            @pl.when(used)
            def _(fill=fill):
                fill.wait()

        def unused_wait(k, carry):
            fill_copy(k).wait()
            return carry

        lax.fori_loop(tile, n_tiles, unused_wait, 0)

    def slots_of_step(step):
        def slot(r, carry):
            t = step * tm + r
            dest_ref[t] = off_scr[cls_ref[t]] + rank_ref[t]
            return carry

        lax.fori_loop(0, tm, slot, 0, unroll=8)

    @pl.when(i == 0)
    def _():
        slots_of_step(0)

    def start_copy(r, d, prio):
        pltpu.make_async_copy(_token_rows(h1t_ref, r, TOKEN_ROWS), _token_rows(xs_ref, d, TOKEN_ROWS),
                              sem.at[prio]).start(priority=prio)

    _start_tokens(tm, lambda r: dest_ref[base + r], start_copy)

    @pl.when(i + 1 < pl.num_programs(0))
    def _():
        slots_of_step(i + 1)

    _wait_tokens(tm, h1t_ref, xs_ref, TOKEN_ROWS, [sem.at[0], sem.at[1]])


def _moe_dispatch(h1t, cls_i, rank_i, cnt_i, tm=512):
    T = h1t.shape[0] // TOKEN_ROWS
    slots = _moe_rows_padded(T)
    n_tiles = slots // MOE_TILE
    smem = pl.BlockSpec(memory_space=pltpu.SMEM)
    return pl.pallas_call(
        functools.partial(_dispatch_kernel, tm=tm, n_tiles=n_tiles),
        grid_spec=pltpu.PrefetchScalarGridSpec(
            num_scalar_prefetch=3,
            grid=(T // tm,),
            in_specs=[pl.BlockSpec((tm * TOKEN_ROWS, LANES), lambda i, *_: (i, 0))],
            out_specs=[pl.BlockSpec(memory_space=pl.ANY), smem, smem, smem, smem],
            scratch_shapes=[pltpu.SMEM((N_CLASSES,), jnp.int32),
                            pltpu.VMEM((MOE_TILE * TOKEN_ROWS, LANES), F32),
                            pltpu.SemaphoreType.DMA((3,))],
        ),
        out_shape=[jax.ShapeDtypeStruct((slots * TOKEN_ROWS, LANES), F32), jax.ShapeDtypeStruct((T,), jnp.int32),
                   jax.ShapeDtypeStruct((n_tiles,), jnp.int32), jax.ShapeDtypeStruct((n_tiles,), jnp.int32),
                   jax.ShapeDtypeStruct((n_tiles,), jnp.int32)],
        compiler_params=_cparams(("arbitrary",)),
        name="moe_dispatch",
    )(cls_i, rank_i, cnt_i, h1t)


def _experts_kernel(tlo_ref, thi_ref, tval_ref, xs_ref, w1l_ref, w3l_ref, w2l_ref, w1h_ref, w3h_ref, w2h_ref,
                    ys_ref):
    j = pl.program_id(0)
    d = functools.partial(jnp.dot, preferred_element_type=F32)

    @pl.when(tval_ref[j] == 1)
    def _():
        xb = _load_token_major(xs_ref, MOE_TILE, 0, TOKEN_ROWS).astype(BF16)

        def expert(w1_ref, w3_ref, w2_ref):
            a = d(xb, w1_ref[...])
            c = d(xb, w3_ref[...])
            return d((a * jax.nn.sigmoid(a) * c).astype(BF16), w2_ref[...])

        _store_token_major(ys_ref.at[0], expert(w1l_ref, w3l_ref, w2l_ref), 0, TOKEN_ROWS)
        _store_token_major(ys_ref.at[1], expert(w1h_ref, w3h_ref, w2h_ref), 0, TOKEN_ROWS)

    @pl.when(tval_ref[j] == 0)
    def _():
        ys_ref[...] = jnp.zeros_like(ys_ref)


def _moe_experts(xs, tlo, thi, tval, w1_bf, w3_bf, w2_bf):
    slots = xs.shape[0] // TOKEN_ROWS
    D = D_MODEL
    wspec = lambda shape, which: pl.BlockSpec(
        (None,) + shape, (lambda j, tlo, thi, tval: (tlo[j], 0, 0)) if which == 0
        else (lambda j, tlo, thi, tval: (thi[j], 0, 0)))
    up = (D, D_EXPERT)
    down = (D_EXPERT, D)
    return pl.pallas_call(
        _experts_kernel,
        grid_spec=pltpu.PrefetchScalarGridSpec(
            num_scalar_prefetch=3,
            grid=(slots // MOE_TILE,),
            in_specs=[pl.BlockSpec((MOE_TILE * TOKEN_ROWS, LANES), lambda j, *_: (j, 0)),
                      wspec(up, 0), wspec(up, 0), wspec(down, 0), wspec(up, 1), wspec(up, 1), wspec(down, 1)],
            out_specs=pl.BlockSpec((None, 2, MOE_TILE * TOKEN_ROWS, LANES), lambda j, *_: (j, 0, 0, 0)),
        ),
        out_shape=jax.ShapeDtypeStruct((slots // MOE_TILE, 2, MOE_TILE * TOKEN_ROWS, LANES), F32),
        compiler_params=_cparams(("arbitrary",)),
        name="moe_experts",
    )(tlo, thi, tval, xs, w1_bf, w3_bf, w2_bf, w1_bf, w3_bf, w2_bf)


def _combine_kernel(dest_ref, h1_ref, route_ref, ys_ref, p_ref, pwg_ref, pbg_ref, pwp_ref, g_ref, b_ref,
                    o_ref, y_scr, ym_scr, sem, *, tm):
    i = pl.program_id(0)
    d = functools.partial(jnp.dot, preferred_element_type=F32)
    tile_shift = MOE_TILE.bit_length() - 1

    def unit_rows(t):
        return pl.ds(pl.multiple_of(t * TOKEN_ROWS, TOKEN_ROWS), TOKEN_ROWS)

    def gather(step, buf):
        def start_copy(r, slot, prio):
            tile = lax.shift_right_logical(slot, tile_shift)
            src = ys_ref.at[tile, :, unit_rows(slot & (MOE_TILE - 1)), :]
            pltpu.make_async_copy(src, y_scr.at[buf, :, unit_rows(r), :],
                                  sem.at[2 * buf + prio]).start(priority=prio)
        _start_tokens(tm, lambda r: dest_ref[step * tm + r], start_copy)

    def wait_gathers(buf):
        rows = pl.ds(0, (tm // 2) * TOKEN_ROWS)
        for prio in range(2):
            pltpu.make_async_copy(ys_ref.at[0, :, rows, :], y_scr.at[buf, :, rows, :],
                                  sem.at[2 * buf + prio]).wait()

    @pl.when(i == 0)
    def _():
        gather(0, 0)

    for buf in range(2):
        @pl.when((i + 1 < pl.num_programs(0)) & ((i + 1) % 2 == buf))
        def _(buf=buf):
            gather(i + 1, buf)

    h1 = h1_ref[...]
    z = d(h1.astype(BF16), pwg_ref[...]) + pbg_ref[...]
    y_ple = jax.nn.sigmoid(z) * d(p_ref[...].astype(BF16), pwp_ref[...])
    rec = route_ref[...]
    for buf in range(2):
        @pl.when(i % 2 == buf)
        def _(buf=buf):
            wait_gathers(buf)
            ym_scr[...] = (rec[:, ROUTE_W_LO:ROUTE_W_LO + 1]
                           * _load_token_major(y_scr.at[buf, 0], tm, 0, TOKEN_ROWS)
                           + rec[:, ROUTE_W_HI:ROUTE_W_HI + 1]
                           * _load_token_major(y_scr.at[buf, 1], tm, 0, TOKEN_ROWS))
    y_moe = ym_scr[...]
    o_ref[...] = _ln_rows(DN_ALPHA * h1 + y_moe + y_ple, g_ref[...], b_ref[...])


def _moe_combine_ple_ln(h1, route, ys, dest, p2, pwg_bf, pbg, pwp_bf, g, b, tm=512):
    T, D = h1.shape
    row = lambda c: pl.BlockSpec((tm, c), lambda i, *_: (i, 0))
    full = lambda r, c: pl.BlockSpec((r, c), lambda i, *_: (0, 0))
    return pl.pallas_call(
        functools.partial(_combine_kernel, tm=tm),
        grid_spec=pltpu.PrefetchScalarGridSpec(
            num_scalar_prefetch=1,
            grid=(T // tm,),
            in_specs=[row(D), row(LANES), pl.BlockSpec(memory_space=pl.ANY), row(PLE_DIM),
                      full(D, D), full(1, D), full(PLE_DIM, D), full(1, D), full(1, D)],
            out_specs=row(D),
            scratch_shapes=[pltpu.VMEM((2, 2, tm * TOKEN_ROWS, LANES), F32), pltpu.VMEM((tm, D), F32),
                            pltpu.SemaphoreType.DMA((4,))],
        ),
        out_shape=jax.ShapeDtypeStruct((T, D), F32),
        compiler_params=_cparams(("arbitrary",)),
        name="moe_combine_ple_ln",
    )(dest, h1, route, ys, p2, pwg_bf, pbg.reshape(1, D), pwp_bf, g.reshape(1, D), b.reshape(1, D))


def _hyena_mixer(u3, kr, ki, consts, conv_w, conv_b, bias_d, n1_total):
    B, S, _ = u3.shape
    vv, x0c = _hy_gate(u3, conv_w, conv_b)
    y = _hy_conv(vv, x0c, kr, ki, consts, bias_d, n1_total)
    return y.reshape(B * S, HY_WIDTH)


def kernel(x, p, ln0_g, ln0_b, w_in, hy_conv_w, hy_conv_b, hy_fw1, hy_fb1, hy_freq1, hy_fw2, hy_fb2, hy_freq2, hy_fw3, hy_bias, att_lq1, att_lk1, att_lq2, att_lk2, att_subln_g, pool_w, pool_b, pool_scale, w_out, ln1_g, ln1_b, moe_wgc, moe_bgc, moe_wgf, moe_bgf, moe_w1, moe_w3, moe_w2, ple_wg, ple_bg, ple_wp, ln2_g, ln2_b):
    B, S, D = x.shape
    L = w_in.shape[0]
    T = B * S
    n1_total = 2 * S // FFT_N2
    C = HY_WIDTH

    kfilt, ssq = _hy_filter(S, hy_fw1, hy_fb1, hy_freq1, hy_fw2, hy_fb2, hy_freq2, hy_fw3)
    kar, kai = _stage_a(kfilt.reshape(L, n1_total, FFT_N2 * C), n1_total)
    k1p = kar.shape[1]
    consts = _stage_b_consts(n1_total)
    kr_all, ki_all = _stage_b_filter(kar.reshape(L, k1p, FFT_N2, C), kai.reshape(L, k1p, FFT_N2, C),
                                     ssq, consts, n1_total)
    tables = _rope_tables(S)

    h = _layer_norm(x.reshape(T, D), ln0_g, ln0_b)
    for i in range(L):
        lam_init = 0.8 - 0.6 * math.exp(-0.3 * i)
        uh, qkv, up = _in_proj(h, w_in[i].astype(BF16), tables, S)
        y_hy = _hyena_mixer(uh.reshape(B, S, 3 * C), kr_all[i], ki_all[i], consts,
                            hy_conv_w[i], hy_conv_b[i], hy_bias[i], n1_total)
        y_att = _diff_attention(qkv.reshape(B, S, 3 * ATT_WIDTH), att_lq1[i], att_lk1[i], att_lq2[i],
                                att_lk2[i], att_subln_g[i], lam_init).reshape(T, ATT_WIDTH)
        y_pool = _pool_mixer(up.reshape(B, S, POOL_WIDTH), pool_w[i], pool_b[i],
                             pool_scale[i]).reshape(T, POOL_WIDTH)
        h1, h1t, route, counts = _out_proj_ln_route(h, y_hy, y_att, y_pool, w_out[i].astype(BF16), ln1_g[i],
                                                    ln1_b[i], moe_wgc[i], moe_bgc[i], moe_wgf[i], moe_bgf[i])
        cls_i = route[:, ROUTE_CLS].astype(jnp.int32)
        rank_i = route[:, ROUTE_RANK].astype(jnp.int32)
        xs, dest, tlo, thi, tval = _moe_dispatch(h1t, cls_i, rank_i, counts[0].astype(jnp.int32))
        ys = _moe_experts(xs, tlo, thi, tval, moe_w1[i].astype(BF16), moe_w3[i].astype(BF16),
                          moe_w2[i].astype(BF16))
        h = _moe_combine_ple_ln(h1, route, ys, dest, p[i].reshape(T, PLE_DIM), ple_wg[i].astype(BF16),
                                ple_bg[i], ple_wp[i].astype(BF16), ln2_g[i], ln2_b[i])
    return h.reshape(B, S, D)
```

```python
import functools
import math

import numpy as np
import jax
import jax.numpy as jnp
from jax import lax
from jax.experimental import pallas as pl
from jax.experimental.pallas import tpu as pltpu

F32 = jnp.float32
BF16 = jnp.bfloat16

D_MODEL = 1024
DEPTH = 4
HY_WIDTH = 256
ATT_WIDTH = 512
ATT_HEADS = 4
ATT_HEAD_DIM = 64
POOL_WINDOWS = (2, 4, 8, 16)
POOL_WIDTH = 256
POOL_GROUP = 64
IN_WIDTH = 3 * HY_WIDTH + 3 * ATT_WIDTH + POOL_WIDTH
ROPE_THETA = 500000.0
ROPE_DIM = ATT_HEAD_DIM // 4
HY_EMB = 33
HY_BANDS = (HY_EMB - 1) // 2
HY_FILTER_HIDDEN = 64
HY_DECAY_TARGET = 1e-2
HY_FAST_DECAY = 0.3
HY_SLOW_DECAY = 1.5
N_GROUPS = 4
EXPERTS_PER_GROUP = 4
N_EXPERTS = 16
D_EXPERT = 256
PLE_DIM = 256
LN_EPS = 1e-5
RMS_EPS = 1e-5
DN_ALPHA = (2 * DEPTH) ** 0.25

LANES = 128
FFT_N2 = 128
GATE_COARSE_LANE = 0
GATE_FINE_LANE = N_GROUPS
NEG_BIG = -1e30
LOG2E = 1.4426950408889634
VMEM_LIMIT = 56 * 1024 * 1024


def _cparams(sem):
    return pltpu.CompilerParams(dimension_semantics=sem, vmem_limit_bytes=VMEM_LIMIT)


def _split(x):
    hi = x.astype(BF16)
    lo = (x - hi.astype(F32)).astype(BF16)
    return hi, lo


def _dot3(ah, al, bh, bl):
    d = functools.partial(jnp.dot, preferred_element_type=F32)
    return d(ah, bh) + (d(ah, bl) + d(al, bh))


def _ln_rows(x, g, b):
    mu = jnp.mean(x, axis=-1, keepdims=True)
    xc = x - mu
    var = jnp.mean(xc * xc, axis=-1, keepdims=True)
    return xc * lax.rsqrt(var + LN_EPS) * g + b


def _ln_kernel(x_ref, g_ref, b_ref, o_ref):
    o_ref[...] = _ln_rows(x_ref[...], g_ref[...], b_ref[...])


def _layer_norm(x2, g, b, tm=512):
    T, D = x2.shape
    return pl.pallas_call(
        _ln_kernel,
        grid=(T // tm,),
        in_specs=[pl.BlockSpec((tm, D), lambda i: (i, 0)),
                  pl.BlockSpec((1, D), lambda i: (0, 0)),
                  pl.BlockSpec((1, D), lambda i: (0, 0))],
        out_specs=pl.BlockSpec((tm, D), lambda i: (i, 0)),
        out_shape=jax.ShapeDtypeStruct((T, D), F32),
        compiler_params=_cparams(("parallel",)),
        name="ln0",
    )(x2, g.reshape(1, D), b.reshape(1, D))


def _rope(x, cos_f, sin_a, sin_b):
    half = ROPE_DIM // 2
    return x * cos_f + pltpu.roll(x, LANES - half, 1) * sin_a + pltpu.roll(x, half, 1) * sin_b


def _inproj_kernel(h_ref, w_ref, cos_ref, sa_ref, sb_ref, uh_ref, qkv_ref, up_ref):
    hb = h_ref[...].astype(BF16)
    mm = lambda c0, c1: jnp.dot(hb, w_ref[:, c0:c1], preferred_element_type=F32)
    o_q = 3 * HY_WIDTH
    o_k = o_q + ATT_WIDTH
    o_v = o_k + ATT_WIDTH
    o_p = o_v + ATT_WIDTH
    uh_ref[...] = mm(0, o_q)
    up_ref[...] = mm(o_p, IN_WIDTH)
    cos_f, sin_a, sin_b = cos_ref[...], sa_ref[...], sb_ref[...]
    q = mm(o_q, o_k)
    k = mm(o_k, o_v)
    qscale = ATT_HEAD_DIM ** -0.5 * LOG2E
    for hd in range(ATT_HEADS):
        sl = slice(hd * LANES, (hd + 1) * LANES)
        qkv_ref[:, sl] = (_rope(q[:, sl], cos_f, sin_a, sin_b) * qscale).astype(BF16)
        qkv_ref[:, ATT_WIDTH + hd * LANES:ATT_WIDTH + (hd + 1) * LANES] = \
            _rope(k[:, sl], cos_f, sin_a, sin_b).astype(BF16)
    qkv_ref[:, 2 * ATT_WIDTH:] = mm(o_v, o_p).astype(BF16)


def _in_proj(h2, w_bf, tables, seq, tm=512):
    T, D = h2.shape
    N = w_bf.shape[1]
    spt = seq // tm
    tspec = pl.BlockSpec((tm, LANES), lambda i: (i % spt, 0))
    row = lambda c: pl.BlockSpec((tm, c), lambda i: (i, 0))
    return pl.pallas_call(
        _inproj_kernel,
        grid=(T // tm,),
        in_specs=[row(D), pl.BlockSpec((D, N), lambda i: (0, 0)), tspec, tspec, tspec],
        out_specs=[row(3 * HY_WIDTH), row(3 * ATT_WIDTH), row(POOL_WIDTH)],
        out_shape=[jax.ShapeDtypeStruct((T, 3 * HY_WIDTH), F32),
                   jax.ShapeDtypeStruct((T, 3 * ATT_WIDTH), BF16),
                   jax.ShapeDtypeStruct((T, POOL_WIDTH), F32)],
        compiler_params=_cparams(("parallel",)),
        name="in_proj",
    )(h2, w_bf, *tables)


def _hy_filter_kernel(band_ref, phase_ref, fw1_ref, fb1_ref, fr1_ref, fw2_ref, fb2_ref, fr2_ref, fw3_ref,
                      dl_ref, k_ref, ssq_ref, z_scr, *, seq, tr):
    j = pl.program_id(0)
    l = pl.program_id(1)
    n = j * tr + lax.broadcasted_iota(jnp.int32, (tr, 1), 0)
    pos = jnp.where(n < seq, n, 2 * seq - n).astype(F32)
    t = pos * (1.0 / (seq - 1))

    @pl.when(l == 0)
    def _():
        wpos = (2.0 * math.pi / seq) * pos
        lane = lax.broadcasted_iota(jnp.int32, (tr, LANES), 1)
        z_scr[...] = jnp.where(lane == 0, t,
                               jnp.where(lane < HY_EMB, jnp.sin(wpos * band_ref[...] + phase_ref[...]), 0.0))

    @pl.when((l == 0) & (j == 0))
    def _():
        ssq_ref[...] = jnp.zeros_like(ssq_ref)

    z = z_scr[...]
    h1 = _dot3(*_split(z), *_split(fw1_ref[...])) + fb1_ref[...]
    h1 = jnp.sin(fr1_ref[...] * h1)
    h2 = _dot3(*_split(h1), *_split(fw2_ref[...])) + fb2_ref[...]
    h2 = jnp.sin(fr2_ref[...] * h2)
    filt = _dot3(*_split(h2), *_split(fw3_ref[...]))
    window = jnp.exp(-t * jnp.abs(dl_ref[...]))
    kk = jnp.where(n < seq, filt[:, :HY_WIDTH], filt[:, HY_WIDTH:]) * window
    kk = jnp.where(n == seq, 0.0, kk)
    k_ref[...] = kk
    ssq_ref[l] += jnp.sum(kk * kk, axis=0, keepdims=True)


def _hy_filter(seq, fw1, fb1, freq1, fw2, fb2, freq2, fw3, tr=1024):
    L = fw1.shape[0]
    n = 2 * seq
    H = HY_FILTER_HIDDEN
    bands = np.linspace(1e-4, HY_BANDS - 1, HY_BANDS)
    bandv = np.zeros((1, LANES), np.float32)
    bandv[0, 1:1 + HY_BANDS] = bands
    bandv[0, 1 + HY_BANDS:HY_EMB] = bands
    phase = np.zeros((1, LANES), np.float32)
    phase[0, 1:1 + HY_BANDS] = 0.5 * math.pi
    phase[0, 1 + HY_BANDS:HY_EMB] = math.pi
    fw1p = jnp.zeros((L, LANES, H), F32).at[:, :HY_EMB].set(fw1)
    max_decay = math.log(HY_DECAY_TARGET) / HY_FAST_DECAY
    min_decay = math.log(HY_DECAY_TARGET) / HY_SLOW_DECAY
    deltas = np.linspace(min_decay, max_decay, HY_WIDTH).astype(np.float32).reshape(1, HY_WIDTH)
    vec = lambda a: a.reshape(L, 1, a.shape[-1])
    lspec = lambda r, c: pl.BlockSpec((None, r, c), lambda j, l: (l, 0, 0))
    lanes = pl.BlockSpec((1, LANES), lambda j, l: (0, 0))
    return pl.pallas_call(
        functools.partial(_hy_filter_kernel, seq=seq, tr=tr),
        grid=(n // tr, L),
        in_specs=[lanes, lanes,
                  lspec(LANES, H), lspec(1, H), lspec(1, H),
                  lspec(H, H), lspec(1, H), lspec(1, H),
                  lspec(H, 2 * HY_WIDTH),
                  pl.BlockSpec((1, HY_WIDTH), lambda j, l: (0, 0))],
        out_specs=[pl.BlockSpec((None, tr, HY_WIDTH), lambda j, l: (l, j, 0)),
                   pl.BlockSpec((L, 1, HY_WIDTH), lambda j, l: (0, 0, 0))],
        out_shape=[jax.ShapeDtypeStruct((L, n, HY_WIDTH), F32),
                   jax.ShapeDtypeStruct((L, 1, HY_WIDTH), F32)],
        scratch_shapes=[pltpu.VMEM((tr, LANES), F32)],
        compiler_params=_cparams(("arbitrary", "arbitrary")),
        name="hy_filter",
    )(jnp.asarray(bandv), jnp.asarray(phase), fw1p, vec(fb1), vec(freq1), fw2, vec(fb2), vec(freq2), fw3,
      jnp.asarray(deltas))


def _k1_rows(n1):
    k1 = n1 // 2 + 1
    return k1, -(-k1 // 8) * 8


def _bf_pair(m):
    m32 = jnp.asarray(np.asarray(m, np.float32))
    return _split(m32)


SUBLANES = 8


def _stage_a_matrix(n1_used, n1_total):
    k1n, k1p = _k1_rows(n1_total)
    k1 = np.arange(k1p)
    valid = (k1 < k1n)[:, None]
    ang = 2.0 * np.pi * np.outer(k1, np.arange(n1_used)) / n1_total
    eye = np.eye(SUBLANES)
    return np.concatenate([np.kron(np.cos(ang) * valid, eye), np.kron(-np.sin(ang) * valid, eye)], axis=0)


def _stage_a_rows(x_ref, a_scr, kah_ref, kal_ref, n1_used, k1p):
    def stage_a(g, carry):
        xg = jnp.concatenate(
            [x_ref[pl.ds(pl.multiple_of(n1 * FFT_N2 + g * SUBLANES, SUBLANES), SUBLANES), :]
             for n1 in range(n1_used)], axis=0)
        ag = _dot3(kah_ref[...], kal_ref[...], *_split(xg))
        gs = pl.ds(pl.multiple_of(g * SUBLANES, SUBLANES), SUBLANES)
        for part in range(2):
            for k1 in range(k1p):
                r = (part * k1p + k1) * SUBLANES
                a_scr[part, k1, gs, :] = ag[r:r + SUBLANES]
        return carry

    lax.fori_loop(0, FFT_N2 // SUBLANES, stage_a, 0, unroll=2)


def _stage_b_consts(n1_total):
    n = n1_total * FFT_N2
    k1n, k1p = _k1_rows(n1_total)
    ang2 = 2.0 * np.pi * np.outer(np.arange(FFT_N2), np.arange(FFT_N2)) / FFT_N2
    c2, s2 = np.cos(ang2), np.sin(ang2)
    mf = np.block([[c2, s2], [-s2, c2]])
    mi = np.block([[c2, -s2], [s2, c2]])
    angt = 2.0 * np.pi * np.outer(np.arange(k1p), np.arange(FFT_N2)) / n
    twc = jnp.asarray(np.cos(angt).astype(np.float32)).reshape(k1p, FFT_N2, 1)
    tws = jnp.asarray(np.sin(angt).astype(np.float32)).reshape(k1p, FFT_N2, 1)
    return _bf_pair(mf), _bf_pair(mi), twc, tws


def _fwd_b(ar, ai, c, s, mfh, mfl):
    tr_ = ar * c + ai * s
    ti_ = ai * c - ar * s
    xh, xl = _split(jnp.concatenate([tr_, ti_], axis=0))
    z = _dot3(mfh, mfl, xh, xl)
    return z[:FFT_N2], z[FFT_N2:]


def _hy_spectrum_kernel(k_ref, ssq_ref, tw_ref, kah_ref, kal_ref, mfh_ref, mfl_ref, kr_ref, ki_ref, a_scr,
                        *, n1_total, k1n, k1p):
    _stage_a_rows(k_ref, a_scr, kah_ref, kal_ref, n1_total, k1p)
    scale = lax.rsqrt(ssq_ref[...] + 1e-6) * (1.0 / (n1_total * FFT_N2))

    def stage_b(k1, carry):
        tw = tw_ref[k1]
        zr, zi = _fwd_b(a_scr[0, k1], a_scr[1, k1], tw[:, 0:1], tw[:, 1:2], mfh_ref[...], mfl_ref[...])
        kr_ref[k1] = zr * scale
        ki_ref[k1] = zi * scale
        return carry

    lax.fori_loop(0, k1n, stage_b, 0, unroll=3 if k1n % 3 == 0 else 1)
    for k1 in range(k1n, k1p):
        kr_ref[k1] = jnp.zeros(kr_ref.shape[1:], F32)
        ki_ref[k1] = jnp.zeros(ki_ref.shape[1:], F32)


def _hy_spectrum(kfilt, ssq, consts, n1_total):
    L, n, C = kfilt.shape
    k1n, k1p = _k1_rows(n1_total)
    (mfh, mfl), _, twc, tws = consts
    tw = jnp.concatenate([twc, tws], axis=-1)
    kah, kal = _bf_pair(_stage_a_matrix(n1_total, n1_total))
    full = lambda a: pl.BlockSpec(a.shape, lambda l, c: (0,) * a.ndim)
    ospec = pl.BlockSpec((None, k1p, FFT_N2, LANES), lambda l, c: (l, 0, 0, c))
    return pl.pallas_call(
        functools.partial(_hy_spectrum_kernel, n1_total=n1_total, k1n=k1n, k1p=k1p),
        grid=(L, C // LANES),
        in_specs=[pl.BlockSpec((None, n, LANES), lambda l, c: (l, 0, c)),
                  pl.BlockSpec((None, 1, LANES), lambda l, c: (l, 0, c)),
                  full(tw), full(kah), full(kal), full(mfh), full(mfl)],
        out_specs=[ospec, ospec],
        out_shape=[jax.ShapeDtypeStruct((L, k1p, FFT_N2, C), F32)] * 2,
        scratch_shapes=[pltpu.VMEM((2, k1p, FFT_N2, LANES), F32)],
        compiler_params=_cparams(("parallel", "parallel")),
        name="hy_spectrum",
    )(kfilt, ssq, tw, kah, kal, mfh, mfl)


def _hy_conv_kernel(vv_ref, x0_ref, kr_ref, ki_ref, tw_ref, kah_ref, kal_ref, kch_ref, kcl_ref,
                    mfh_ref, mfl_ref, mih_ref, mil_ref, bias_ref, o_ref, a_scr, *, n1u, k1n, k1p):
    groups = FFT_N2 // SUBLANES

    def rows(n1, g):
        return pl.ds(pl.multiple_of(n1 * FFT_N2 + g * SUBLANES, SUBLANES), SUBLANES)

    _stage_a_rows(vv_ref, a_scr, kah_ref, kal_ref, n1u, k1p)

    def stage_b(k1, carry):
        tw = tw_ref[k1]
        c = tw[:, 0:1]
        s = tw[:, 1:2]
        zr, zi = _fwd_b(a_scr[0, k1], a_scr[1, k1], c, s, mfh_ref[...], mfl_ref[...])
        kr = kr_ref[k1]
        ki = ki_ref[k1]
        pr = zr * kr - zi * ki
        pi = zr * ki + zi * kr
        y = _dot3(mih_ref[...], mil_ref[...], *_split(jnp.concatenate([pr, pi], axis=0)))
        yr = y[:FFT_N2]
        yi = y[FFT_N2:]
        a_scr[0, k1] = yr * c - yi * s
        a_scr[1, k1] = yi * c + yr * s
        return carry

    lax.fori_loop(0, k1n, stage_b, 0, unroll=11 if k1n % 11 == 0 else 1)

    def stage_c(g, carry):
        gs = pl.ds(pl.multiple_of(g * SUBLANES, SUBLANES), SUBLANES)
        bg = jnp.concatenate([a_scr[part, k1, gs, :] for part in range(2) for k1 in range(k1p)], axis=0)
        yg = _dot3(kch_ref[...], kcl_ref[...], *_split(bg))
        for n1 in range(n1u):
            r = rows(n1, g)
            o_ref[r, :] = (yg[n1 * SUBLANES:(n1 + 1) * SUBLANES] + vv_ref[r, :] * bias_ref[...]) * x0_ref[r, :]
        return carry

    lax.fori_loop(0, groups, stage_c, 0, unroll=2)


def _hy_conv(vv, x0c, kr3, ki3, consts, bias_d, n1_total):
    B, S, C = vv.shape
    n1u = S // FFT_N2
    k1n, k1p = _k1_rows(n1_total)
    (mfh, mfl), (mih, mil), twc, tws = consts
    tw = jnp.concatenate([twc, tws], axis=-1)
    k1 = np.arange(k1p)
    eye = np.eye(SUBLANES)
    ka = _stage_a_matrix(n1u, n1_total)
    w = np.where((k1 == 0) | (k1 == n1_total // 2), 1.0, 2.0) * (k1 < k1n)
    ang_c = 2.0 * np.pi * np.outer(np.arange(n1u), k1) / n1_total
    kc = np.concatenate([np.kron(np.cos(ang_c) * w[None, :], eye), np.kron(-np.sin(ang_c) * w[None, :], eye)], axis=1)
    kah, kal = _bf_pair(ka)
    kch, kcl = _bf_pair(kc)
    nct = C // LANES
    xspec = pl.BlockSpec((None, S, LANES), lambda c, b: (b, 0, c))
    kspec = pl.BlockSpec((k1p, FFT_N2, LANES), lambda c, b: (0, 0, c))
    full = lambda a: pl.BlockSpec(a.shape, lambda c, b: (0,) * a.ndim)
    return pl.pallas_call(
        functools.partial(_hy_conv_kernel, n1u=n1u, k1n=k1n, k1p=k1p),
        grid=(nct, B),
        in_specs=[xspec, xspec, kspec, kspec, full(tw), full(kah), full(kal), full(kch), full(kcl),
                  full(mfh), full(mfl), full(mih), full(mil),
                  pl.BlockSpec((1, LANES), lambda c, b: (0, c))],
        out_specs=xspec,
        out_shape=jax.ShapeDtypeStruct((B, S, C), F32),
        scratch_shapes=[pltpu.VMEM((2, k1p, FFT_N2, LANES), F32)],
        compiler_params=_cparams(("parallel", "parallel")),
        name="hy_conv",
    )(vv, x0c, kr3, ki3, tw, kah, kal, kch, kcl, mfh, mfl, mih, mil, bias_d.reshape(1, C))


def _conv3(x, w_ref, b_ref):
    S = x.shape[0]
    row = lax.broadcasted_iota(jnp.int32, x.shape, 0)
    prev = jnp.where(row == 0, 0.0, pltpu.roll(x, 1, 0))
    nxt = jnp.where(row == S - 1, 0.0, pltpu.roll(x, S - 1, 0))
    return prev * w_ref[0:1, :] + x * w_ref[1:2, :] + nxt * w_ref[2:3, :] + b_ref[...]


def _hy_gate_kernel(x0_ref, x1_ref, v_ref, w0_ref, w1_ref, w2_ref, b0_ref, b1_ref, b2_ref,
                    vv_ref, x0c_ref):
    x0c_ref[...] = _conv3(x0_ref[...], w0_ref, b0_ref)
    vv_ref[...] = _conv3(v_ref[...], w2_ref, b2_ref) * _conv3(x1_ref[...], w1_ref, b1_ref)


def _hy_gate(u3, conv_w, conv_b):
    B, S, _ = u3.shape
    nct = HY_WIDTH // LANES
    conv_b2 = conv_b.reshape(1, 3 * HY_WIDTH)
    uspec = lambda part: pl.BlockSpec((None, S, LANES), lambda b, c: (b, 0, part * nct + c))
    wspec = lambda part: pl.BlockSpec((3, LANES), lambda b, c: (0, part * nct + c))
    bspec = lambda part: pl.BlockSpec((1, LANES), lambda b, c: (0, part * nct + c))
    ospec = pl.BlockSpec((None, S, LANES), lambda b, c: (b, 0, c))
    return pl.pallas_call(
        _hy_gate_kernel,
        grid=(B, nct),
        in_specs=[uspec(0), uspec(1), uspec(2), wspec(0), wspec(1), wspec(2),
                  bspec(0), bspec(1), bspec(2)],
        out_specs=[ospec, ospec],
        out_shape=[jax.ShapeDtypeStruct((B, S, HY_WIDTH), F32)] * 2,
        compiler_params=_cparams(("parallel", "parallel")),
        name="hy_gate",
    )(u3, u3, u3, conv_w, conv_w, conv_w, conv_b2, conv_b2, conv_b2)


def _attn_kernel(q_ref, k_ref, v_ref, lq1_ref, lk1_ref, lq2_ref, lk2_ref, g_ref, o_ref,
                 s_scr, p_scr, vx_scr, kt_scr, *, lam_init, tq):
    S = q_ref.shape[0]
    nblk = S // tq
    lam = (jnp.exp(jnp.sum(lq1_ref[...] * lk1_ref[...], keepdims=True))
           - jnp.exp(jnp.sum(lq2_ref[...] * lk2_ref[...], keepdims=True)) + lam_init)

    def blk(i):
        return pl.ds(pl.multiple_of(i * tq, tq), tq)

    kt = k_ref[...].T
    feat = lax.broadcasted_iota(jnp.int32, kt.shape, 0)
    kt_scr[0] = jnp.where(feat < ATT_HEAD_DIM, kt, jnp.zeros_like(kt))
    kt_scr[1] = jnp.where(feat >= ATT_HEAD_DIM, kt, jnp.zeros_like(kt))

    def scores(i, slot):
        q = q_ref[blk(i), :]
        for c in range(2):
            s_scr[slot, c] = jnp.dot(q, kt_scr[c], preferred_element_type=F32)

    vx_scr[:, :LANES] = v_ref[...]
    vx_scr[:, LANES:] = (lax.broadcasted_iota(jnp.int32, (S, LANES), 1) == 0).astype(BF16)

    def softmax(slot):
        for c in range(2):
            s = s_scr[slot, c]
            m = jnp.max(s, axis=-1, keepdims=True)
            p_scr[slot, c] = jnp.exp2(s - m).astype(BF16)

    def values(i, slot):
        oe1 = jnp.dot(p_scr[slot, 0], vx_scr[...], preferred_element_type=F32)
        oe2 = jnp.dot(p_scr[slot, 1], vx_scr[...], preferred_element_type=F32)
        o = oe1[:, :LANES] / oe1[:, LANES:LANES + 1] - oe2[:, :LANES] * (lam / oe2[:, LANES:LANES + 1])
        o = o * lax.rsqrt(jnp.mean(o * o, axis=-1, keepdims=True) + RMS_EPS) * g_ref[...]
        o_ref[blk(i), :] = o * (1.0 - lam_init)

    scores(0, 0)
    softmax(0)
    scores(1, 1)

    def pair(j, carry):
        t = 2 * j
        values(t - 2, 0)
        softmax(1)
        scores(t, 0)
        values(t - 1, 1)
        softmax(0)
        scores(t + 1, 1)
        return carry

    lax.fori_loop(1, nblk // 2, pair, 0)
    values(nblk - 2, 0)
    softmax(1)
    values(nblk - 1, 1)


def _rope_tables(seq):
    pos = np.arange(seq, dtype=np.float64)
    inv_freq = np.power(ROPE_THETA, -np.arange(0, ROPE_DIM, 2, dtype=np.float64) / ROPE_DIM)
    ang = pos[:, None] * inv_freq[None, :]
    half = ROPE_DIM // 2
    cos_f = np.ones((seq, LANES), np.float32)
    sin_a = np.zeros((seq, LANES), np.float32)
    sin_b = np.zeros((seq, LANES), np.float32)
    for base in range(0, LANES, ATT_HEAD_DIM):
        cos_f[:, base:base + half] = np.cos(ang)
        cos_f[:, base + half:base + ROPE_DIM] = np.cos(ang)
        sin_a[:, base:base + half] = -np.sin(ang)
        sin_b[:, base + half:base + ROPE_DIM] = np.sin(ang)
    return jnp.asarray(cos_f), jnp.asarray(sin_a), jnp.asarray(sin_b)


def _diff_attention(qkv3, lq1, lk1, lq2, lk2, subln_g, lam_init, tq=256):
    B, S, _ = qkv3.shape
    nb = ATT_WIDTH // LANES
    assert S % (2 * tq) == 0
    spec = lambda part: pl.BlockSpec((None, S, LANES), lambda b, h: (b, 0, part * nb + h))
    vec = pl.BlockSpec((1, ATT_HEAD_DIM), lambda b, h: (0, 0))
    r1 = lambda a: a.reshape(1, -1)
    return pl.pallas_call(
        functools.partial(_attn_kernel, lam_init=lam_init, tq=tq),
        grid=(B, ATT_HEADS),
        in_specs=[spec(0), spec(1), spec(2), vec, vec, vec, vec,
                  pl.BlockSpec((1, LANES), lambda b, h: (0, 0))],
        out_specs=pl.BlockSpec((None, S, LANES), lambda b, h: (b, 0, h)),
        out_shape=jax.ShapeDtypeStruct((B, S, ATT_WIDTH), F32),
        scratch_shapes=[pltpu.VMEM((2, 2, tq, S), F32), pltpu.VMEM((2, 2, tq, S), BF16),
                        pltpu.VMEM((S, 2 * LANES), BF16), pltpu.VMEM((2, LANES, S), BF16)],
        compiler_params=_cparams(("parallel", "parallel")),
        name="diff_attn",
    )(qkv3, qkv3, qkv3, r1(lq1), r1(lk1), r1(lq2), r1(lk2), r1(subln_g))


def _pool_kernel(u_ref, w_ref, b_ref, sc_ref, o_ref):
    ct = pl.program_id(1)
    x = u_ref[...]
    S = x.shape[0]
    row = lax.broadcasted_iota(jnp.int32, x.shape, 0)
    lane = lax.broadcasted_iota(jnp.int32, x.shape, 1)

    def shifted(d):
        if d == 0:
            return x
        r = pltpu.roll(x, (-d) % S, 0)
        return jnp.where((row + d >= 0) & (row + d < S), r, 0.0)

    sums = {}
    acc = shifted(-1) + x
    sums[2] = acc
    lo, hi = -1, 0
    for w in POOL_WINDOWS[1:]:
        for d in list(range(-(w // 2), lo)) + list(range(hi + 1, w // 2)):
            acc = acc + shifted(d)
        lo, hi = -(w // 2), w // 2 - 1
        sums[w] = acc
    grp = 2 * ct + (lane >= POOL_GROUP).astype(jnp.int32)
    win_sum = sums[POOL_WINDOWS[-1]]
    half = jnp.full(x.shape, POOL_WINDOWS[-1] // 2, jnp.int32)
    for g in range(len(POOL_WINDOWS) - 2, -1, -1):
        win_sum = jnp.where(grp == g, sums[POOL_WINDOWS[g]], win_sum)
        half = jnp.where(grp == g, POOL_WINDOWS[g] // 2, half)
    cnt = jnp.minimum(row + half - 1, S - 1) - jnp.maximum(row - half, 0) + 1
    y = win_sum / cnt.astype(F32) - x
    y = _dot3(*_split(y), *_split(w_ref[...])) + b_ref[...]
    o_ref[...] = y * sc_ref[...]


def _pool_mixer(u3, w, b, scale):
    B, S, _ = u3.shape
    nct = POOL_WIDTH // LANES
    gpt = LANES // POOL_GROUP
    wbd = jnp.zeros((nct, LANES, LANES), F32)
    for g in range(len(POOL_WINDOWS)):
        t, o = divmod(g, gpt)
        wbd = wbd.at[t, o * POOL_GROUP:(o + 1) * POOL_GROUP, o * POOL_GROUP:(o + 1) * POOL_GROUP].set(w[g])
    vspec = pl.BlockSpec((1, LANES), lambda bb, c: (0, c))
    return pl.pallas_call(
        _pool_kernel,
        grid=(B, nct),
        in_specs=[pl.BlockSpec((None, S, LANES), lambda bb, c: (bb, 0, c)),
                  pl.BlockSpec((None, LANES, LANES), lambda bb, c: (c, 0, 0)),
                  vspec, vspec],
        out_specs=pl.BlockSpec((None, S, LANES), lambda bb, c: (bb, 0, c)),
        out_shape=jax.ShapeDtypeStruct((B, S, POOL_WIDTH), F32),
        compiler_params=_cparams(("parallel", "parallel")),
        name="pool_mixer",
    )(u3, wbd, b.reshape(1, POOL_WIDTH), scale.reshape(1, POOL_WIDTH))


PAIRS_PER_GROUP = 6
N_CLASSES = N_GROUPS * PAIRS_PER_GROUP
ROUTE_W_LO, ROUTE_W_HI, ROUTE_CLS, ROUTE_RANK = 0, 1, 2, 3
TOKEN_ROWS = D_MODEL // LANES


def _store_token_major(ref, x, first_row, unit_rows):
    n = x.shape[0]
    for j in range(TOKEN_ROWS):
        ref[pl.ds(first_row + j, n, stride=unit_rows), :] = x[:, j * LANES:(j + 1) * LANES]


def _load_token_major(ref, n, first_row, unit_rows):
    return jnp.concatenate([ref[pl.ds(first_row + j, n, stride=unit_rows), :] for j in range(TOKEN_ROWS)],
                           axis=1)


def _class_experts(c):
    g, pidx = divmod(c, PAIRS_PER_GROUP)
    pairs = [(a, b) for a in range(EXPERTS_PER_GROUP) for b in range(a + 1, EXPERTS_PER_GROUP)]
    lo, hi = pairs[pidx]
    return g * EXPERTS_PER_GROUP + lo, g * EXPERTS_PER_GROUP + hi


def _outproj_kernel(h_ref, yh_ref, ya_ref, yp_ref, wh_ref, wa_ref, wp_ref, g_ref, b_ref,
                    wgh_ref, wgl_ref, bg_ref, tri_ref, h1_ref, h1t_ref, route_ref, cnt_ref, carry_scr):
    @pl.when(pl.program_id(0) == 0)
    def _():
        carry_scr[...] = jnp.zeros_like(carry_scr)

    sub = tri_ref.shape[0]
    carry = carry_scr[...]
    for s in range(h_ref.shape[0] // sub):
        carry = _outproj_rows(slice(s * sub, (s + 1) * sub), s * sub * TOKEN_ROWS, carry,
                              h_ref, yh_ref, ya_ref, yp_ref, wh_ref, wa_ref, wp_ref, g_ref, b_ref,
                              wgh_ref, wgl_ref, bg_ref, tri_ref, h1_ref, h1t_ref, route_ref)
    carry_scr[...] = carry
    cnt_ref[...] = carry


def _outproj_rows(rows, first_unit_row, carry, h_ref, yh_ref, ya_ref, yp_ref, wh_ref, wa_ref, wp_ref,
                  g_ref, b_ref, wgh_ref, wgl_ref, bg_ref, tri_ref, h1_ref, h1t_ref, route_ref):
    d = functools.partial(jnp.dot, preferred_element_type=F32)
    mix = (d(yh_ref[rows, :].astype(BF16), wh_ref[...]) + d(ya_ref[rows, :].astype(BF16), wa_ref[...])
           + d(yp_ref[rows, :].astype(BF16), wp_ref[...]))
    h1 = _ln_rows(DN_ALPHA * h_ref[rows, :] + mix, g_ref[...], b_ref[...])
    h1_ref[rows, :] = h1
    _store_token_major(h1t_ref, h1, first_unit_row, TOKEN_ROWS)

    logit = _dot3(*_split(h1), wgh_ref[...], wgl_ref[...]) + bg_ref[...]
    lni = lax.broadcasted_iota(jnp.int32, logit.shape, 1)
    ln = lni.astype(F32)
    grp = lax.shift_right_arithmetic(lni - GATE_FINE_LANE, 2).astype(F32)
    first = lambda mask: jnp.min(jnp.where(mask, ln, float(LANES)), axis=-1, keepdims=True)
    cmask = lni < N_GROUPS
    lc = jnp.where(cmask, logit, NEG_BIG)
    mc = jnp.max(lc, axis=-1, keepdims=True)
    gw = 1.0 / jnp.sum(jnp.where(cmask, jnp.exp(lc - mc), 0.0), axis=-1, keepdims=True)
    gi = first(cmask & (lc == mc))
    fmask = (lni >= GATE_FINE_LANE) & (lni < GATE_FINE_LANE + N_EXPERTS) & (grp == gi)
    lf = jnp.where(fmask, logit, NEG_BIG)
    m1 = jnp.max(lf, axis=-1, keepdims=True)
    i1 = first(fmask & (lf == m1))
    rest = fmask & (ln != i1)
    lf2 = jnp.where(rest, logit, NEG_BIG)
    m2 = jnp.max(lf2, axis=-1, keepdims=True)
    i2 = first(rest & (lf2 == m2))
    e2 = jnp.exp(m2 - m1)
    w1 = gw / (1.0 + e2)
    w2 = gw * e2 / (1.0 + e2)

    j1 = i1 - GATE_FINE_LANE - EXPERTS_PER_GROUP * gi
    j2 = i2 - GATE_FINE_LANE - EXPERTS_PER_GROUP * gi
    lo = jnp.minimum(j1, j2)
    hi = jnp.maximum(j1, j2)
    w_lo = jnp.where(j1 < j2, w1, w2)
    w_hi = jnp.where(j1 < j2, w2, w1)
    base = jnp.where(lo == 0.0, 0.0, jnp.where(lo == 1.0, 3.0, 5.0))
    cls = gi * PAIRS_PER_GROUP + base + hi - lo - 1.0
    onehot = ln == cls
    before = jnp.dot(tri_ref[...], onehot.astype(BF16), preferred_element_type=F32)
    rank = jnp.sum(jnp.where(onehot, before + carry, 0.0), axis=-1, keepdims=True)
    route_ref[rows, :] = jnp.where(lni == ROUTE_W_LO, w_lo,
                                   jnp.where(lni == ROUTE_W_HI, w_hi,
                                             jnp.where(lni == ROUTE_CLS, cls,
                                                       jnp.where(lni == ROUTE_RANK, rank, 0.0))))
    return carry + jnp.sum(onehot.astype(F32), axis=0, keepdims=True)


def _out_proj_ln_route(h2, yh, ya, yp, w_out_bf, g, b, wgc, bgc, wgf, bgf, tm=512):
    T, D = h2.shape
    sub = tm
    tri = jnp.asarray(np.tril(np.ones((sub, sub), np.float32), -1)).astype(BF16)
    wg = jnp.zeros((D, LANES), F32).at[:, :N_GROUPS].set(wgc)
    wg = wg.at[:, GATE_FINE_LANE:GATE_FINE_LANE + N_EXPERTS].set(wgf)
    bg = jnp.zeros((1, LANES), F32).at[0, :N_GROUPS].set(bgc)
    bg = bg.at[0, GATE_FINE_LANE:GATE_FINE_LANE + N_EXPERTS].set(bgf)
    wgh, wgl = _split(wg)
    o1, o2 = HY_WIDTH, HY_WIDTH + ATT_WIDTH
    row = lambda c: pl.BlockSpec((tm, c), lambda i: (i, 0))
    full = lambda r, c: pl.BlockSpec((r, c), lambda i: (0, 0))
    return pl.pallas_call(
        _outproj_kernel,
        grid=(T // tm,),
        in_specs=[row(D), row(HY_WIDTH), row(ATT_WIDTH), row(POOL_WIDTH),
                  full(HY_WIDTH, D), full(ATT_WIDTH, D), full(POOL_WIDTH, D),
                  full(1, D), full(1, D), full(D, LANES), full(D, LANES), full(1, LANES), full(sub, sub)],
        out_specs=[row(D), pl.BlockSpec((tm * TOKEN_ROWS, LANES), lambda i: (i, 0)), row(LANES), full(1, LANES)],
        out_shape=[jax.ShapeDtypeStruct((T, D), F32), jax.ShapeDtypeStruct((T * TOKEN_ROWS, LANES), F32),
                   jax.ShapeDtypeStruct((T, LANES), F32), jax.ShapeDtypeStruct((1, LANES), F32)],
        scratch_shapes=[pltpu.VMEM((1, LANES), F32)],
        compiler_params=_cparams(("arbitrary",)),
        name="out_proj_ln_route",
    )(h2, yh, ya, yp, w_out_bf[:o1], w_out_bf[o1:o2], w_out_bf[o2:], g.reshape(1, D), b.reshape(1, D),
      wgh, wgl, bg, tri)


MOE_TILE = 256
DMA_BATCH = 256


def _moe_rows_padded(T):
    return T + N_CLASSES * MOE_TILE


def _token_rows(ref, t, unit_rows):
    return ref.at[pl.ds(pl.multiple_of(t * unit_rows, unit_rows), unit_rows)]


DMA_GROUP = 8


def _start_tokens(n, slot_of, start_copy):
    def group(g, carry):
        first = g * DMA_GROUP
        slots = [slot_of(first + k) for k in range(DMA_GROUP)]
        for k in range(DMA_GROUP):
            start_copy(first + k, slots[k], k % 2)
        return carry

    lax.fori_loop(0, n // DMA_GROUP, group, 0)


def _wait_tokens(n, src_ref, dst_ref, unit_rows, sems):
    rows = pl.ds(0, (n // 2) * unit_rows)
    for sem in sems:
        pltpu.make_async_copy(src_ref.at[rows], dst_ref.at[rows], sem).wait()


def _dispatch_kernel(cls_ref, rank_ref, cnt_ref, h1t_ref, xs_ref, dest_ref, tlo_ref, thi_ref, tval_ref,
                     off_scr, zero_scr, sem, *, tm, n_tiles):
    i = pl.program_id(0)
    base = i * tm
    tile_rows = MOE_TILE * TOKEN_ROWS

    def fill_copy(tile):
        rows = pl.ds(pl.multiple_of(tile * tile_rows, tile_rows), tile_rows)
        return pltpu.make_async_copy(zero_scr, xs_ref.at[rows], sem.at[2])

    @pl.when(i == 0)
    def _():
        zero_scr[...] = jnp.zeros_like(zero_scr)
        off = jnp.int32(0)
        tile = jnp.int32(0)
        fills = []
        for c in range(N_CLASSES):
            n = cnt_ref[c]
            nt = lax.shift_right_logical(n + (MOE_TILE - 1), MOE_TILE.bit_length() - 1)
            off_scr[c] = off
            e_lo, e_hi = _class_experts(c)

            def mark(k, carry, tile=tile, e_lo=e_lo, e_hi=e_hi):
                tlo_ref[tile + k] = e_lo
                thi_ref[tile + k] = e_hi
                tval_ref[tile + k] = 1
                return carry

            lax.fori_loop(0, nt, mark, 0)
            fill = fill_copy(tile + nt - 1)
            fills.append((nt > 0, fill))

            @pl.when(nt > 0)
            def _(fill=fill):
                fill.start()

            off = off + nt * MOE_TILE
            tile = tile + nt

        def unused(k, carry):
            tlo_ref[k] = 0
            thi_ref[k] = 0
            tval_ref[k] = 0
            fill_copy(k).start()
            return carry

        lax.fori_loop(tile, n_tiles, unused, 0)
        for used, fill in fills:
            @pl.when(used)
            def _(fill=fill):
                fill.wait()

        def unused_wait(k, carry):
            fill_copy(k).wait()
            return carry

        lax.fori_loop(tile, n_tiles, unused_wait, 0)

    def slots_of_step(step):
        def slot(r, carry):
            t = step * tm + r
            dest_ref[t] = off_scr[cls_ref[t]] + rank_ref[t]
            return carry

        lax.fori_loop(0, tm, slot, 0, unroll=8)

    @pl.when(i == 0)
    def _():
        slots_of_step(0)

    def start_copy(r, d, prio):
        pltpu.make_async_copy(_token_rows(h1t_ref, r, TOKEN_ROWS), _token_rows(xs_ref, d, TOKEN_ROWS),
                              sem.at[prio]).start(priority=prio)

    _start_tokens(tm, lambda r: dest_ref[base + r], start_copy)

    @pl.when(i + 1 < pl.num_programs(0))
    def _():
        slots_of_step(i + 1)

    _wait_tokens(tm, h1t_ref, xs_ref, TOKEN_ROWS, [sem.at[0], sem.at[1]])


def _moe_dispatch(h1t, cls_i, rank_i, cnt_i, tm=512):
    T = h1t.shape[0] // TOKEN_ROWS
    slots = _moe_rows_padded(T)
    n_tiles = slots // MOE_TILE
    smem = pl.BlockSpec(memory_space=pltpu.SMEM)
    return pl.pallas_call(
        functools.partial(_dispatch_kernel, tm=tm, n_tiles=n_tiles),
        grid_spec=pltpu.PrefetchScalarGridSpec(
            num_scalar_prefetch=3,
            grid=(T // tm,),
            in_specs=[pl.BlockSpec((tm * TOKEN_ROWS, LANES), lambda i, *_: (i, 0))],
            out_specs=[pl.BlockSpec(memory_space=pl.ANY), smem, smem, smem, smem],
            scratch_shapes=[pltpu.SMEM((N_CLASSES,), jnp.int32),
                            pltpu.VMEM((MOE_TILE * TOKEN_ROWS, LANES), F32),
                            pltpu.SemaphoreType.DMA((3,))],
        ),
        out_shape=[jax.ShapeDtypeStruct((slots * TOKEN_ROWS, LANES), F32), jax.ShapeDtypeStruct((T,), jnp.int32),
                   jax.ShapeDtypeStruct((n_tiles,), jnp.int32), jax.ShapeDtypeStruct((n_tiles,), jnp.int32),
                   jax.ShapeDtypeStruct((n_tiles,), jnp.int32)],
        compiler_params=_cparams(("arbitrary",)),
        name="moe_dispatch",
    )(cls_i, rank_i, cnt_i, h1t)


def _experts_kernel(tlo_ref, thi_ref, tval_ref, xs_ref, w1l_ref, w3l_ref, w2l_ref, w1h_ref, w3h_ref, w2h_ref,
                    ys_ref):
    j = pl.program_id(0)
    d = functools.partial(jnp.dot, preferred_element_type=F32)

    @pl.when(tval_ref[j] == 1)
    def _():
        xb = _load_token_major(xs_ref, MOE_TILE, 0, TOKEN_ROWS).astype(BF16)

        def expert(w1_ref, w3_ref, w2_ref):
            a = d(xb, w1_ref[...])
            c = d(xb, w3_ref[...])
            return d((a * jax.nn.sigmoid(a) * c).astype(BF16), w2_ref[...])

        _store_token_major(ys_ref.at[0], expert(w1l_ref, w3l_ref, w2l_ref), 0, TOKEN_ROWS)
        _store_token_major(ys_ref.at[1], expert(w1h_ref, w3h_ref, w2h_ref), 0, TOKEN_ROWS)

    @pl.when(tval_ref[j] == 0)
    def _():
        ys_ref[...] = jnp.zeros_like(ys_ref)


def _moe_experts(xs, tlo, thi, tval, w1_bf, w3_bf, w2_bf):
    slots = xs.shape[0] // TOKEN_ROWS
    D = D_MODEL
    wspec = lambda shape, which: pl.BlockSpec(
        (None,) + shape, (lambda j, tlo, thi, tval: (tlo[j], 0, 0)) if which == 0
        else (lambda j, tlo, thi, tval: (thi[j], 0, 0)))
    up = (D, D_EXPERT)
    down = (D_EXPERT, D)
    return pl.pallas_call(
        _experts_kernel,
        grid_spec=pltpu.PrefetchScalarGridSpec(
            num_scalar_prefetch=3,
            grid=(slots // MOE_TILE,),
            in_specs=[pl.BlockSpec((MOE_TILE * TOKEN_ROWS, LANES), lambda j, *_: (j, 0)),
                      wspec(up, 0), wspec(up, 0), wspec(down, 0), wspec(up, 1), wspec(up, 1), wspec(down, 1)],
            out_specs=pl.BlockSpec((None, 2, MOE_TILE * TOKEN_ROWS, LANES), lambda j, *_: (j, 0, 0, 0)),
        ),
        out_shape=jax.ShapeDtypeStruct((slots // MOE_TILE, 2, MOE_TILE * TOKEN_ROWS, LANES), F32),
        compiler_params=_cparams(("arbitrary",)),
        name="moe_experts",
    )(tlo, thi, tval, xs, w1_bf, w3_bf, w2_bf, w1_bf, w3_bf, w2_bf)


def _combine_kernel(dest_ref, h1_ref, route_ref, ys_ref, p_ref, pwg_ref, pbg_ref, pwp_ref, g_ref, b_ref,
                    o_ref, y_scr, ym_scr, sem, *, tm):
    i = pl.program_id(0)
    d = functools.partial(jnp.dot, preferred_element_type=F32)
    tile_shift = MOE_TILE.bit_length() - 1

    def unit_rows(t):
        return pl.ds(pl.multiple_of(t * TOKEN_ROWS, TOKEN_ROWS), TOKEN_ROWS)

    def gather(step, buf):
        def start_copy(r, slot, prio):
            tile = lax.shift_right_logical(slot, tile_shift)
            src = ys_ref.at[tile, :, unit_rows(slot & (MOE_TILE - 1)), :]
            pltpu.make_async_copy(src, y_scr.at[buf, :, unit_rows(r), :],
                                  sem.at[2 * buf + prio]).start(priority=prio)
        _start_tokens(tm, lambda r: dest_ref[step * tm + r], start_copy)

    def wait_gathers(buf):
        rows = pl.ds(0, (tm // 2) * TOKEN_ROWS)
        for prio in range(2):
            pltpu.make_async_copy(ys_ref.at[0, :, rows, :], y_scr.at[buf, :, rows, :],
                                  sem.at[2 * buf + prio]).wait()

    @pl.when(i == 0)
    def _():
        gather(0, 0)

    for buf in range(2):
        @pl.when((i + 1 < pl.num_programs(0)) & ((i + 1) % 2 == buf))
        def _(buf=buf):
            gather(i + 1, buf)

    sub = tm // 2
    halves = [slice(s * sub, (s + 1) * sub) for s in range(2)]
    y_ple = []
    for rows in halves:
        z = d(h1_ref[rows, :].astype(BF16), pwg_ref[...]) + pbg_ref[...]
        y_ple.append(jax.nn.sigmoid(z) * d(p_ref[rows, :].astype(BF16), pwp_ref[...]))
    for buf in range(2):
        @pl.when(i % 2 == buf)
        def _(buf=buf):
            wait_gathers(buf)
            for s, rows in enumerate(halves):
                rec = route_ref[rows, :]
                first = s * sub * TOKEN_ROWS
                ym_scr[rows, :] = (rec[:, ROUTE_W_LO:ROUTE_W_LO + 1]
                                   * _load_token_major(y_scr.at[buf, 0], sub, first, TOKEN_ROWS)
                                   + rec[:, ROUTE_W_HI:ROUTE_W_HI + 1]
                                   * _load_token_major(y_scr.at[buf, 1], sub, first, TOKEN_ROWS))
    for s, rows in enumerate(halves):
        r_ = DN_ALPHA * h1_ref[rows, :] + ym_scr[rows, :] + y_ple[s]
        o_ref[rows, :] = _ln_rows(r_, g_ref[...], b_ref[...])


def _moe_combine_ple_ln(h1, route, ys, dest, p2, pwg_bf, pbg, pwp_bf, g, b, tm=512):
    T, D = h1.shape
    row = lambda c: pl.BlockSpec((tm, c), lambda i, *_: (i, 0))
    full = lambda r, c: pl.BlockSpec((r, c), lambda i, *_: (0, 0))
    return pl.pallas_call(
        functools.partial(_combine_kernel, tm=tm),
        grid_spec=pltpu.PrefetchScalarGridSpec(
            num_scalar_prefetch=1,
            grid=(T // tm,),
            in_specs=[row(D), row(LANES), pl.BlockSpec(memory_space=pl.ANY), row(PLE_DIM),
                      full(D, D), full(1, D), full(PLE_DIM, D), full(1, D), full(1, D)],
            out_specs=row(D),
            scratch_shapes=[pltpu.VMEM((2, 2, tm * TOKEN_ROWS, LANES), F32), pltpu.VMEM((tm, D), F32),
                            pltpu.SemaphoreType.DMA((4,))],
        ),
        out_shape=jax.ShapeDtypeStruct((T, D), F32),
        compiler_params=_cparams(("arbitrary",)),
        name="moe_combine_ple_ln",
    )(dest, h1, route, ys, p2, pwg_bf, pbg.reshape(1, D), pwp_bf, g.reshape(1, D), b.reshape(1, D))


def _hyena_mixer(u3, kr, ki, consts, conv_w, conv_b, bias_d, n1_total):
    B, S, _ = u3.shape
    vv, x0c = _hy_gate(u3, conv_w, conv_b)
    y = _hy_conv(vv, x0c, kr, ki, consts, bias_d, n1_total)
    return y.reshape(B * S, HY_WIDTH)


def kernel(x, p, ln0_g, ln0_b, w_in, hy_conv_w, hy_conv_b, hy_fw1, hy_fb1, hy_freq1, hy_fw2, hy_fb2, hy_freq2, hy_fw3, hy_bias, att_lq1, att_lk1, att_lq2, att_lk2, att_subln_g, pool_w, pool_b, pool_scale, w_out, ln1_g, ln1_b, moe_wgc, moe_bgc, moe_wgf, moe_bgf, moe_w1, moe_w3, moe_w2, ple_wg, ple_bg, ple_wp, ln2_g, ln2_b):
    B, S, D = x.shape
    L = w_in.shape[0]
    T = B * S
    n1_total = 2 * S // FFT_N2
    C = HY_WIDTH

    kfilt, ssq = _hy_filter(S, hy_fw1, hy_fb1, hy_freq1, hy_fw2, hy_fb2, hy_freq2, hy_fw3)
    consts = _stage_b_consts(n1_total)
    kr_all, ki_all = _hy_spectrum(kfilt, ssq, consts, n1_total)
    tables = _rope_tables(S)

    h = _layer_norm(x.reshape(T, D), ln0_g, ln0_b)
    for i in range(L):
        lam_init = 0.8 - 0.6 * math.exp(-0.3 * i)
        uh, qkv, up = _in_proj(h, w_in[i].astype(BF16), tables, S)
        y_hy = _hyena_mixer(uh.reshape(B, S, 3 * C), kr_all[i], ki_all[i], consts,
                            hy_conv_w[i], hy_conv_b[i], hy_bias[i], n1_total)
        y_att = _diff_attention(qkv.reshape(B, S, 3 * ATT_WIDTH), att_lq1[i], att_lk1[i], att_lq2[i],
                                att_lk2[i], att_subln_g[i], lam_init).reshape(T, ATT_WIDTH)
        y_pool = _pool_mixer(up.reshape(B, S, POOL_WIDTH), pool_w[i], pool_b[i],
                             pool_scale[i]).reshape(T, POOL_WIDTH)
        h1, h1t, route, counts = _out_proj_ln_route(h, y_hy, y_att, y_pool, w_out[i].astype(BF16), ln1_g[i],
                                                    ln1_b[i], moe_wgc[i], moe_bgc[i], moe_wgf[i], moe_bgf[i])
        cls_i = route[:, ROUTE_CLS].astype(jnp.int32)
        rank_i = route[:, ROUTE_RANK].astype(jnp.int32)
        xs, dest, tlo, thi, tval = _moe_dispatch(h1t, cls_i, rank_i, counts[0].astype(jnp.int32))
        ys = _moe_experts(xs, tlo, thi, tval, moe_w1[i].astype(BF16), moe_w3[i].astype(BF16),
                          moe_w2[i].astype(BF16))
        h = _moe_combine_ple_ln(h1, route, ys, dest, p[i].reshape(T, PLE_DIM), ple_wg[i].astype(BF16),
                                ple_bg[i], ple_wp[i].astype(BF16), ln2_g[i], ln2_b[i])
    return h.reshape(B, S, D)
```

```python
import functools
import math

import numpy as np
import jax
import jax.numpy as jnp
from jax import lax
from jax.experimental import pallas as pl
from jax.experimental.pallas import tpu as pltpu

F32 = jnp.float32
BF16 = jnp.bfloat16

D_MODEL = 1024
DEPTH = 4
HY_WIDTH = 256
ATT_WIDTH = 512
ATT_HEADS = 4
ATT_HEAD_DIM = 64
POOL_WINDOWS = (2, 4, 8, 16)
POOL_WIDTH = 256
POOL_GROUP = 64
IN_WIDTH = 3 * HY_WIDTH + 3 * ATT_WIDTH + POOL_WIDTH
ROPE_THETA = 500000.0
ROPE_DIM = ATT_HEAD_DIM // 4
HY_EMB = 33
HY_BANDS = (HY_EMB - 1) // 2
HY_FILTER_HIDDEN = 64
HY_DECAY_TARGET = 1e-2
HY_FAST_DECAY = 0.3
HY_SLOW_DECAY = 1.5
N_GROUPS = 4
EXPERTS_PER_GROUP = 4
N_EXPERTS = 16
D_EXPERT = 256
PLE_DIM = 256
LN_EPS = 1e-5
RMS_EPS = 1e-5
DN_ALPHA = (2 * DEPTH) ** 0.25

LANES = 128
FFT_N2 = 128
GATE_COARSE_LANE = 0
GATE_FINE_LANE = N_GROUPS
NEG_BIG = -1e30
LOG2E = 1.4426950408889634
VMEM_LIMIT = 56 * 1024 * 1024


def _cparams(sem):
    return pltpu.CompilerParams(dimension_semantics=sem, vmem_limit_bytes=VMEM_LIMIT)


def _split(x):
    hi = x.astype(BF16)
    lo = (x - hi.astype(F32)).astype(BF16)
    return hi, lo


def _dot3(ah, al, bh, bl):
    d = functools.partial(jnp.dot, preferred_element_type=F32)
    return d(ah, bh) + (d(ah, bl) + d(al, bh))


def _ln_rows(x, g, b):
    mu = jnp.mean(x, axis=-1, keepdims=True)
    xc = x - mu
    var = jnp.mean(xc * xc, axis=-1, keepdims=True)
    return xc * lax.rsqrt(var + LN_EPS) * g + b


def _ln_kernel(x_ref, g_ref, b_ref, o_ref):
    o_ref[...] = _ln_rows(x_ref[...], g_ref[...], b_ref[...])


def _layer_norm(x2, g, b, tm=512):
    T, D = x2.shape
    return pl.pallas_call(
        _ln_kernel,
        grid=(T // tm,),
        in_specs=[pl.BlockSpec((tm, D), lambda i: (i, 0)),
                  pl.BlockSpec((1, D), lambda i: (0, 0)),
                  pl.BlockSpec((1, D), lambda i: (0, 0))],
        out_specs=pl.BlockSpec((tm, D), lambda i: (i, 0)),
        out_shape=jax.ShapeDtypeStruct((T, D), F32),
        compiler_params=_cparams(("parallel",)),
        name="ln0",
    )(x2, g.reshape(1, D), b.reshape(1, D))


def _rope(x, cos_f, sin_a, sin_b):
    half = ROPE_DIM // 2
    return x * cos_f + pltpu.roll(x, LANES - half, 1) * sin_a + pltpu.roll(x, half, 1) * sin_b


def _inproj_kernel(h_ref, w_ref, cos_ref, sa_ref, sb_ref, uh_ref, qkv_ref, up_ref):
    hb = h_ref[...].astype(BF16)
    mm = lambda c0, c1: jnp.dot(hb, w_ref[:, c0:c1], preferred_element_type=F32)
    o_q = 3 * HY_WIDTH
    o_k = o_q + ATT_WIDTH
    o_v = o_k + ATT_WIDTH
    o_p = o_v + ATT_WIDTH
    uh_ref[...] = mm(0, o_q)
    up_ref[...] = mm(o_p, IN_WIDTH)
    cos_f, sin_a, sin_b = cos_ref[...], sa_ref[...], sb_ref[...]
    q = mm(o_q, o_k)
    k = mm(o_k, o_v)
    qscale = ATT_HEAD_DIM ** -0.5 * LOG2E
    for hd in range(ATT_HEADS):
        sl = slice(hd * LANES, (hd + 1) * LANES)
        qkv_ref[:, sl] = (_rope(q[:, sl], cos_f, sin_a, sin_b) * qscale).astype(BF16)
        qkv_ref[:, ATT_WIDTH + hd * LANES:ATT_WIDTH + (hd + 1) * LANES] = \
            _rope(k[:, sl], cos_f, sin_a, sin_b).astype(BF16)
    qkv_ref[:, 2 * ATT_WIDTH:] = mm(o_v, o_p).astype(BF16)


def _in_proj(h2, w_bf, layer, tables, seq, tm=512):
    T, D = h2.shape
    N = w_bf.shape[2]
    spt = seq // tm
    tspec = pl.BlockSpec((tm, LANES), lambda i: (i % spt, 0))
    row = lambda c: pl.BlockSpec((tm, c), lambda i: (i, 0))
    return pl.pallas_call(
        _inproj_kernel,
        grid=(T // tm,),
        in_specs=[row(D), pl.BlockSpec((None, D, N), lambda i: (layer, 0, 0)), tspec, tspec, tspec],
        out_specs=[row(3 * HY_WIDTH), row(3 * ATT_WIDTH), row(POOL_WIDTH)],
        out_shape=[jax.ShapeDtypeStruct((T, 3 * HY_WIDTH), F32),
                   jax.ShapeDtypeStruct((T, 3 * ATT_WIDTH), BF16),
                   jax.ShapeDtypeStruct((T, POOL_WIDTH), F32)],
        compiler_params=_cparams(("parallel",)),
        name="in_proj",
    )(h2, w_bf, *tables)


def _hy_filter_kernel(band_ref, phase_ref, fw1_ref, fb1_ref, fr1_ref, fw2_ref, fb2_ref, fr2_ref, fw3_ref,
                      dl_ref, k_ref, ssq_ref, z_scr, *, seq, tr):
    j = pl.program_id(0)
    l = pl.program_id(1)
    n = j * tr + lax.broadcasted_iota(jnp.int32, (tr, 1), 0)
    pos = jnp.where(n < seq, n, 2 * seq - n).astype(F32)
    t = pos * (1.0 / (seq - 1))

    @pl.when(l == 0)
    def _():
        wpos = (2.0 * math.pi / seq) * pos
        lane = lax.broadcasted_iota(jnp.int32, (tr, LANES), 1)
        z_scr[...] = jnp.where(lane == 0, t,
                               jnp.where(lane < HY_EMB, jnp.sin(wpos * band_ref[...] + phase_ref[...]), 0.0))

    @pl.when((l == 0) & (j == 0))
    def _():
        ssq_ref[...] = jnp.zeros_like(ssq_ref)

    z = z_scr[...]
    h1 = _dot3(*_split(z), *_split(fw1_ref[...])) + fb1_ref[...]
    h1 = jnp.sin(fr1_ref[...] * h1)
    h2 = _dot3(*_split(h1), *_split(fw2_ref[...])) + fb2_ref[...]
    h2 = jnp.sin(fr2_ref[...] * h2)
    filt = _dot3(*_split(h2), *_split(fw3_ref[...]))
    window = jnp.exp(-t * jnp.abs(dl_ref[...]))
    kk = jnp.where(n < seq, filt[:, :HY_WIDTH], filt[:, HY_WIDTH:]) * window
    kk = jnp.where(n == seq, 0.0, kk)
    k_ref[...] = kk
    ssq_ref[l] += jnp.sum(kk * kk, axis=0, keepdims=True)


def _hy_filter(seq, fw1, fb1, freq1, fw2, fb2, freq2, fw3, tr=1024):
    L = fw1.shape[0]
    n = 2 * seq
    H = HY_FILTER_HIDDEN
    bands = np.linspace(1e-4, HY_BANDS - 1, HY_BANDS)
    bandv = np.zeros((1, LANES), np.float32)
    bandv[0, 1:1 + HY_BANDS] = bands
    bandv[0, 1 + HY_BANDS:HY_EMB] = bands
    phase = np.zeros((1, LANES), np.float32)
    phase[0, 1:1 + HY_BANDS] = 0.5 * math.pi
    phase[0, 1 + HY_BANDS:HY_EMB] = math.pi
    fw1p = jnp.zeros((L, LANES, H), F32).at[:, :HY_EMB].set(fw1)
    max_decay = math.log(HY_DECAY_TARGET) / HY_FAST_DECAY
    min_decay = math.log(HY_DECAY_TARGET) / HY_SLOW_DECAY
    deltas = np.linspace(min_decay, max_decay, HY_WIDTH).astype(np.float32).reshape(1, HY_WIDTH)
    vec = lambda a: a.reshape(L, 1, a.shape[-1])
    lspec = lambda r, c: pl.BlockSpec((None, r, c), lambda j, l: (l, 0, 0))
    lanes = pl.BlockSpec((1, LANES), lambda j, l: (0, 0))
    return pl.pallas_call(
        functools.partial(_hy_filter_kernel, seq=seq, tr=tr),
        grid=(n // tr, L),
        in_specs=[lanes, lanes,
                  lspec(LANES, H), lspec(1, H), lspec(1, H),
                  lspec(H, H), lspec(1, H), lspec(1, H),
                  lspec(H, 2 * HY_WIDTH),
                  pl.BlockSpec((1, HY_WIDTH), lambda j, l: (0, 0))],
        out_specs=[pl.BlockSpec((None, tr, HY_WIDTH), lambda j, l: (l, j, 0)),
                   pl.BlockSpec((L, 1, HY_WIDTH), lambda j, l: (0, 0, 0))],
        out_shape=[jax.ShapeDtypeStruct((L, n, HY_WIDTH), F32),
                   jax.ShapeDtypeStruct((L, 1, HY_WIDTH), F32)],
        scratch_shapes=[pltpu.VMEM((tr, LANES), F32)],
        compiler_params=_cparams(("arbitrary", "arbitrary")),
        name="hy_filter",
    )(jnp.asarray(bandv), jnp.asarray(phase), fw1p, vec(fb1), vec(freq1), fw2, vec(fb2), vec(freq2), fw3,
      jnp.asarray(deltas))


def _k1_rows(n1):
    k1 = n1 // 2 + 1
    return k1, -(-k1 // 8) * 8


def _bf_pair(m):
    m32 = jnp.asarray(np.asarray(m, np.float32))
    return _split(m32)


SUBLANES = 8


def _stage_a_matrix(n1_used, n1_total):
    k1n, k1p = _k1_rows(n1_total)
    k1 = np.arange(k1p)
    valid = (k1 < k1n)[:, None]
    ang = 2.0 * np.pi * np.outer(k1, np.arange(n1_used)) / n1_total
    eye = np.eye(SUBLANES)
    return np.concatenate([np.kron(np.cos(ang) * valid, eye), np.kron(-np.sin(ang) * valid, eye)], axis=0)


def _stage_a_rows(x_ref, a_scr, kah_ref, kal_ref, n1_used, k1p):
    def stage_a(g, carry):
        xg = jnp.concatenate(
            [x_ref[pl.ds(pl.multiple_of(n1 * FFT_N2 + g * SUBLANES, SUBLANES), SUBLANES), :]
             for n1 in range(n1_used)], axis=0)
        ag = _dot3(kah_ref[...], kal_ref[...], *_split(xg))
        gs = pl.ds(pl.multiple_of(g * SUBLANES, SUBLANES), SUBLANES)
        for part in range(2):
            for k1 in range(k1p):
                r = (part * k1p + k1) * SUBLANES
                a_scr[part, k1, gs, :] = ag[r:r + SUBLANES]
        return carry

    lax.fori_loop(0, FFT_N2 // SUBLANES, stage_a, 0, unroll=2)


def _stage_b_consts(n1_total):
    n = n1_total * FFT_N2
    k1n, k1p = _k1_rows(n1_total)
    ang2 = 2.0 * np.pi * np.outer(np.arange(FFT_N2), np.arange(FFT_N2)) / FFT_N2
    c2, s2 = np.cos(ang2), np.sin(ang2)
    mf = np.block([[c2, s2], [-s2, c2]])
    mi = np.block([[c2, -s2], [s2, c2]])
    angt = 2.0 * np.pi * np.outer(np.arange(k1p), np.arange(FFT_N2)) / n
    twc = jnp.asarray(np.cos(angt).astype(np.float32)).reshape(k1p, FFT_N2, 1)
    tws = jnp.asarray(np.sin(angt).astype(np.float32)).reshape(k1p, FFT_N2, 1)
    return _bf_pair(mf), _bf_pair(mi), twc, tws


def _fwd_b(ar, ai, c, s, mfh, mfl):
    tr_ = ar * c + ai * s
    ti_ = ai * c - ar * s
    xh, xl = _split(jnp.concatenate([tr_, ti_], axis=0))
    z = _dot3(mfh, mfl, xh, xl)
    return z[:FFT_N2], z[FFT_N2:]


def _hy_spectrum_kernel(k_ref, ssq_ref, tw_ref, kah_ref, kal_ref, mfh_ref, mfl_ref, kr_ref, ki_ref, a_scr,
                        *, n1_total, k1n, k1p):
    _stage_a_rows(k_ref, a_scr, kah_ref, kal_ref, n1_total, k1p)
    scale = lax.rsqrt(ssq_ref[...] + 1e-6) * (1.0 / (n1_total * FFT_N2))

    def stage_b(k1, carry):
        tw = tw_ref[k1]
        zr, zi = _fwd_b(a_scr[0, k1], a_scr[1, k1], tw[:, 0:1], tw[:, 1:2], mfh_ref[...], mfl_ref[...])
        kr_ref[k1] = zr * scale
        ki_ref[k1] = zi * scale
        return carry

    lax.fori_loop(0, k1n, stage_b, 0, unroll=3 if k1n % 3 == 0 else 1)
    for k1 in range(k1n, k1p):
        kr_ref[k1] = jnp.zeros(kr_ref.shape[1:], F32)
        ki_ref[k1] = jnp.zeros(ki_ref.shape[1:], F32)


def _hy_spectrum(kfilt, ssq, consts, n1_total):
    L, n, C = kfilt.shape
    k1n, k1p = _k1_rows(n1_total)
    (mfh, mfl), _, twc, tws = consts
    tw = jnp.concatenate([twc, tws], axis=-1)
    kah, kal = _bf_pair(_stage_a_matrix(n1_total, n1_total))
    full = lambda a: pl.BlockSpec(a.shape, lambda l, c: (0,) * a.ndim)
    ospec = pl.BlockSpec((None, k1p, FFT_N2, LANES), lambda l, c: (l, 0, 0, c))
    return pl.pallas_call(
        functools.partial(_hy_spectrum_kernel, n1_total=n1_total, k1n=k1n, k1p=k1p),
        grid=(L, C // LANES),
        in_specs=[pl.BlockSpec((None, n, LANES), lambda l, c: (l, 0, c)),
                  pl.BlockSpec((None, 1, LANES), lambda l, c: (l, 0, c)),
                  full(tw), full(kah), full(kal), full(mfh), full(mfl)],
        out_specs=[ospec, ospec],
        out_shape=[jax.ShapeDtypeStruct((L, k1p, FFT_N2, C), F32)] * 2,
        scratch_shapes=[pltpu.VMEM((2, k1p, FFT_N2, LANES), F32)],
        compiler_params=_cparams(("parallel", "parallel")),
        name="hy_spectrum",
    )(kfilt, ssq, tw, kah, kal, mfh, mfl)


def _hy_conv_kernel(vv_ref, x0_ref, kr_ref, ki_ref, tw_ref, kah_ref, kal_ref, kch_ref, kcl_ref,
                    mfh_ref, mfl_ref, mih_ref, mil_ref, bias_ref, o_ref, a_scr, *, n1u, k1n, k1p):
    groups = FFT_N2 // SUBLANES

    def rows(n1, g):
        return pl.ds(pl.multiple_of(n1 * FFT_N2 + g * SUBLANES, SUBLANES), SUBLANES)

    _stage_a_rows(vv_ref, a_scr, kah_ref, kal_ref, n1u, k1p)

    def stage_b(k1, carry):
        tw = tw_ref[k1]
        c = tw[:, 0:1]
        s = tw[:, 1:2]
        zr, zi = _fwd_b(a_scr[0, k1], a_scr[1, k1], c, s, mfh_ref[...], mfl_ref[...])
        kr = kr_ref[k1]
        ki = ki_ref[k1]
        pr = zr * kr - zi * ki
        pi = zr * ki + zi * kr
        y = _dot3(mih_ref[...], mil_ref[...], *_split(jnp.concatenate([pr, pi], axis=0)))
        yr = y[:FFT_N2]
        yi = y[FFT_N2:]
        a_scr[0, k1] = yr * c - yi * s
        a_scr[1, k1] = yi * c + yr * s
        return carry

    lax.fori_loop(0, k1n, stage_b, 0, unroll=11 if k1n % 11 == 0 else 1)

    def stage_c(g, carry):
        gs = pl.ds(pl.multiple_of(g * SUBLANES, SUBLANES), SUBLANES)
        bg = jnp.concatenate([a_scr[part, k1, gs, :] for part in range(2) for k1 in range(k1p)], axis=0)
        yg = _dot3(kch_ref[...], kcl_ref[...], *_split(bg))
        for n1 in range(n1u):
            r = rows(n1, g)
            o_ref[r, :] = (yg[n1 * SUBLANES:(n1 + 1) * SUBLANES] + vv_ref[r, :] * bias_ref[...]) * x0_ref[r, :]
        return carry

    lax.fori_loop(0, groups, stage_c, 0, unroll=2)


def _hy_conv(vv, x0c, kr4, ki4, layer, consts, bias_d, n1_total):
    B, S, C = vv.shape
    n1u = S // FFT_N2
    k1n, k1p = _k1_rows(n1_total)
    (mfh, mfl), (mih, mil), twc, tws = consts
    tw = jnp.concatenate([twc, tws], axis=-1)
    k1 = np.arange(k1p)
    eye = np.eye(SUBLANES)
    ka = _stage_a_matrix(n1u, n1_total)
    w = np.where((k1 == 0) | (k1 == n1_total // 2), 1.0, 2.0) * (k1 < k1n)
    ang_c = 2.0 * np.pi * np.outer(np.arange(n1u), k1) / n1_total
    kc = np.concatenate([np.kron(np.cos(ang_c) * w[None, :], eye), np.kron(-np.sin(ang_c) * w[None, :], eye)], axis=1)
    kah, kal = _bf_pair(ka)
    kch, kcl = _bf_pair(kc)
    nct = C // LANES
    xspec = pl.BlockSpec((None, S, LANES), lambda c, b: (b, 0, c))
    kspec = pl.BlockSpec((None, k1p, FFT_N2, LANES), lambda c, b: (layer, 0, 0, c))
    full = lambda a: pl.BlockSpec(a.shape, lambda c, b: (0,) * a.ndim)
    return pl.pallas_call(
        functools.partial(_hy_conv_kernel, n1u=n1u, k1n=k1n, k1p=k1p),
        grid=(nct, B),
        in_specs=[xspec, xspec, kspec, kspec, full(tw), full(kah), full(kal), full(kch), full(kcl),
                  full(mfh), full(mfl), full(mih), full(mil),
                  pl.BlockSpec((1, LANES), lambda c, b: (0, c))],
        out_specs=xspec,
        out_shape=jax.ShapeDtypeStruct((B, S, C), F32),
        scratch_shapes=[pltpu.VMEM((2, k1p, FFT_N2, LANES), F32)],
        compiler_params=_cparams(("parallel", "parallel")),
        name="hy_conv",
    )(vv, x0c, kr4, ki4, tw, kah, kal, kch, kcl, mfh, mfl, mih, mil, bias_d.reshape(1, C))


def _conv3(x, w_ref, b_ref):
    S = x.shape[0]
    row = lax.broadcasted_iota(jnp.int32, x.shape, 0)
    prev = jnp.where(row == 0, 0.0, pltpu.roll(x, 1, 0))
    nxt = jnp.where(row == S - 1, 0.0, pltpu.roll(x, S - 1, 0))
    return prev * w_ref[0:1, :] + x * w_ref[1:2, :] + nxt * w_ref[2:3, :] + b_ref[...]


def _hy_gate_kernel(x0_ref, x1_ref, v_ref, w0_ref, w1_ref, w2_ref, b0_ref, b1_ref, b2_ref,
                    vv_ref, x0c_ref):
    x0c_ref[...] = _conv3(x0_ref[...], w0_ref, b0_ref)
    vv_ref[...] = _conv3(v_ref[...], w2_ref, b2_ref) * _conv3(x1_ref[...], w1_ref, b1_ref)


def _hy_gate(u3, conv_w, conv_b):
    B, S, _ = u3.shape
    nct = HY_WIDTH // LANES
    conv_b2 = conv_b.reshape(1, 3 * HY_WIDTH)
    uspec = lambda part: pl.BlockSpec((None, S, LANES), lambda b, c: (b, 0, part * nct + c))
    wspec = lambda part: pl.BlockSpec((3, LANES), lambda b, c: (0, part * nct + c))
    bspec = lambda part: pl.BlockSpec((1, LANES), lambda b, c: (0, part * nct + c))
    ospec = pl.BlockSpec((None, S, LANES), lambda b, c: (b, 0, c))
    return pl.pallas_call(
        _hy_gate_kernel,
        grid=(B, nct),
        in_specs=[uspec(0), uspec(1), uspec(2), wspec(0), wspec(1), wspec(2),
                  bspec(0), bspec(1), bspec(2)],
        out_specs=[ospec, ospec],
        out_shape=[jax.ShapeDtypeStruct((B, S, HY_WIDTH), F32)] * 2,
        compiler_params=_cparams(("parallel", "parallel")),
        name="hy_gate",
    )(u3, u3, u3, conv_w, conv_w, conv_w, conv_b2, conv_b2, conv_b2)


def _attn_kernel(q_ref, k_ref, v_ref, lq1_ref, lk1_ref, lq2_ref, lk2_ref, g_ref, o_ref,
                 s_scr, p_scr, vx_scr, kt_scr, *, lam_init, tq):
    S = q_ref.shape[0]
    nblk = S // tq
    lam = (jnp.exp(jnp.sum(lq1_ref[...] * lk1_ref[...], keepdims=True))
           - jnp.exp(jnp.sum(lq2_ref[...] * lk2_ref[...], keepdims=True)) + lam_init)

    def blk(i):
        return pl.ds(pl.multiple_of(i * tq, tq), tq)

    kt = k_ref[...].T
    feat = lax.broadcasted_iota(jnp.int32, kt.shape, 0)
    kt_scr[0] = jnp.where(feat < ATT_HEAD_DIM, kt, jnp.zeros_like(kt))
    kt_scr[1] = jnp.where(feat >= ATT_HEAD_DIM, kt, jnp.zeros_like(kt))

    def scores(i, slot):
        q = q_ref[blk(i), :]
        for c in range(2):
            s_scr[slot, c] = jnp.dot(q, kt_scr[c], preferred_element_type=F32)

    vx_scr[:, :LANES] = v_ref[...]
    vx_scr[:, LANES:] = (lax.broadcasted_iota(jnp.int32, (S, LANES), 1) == 0).astype(BF16)

    def softmax(slot):
        for c in range(2):
            s = s_scr[slot, c]
            m = jnp.max(s, axis=-1, keepdims=True)
            p_scr[slot, c] = jnp.exp2(s - m).astype(BF16)

    def values(i, slot):
        oe1 = jnp.dot(p_scr[slot, 0], vx_scr[...], preferred_element_type=F32)
        oe2 = jnp.dot(p_scr[slot, 1], vx_scr[...], preferred_element_type=F32)
        o = oe1[:, :LANES] / oe1[:, LANES:LANES + 1] - oe2[:, :LANES] * (lam / oe2[:, LANES:LANES + 1])
        o = o * lax.rsqrt(jnp.mean(o * o, axis=-1, keepdims=True) + RMS_EPS) * g_ref[...]
        o_ref[blk(i), :] = o * (1.0 - lam_init)

    scores(0, 0)
    softmax(0)
    scores(1, 1)

    def pair(j, carry):
        t = 2 * j
        values(t - 2, 0)
        softmax(1)
        scores(t, 0)
        values(t - 1, 1)
        softmax(0)
        scores(t + 1, 1)
        return carry

    lax.fori_loop(1, nblk // 2, pair, 0)
    values(nblk - 2, 0)
    softmax(1)
    values(nblk - 1, 1)


def _rope_tables(seq):
    pos = np.arange(seq, dtype=np.float64)
    inv_freq = np.power(ROPE_THETA, -np.arange(0, ROPE_DIM, 2, dtype=np.float64) / ROPE_DIM)
    ang = pos[:, None] * inv_freq[None, :]
    half = ROPE_DIM // 2
    cos_f = np.ones((seq, LANES), np.float32)
    sin_a = np.zeros((seq, LANES), np.float32)
    sin_b = np.zeros((seq, LANES), np.float32)
    for base in range(0, LANES, ATT_HEAD_DIM):
        cos_f[:, base:base + half] = np.cos(ang)
        cos_f[:, base + half:base + ROPE_DIM] = np.cos(ang)
        sin_a[:, base:base + half] = -np.sin(ang)
        sin_b[:, base + half:base + ROPE_DIM] = np.sin(ang)
    return jnp.asarray(cos_f), jnp.asarray(sin_a), jnp.asarray(sin_b)


def _diff_attention(qkv3, lq1, lk1, lq2, lk2, subln_g, lam_init, tq=256):
    B, S, _ = qkv3.shape
    nb = ATT_WIDTH // LANES
    assert S % (2 * tq) == 0
    spec = lambda part: pl.BlockSpec((None, S, LANES), lambda b, h: (b, 0, part * nb + h))
    vec = pl.BlockSpec((1, ATT_HEAD_DIM), lambda b, h: (0, 0))
    r1 = lambda a: a.reshape(1, -1)
    return pl.pallas_call(
        functools.partial(_attn_kernel, lam_init=lam_init, tq=tq),
        grid=(B, ATT_HEADS),
        in_specs=[spec(0), spec(1), spec(2), vec, vec, vec, vec,
                  pl.BlockSpec((1, LANES), lambda b, h: (0, 0))],
        out_specs=pl.BlockSpec((None, S, LANES), lambda b, h: (b, 0, h)),
        out_shape=jax.ShapeDtypeStruct((B, S, ATT_WIDTH), F32),
        scratch_shapes=[pltpu.VMEM((2, 2, tq, S), F32), pltpu.VMEM((2, 2, tq, S), BF16),
                        pltpu.VMEM((S, 2 * LANES), BF16), pltpu.VMEM((2, LANES, S), BF16)],
        compiler_params=_cparams(("parallel", "parallel")),
        name="diff_attn",
    )(qkv3, qkv3, qkv3, r1(lq1), r1(lk1), r1(lq2), r1(lk2), r1(subln_g))


def _pool_kernel(u_ref, w_ref, b_ref, sc_ref, o_ref):
    ct = pl.program_id(1)
    x = u_ref[...]
    S = x.shape[0]
    row = lax.broadcasted_iota(jnp.int32, x.shape, 0)
    lane = lax.broadcasted_iota(jnp.int32, x.shape, 1)

    def shifted(d):
        if d == 0:
            return x
        r = pltpu.roll(x, (-d) % S, 0)
        return jnp.where((row + d >= 0) & (row + d < S), r, 0.0)

    sums = {}
    acc = shifted(-1) + x
    sums[2] = acc
    lo, hi = -1, 0
    for w in POOL_WINDOWS[1:]:
        for d in list(range(-(w // 2), lo)) + list(range(hi + 1, w // 2)):
            acc = acc + shifted(d)
        lo, hi = -(w // 2), w // 2 - 1
        sums[w] = acc
    grp = 2 * ct + (lane >= POOL_GROUP).astype(jnp.int32)
    win_sum = sums[POOL_WINDOWS[-1]]
    half = jnp.full(x.shape, POOL_WINDOWS[-1] // 2, jnp.int32)
    for g in range(len(POOL_WINDOWS) - 2, -1, -1):
        win_sum = jnp.where(grp == g, sums[POOL_WINDOWS[g]], win_sum)
        half = jnp.where(grp == g, POOL_WINDOWS[g] // 2, half)
    cnt = jnp.minimum(row + half - 1, S - 1) - jnp.maximum(row - half, 0) + 1
    y = win_sum / cnt.astype(F32) - x
    y = _dot3(*_split(y), *_split(w_ref[...])) + b_ref[...]
    o_ref[...] = y * sc_ref[...]


def _pool_mixer(u3, w, b, scale):
    B, S, _ = u3.shape
    nct = POOL_WIDTH // LANES
    gpt = LANES // POOL_GROUP
    wbd = jnp.zeros((nct, LANES, LANES), F32)
    for g in range(len(POOL_WINDOWS)):
        t, o = divmod(g, gpt)
        wbd = wbd.at[t, o * POOL_GROUP:(o + 1) * POOL_GROUP, o * POOL_GROUP:(o + 1) * POOL_GROUP].set(w[g])
    vspec = pl.BlockSpec((1, LANES), lambda bb, c: (0, c))
    return pl.pallas_call(
        _pool_kernel,
        grid=(B, nct),
        in_specs=[pl.BlockSpec((None, S, LANES), lambda bb, c: (bb, 0, c)),
                  pl.BlockSpec((None, LANES, LANES), lambda bb, c: (c, 0, 0)),
                  vspec, vspec],
        out_specs=pl.BlockSpec((None, S, LANES), lambda bb, c: (bb, 0, c)),
        out_shape=jax.ShapeDtypeStruct((B, S, POOL_WIDTH), F32),
        compiler_params=_cparams(("parallel", "parallel")),
        name="pool_mixer",
    )(u3, wbd, b.reshape(1, POOL_WIDTH), scale.reshape(1, POOL_WIDTH))


PAIRS_PER_GROUP = 6
N_CLASSES = N_GROUPS * PAIRS_PER_GROUP
ROUTE_W_LO, ROUTE_W_HI, ROUTE_CLS, ROUTE_RANK = 0, 1, 2, 3
TOKEN_ROWS = D_MODEL // LANES


def _store_token_major(ref, x, first_row, unit_rows):
    n = x.shape[0]
    for j in range(TOKEN_ROWS):
        ref[pl.ds(first_row + j, n, stride=unit_rows), :] = x[:, j * LANES:(j + 1) * LANES]


def _load_token_major(ref, n, first_row, unit_rows):
    return jnp.concatenate([ref[pl.ds(first_row + j, n, stride=unit_rows), :] for j in range(TOKEN_ROWS)],
                           axis=1)


def _class_experts(c):
    g, pidx = divmod(c, PAIRS_PER_GROUP)
    pairs = [(a, b) for a in range(EXPERTS_PER_GROUP) for b in range(a + 1, EXPERTS_PER_GROUP)]
    lo, hi = pairs[pidx]
    return g * EXPERTS_PER_GROUP + lo, g * EXPERTS_PER_GROUP + hi


def _outproj_kernel(h_ref, yh_ref, ya_ref, yp_ref, w_ref, g_ref, b_ref,
                    wgh_ref, wgl_ref, bg_ref, tri_ref, h1_ref, h1t_ref, route_ref, cnt_ref, carry_scr):
    @pl.when(pl.program_id(0) == 0)
    def _():
        carry_scr[...] = jnp.zeros_like(carry_scr)

    sub = tri_ref.shape[0]
    carry = carry_scr[...]
    for s in range(h_ref.shape[0] // sub):
        carry = _outproj_rows(slice(s * sub, (s + 1) * sub), s * sub * TOKEN_ROWS, carry,
                              h_ref, yh_ref, ya_ref, yp_ref, w_ref, g_ref, b_ref,
                              wgh_ref, wgl_ref, bg_ref, tri_ref, h1_ref, h1t_ref, route_ref)
    carry_scr[...] = carry
    cnt_ref[...] = carry


def _outproj_rows(rows, first_unit_row, carry, h_ref, yh_ref, ya_ref, yp_ref, w_ref,
                  g_ref, b_ref, wgh_ref, wgl_ref, bg_ref, tri_ref, h1_ref, h1t_ref, route_ref):
    d = functools.partial(jnp.dot, preferred_element_type=F32)
    o1, o2 = HY_WIDTH, HY_WIDTH + ATT_WIDTH
    mix = (d(yh_ref[rows, :].astype(BF16), w_ref[:o1, :]) + d(ya_ref[rows, :].astype(BF16), w_ref[o1:o2, :])
           + d(yp_ref[rows, :].astype(BF16), w_ref[o2:, :]))
    h1 = _ln_rows(DN_ALPHA * h_ref[rows, :] + mix, g_ref[...], b_ref[...])
    h1_ref[rows, :] = h1
    _store_token_major(h1t_ref, h1, first_unit_row, TOKEN_ROWS)

    logit = _dot3(*_split(h1), wgh_ref[...], wgl_ref[...]) + bg_ref[...]
    lni = lax.broadcasted_iota(jnp.int32, logit.shape, 1)
    ln = lni.astype(F32)
    grp = lax.shift_right_arithmetic(lni - GATE_FINE_LANE, 2).astype(F32)
    first = lambda mask: jnp.min(jnp.where(mask, ln, float(LANES)), axis=-1, keepdims=True)
    cmask = lni < N_GROUPS
    lc = jnp.where(cmask, logit, NEG_BIG)
    mc = jnp.max(lc, axis=-1, keepdims=True)
    gw = 1.0 / jnp.sum(jnp.where(cmask, jnp.exp(lc - mc), 0.0), axis=-1, keepdims=True)
    gi = first(cmask & (lc == mc))
    fmask = (lni >= GATE_FINE_LANE) & (lni < GATE_FINE_LANE + N_EXPERTS) & (grp == gi)
    lf = jnp.where(fmask, logit, NEG_BIG)
    m1 = jnp.max(lf, axis=-1, keepdims=True)
    i1 = first(fmask & (lf == m1))
    rest = fmask & (ln != i1)
    lf2 = jnp.where(rest, logit, NEG_BIG)
    m2 = jnp.max(lf2, axis=-1, keepdims=True)
    i2 = first(rest & (lf2 == m2))
    e2 = jnp.exp(m2 - m1)
    w1 = gw / (1.0 + e2)
    w2 = gw * e2 / (1.0 + e2)

    j1 = i1 - GATE_FINE_LANE - EXPERTS_PER_GROUP * gi
    j2 = i2 - GATE_FINE_LANE - EXPERTS_PER_GROUP * gi
    lo = jnp.minimum(j1, j2)
    hi = jnp.maximum(j1, j2)
    w_lo = jnp.where(j1 < j2, w1, w2)
    w_hi = jnp.where(j1 < j2, w2, w1)
    base = jnp.where(lo == 0.0, 0.0, jnp.where(lo == 1.0, 3.0, 5.0))
    cls = gi * PAIRS_PER_GROUP + base + hi - lo - 1.0
    onehot = ln == cls
    before = jnp.dot(tri_ref[...], onehot.astype(BF16), preferred_element_type=F32)
    rank = jnp.sum(jnp.where(onehot, before + carry, 0.0), axis=-1, keepdims=True)
    route_ref[rows, :] = jnp.where(lni == ROUTE_W_LO, w_lo,
                                   jnp.where(lni == ROUTE_W_HI, w_hi,
                                             jnp.where(lni == ROUTE_CLS, cls,
                                                       jnp.where(lni == ROUTE_RANK, rank, 0.0))))
    return carry + jnp.sum(onehot.astype(F32), axis=0, keepdims=True)


def _out_proj_ln_route(h2, yh, ya, yp, w_out_bf, layer, g, b, wgc, bgc, wgf, bgf, tm=512):
    T, D = h2.shape
    sub = tm
    tri = jnp.asarray(np.tril(np.ones((sub, sub), np.float32), -1)).astype(BF16)
    wg = jnp.zeros((D, LANES), F32).at[:, :N_GROUPS].set(wgc)
    wg = wg.at[:, GATE_FINE_LANE:GATE_FINE_LANE + N_EXPERTS].set(wgf)
    bg = jnp.zeros((1, LANES), F32).at[0, :N_GROUPS].set(bgc)
    bg = bg.at[0, GATE_FINE_LANE:GATE_FINE_LANE + N_EXPERTS].set(bgf)
    wgh, wgl = _split(wg)
    row = lambda c: pl.BlockSpec((tm, c), lambda i: (i, 0))
    full = lambda r, c: pl.BlockSpec((r, c), lambda i: (0, 0))
    return pl.pallas_call(
        _outproj_kernel,
        grid=(T // tm,),
        in_specs=[row(D), row(HY_WIDTH), row(ATT_WIDTH), row(POOL_WIDTH),
                  pl.BlockSpec((None, D, D), lambda i: (layer, 0, 0)),
                  full(1, D), full(1, D), full(D, LANES), full(D, LANES), full(1, LANES), full(sub, sub)],
        out_specs=[row(D), pl.BlockSpec((tm * TOKEN_ROWS, LANES), lambda i: (i, 0)), row(LANES), full(1, LANES)],
        out_shape=[jax.ShapeDtypeStruct((T, D), F32), jax.ShapeDtypeStruct((T * TOKEN_ROWS, LANES), F32),
                   jax.ShapeDtypeStruct((T, LANES), F32), jax.ShapeDtypeStruct((1, LANES), F32)],
        scratch_shapes=[pltpu.VMEM((1, LANES), F32)],
        compiler_params=_cparams(("arbitrary",)),
        name="out_proj_ln_route",
    )(h2, yh, ya, yp, w_out_bf, g.reshape(1, D), b.reshape(1, D), wgh, wgl, bg, tri)


MOE_TILE = 256
DMA_BATCH = 256


def _moe_rows_padded(T):
    return T + N_CLASSES * MOE_TILE


def _token_rows(ref, t, unit_rows):
    return ref.at[pl.ds(pl.multiple_of(t * unit_rows, unit_rows), unit_rows)]


DMA_GROUP = 8


def _start_tokens(n, slot_of, start_copy):
    def group(g, carry):
        first = g * DMA_GROUP
        slots = [slot_of(first + k) for k in range(DMA_GROUP)]
        for k in range(DMA_GROUP):
            start_copy(first + k, slots[k], k % 2)
        return carry

    lax.fori_loop(0, n // DMA_GROUP, group, 0)


def _wait_tokens(n, src_ref, dst_ref, unit_rows, sems):
    rows = pl.ds(0, (n // 2) * unit_rows)
    for sem in sems:
        pltpu.make_async_copy(src_ref.at[rows], dst_ref.at[rows], sem).wait()


def _dispatch_kernel(cls_ref, rank_ref, cnt_ref, h1t_ref, xs_ref, dest_ref, tlo_ref, thi_ref, tval_ref,
                     off_scr, zero_scr, sem, *, tm, n_tiles):
    i = pl.program_id(0)
    base = i * tm
    tile_rows = MOE_TILE * TOKEN_ROWS

    def fill_copy(tile):
        rows = pl.ds(pl.multiple_of(tile * tile_rows, tile_rows), tile_rows)
        return pltpu.make_async_copy(zero_scr, xs_ref.at[rows], sem.at[2])

    @pl.when(i == 0)
    def _():
        zero_scr[...] = jnp.zeros_like(zero_scr)
        off = jnp.int32(0)
        tile = jnp.int32(0)
        fills = []
        for c in range(N_CLASSES):
            n = cnt_ref[c]
            nt = lax.shift_right_logical(n + (MOE_TILE - 1), MOE_TILE.bit_length() - 1)
            off_scr[c] = off
            e_lo, e_hi = _class_experts(c)

            def mark(k, carry, tile=tile, e_lo=e_lo, e_hi=e_hi):
                tlo_ref[tile + k] = e_lo
                thi_ref[tile + k] = e_hi
                tval_ref[tile + k] = 1
                return carry

            lax.fori_loop(0, nt, mark, 0)
            fill = fill_copy(tile + nt - 1)
            fills.append((nt > 0, fill))

            @pl.when(nt > 0)
            def _(fill=fill):
                fill.start()

            off = off + nt * MOE_TILE
            tile = tile + nt

        def unused(k, carry):
            tlo_ref[k] = 0
            thi_ref[k] = 0
            tval_ref[k] = 0
            fill_copy(k).start()
            return carry

        lax.fori_loop(tile, n_tiles, unused, 0)
        for used, fill in fills:
            @pl.when(used)
            def _(fill=fill):
                fill.wait()

        def unused_wait(k, carry):
            fill_copy(k).wait()
            return carry

        lax.fori_loop(tile, n_tiles, unused_wait, 0)

    def slots_of_step(step):
        def slot(r, carry):
            t = step * tm + r
            dest_ref[t] = off_scr[cls_ref[t]] + rank_ref[t]
            return carry

        lax.fori_loop(0, tm, slot, 0, unroll=8)

    @pl.when(i == 0)
    def _():
        slots_of_step(0)

    def start_copy(r, d, prio):
        pltpu.make_async_copy(_token_rows(h1t_ref, r, TOKEN_ROWS), _token_rows(xs_ref, d, TOKEN_ROWS),
                              sem.at[prio]).start(priority=prio)

    _start_tokens(tm, lambda r: dest_ref[base + r], start_copy)

    @pl.when(i + 1 < pl.num_programs(0))
    def _():
        slots_of_step(i + 1)

    _wait_tokens(tm, h1t_ref, xs_ref, TOKEN_ROWS, [sem.at[0], sem.at[1]])


def _moe_dispatch(h1t, cls_i, rank_i, cnt_i, tm=512):
    T = h1t.shape[0] // TOKEN_ROWS
    slots = _moe_rows_padded(T)
    n_tiles = slots // MOE_TILE
    smem = pl.BlockSpec(memory_space=pltpu.SMEM)
    return pl.pallas_call(
        functools.partial(_dispatch_kernel, tm=tm, n_tiles=n_tiles),
        grid_spec=pltpu.PrefetchScalarGridSpec(
            num_scalar_prefetch=3,
            grid=(T // tm,),
            in_specs=[pl.BlockSpec((tm * TOKEN_ROWS, LANES), lambda i, *_: (i, 0))],
            out_specs=[pl.BlockSpec(memory_space=pl.ANY), smem, smem, smem, smem],
            scratch_shapes=[pltpu.SMEM((N_CLASSES,), jnp.int32),
                            pltpu.VMEM((MOE_TILE * TOKEN_ROWS, LANES), F32),
                            pltpu.SemaphoreType.DMA((3,))],
        ),
        out_shape=[jax.ShapeDtypeStruct((slots * TOKEN_ROWS, LANES), F32), jax.ShapeDtypeStruct((T,), jnp.int32),
                   jax.ShapeDtypeStruct((n_tiles,), jnp.int32), jax.ShapeDtypeStruct((n_tiles,), jnp.int32),
                   jax.ShapeDtypeStruct((n_tiles,), jnp.int32)],
        compiler_params=_cparams(("arbitrary",)),
        name="moe_dispatch",
    )(cls_i, rank_i, cnt_i, h1t)


def _experts_kernel(tlo_ref, thi_ref, tval_ref, xs_ref, w1l_ref, w3l_ref, w2l_ref, w1h_ref, w3h_ref, w2h_ref,
                    ys_ref):
    j = pl.program_id(0)
    d = functools.partial(jnp.dot, preferred_element_type=F32)

    @pl.when(tval_ref[j] == 1)
    def _():
        xb = _load_token_major(xs_ref, MOE_TILE, 0, TOKEN_ROWS).astype(BF16)

        def expert(w1_ref, w3_ref, w2_ref):
            a = d(xb, w1_ref[...])
            c = d(xb, w3_ref[...])
            return d((a * jax.nn.sigmoid(a) * c).astype(BF16), w2_ref[...])

        _store_token_major(ys_ref.at[0], expert(w1l_ref, w3l_ref, w2l_ref), 0, TOKEN_ROWS)
        _store_token_major(ys_ref.at[1], expert(w1h_ref, w3h_ref, w2h_ref), 0, TOKEN_ROWS)

    @pl.when(tval_ref[j] == 0)
    def _():
        ys_ref[...] = jnp.zeros_like(ys_ref)


def _moe_experts(xs, tlo, thi, tval, w1_bf, w3_bf, w2_bf, layer):
    slots = xs.shape[0] // TOKEN_ROWS
    D = D_MODEL
    wspec = lambda shape, which: pl.BlockSpec(
        (None, None) + shape, (lambda j, tlo, thi, tval: (layer, tlo[j], 0, 0)) if which == 0
        else (lambda j, tlo, thi, tval: (layer, thi[j], 0, 0)))
    up = (D, D_EXPERT)
    down = (D_EXPERT, D)
    return pl.pallas_call(
        _experts_kernel,
        grid_spec=pltpu.PrefetchScalarGridSpec(
            num_scalar_prefetch=3,
            grid=(slots // MOE_TILE,),
            in_specs=[pl.BlockSpec((MOE_TILE * TOKEN_ROWS, LANES), lambda j, *_: (j, 0)),
                      wspec(up, 0), wspec(up, 0), wspec(down, 0), wspec(up, 1), wspec(up, 1), wspec(down, 1)],
            out_specs=pl.BlockSpec((None, 2, MOE_TILE * TOKEN_ROWS, LANES), lambda j, *_: (j, 0, 0, 0)),
        ),
        out_shape=jax.ShapeDtypeStruct((slots // MOE_TILE, 2, MOE_TILE * TOKEN_ROWS, LANES), F32),
        compiler_params=_cparams(("arbitrary",)),
        name="moe_experts",
    )(tlo, thi, tval, xs, w1_bf, w3_bf, w2_bf, w1_bf, w3_bf, w2_bf)


def _combine_kernel(dest_ref, h1_ref, route_ref, ys_ref, p_ref, pwg_ref, pbg_ref, pwp_ref, g_ref, b_ref,
                    o_ref, y_scr, ym_scr, sem, *, tm):
    i = pl.program_id(0)
    d = functools.partial(jnp.dot, preferred_element_type=F32)
    tile_shift = MOE_TILE.bit_length() - 1

    def unit_rows(t):
        return pl.ds(pl.multiple_of(t * TOKEN_ROWS, TOKEN_ROWS), TOKEN_ROWS)

    def gather(step, buf):
        def start_copy(r, slot, prio):
            tile = lax.shift_right_logical(slot, tile_shift)
            src = ys_ref.at[tile, :, unit_rows(slot & (MOE_TILE - 1)), :]
            pltpu.make_async_copy(src, y_scr.at[buf, :, unit_rows(r), :],
                                  sem.at[2 * buf + prio]).start(priority=prio)
        _start_tokens(tm, lambda r: dest_ref[step * tm + r], start_copy)

    def wait_gathers(buf):
        rows = pl.ds(0, (tm // 2) * TOKEN_ROWS)
        for prio in range(2):
            pltpu.make_async_copy(ys_ref.at[0, :, rows, :], y_scr.at[buf, :, rows, :],
                                  sem.at[2 * buf + prio]).wait()

    @pl.when(i == 0)
    def _():
        gather(0, 0)

    for buf in range(2):
        @pl.when((i + 1 < pl.num_programs(0)) & ((i + 1) % 2 == buf))
        def _(buf=buf):
            gather(i + 1, buf)

    sub = tm // 2
    halves = [slice(s * sub, (s + 1) * sub) for s in range(2)]
    y_ple = []
    for rows in halves:
        z = d(h1_ref[rows, :].astype(BF16), pwg_ref[...]) + pbg_ref[...]
        y_ple.append(jax.nn.sigmoid(z) * d(p_ref[rows, :].astype(BF16), pwp_ref[...]))
    for buf in range(2):
        @pl.when(i % 2 == buf)
        def _(buf=buf):
            wait_gathers(buf)
            for s, rows in enumerate(halves):
                rec = route_ref[rows, :]
                first = s * sub * TOKEN_ROWS
                ym_scr[rows, :] = (rec[:, ROUTE_W_LO:ROUTE_W_LO + 1]
                                   * _load_token_major(y_scr.at[buf, 0], sub, first, TOKEN_ROWS)
                                   + rec[:, ROUTE_W_HI:ROUTE_W_HI + 1]
                                   * _load_token_major(y_scr.at[buf, 1], sub, first, TOKEN_ROWS))
    for s, rows in enumerate(halves):
        r_ = DN_ALPHA * h1_ref[rows, :] + ym_scr[rows, :] + y_ple[s]
        o_ref[rows, :] = _ln_rows(r_, g_ref[...], b_ref[...])


def _moe_combine_ple_ln(h1, route, ys, dest, p3, pwg_bf, pbg, pwp_bf, layer, g, b, tm=512):
    T, D = h1.shape
    row = lambda c: pl.BlockSpec((tm, c), lambda i, *_: (i, 0))
    full = lambda r, c: pl.BlockSpec((r, c), lambda i, *_: (0, 0))
    lay = lambda r, c: pl.BlockSpec((None, r, c), lambda i, *_: (layer, 0, 0))
    return pl.pallas_call(
        functools.partial(_combine_kernel, tm=tm),
        grid_spec=pltpu.PrefetchScalarGridSpec(
            num_scalar_prefetch=1,
            grid=(T // tm,),
            in_specs=[row(D), row(LANES), pl.BlockSpec(memory_space=pl.ANY),
                      pl.BlockSpec((None, tm, PLE_DIM), lambda i, *_: (layer, i, 0)),
                      lay(D, D), full(1, D), lay(PLE_DIM, D), full(1, D), full(1, D)],
            out_specs=row(D),
            scratch_shapes=[pltpu.VMEM((2, 2, tm * TOKEN_ROWS, LANES), F32), pltpu.VMEM((tm, D), F32),
                            pltpu.SemaphoreType.DMA((4,))],
        ),
        out_shape=jax.ShapeDtypeStruct((T, D), F32),
        compiler_params=_cparams(("arbitrary",)),
        name="moe_combine_ple_ln",
    )(dest, h1, route, ys, p3, pwg_bf, pbg.reshape(1, D), pwp_bf, g.reshape(1, D), b.reshape(1, D))


def _hyena_mixer(u3, kr4, ki4, layer, consts, conv_w, conv_b, bias_d, n1_total):
    B, S, _ = u3.shape
    vv, x0c = _hy_gate(u3, conv_w, conv_b)
    y = _hy_conv(vv, x0c, kr4, ki4, layer, consts, bias_d, n1_total)
    return y.reshape(B * S, HY_WIDTH)


def kernel(x, p, ln0_g, ln0_b, w_in, hy_conv_w, hy_conv_b, hy_fw1, hy_fb1, hy_freq1, hy_fw2, hy_fb2, hy_freq2, hy_fw3, hy_bias, att_lq1, att_lk1, att_lq2, att_lk2, att_subln_g, pool_w, pool_b, pool_scale, w_out, ln1_g, ln1_b, moe_wgc, moe_bgc, moe_wgf, moe_bgf, moe_w1, moe_w3, moe_w2, ple_wg, ple_bg, ple_wp, ln2_g, ln2_b):
    B, S, D = x.shape
    L = w_in.shape[0]
    T = B * S
    n1_total = 2 * S // FFT_N2
    C = HY_WIDTH

    kfilt, ssq = _hy_filter(S, hy_fw1, hy_fb1, hy_freq1, hy_fw2, hy_fb2, hy_freq2, hy_fw3)
    consts = _stage_b_consts(n1_total)
    kr_all, ki_all = _hy_spectrum(kfilt, ssq, consts, n1_total)
    tables = _rope_tables(S)

    w_in_bf, w_out_bf = w_in.astype(BF16), w_out.astype(BF16)
    w1_bf, w3_bf, w2_bf = moe_w1.astype(BF16), moe_w3.astype(BF16), moe_w2.astype(BF16)
    pwg_bf, pwp_bf = ple_wg.astype(BF16), ple_wp.astype(BF16)
    p3 = p.reshape(L, T, PLE_DIM)

    h = _layer_norm(x.reshape(T, D), ln0_g, ln0_b)
    for i in range(L):
        lam_init = 0.8 - 0.6 * math.exp(-0.3 * i)
        uh, qkv, up = _in_proj(h, w_in_bf, i, tables, S)
        y_hy = _hyena_mixer(uh.reshape(B, S, 3 * C), kr_all, ki_all, i, consts,
                            hy_conv_w[i], hy_conv_b[i], hy_bias[i], n1_total)
        y_att = _diff_attention(qkv.reshape(B, S, 3 * ATT_WIDTH), att_lq1[i], att_lk1[i], att_lq2[i],
                                att_lk2[i], att_subln_g[i], lam_init).reshape(T, ATT_WIDTH)
        y_pool = _pool_mixer(up.reshape(B, S, POOL_WIDTH), pool_w[i], pool_b[i],
                             pool_scale[i]).reshape(T, POOL_WIDTH)
        h1, h1t, route, counts = _out_proj_ln_route(h, y_hy, y_att, y_pool, w_out_bf, i, ln1_g[i],
                                                    ln1_b[i], moe_wgc[i], moe_bgc[i], moe_wgf[i], moe_bgf[i])
        cls_i = route[:, ROUTE_CLS].astype(jnp.int32)
        rank_i = route[:, ROUTE_RANK].astype(jnp.int32)
        xs, dest, tlo, thi, tval = _moe_dispatch(h1t, cls_i, rank_i, counts[0].astype(jnp.int32))
        ys = _moe_experts(xs, tlo, thi, tval, w1_bf, w3_bf, w2_bf, i)
        h = _moe_combine_ple_ln(h1, route, ys, dest, p3, pwg_bf, ple_bg[i], pwp_bf, i, ln2_g[i], ln2_b[i])
    return h.reshape(B, S, D)
```

```python
import functools
import math

import numpy as np
import jax
import jax.numpy as jnp
from jax import lax
from jax.experimental import pallas as pl
from jax.experimental.pallas import tpu as pltpu

F32 = jnp.float32
BF16 = jnp.bfloat16

D_MODEL = 1024
DEPTH = 4
HY_WIDTH = 256
ATT_WIDTH = 512
ATT_HEADS = 4
ATT_HEAD_DIM = 64
POOL_WINDOWS = (2, 4, 8, 16)
POOL_WIDTH = 256
POOL_GROUP = 64
IN_WIDTH = 3 * HY_WIDTH + 3 * ATT_WIDTH + POOL_WIDTH
ROPE_THETA = 500000.0
ROPE_DIM = ATT_HEAD_DIM // 4
HY_EMB = 33
HY_BANDS = (HY_EMB - 1) // 2
HY_FILTER_HIDDEN = 64
HY_DECAY_TARGET = 1e-2
HY_FAST_DECAY = 0.3
HY_SLOW_DECAY = 1.5
N_GROUPS = 4
EXPERTS_PER_GROUP = 4
N_EXPERTS = 16
D_EXPERT = 256
PLE_DIM = 256
LN_EPS = 1e-5
RMS_EPS = 1e-5
DN_ALPHA = (2 * DEPTH) ** 0.25

LANES = 128
FFT_N2 = 128
GATE_COARSE_LANE = 0
GATE_FINE_LANE = N_GROUPS
NEG_BIG = -1e30
LOG2E = 1.4426950408889634
VMEM_LIMIT = 56 * 1024 * 1024


def _cparams(sem):
    return pltpu.CompilerParams(dimension_semantics=sem, vmem_limit_bytes=VMEM_LIMIT)


def _split(x):
    hi = x.astype(BF16)
    lo = (x - hi.astype(F32)).astype(BF16)
    return hi, lo


def _dot3(ah, al, bh, bl):
    d = functools.partial(jnp.dot, preferred_element_type=F32)
    return d(ah, bh) + (d(ah, bl) + d(al, bh))


def _ln_rows(x, g, b):
    mu = jnp.mean(x, axis=-1, keepdims=True)
    xc = x - mu
    var = jnp.mean(xc * xc, axis=-1, keepdims=True)
    return xc * lax.rsqrt(var + LN_EPS) * g + b


def _ln_kernel(x_ref, g_ref, b_ref, o_ref):
    o_ref[...] = _ln_rows(x_ref[...], g_ref[...], b_ref[...])


def _layer_norm(x2, g, b, tm=512):
    T, D = x2.shape
    return pl.pallas_call(
        _ln_kernel,
        grid=(T // tm,),
        in_specs=[pl.BlockSpec((tm, D), lambda i: (i, 0)),
                  pl.BlockSpec((1, D), lambda i: (0, 0)),
                  pl.BlockSpec((1, D), lambda i: (0, 0))],
        out_specs=pl.BlockSpec((tm, D), lambda i: (i, 0)),
        out_shape=jax.ShapeDtypeStruct((T, D), F32),
        compiler_params=_cparams(("parallel",)),
        name="ln0",
    )(x2, g.reshape(1, D), b.reshape(1, D))


def _rope(x, cos_f, sin_a, sin_b):
    half = ROPE_DIM // 2
    return x * cos_f + pltpu.roll(x, LANES - half, 1) * sin_a + pltpu.roll(x, half, 1) * sin_b


def _inproj_kernel(h_ref, w_ref, cos_ref, sa_ref, sb_ref, uh_ref, qkv_ref, up_ref):
    hb = h_ref[...].astype(BF16)
    mm = lambda c0, c1: jnp.dot(hb, w_ref[:, c0:c1], preferred_element_type=F32)
    o_q = 3 * HY_WIDTH
    o_k = o_q + ATT_WIDTH
    o_v = o_k + ATT_WIDTH
    o_p = o_v + ATT_WIDTH
    uh_ref[...] = mm(0, o_q)
    up_ref[...] = mm(o_p, IN_WIDTH)
    cos_f, sin_a, sin_b = cos_ref[...], sa_ref[...], sb_ref[...]
    q = mm(o_q, o_k)
    k = mm(o_k, o_v)
    qscale = ATT_HEAD_DIM ** -0.5 * LOG2E
    for hd in range(ATT_HEADS):
        sl = slice(hd * LANES, (hd + 1) * LANES)
        qkv_ref[:, sl] = (_rope(q[:, sl], cos_f, sin_a, sin_b) * qscale).astype(BF16)
        qkv_ref[:, ATT_WIDTH + hd * LANES:ATT_WIDTH + (hd + 1) * LANES] = \
            _rope(k[:, sl], cos_f, sin_a, sin_b).astype(BF16)
    qkv_ref[:, 2 * ATT_WIDTH:] = mm(o_v, o_p).astype(BF16)


def _in_proj(h2, w_bf, layer, tables, seq, tm=512):
    T, D = h2.shape
    N = w_bf.shape[2]
    spt = seq // tm
    tspec = pl.BlockSpec((tm, LANES), lambda i: (i % spt, 0))
    row = lambda c: pl.BlockSpec((tm, c), lambda i: (i, 0))
    return pl.pallas_call(
        _inproj_kernel,
        grid=(T // tm,),
        in_specs=[row(D), pl.BlockSpec((None, D, N), lambda i: (layer, 0, 0)), tspec, tspec, tspec],
        out_specs=[row(3 * HY_WIDTH), row(3 * ATT_WIDTH), row(POOL_WIDTH)],
        out_shape=[jax.ShapeDtypeStruct((T, 3 * HY_WIDTH), F32),
                   jax.ShapeDtypeStruct((T, 3 * ATT_WIDTH), BF16),
                   jax.ShapeDtypeStruct((T, POOL_WIDTH), F32)],
        compiler_params=_cparams(("parallel",)),
        name="in_proj",
    )(h2, w_bf, *tables)


def _hy_filter_kernel(band_ref, phase_ref, fw1_ref, fb1_ref, fr1_ref, fw2_ref, fb2_ref, fr2_ref, fw3_ref,
                      dl_ref, k_ref, ssq_ref, z_scr, *, seq, tr):
    j = pl.program_id(0)
    l = pl.program_id(1)
    n = j * tr + lax.broadcasted_iota(jnp.int32, (tr, 1), 0)
    pos = jnp.where(n < seq, n, 2 * seq - n).astype(F32)
    t = pos * (1.0 / (seq - 1))

    @pl.when(l == 0)
    def _():
        wpos = (2.0 * math.pi / seq) * pos
        lane = lax.broadcasted_iota(jnp.int32, (tr, LANES), 1)
        z_scr[...] = jnp.where(lane == 0, t,
                               jnp.where(lane < HY_EMB, jnp.sin(wpos * band_ref[...] + phase_ref[...]), 0.0))

    @pl.when((l == 0) & (j == 0))
    def _():
        ssq_ref[...] = jnp.zeros_like(ssq_ref)

    z = z_scr[...]
    h1 = _dot3(*_split(z), *_split(fw1_ref[...])) + fb1_ref[...]
    h1 = jnp.sin(fr1_ref[...] * h1)
    h2 = _dot3(*_split(h1), *_split(fw2_ref[...])) + fb2_ref[...]
    h2 = jnp.sin(fr2_ref[...] * h2)
    filt = _dot3(*_split(h2), *_split(fw3_ref[...]))
    window = jnp.exp(-t * jnp.abs(dl_ref[...]))
    kk = jnp.where(n < seq, filt[:, :HY_WIDTH], filt[:, HY_WIDTH:]) * window
    kk = jnp.where(n == seq, 0.0, kk)
    k_ref[...] = kk
    ssq_ref[l] += jnp.sum(kk * kk, axis=0, keepdims=True)


def _hy_filter(seq, fw1, fb1, freq1, fw2, fb2, freq2, fw3, tr=1024):
    L = fw1.shape[0]
    n = 2 * seq
    H = HY_FILTER_HIDDEN
    bands = np.linspace(1e-4, HY_BANDS - 1, HY_BANDS)
    bandv = np.zeros((1, LANES), np.float32)
    bandv[0, 1:1 + HY_BANDS] = bands
    bandv[0, 1 + HY_BANDS:HY_EMB] = bands
    phase = np.zeros((1, LANES), np.float32)
    phase[0, 1:1 + HY_BANDS] = 0.5 * math.pi
    phase[0, 1 + HY_BANDS:HY_EMB] = math.pi
    fw1p = jnp.zeros((L, LANES, H), F32).at[:, :HY_EMB].set(fw1)
    max_decay = math.log(HY_DECAY_TARGET) / HY_FAST_DECAY
    min_decay = math.log(HY_DECAY_TARGET) / HY_SLOW_DECAY
    deltas = np.linspace(min_decay, max_decay, HY_WIDTH).astype(np.float32).reshape(1, HY_WIDTH)
    vec = lambda a: a.reshape(L, 1, a.shape[-1])
    lspec = lambda r, c: pl.BlockSpec((None, r, c), lambda j, l: (l, 0, 0))
    lanes = pl.BlockSpec((1, LANES), lambda j, l: (0, 0))
    return pl.pallas_call(
        functools.partial(_hy_filter_kernel, seq=seq, tr=tr),
        grid=(n // tr, L),
        in_specs=[lanes, lanes,
                  lspec(LANES, H), lspec(1, H), lspec(1, H),
                  lspec(H, H), lspec(1, H), lspec(1, H),
                  lspec(H, 2 * HY_WIDTH),
                  pl.BlockSpec((1, HY_WIDTH), lambda j, l: (0, 0))],
        out_specs=[pl.BlockSpec((None, tr, HY_WIDTH), lambda j, l: (l, j, 0)),
                   pl.BlockSpec((L, 1, HY_WIDTH), lambda j, l: (0, 0, 0))],
        out_shape=[jax.ShapeDtypeStruct((L, n, HY_WIDTH), F32),
                   jax.ShapeDtypeStruct((L, 1, HY_WIDTH), F32)],
        scratch_shapes=[pltpu.VMEM((tr, LANES), F32)],
        compiler_params=_cparams(("arbitrary", "arbitrary")),
        name="hy_filter",
    )(jnp.asarray(bandv), jnp.asarray(phase), fw1p, vec(fb1), vec(freq1), fw2, vec(fb2), vec(freq2), fw3,
      jnp.asarray(deltas))


def _k1_rows(n1):
    k1 = n1 // 2 + 1
    return k1, -(-k1 // 8) * 8


def _bf_pair(m):
    m32 = jnp.asarray(np.asarray(m, np.float32))
    return _split(m32)


SUBLANES = 8


def _stage_a_matrix(n1_used, n1_total):
    k1n, k1p = _k1_rows(n1_total)
    k1 = np.arange(k1p)
    valid = (k1 < k1n)[:, None]
    ang = 2.0 * np.pi * np.outer(k1, np.arange(n1_used)) / n1_total
    eye = np.eye(SUBLANES)
    return np.concatenate([np.kron(np.cos(ang) * valid, eye), np.kron(-np.sin(ang) * valid, eye)], axis=0)


def _stage_a_rows(x_ref, a_scr, kah_ref, kal_ref, n1_used, k1p):
    def stage_a(g, carry):
        xg = jnp.concatenate(
            [x_ref[pl.ds(pl.multiple_of(n1 * FFT_N2 + g * SUBLANES, SUBLANES), SUBLANES), :]
             for n1 in range(n1_used)], axis=0)
        ag = _dot3(kah_ref[...], kal_ref[...], *_split(xg))
        gs = pl.ds(pl.multiple_of(g * SUBLANES, SUBLANES), SUBLANES)
        for part in range(2):
            for k1 in range(k1p):
                r = (part * k1p + k1) * SUBLANES
                a_scr[part, k1, gs, :] = ag[r:r + SUBLANES]
        return carry

    lax.fori_loop(0, FFT_N2 // SUBLANES, stage_a, 0, unroll=2)


def _stage_b_consts(n1_total):
    n = n1_total * FFT_N2
    k1n, k1p = _k1_rows(n1_total)
    ang2 = 2.0 * np.pi * np.outer(np.arange(FFT_N2), np.arange(FFT_N2)) / FFT_N2
    c2, s2 = np.cos(ang2), np.sin(ang2)
    mf = np.block([[c2, s2], [-s2, c2]])
    mi = np.block([[c2, -s2], [s2, c2]])
    angt = 2.0 * np.pi * np.outer(np.arange(k1p), np.arange(FFT_N2)) / n
    twc = jnp.asarray(np.cos(angt).astype(np.float32)).reshape(k1p, FFT_N2, 1)
    tws = jnp.asarray(np.sin(angt).astype(np.float32)).reshape(k1p, FFT_N2, 1)
    return _bf_pair(mf), _bf_pair(mi), twc, tws


def _fwd_b(ar, ai, c, s, mfh, mfl):
    tr_ = ar * c + ai * s
    ti_ = ai * c - ar * s
    xh, xl = _split(jnp.concatenate([tr_, ti_], axis=0))
    z = _dot3(mfh, mfl, xh, xl)
    return z[:FFT_N2], z[FFT_N2:]


def _hy_spectrum_kernel(k_ref, ssq_ref, tw_ref, kah_ref, kal_ref, mfh_ref, mfl_ref, kr_ref, ki_ref, a_scr,
                        *, n1_total, k1n, k1p):
    _stage_a_rows(k_ref, a_scr, kah_ref, kal_ref, n1_total, k1p)
    scale = lax.rsqrt(ssq_ref[...] + 1e-6) * (1.0 / (n1_total * FFT_N2))

    def stage_b(k1, carry):
        tw = tw_ref[k1]
        zr, zi = _fwd_b(a_scr[0, k1], a_scr[1, k1], tw[:, 0:1], tw[:, 1:2], mfh_ref[...], mfl_ref[...])
        kr_ref[k1] = zr * scale
        ki_ref[k1] = zi * scale
        return carry

    lax.fori_loop(0, k1n, stage_b, 0, unroll=3 if k1n % 3 == 0 else 1)
    for k1 in range(k1n, k1p):
        kr_ref[k1] = jnp.zeros(kr_ref.shape[1:], F32)
        ki_ref[k1] = jnp.zeros(ki_ref.shape[1:], F32)


def _hy_spectrum(kfilt, ssq, consts, n1_total):
    L, n, C = kfilt.shape
    k1n, k1p = _k1_rows(n1_total)
    (mfh, mfl), _, twc, tws = consts
    tw = jnp.concatenate([twc, tws], axis=-1)
    kah, kal = _bf_pair(_stage_a_matrix(n1_total, n1_total))
    full = lambda a: pl.BlockSpec(a.shape, lambda l, c: (0,) * a.ndim)
    ospec = pl.BlockSpec((None, k1p, FFT_N2, LANES), lambda l, c: (l, 0, 0, c))
    return pl.pallas_call(
        functools.partial(_hy_spectrum_kernel, n1_total=n1_total, k1n=k1n, k1p=k1p),
        grid=(L, C // LANES),
        in_specs=[pl.BlockSpec((None, n, LANES), lambda l, c: (l, 0, c)),
                  pl.BlockSpec((None, 1, LANES), lambda l, c: (l, 0, c)),
                  full(tw), full(kah), full(kal), full(mfh), full(mfl)],
        out_specs=[ospec, ospec],
        out_shape=[jax.ShapeDtypeStruct((L, k1p, FFT_N2, C), F32)] * 2,
        scratch_shapes=[pltpu.VMEM((2, k1p, FFT_N2, LANES), F32)],
        compiler_params=_cparams(("parallel", "parallel")),
        name="hy_spectrum",
    )(kfilt, ssq, tw, kah, kal, mfh, mfl)


def _hy_conv_kernel(vv_ref, x0_ref, kr_ref, ki_ref, tw_ref, kah_ref, kal_ref, kch_ref, kcl_ref,
                    mfh_ref, mfl_ref, mih_ref, mil_ref, bias_ref, o_ref, a_scr, *, n1u, k1n, k1p):
    groups = FFT_N2 // SUBLANES

    def rows(n1, g):
        return pl.ds(pl.multiple_of(n1 * FFT_N2 + g * SUBLANES, SUBLANES), SUBLANES)

    _stage_a_rows(vv_ref, a_scr, kah_ref, kal_ref, n1u, k1p)

    def stage_b(k1, carry):
        tw = tw_ref[k1]
        c = tw[:, 0:1]
        s = tw[:, 1:2]
        zr, zi = _fwd_b(a_scr[0, k1], a_scr[1, k1], c, s, mfh_ref[...], mfl_ref[...])
        kr = kr_ref[k1]
        ki = ki_ref[k1]
        pr = zr * kr - zi * ki
        pi = zr * ki + zi * kr
        y = _dot3(mih_ref[...], mil_ref[...], *_split(jnp.concatenate([pr, pi], axis=0)))
        yr = y[:FFT_N2]
        yi = y[FFT_N2:]
        a_scr[0, k1] = yr * c - yi * s
        a_scr[1, k1] = yi * c + yr * s
        return carry

    lax.fori_loop(0, k1n, stage_b, 0, unroll=11 if k1n % 11 == 0 else 1)

    def stage_c(g, carry):
        gs = pl.ds(pl.multiple_of(g * SUBLANES, SUBLANES), SUBLANES)
        bg = jnp.concatenate([a_scr[part, k1, gs, :] for part in range(2) for k1 in range(k1p)], axis=0)
        yg = _dot3(kch_ref[...], kcl_ref[...], *_split(bg))
        for n1 in range(n1u):
            r = rows(n1, g)
            o_ref[r, :] = (yg[n1 * SUBLANES:(n1 + 1) * SUBLANES] + vv_ref[r, :] * bias_ref[...]) * x0_ref[r, :]
        return carry

    lax.fori_loop(0, groups, stage_c, 0, unroll=2)


def _hy_conv(vv, x0c, kr4, ki4, layer, consts, bias_d, n1_total):
    B, S, C = vv.shape
    n1u = S // FFT_N2
    k1n, k1p = _k1_rows(n1_total)
    (mfh, mfl), (mih, mil), twc, tws = consts
    tw = jnp.concatenate([twc, tws], axis=-1)
    k1 = np.arange(k1p)
    eye = np.eye(SUBLANES)
    ka = _stage_a_matrix(n1u, n1_total)
    w = np.where((k1 == 0) | (k1 == n1_total // 2), 1.0, 2.0) * (k1 < k1n)
    ang_c = 2.0 * np.pi * np.outer(np.arange(n1u), k1) / n1_total
    kc = np.concatenate([np.kron(np.cos(ang_c) * w[None, :], eye), np.kron(-np.sin(ang_c) * w[None, :], eye)], axis=1)
    kah, kal = _bf_pair(ka)
    kch, kcl = _bf_pair(kc)
    nct = C // LANES
    xspec = pl.BlockSpec((None, S, LANES), lambda c, b: (b, 0, c))
    kspec = pl.BlockSpec((None, k1p, FFT_N2, LANES), lambda c, b: (layer, 0, 0, c))
    full = lambda a: pl.BlockSpec(a.shape, lambda c, b: (0,) * a.ndim)
    return pl.pallas_call(
        functools.partial(_hy_conv_kernel, n1u=n1u, k1n=k1n, k1p=k1p),
        grid=(nct, B),
        in_specs=[xspec, xspec, kspec, kspec, full(tw), full(kah), full(kal), full(kch), full(kcl),
                  full(mfh), full(mfl), full(mih), full(mil),
                  pl.BlockSpec((1, LANES), lambda c, b: (0, c))],
        out_specs=xspec,
        out_shape=jax.ShapeDtypeStruct((B, S, C), F32),
        scratch_shapes=[pltpu.VMEM((2, k1p, FFT_N2, LANES), F32)],
        compiler_params=_cparams(("parallel", "parallel")),
        name="hy_conv",
    )(vv, x0c, kr4, ki4, tw, kah, kal, kch, kcl, mfh, mfl, mih, mil, bias_d.reshape(1, C))


def _conv3(x, w_ref, b_ref):
    S = x.shape[0]
    row = lax.broadcasted_iota(jnp.int32, x.shape, 0)
    prev = jnp.where(row == 0, 0.0, pltpu.roll(x, 1, 0))
    nxt = jnp.where(row == S - 1, 0.0, pltpu.roll(x, S - 1, 0))
    return prev * w_ref[0:1, :] + x * w_ref[1:2, :] + nxt * w_ref[2:3, :] + b_ref[...]


def _hy_gate_kernel(x0_ref, x1_ref, v_ref, w0_ref, w1_ref, w2_ref, b0_ref, b1_ref, b2_ref,
                    vv_ref, x0c_ref):
    x0c_ref[...] = _conv3(x0_ref[...], w0_ref, b0_ref)
    vv_ref[...] = _conv3(v_ref[...], w2_ref, b2_ref) * _conv3(x1_ref[...], w1_ref, b1_ref)


def _hy_gate(u3, conv_w, conv_b):
    B, S, _ = u3.shape
    nct = HY_WIDTH // LANES
    conv_b2 = conv_b.reshape(1, 3 * HY_WIDTH)
    uspec = lambda part: pl.BlockSpec((None, S, LANES), lambda b, c: (b, 0, part * nct + c))
    wspec = lambda part: pl.BlockSpec((3, LANES), lambda b, c: (0, part * nct + c))
    bspec = lambda part: pl.BlockSpec((1, LANES), lambda b, c: (0, part * nct + c))
    ospec = pl.BlockSpec((None, S, LANES), lambda b, c: (b, 0, c))
    return pl.pallas_call(
        _hy_gate_kernel,
        grid=(B, nct),
        in_specs=[uspec(0), uspec(1), uspec(2), wspec(0), wspec(1), wspec(2),
                  bspec(0), bspec(1), bspec(2)],
        out_specs=[ospec, ospec],
        out_shape=[jax.ShapeDtypeStruct((B, S, HY_WIDTH), F32)] * 2,
        compiler_params=_cparams(("parallel", "parallel")),
        name="hy_gate",
    )(u3, u3, u3, conv_w, conv_w, conv_w, conv_b2, conv_b2, conv_b2)


def _attn_kernel(q_ref, k_ref, v_ref, lq1_ref, lk1_ref, lq2_ref, lk2_ref, g_ref, o_ref,
                 s_scr, p_scr, vx_scr, kt_scr, *, lam_init, tq):
    S = q_ref.shape[0]
    nblk = S // tq
    lam = (jnp.exp(jnp.sum(lq1_ref[...] * lk1_ref[...], keepdims=True))
           - jnp.exp(jnp.sum(lq2_ref[...] * lk2_ref[...], keepdims=True)) + lam_init)

    def blk(i):
        return pl.ds(pl.multiple_of(i * tq, tq), tq)

    kt = k_ref[...].T
    feat = lax.broadcasted_iota(jnp.int32, kt.shape, 0)
    kt_scr[0] = jnp.where(feat < ATT_HEAD_DIM, kt, jnp.zeros_like(kt))
    kt_scr[1] = jnp.where(feat >= ATT_HEAD_DIM, kt, jnp.zeros_like(kt))

    def scores(i, slot):
        q = q_ref[blk(i), :]
        for c in range(2):
            s_scr[slot, c] = jnp.dot(q, kt_scr[c], preferred_element_type=F32)

    vx_scr[:, :LANES] = v_ref[...]
    vx_scr[:, LANES:] = (lax.broadcasted_iota(jnp.int32, (S, LANES), 1) == 0).astype(BF16)

    def softmax(slot):
        for c in range(2):
            s = s_scr[slot, c]
            m = jnp.max(s, axis=-1, keepdims=True)
            p_scr[slot, c] = jnp.exp2(s - m).astype(BF16)

    def values(i, slot):
        oe1 = jnp.dot(p_scr[slot, 0], vx_scr[...], preferred_element_type=F32)
        oe2 = jnp.dot(p_scr[slot, 1], vx_scr[...], preferred_element_type=F32)
        o = oe1[:, :LANES] / oe1[:, LANES:LANES + 1] - oe2[:, :LANES] * (lam / oe2[:, LANES:LANES + 1])
        o = o * lax.rsqrt(jnp.mean(o * o, axis=-1, keepdims=True) + RMS_EPS) * g_ref[...]
        o_ref[blk(i), :] = o * (1.0 - lam_init)

    scores(0, 0)
    softmax(0)
    scores(1, 1)

    def pair(j, carry):
        t = 2 * j
        values(t - 2, 0)
        softmax(1)
        scores(t, 0)
        values(t - 1, 1)
        softmax(0)
        scores(t + 1, 1)
        return carry

    lax.fori_loop(1, nblk // 2, pair, 0)
    values(nblk - 2, 0)
    softmax(1)
    values(nblk - 1, 1)


def _rope_tables(seq):
    pos = np.arange(seq, dtype=np.float64)
    inv_freq = np.power(ROPE_THETA, -np.arange(0, ROPE_DIM, 2, dtype=np.float64) / ROPE_DIM)
    ang = pos[:, None] * inv_freq[None, :]
    half = ROPE_DIM // 2
    cos_f = np.ones((seq, LANES), np.float32)
    sin_a = np.zeros((seq, LANES), np.float32)
    sin_b = np.zeros((seq, LANES), np.float32)
    for base in range(0, LANES, ATT_HEAD_DIM):
        cos_f[:, base:base + half] = np.cos(ang)
        cos_f[:, base + half:base + ROPE_DIM] = np.cos(ang)
        sin_a[:, base:base + half] = -np.sin(ang)
        sin_b[:, base + half:base + ROPE_DIM] = np.sin(ang)
    return jnp.asarray(cos_f), jnp.asarray(sin_a), jnp.asarray(sin_b)


def _diff_attention(qkv3, lq1, lk1, lq2, lk2, subln_g, lam_init, tq=256):
    B, S, _ = qkv3.shape
    nb = ATT_WIDTH // LANES
    assert S % (2 * tq) == 0
    spec = lambda part: pl.BlockSpec((None, S, LANES), lambda b, h: (b, 0, part * nb + h))
    vec = pl.BlockSpec((1, ATT_HEAD_DIM), lambda b, h: (0, 0))
    r1 = lambda a: a.reshape(1, -1)
    return pl.pallas_call(
        functools.partial(_attn_kernel, lam_init=lam_init, tq=tq),
        grid=(B, ATT_HEADS),
        in_specs=[spec(0), spec(1), spec(2), vec, vec, vec, vec,
                  pl.BlockSpec((1, LANES), lambda b, h: (0, 0))],
        out_specs=pl.BlockSpec((None, S, LANES), lambda b, h: (b, 0, h)),
        out_shape=jax.ShapeDtypeStruct((B, S, ATT_WIDTH), F32),
        scratch_shapes=[pltpu.VMEM((2, 2, tq, S), F32), pltpu.VMEM((2, 2, tq, S), BF16),
                        pltpu.VMEM((S, 2 * LANES), BF16), pltpu.VMEM((2, LANES, S), BF16)],
        compiler_params=_cparams(("parallel", "parallel")),
        name="diff_attn",
    )(qkv3, qkv3, qkv3, r1(lq1), r1(lk1), r1(lq2), r1(lk2), r1(subln_g))


def _pool_kernel(u_ref, w_ref, b_ref, sc_ref, o_ref):
    ct = pl.program_id(1)
    x = u_ref[...]
    S = x.shape[0]
    row = lax.broadcasted_iota(jnp.int32, x.shape, 0)
    lane = lax.broadcasted_iota(jnp.int32, x.shape, 1)

    def shifted(d):
        if d == 0:
            return x
        r = pltpu.roll(x, (-d) % S, 0)
        return jnp.where((row + d >= 0) & (row + d < S), r, 0.0)

    sums = {}
    acc = shifted(-1) + x
    sums[2] = acc
    lo, hi = -1, 0
    for w in POOL_WINDOWS[1:]:
        for d in list(range(-(w // 2), lo)) + list(range(hi + 1, w // 2)):
            acc = acc + shifted(d)
        lo, hi = -(w // 2), w // 2 - 1
        sums[w] = acc
    grp = 2 * ct + (lane >= POOL_GROUP).astype(jnp.int32)
    win_sum = sums[POOL_WINDOWS[-1]]
    half = jnp.full(x.shape, POOL_WINDOWS[-1] // 2, jnp.int32)
    for g in range(len(POOL_WINDOWS) - 2, -1, -1):
        win_sum = jnp.where(grp == g, sums[POOL_WINDOWS[g]], win_sum)
        half = jnp.where(grp == g, POOL_WINDOWS[g] // 2, half)
    cnt = jnp.minimum(row + half - 1, S - 1) - jnp.maximum(row - half, 0) + 1
    y = win_sum / cnt.astype(F32) - x
    y = _dot3(*_split(y), *_split(w_ref[...])) + b_ref[...]
    o_ref[...] = y * sc_ref[...]


def _pool_mixer(u3, w, b, scale):
    B, S, _ = u3.shape
    nct = POOL_WIDTH // LANES
    gpt = LANES // POOL_GROUP
    wbd = jnp.zeros((nct, LANES, LANES), F32)
    for g in range(len(POOL_WINDOWS)):
        t, o = divmod(g, gpt)
        wbd = wbd.at[t, o * POOL_GROUP:(o + 1) * POOL_GROUP, o * POOL_GROUP:(o + 1) * POOL_GROUP].set(w[g])
    vspec = pl.BlockSpec((1, LANES), lambda bb, c: (0, c))
    return pl.pallas_call(
        _pool_kernel,
        grid=(B, nct),
        in_specs=[pl.BlockSpec((None, S, LANES), lambda bb, c: (bb, 0, c)),
                  pl.BlockSpec((None, LANES, LANES), lambda bb, c: (c, 0, 0)),
                  vspec, vspec],
        out_specs=pl.BlockSpec((None, S, LANES), lambda bb, c: (bb, 0, c)),
        out_shape=jax.ShapeDtypeStruct((B, S, POOL_WIDTH), F32),
        compiler_params=_cparams(("parallel", "parallel")),
        name="pool_mixer",
    )(u3, wbd, b.reshape(1, POOL_WIDTH), scale.reshape(1, POOL_WIDTH))


PAIRS_PER_GROUP = 6
N_CLASSES = N_GROUPS * PAIRS_PER_GROUP
ROUTE_W_LO, ROUTE_W_HI, ROUTE_CLS, ROUTE_RANK = 0, 1, 2, 3
TOKEN_ROWS = D_MODEL // LANES


def _store_token_major(ref, x, first_row, unit_rows):
    n = x.shape[0]
    for j in range(TOKEN_ROWS):
        ref[pl.ds(first_row + j, n, stride=unit_rows), :] = x[:, j * LANES:(j + 1) * LANES]


def _load_token_major(ref, n, first_row, unit_rows):
    return jnp.concatenate([ref[pl.ds(first_row + j, n, stride=unit_rows), :] for j in range(TOKEN_ROWS)],
                           axis=1)


def _class_experts(c):
    g, pidx = divmod(c, PAIRS_PER_GROUP)
    pairs = [(a, b) for a in range(EXPERTS_PER_GROUP) for b in range(a + 1, EXPERTS_PER_GROUP)]
    lo, hi = pairs[pidx]
    return g * EXPERTS_PER_GROUP + lo, g * EXPERTS_PER_GROUP + hi


def _outproj_kernel(h_ref, yh_ref, ya_ref, yp_ref, w_ref, g_ref, b_ref,
                    wgh_ref, wgl_ref, bg_ref, tri_ref, h1_ref, h1t_ref, route_ref, cnt_ref, carry_scr):
    @pl.when(pl.program_id(0) == 0)
    def _():
        carry_scr[...] = jnp.zeros_like(carry_scr)

    sub = tri_ref.shape[0]
    carry = carry_scr[...]
    for s in range(h_ref.shape[0] // sub):
        carry = _outproj_rows(slice(s * sub, (s + 1) * sub), s * sub * TOKEN_ROWS, carry,
                              h_ref, yh_ref, ya_ref, yp_ref, w_ref, g_ref, b_ref,
                              wgh_ref, wgl_ref, bg_ref, tri_ref, h1_ref, h1t_ref, route_ref)
    carry_scr[...] = carry
    cnt_ref[...] = carry


def _outproj_rows(rows, first_unit_row, carry, h_ref, yh_ref, ya_ref, yp_ref, w_ref,
                  g_ref, b_ref, wgh_ref, wgl_ref, bg_ref, tri_ref, h1_ref, h1t_ref, route_ref):
    d = functools.partial(jnp.dot, preferred_element_type=F32)
    o1, o2 = HY_WIDTH, HY_WIDTH + ATT_WIDTH
    mix = (d(yh_ref[rows, :].astype(BF16), w_ref[:o1, :]) + d(ya_ref[rows, :].astype(BF16), w_ref[o1:o2, :])
           + d(yp_ref[rows, :].astype(BF16), w_ref[o2:, :]))
    h1 = _ln_rows(DN_ALPHA * h_ref[rows, :] + mix, g_ref[...], b_ref[...])
    h1_ref[rows, :] = h1
    _store_token_major(h1t_ref, h1, first_unit_row, TOKEN_ROWS)

    logit = _dot3(*_split(h1), wgh_ref[...], wgl_ref[...]) + bg_ref[...]
    lni = lax.broadcasted_iota(jnp.int32, logit.shape, 1)
    ln = lni.astype(F32)
    grp = lax.shift_right_arithmetic(lni - GATE_FINE_LANE, 2).astype(F32)
    first = lambda mask: jnp.min(jnp.where(mask, ln, float(LANES)), axis=-1, keepdims=True)
    cmask = lni < N_GROUPS
    lc = jnp.where(cmask, logit, NEG_BIG)
    mc = jnp.max(lc, axis=-1, keepdims=True)
    gw = 1.0 / jnp.sum(jnp.where(cmask, jnp.exp(lc - mc), 0.0), axis=-1, keepdims=True)
    gi = first(cmask & (lc == mc))
    fmask = (lni >= GATE_FINE_LANE) & (lni < GATE_FINE_LANE + N_EXPERTS) & (grp == gi)
    lf = jnp.where(fmask, logit, NEG_BIG)
    m1 = jnp.max(lf, axis=-1, keepdims=True)
    i1 = first(fmask & (lf == m1))
    rest = fmask & (ln != i1)
    lf2 = jnp.where(rest, logit, NEG_BIG)
    m2 = jnp.max(lf2, axis=-1, keepdims=True)
    i2 = first(rest & (lf2 == m2))
    e2 = jnp.exp(m2 - m1)
    w1 = gw / (1.0 + e2)
    w2 = gw * e2 / (1.0 + e2)

    j1 = i1 - GATE_FINE_LANE - EXPERTS_PER_GROUP * gi
    j2 = i2 - GATE_FINE_LANE - EXPERTS_PER_GROUP * gi
    lo = jnp.minimum(j1, j2)
    hi = jnp.maximum(j1, j2)
    w_lo = jnp.where(j1 < j2, w1, w2)
    w_hi = jnp.where(j1 < j2, w2, w1)
    base = jnp.where(lo == 0.0, 0.0, jnp.where(lo == 1.0, 3.0, 5.0))
    cls = gi * PAIRS_PER_GROUP + base + hi - lo - 1.0
    onehot = ln == cls
    before = jnp.dot(tri_ref[...], onehot.astype(BF16), preferred_element_type=F32)
    rank = jnp.sum(jnp.where(onehot, before + carry, 0.0), axis=-1, keepdims=True)
    route_ref[rows, :] = jnp.where(lni == ROUTE_W_LO, w_lo,
                                   jnp.where(lni == ROUTE_W_HI, w_hi,
                                             jnp.where(lni == ROUTE_CLS, cls,
                                                       jnp.where(lni == ROUTE_RANK, rank, 0.0))))
    return carry + jnp.sum(onehot.astype(F32), axis=0, keepdims=True)


def _out_proj_ln_route(h2, yh, ya, yp, w_out_bf, layer, g, b, wgc, bgc, wgf, bgf, tm=512):
    T, D = h2.shape
    sub = tm
    tri = jnp.asarray(np.tril(np.ones((sub, sub), np.float32), -1)).astype(BF16)
    wg = jnp.zeros((D, LANES), F32).at[:, :N_GROUPS].set(wgc)
    wg = wg.at[:, GATE_FINE_LANE:GATE_FINE_LANE + N_EXPERTS].set(wgf)
    bg = jnp.zeros((1, LANES), F32).at[0, :N_GROUPS].set(bgc)
    bg = bg.at[0, GATE_FINE_LANE:GATE_FINE_LANE + N_EXPERTS].set(bgf)
    wgh, wgl = _split(wg)
    row = lambda c: pl.BlockSpec((tm, c), lambda i: (i, 0))
    full = lambda r, c: pl.BlockSpec((r, c), lambda i: (0, 0))
    return pl.pallas_call(
        _outproj_kernel,
        grid=(T // tm,),
        in_specs=[row(D), row(HY_WIDTH), row(ATT_WIDTH), row(POOL_WIDTH),
                  pl.BlockSpec((None, D, D), lambda i: (layer, 0, 0)),
                  full(1, D), full(1, D), full(D, LANES), full(D, LANES), full(1, LANES), full(sub, sub)],
        out_specs=[row(D), pl.BlockSpec((tm * TOKEN_ROWS, LANES), lambda i: (i, 0)), row(LANES), full(1, LANES)],
        out_shape=[jax.ShapeDtypeStruct((T, D), F32), jax.ShapeDtypeStruct((T * TOKEN_ROWS, LANES), F32),
                   jax.ShapeDtypeStruct((T, LANES), F32), jax.ShapeDtypeStruct((1, LANES), F32)],
        scratch_shapes=[pltpu.VMEM((1, LANES), F32)],
        compiler_params=_cparams(("arbitrary",)),
        name="out_proj_ln_route",
    )(h2, yh, ya, yp, w_out_bf, g.reshape(1, D), b.reshape(1, D), wgh, wgl, bg, tri)


MOE_TILE = 256
DMA_BATCH = 256


def _moe_rows_padded(T):
    return T + N_CLASSES * MOE_TILE


def _token_rows(ref, t, unit_rows):
    return ref.at[pl.ds(pl.multiple_of(t * unit_rows, unit_rows), unit_rows)]


DMA_GROUP = 8


def _start_tokens(n, slot_of, start_copy):
    def group(g, carry):
        first = g * DMA_GROUP
        slots = [slot_of(first + k) for k in range(DMA_GROUP)]
        for k in range(DMA_GROUP):
            start_copy(first + k, slots[k], k % 2)
        return carry

    lax.fori_loop(0, n // DMA_GROUP, group, 0)


def _wait_tokens(n, src_ref, dst_ref, unit_rows, sems):
    rows = pl.ds(0, (n // 2) * unit_rows)
    for sem in sems:
        pltpu.make_async_copy(src_ref.at[rows], dst_ref.at[rows], sem).wait()


def _dispatch_kernel(cls_ref, rank_ref, cnt_ref, h1t_ref, xs_ref, dest_ref, tlo_ref, thi_ref, tval_ref,
                     off_scr, zero_scr, sem, *, tm, n_tiles):
    i = pl.program_id(0)
    base = i * tm
    tile_rows = MOE_TILE * TOKEN_ROWS

    def fill_copy(tile):
        rows = pl.ds(pl.multiple_of(tile * tile_rows, tile_rows), tile_rows)
        return pltpu.make_async_copy(zero_scr, xs_ref.at[rows], sem.at[2])

    @pl.when(i == 0)
    def _():
        zero_scr[...] = jnp.zeros_like(zero_scr)
        off = jnp.int32(0)
        tile = jnp.int32(0)
        fills = []
        for c in range(N_CLASSES):
            n = cnt_ref[c]
            nt = lax.shift_right_logical(n + (MOE_TILE - 1), MOE_TILE.bit_length() - 1)
            off_scr[c] = off
            e_lo, e_hi = _class_experts(c)

            def mark(k, carry, tile=tile, e_lo=e_lo, e_hi=e_hi):
                tlo_ref[tile + k] = e_lo
                thi_ref[tile + k] = e_hi
                tval_ref[tile + k] = 1
                return carry

            lax.fori_loop(0, nt, mark, 0)
            fill = fill_copy(tile + nt - 1)
            fills.append((nt > 0, fill))

            @pl.when(nt > 0)
            def _(fill=fill):
                fill.start()

            off = off + nt * MOE_TILE
            tile = tile + nt

        def unused(k, carry):
            tlo_ref[k] = 0
            thi_ref[k] = 0
            tval_ref[k] = 0
            fill_copy(k).start()
            return carry

        lax.fori_loop(tile, n_tiles, unused, 0)
        for used, fill in fills:
            @pl.when(used)
            def _(fill=fill):
                fill.wait()

        def unused_wait(k, carry):
            fill_copy(k).wait()
            return carry

        lax.fori_loop(tile, n_tiles, unused_wait, 0)

    def slots_of_step(step):
        def slot(r, carry):
            t = step * tm + r
            dest_ref[t] = off_scr[cls_ref[t]] + rank_ref[t]
            return carry

        lax.fori_loop(0, tm, slot, 0, unroll=8)

    @pl.when(i == 0)
    def _():
        slots_of_step(0)

    def start_copy(r, d, prio):
        pltpu.make_async_copy(_token_rows(h1t_ref, r, TOKEN_ROWS), _token_rows(xs_ref, d, TOKEN_ROWS),
                              sem.at[prio]).start(priority=prio)

    _start_tokens(tm, lambda r: dest_ref[base + r], start_copy)

    @pl.when(i + 1 < pl.num_programs(0))
    def _():
        slots_of_step(i + 1)

    _wait_tokens(tm, h1t_ref, xs_ref, TOKEN_ROWS, [sem.at[0], sem.at[1]])


def _moe_dispatch(h1t, cls_i, rank_i, cnt_i, tm=1024):
    T = h1t.shape[0] // TOKEN_ROWS
    slots = _moe_rows_padded(T)
    n_tiles = slots // MOE_TILE
    smem = pl.BlockSpec(memory_space=pltpu.SMEM)
    return pl.pallas_call(
        functools.partial(_dispatch_kernel, tm=tm, n_tiles=n_tiles),
        grid_spec=pltpu.PrefetchScalarGridSpec(
            num_scalar_prefetch=3,
            grid=(T // tm,),
            in_specs=[pl.BlockSpec((tm * TOKEN_ROWS, LANES), lambda i, *_: (i, 0))],
            out_specs=[pl.BlockSpec(memory_space=pl.ANY), smem, smem, smem, smem],
            scratch_shapes=[pltpu.SMEM((N_CLASSES,), jnp.int32),
                            pltpu.VMEM((MOE_TILE * TOKEN_ROWS, LANES), F32),
                            pltpu.SemaphoreType.DMA((3,))],
        ),
        out_shape=[jax.ShapeDtypeStruct((slots * TOKEN_ROWS, LANES), F32), jax.ShapeDtypeStruct((T,), jnp.int32),
                   jax.ShapeDtypeStruct((n_tiles,), jnp.int32), jax.ShapeDtypeStruct((n_tiles,), jnp.int32),
                   jax.ShapeDtypeStruct((n_tiles,), jnp.int32)],
        compiler_params=_cparams(("arbitrary",)),
        name="moe_dispatch",
    )(cls_i, rank_i, cnt_i, h1t)


def _experts_kernel(tlo_ref, thi_ref, tval_ref, xs_ref, w1l_ref, w3l_ref, w2l_ref, w1h_ref, w3h_ref, w2h_ref,
                    ys_ref):
    j = pl.program_id(0)
    d = functools.partial(jnp.dot, preferred_element_type=F32)

    @pl.when(tval_ref[j] == 1)
    def _():
        xb = _load_token_major(xs_ref, MOE_TILE, 0, TOKEN_ROWS).astype(BF16)

        def expert(w1_ref, w3_ref, w2_ref):
            a = d(xb, w1_ref[...])
            c = d(xb, w3_ref[...])
            return d((a * jax.nn.sigmoid(a) * c).astype(BF16), w2_ref[...])

        _store_token_major(ys_ref.at[0], expert(w1l_ref, w3l_ref, w2l_ref), 0, TOKEN_ROWS)
        _store_token_major(ys_ref.at[1], expert(w1h_ref, w3h_ref, w2h_ref), 0, TOKEN_ROWS)

    @pl.when(tval_ref[j] == 0)
    def _():
        ys_ref[...] = jnp.zeros_like(ys_ref)


def _moe_experts(xs, tlo, thi, tval, w1_bf, w3_bf, w2_bf, layer):
    slots = xs.shape[0] // TOKEN_ROWS
    D = D_MODEL
    wspec = lambda shape, which: pl.BlockSpec(
        (None, None) + shape, (lambda j, tlo, thi, tval: (layer, tlo[j], 0, 0)) if which == 0
        else (lambda j, tlo, thi, tval: (layer, thi[j], 0, 0)))
    up = (D, D_EXPERT)
    down = (D_EXPERT, D)
    return pl.pallas_call(
        _experts_kernel,
        grid_spec=pltpu.PrefetchScalarGridSpec(
            num_scalar_prefetch=3,
            grid=(slots // MOE_TILE,),
            in_specs=[pl.BlockSpec((MOE_TILE * TOKEN_ROWS, LANES), lambda j, *_: (j, 0)),
                      wspec(up, 0), wspec(up, 0), wspec(down, 0), wspec(up, 1), wspec(up, 1), wspec(down, 1)],
            out_specs=pl.BlockSpec((None, 2, MOE_TILE * TOKEN_ROWS, LANES), lambda j, *_: (j, 0, 0, 0)),
        ),
        out_shape=jax.ShapeDtypeStruct((slots // MOE_TILE, 2, MOE_TILE * TOKEN_ROWS, LANES), F32),
        compiler_params=_cparams(("arbitrary",)),
        name="moe_experts",
    )(tlo, thi, tval, xs, w1_bf, w3_bf, w2_bf, w1_bf, w3_bf, w2_bf)


def _combine_kernel(dest_ref, h1_ref, route_ref, ys_ref, p_ref, pwg_ref, pbg_ref, pwp_ref, g_ref, b_ref,
                    o_ref, y_scr, ym_scr, sem, *, tm):
    i = pl.program_id(0)
    d = functools.partial(jnp.dot, preferred_element_type=F32)
    tile_shift = MOE_TILE.bit_length() - 1

    def unit_rows(t):
        return pl.ds(pl.multiple_of(t * TOKEN_ROWS, TOKEN_ROWS), TOKEN_ROWS)

    def gather(step, buf):
        def start_copy(r, slot, prio):
            tile = lax.shift_right_logical(slot, tile_shift)
            src = ys_ref.at[tile, :, unit_rows(slot & (MOE_TILE - 1)), :]
            pltpu.make_async_copy(src, y_scr.at[buf, :, unit_rows(r), :],
                                  sem.at[2 * buf + prio]).start(priority=prio)
        _start_tokens(tm, lambda r: dest_ref[step * tm + r], start_copy)

    def wait_gathers(buf):
        rows = pl.ds(0, (tm // 2) * TOKEN_ROWS)
        for prio in range(2):
            pltpu.make_async_copy(ys_ref.at[0, :, rows, :], y_scr.at[buf, :, rows, :],
                                  sem.at[2 * buf + prio]).wait()

    @pl.when(i == 0)
    def _():
        gather(0, 0)

    for buf in range(2):
        @pl.when((i + 1 < pl.num_programs(0)) & ((i + 1) % 2 == buf))
        def _(buf=buf):
            gather(i + 1, buf)

    sub = tm // 2
    halves = [slice(s * sub, (s + 1) * sub) for s in range(2)]
    y_ple = []
    for rows in halves:
        z = d(h1_ref[rows, :].astype(BF16), pwg_ref[...]) + pbg_ref[...]
        y_ple.append(jax.nn.sigmoid(z) * d(p_ref[rows, :].astype(BF16), pwp_ref[...]))
    for buf in range(2):
        @pl.when(i % 2 == buf)
        def _(buf=buf):
            wait_gathers(buf)
            for s, rows in enumerate(halves):
                rec = route_ref[rows, :]
                first = s * sub * TOKEN_ROWS
                ym_scr[rows, :] = (rec[:, ROUTE_W_LO:ROUTE_W_LO + 1]
                                   * _load_token_major(y_scr.at[buf, 0], sub, first, TOKEN_ROWS)
                                   + rec[:, ROUTE_W_HI:ROUTE_W_HI + 1]
                                   * _load_token_major(y_scr.at[buf, 1], sub, first, TOKEN_ROWS))
    for s, rows in enumerate(halves):
        r_ = DN_ALPHA * h1_ref[rows, :] + ym_scr[rows, :] + y_ple[s]
        o_ref[rows, :] = _ln_rows(r_, g_ref[...], b_ref[...])


def _moe_combine_ple_ln(h1, route, ys, dest, p3, pwg_bf, pbg, pwp_bf, layer, g, b, tm=512):
    T, D = h1.shape
    row = lambda c: pl.BlockSpec((tm, c), lambda i, *_: (i, 0))
    full = lambda r, c: pl.BlockSpec((r, c), lambda i, *_: (0, 0))
    lay = lambda r, c: pl.BlockSpec((None, r, c), lambda i, *_: (layer, 0, 0))
    return pl.pallas_call(
        functools.partial(_combine_kernel, tm=tm),
        grid_spec=pltpu.PrefetchScalarGridSpec(
            num_scalar_prefetch=1,
            grid=(T // tm,),
            in_specs=[row(D), row(LANES), pl.BlockSpec(memory_space=pl.ANY),
                      pl.BlockSpec((None, tm, PLE_DIM), lambda i, *_: (layer, i, 0)),
                      lay(D, D), full(1, D), lay(PLE_DIM, D), full(1, D), full(1, D)],
            out_specs=row(D),
            scratch_shapes=[pltpu.VMEM((2, 2, tm * TOKEN_ROWS, LANES), F32), pltpu.VMEM((tm, D), F32),
                            pltpu.SemaphoreType.DMA((4,))],
        ),
        out_shape=jax.ShapeDtypeStruct((T, D), F32),
        compiler_params=_cparams(("arbitrary",)),
        name="moe_combine_ple_ln",
    )(dest, h1, route, ys, p3, pwg_bf, pbg.reshape(1, D), pwp_bf, g.reshape(1, D), b.reshape(1, D))


def _hyena_mixer(u3, kr4, ki4, layer, consts, conv_w, conv_b, bias_d, n1_total):
    B, S, _ = u3.shape
    vv, x0c = _hy_gate(u3, conv_w, conv_b)
    y = _hy_conv(vv, x0c, kr4, ki4, layer, consts, bias_d, n1_total)
    return y.reshape(B * S, HY_WIDTH)


def kernel(x, p, ln0_g, ln0_b, w_in, hy_conv_w, hy_conv_b, hy_fw1, hy_fb1, hy_freq1, hy_fw2, hy_fb2, hy_freq2, hy_fw3, hy_bias, att_lq1, att_lk1, att_lq2, att_lk2, att_subln_g, pool_w, pool_b, pool_scale, w_out, ln1_g, ln1_b, moe_wgc, moe_bgc, moe_wgf, moe_bgf, moe_w1, moe_w3, moe_w2, ple_wg, ple_bg, ple_wp, ln2_g, ln2_b):
    B, S, D = x.shape
    L = w_in.shape[0]
    T = B * S
    n1_total = 2 * S // FFT_N2
    C = HY_WIDTH

    kfilt, ssq = _hy_filter(S, hy_fw1, hy_fb1, hy_freq1, hy_fw2, hy_fb2, hy_freq2, hy_fw3)
    consts = _stage_b_consts(n1_total)
    kr_all, ki_all = _hy_spectrum(kfilt, ssq, consts, n1_total)
    tables = _rope_tables(S)

    w_in_bf, w_out_bf = w_in.astype(BF16), w_out.astype(BF16)
    w1_bf, w3_bf, w2_bf = moe_w1.astype(BF16), moe_w3.astype(BF16), moe_w2.astype(BF16)
    pwg_bf, pwp_bf = ple_wg.astype(BF16), ple_wp.astype(BF16)
    p3 = p.reshape(L, T, PLE_DIM)

    h = _layer_norm(x.reshape(T, D), ln0_g, ln0_b)
    for i in range(L):
        lam_init = 0.8 - 0.6 * math.exp(-0.3 * i)
        uh, qkv, up = _in_proj(h, w_in_bf, i, tables, S)
        y_hy = _hyena_mixer(uh.reshape(B, S, 3 * C), kr_all, ki_all, i, consts,
                            hy_conv_w[i], hy_conv_b[i], hy_bias[i], n1_total)
        y_att = _diff_attention(qkv.reshape(B, S, 3 * ATT_WIDTH), att_lq1[i], att_lk1[i], att_lq2[i],
                                att_lk2[i], att_subln_g[i], lam_init).reshape(T, ATT_WIDTH)
        y_pool = _pool_mixer(up.reshape(B, S, POOL_WIDTH), pool_w[i], pool_b[i],
                             pool_scale[i]).reshape(T, POOL_WIDTH)
        h1, h1t, route, counts = _out_proj_ln_route(h, y_hy, y_att, y_pool, w_out_bf, i, ln1_g[i],
                                                    ln1_b[i], moe_wgc[i], moe_bgc[i], moe_wgf[i], moe_bgf[i])
        cls_i = route[:, ROUTE_CLS].astype(jnp.int32)
        rank_i = route[:, ROUTE_RANK].astype(jnp.int32)
        xs, dest, tlo, thi, tval = _moe_dispatch(h1t, cls_i, rank_i, counts[0].astype(jnp.int32))
        ys = _moe_experts(xs, tlo, thi, tval, w1_bf, w3_bf, w2_bf, i)
        h = _moe_combine_ple_ln(h1, route, ys, dest, p3, pwg_bf, ple_bg[i], pwp_bf, i, ln2_g[i], ln2_b[i])
    return h.reshape(B, S, D)
```

```python
import functools
import math

import numpy as np
import jax
import jax.numpy as jnp
from jax import lax
from jax.experimental import pallas as pl
from jax.experimental.pallas import tpu as pltpu

F32 = jnp.float32
BF16 = jnp.bfloat16

D_MODEL = 1024
DEPTH = 4
HY_WIDTH = 256
ATT_WIDTH = 512
ATT_HEADS = 4
ATT_HEAD_DIM = 64
POOL_WINDOWS = (2, 4, 8, 16)
POOL_WIDTH = 256
POOL_GROUP = 64
IN_WIDTH = 3 * HY_WIDTH + 3 * ATT_WIDTH + POOL_WIDTH
ROPE_THETA = 500000.0
ROPE_DIM = ATT_HEAD_DIM // 4
HY_EMB = 33
HY_BANDS = (HY_EMB - 1) // 2
HY_FILTER_HIDDEN = 64
HY_DECAY_TARGET = 1e-2
HY_FAST_DECAY = 0.3
HY_SLOW_DECAY = 1.5
N_GROUPS = 4
EXPERTS_PER_GROUP = 4
N_EXPERTS = 16
D_EXPERT = 256
PLE_DIM = 256
LN_EPS = 1e-5
RMS_EPS = 1e-5
DN_ALPHA = (2 * DEPTH) ** 0.25

LANES = 128
FFT_N2 = 128
GATE_COARSE_LANE = 0
GATE_FINE_LANE = N_GROUPS
NEG_BIG = -1e30
LOG2E = 1.4426950408889634
VMEM_LIMIT = 56 * 1024 * 1024


def _cparams(sem):
    return pltpu.CompilerParams(dimension_semantics=sem, vmem_limit_bytes=VMEM_LIMIT)


def _split(x):
    hi = x.astype(BF16)
    lo = (x - hi.astype(F32)).astype(BF16)
    return hi, lo


def _dot3(ah, al, bh, bl):
    d = functools.partial(jnp.dot, preferred_element_type=F32)
    return d(ah, bh) + (d(ah, bl) + d(al, bh))


def _ln_rows(x, g, b):
    mu = jnp.mean(x, axis=-1, keepdims=True)
    xc = x - mu
    var = jnp.mean(xc * xc, axis=-1, keepdims=True)
    return xc * lax.rsqrt(var + LN_EPS) * g + b


def _ln_kernel(x_ref, g_ref, b_ref, o_ref):
    o_ref[...] = _ln_rows(x_ref[...], g_ref[...], b_ref[...])


def _layer_norm(x2, g, b, tm=512):
    T, D = x2.shape
    return pl.pallas_call(
        _ln_kernel,
        grid=(T // tm,),
        in_specs=[pl.BlockSpec((tm, D), lambda i: (i, 0)),
                  pl.BlockSpec((1, D), lambda i: (0, 0)),
                  pl.BlockSpec((1, D), lambda i: (0, 0))],
        out_specs=pl.BlockSpec((tm, D), lambda i: (i, 0)),
        out_shape=jax.ShapeDtypeStruct((T, D), F32),
        compiler_params=_cparams(("parallel",)),
        name="ln0",
    )(x2, g.reshape(1, D), b.reshape(1, D))


def _rope(x, cos_f, sin_a, sin_b):
    half = ROPE_DIM // 2
    return x * cos_f + pltpu.roll(x, LANES - half, 1) * sin_a + pltpu.roll(x, half, 1) * sin_b


def _inproj_kernel(h_ref, w_ref, cos_ref, sa_ref, sb_ref, uh_ref, qkv_ref, up_ref):
    hb = h_ref[...].astype(BF16)
    mm = lambda c0, c1: jnp.dot(hb, w_ref[:, c0:c1], preferred_element_type=F32)
    o_q = 3 * HY_WIDTH
    o_k = o_q + ATT_WIDTH
    o_v = o_k + ATT_WIDTH
    o_p = o_v + ATT_WIDTH
    uh_ref[...] = mm(0, o_q)
    up_ref[...] = mm(o_p, IN_WIDTH)
    cos_f, sin_a, sin_b = cos_ref[...], sa_ref[...], sb_ref[...]
    q = mm(o_q, o_k)
    k = mm(o_k, o_v)
    qscale = ATT_HEAD_DIM ** -0.5 * LOG2E
    for hd in range(ATT_HEADS):
        sl = slice(hd * LANES, (hd + 1) * LANES)
        qkv_ref[:, sl] = (_rope(q[:, sl], cos_f, sin_a, sin_b) * qscale).astype(BF16)
        qkv_ref[:, ATT_WIDTH + hd * LANES:ATT_WIDTH + (hd + 1) * LANES] = \
            _rope(k[:, sl], cos_f, sin_a, sin_b).astype(BF16)
    qkv_ref[:, 2 * ATT_WIDTH:] = mm(o_v, o_p).astype(BF16)


def _in_proj(h2, w_bf, layer, tables, seq, tm=512):
    T, D = h2.shape
    N = w_bf.shape[2]
    spt = seq // tm
    tspec = pl.BlockSpec((tm, LANES), lambda i: (i % spt, 0))
    row = lambda c: pl.BlockSpec((tm, c), lambda i: (i, 0))
    return pl.pallas_call(
        _inproj_kernel,
        grid=(T // tm,),
        in_specs=[row(D), pl.BlockSpec((None, D, N), lambda i: (layer, 0, 0)), tspec, tspec, tspec],
        out_specs=[row(3 * HY_WIDTH), row(3 * ATT_WIDTH), row(POOL_WIDTH)],
        out_shape=[jax.ShapeDtypeStruct((T, 3 * HY_WIDTH), F32),
                   jax.ShapeDtypeStruct((T, 3 * ATT_WIDTH), BF16),
                   jax.ShapeDtypeStruct((T, POOL_WIDTH), F32)],
        compiler_params=_cparams(("parallel",)),
        name="in_proj",
    )(h2, w_bf, *tables)


def _hy_filter_kernel(band_ref, phase_ref, fw1_ref, fb1_ref, fr1_ref, fw2_ref, fb2_ref, fr2_ref, fw3_ref,
                      dl_ref, k_ref, ssq_ref, z_scr, *, seq, tr):
    j = pl.program_id(0)
    l = pl.program_id(1)
    n = j * tr + lax.broadcasted_iota(jnp.int32, (tr, 1), 0)
    pos = jnp.where(n < seq, n, 2 * seq - n).astype(F32)
    t = pos * (1.0 / (seq - 1))

    @pl.when(l == 0)
    def _():
        wpos = (2.0 * math.pi / seq) * pos
        lane = lax.broadcasted_iota(jnp.int32, (tr, LANES), 1)
        z_scr[...] = jnp.where(lane == 0, t,
                               jnp.where(lane < HY_EMB, jnp.sin(wpos * band_ref[...] + phase_ref[...]), 0.0))

    @pl.when((l == 0) & (j == 0))
    def _():
        ssq_ref[...] = jnp.zeros_like(ssq_ref)

    z = z_scr[...]
    h1 = _dot3(*_split(z), *_split(fw1_ref[...])) + fb1_ref[...]
    h1 = jnp.sin(fr1_ref[...] * h1)
    h2 = _dot3(*_split(h1), *_split(fw2_ref[...])) + fb2_ref[...]
    h2 = jnp.sin(fr2_ref[...] * h2)
    filt = _dot3(*_split(h2), *_split(fw3_ref[...]))
    window = jnp.exp(-t * jnp.abs(dl_ref[...]))
    kk = jnp.where(n < seq, filt[:, :HY_WIDTH], filt[:, HY_WIDTH:]) * window
    kk = jnp.where(n == seq, 0.0, kk)
    k_ref[...] = kk
    ssq_ref[l] += jnp.sum(kk * kk, axis=0, keepdims=True)


def _hy_filter(seq, fw1, fb1, freq1, fw2, fb2, freq2, fw3, tr=1024):
    L = fw1.shape[0]
    n = 2 * seq
    H = HY_FILTER_HIDDEN
    bands = np.linspace(1e-4, HY_BANDS - 1, HY_BANDS)
    bandv = np.zeros((1, LANES), np.float32)
    bandv[0, 1:1 + HY_BANDS] = bands
    bandv[0, 1 + HY_BANDS:HY_EMB] = bands
    phase = np.zeros((1, LANES), np.float32)
    phase[0, 1:1 + HY_BANDS] = 0.5 * math.pi
    phase[0, 1 + HY_BANDS:HY_EMB] = math.pi
    fw1p = jnp.zeros((L, LANES, H), F32).at[:, :HY_EMB].set(fw1)
    max_decay = math.log(HY_DECAY_TARGET) / HY_FAST_DECAY
    min_decay = math.log(HY_DECAY_TARGET) / HY_SLOW_DECAY
    deltas = np.linspace(min_decay, max_decay, HY_WIDTH).astype(np.float32).reshape(1, HY_WIDTH)
    vec = lambda a: a.reshape(L, 1, a.shape[-1])
    lspec = lambda r, c: pl.BlockSpec((None, r, c), lambda j, l: (l, 0, 0))
    lanes = pl.BlockSpec((1, LANES), lambda j, l: (0, 0))
    return pl.pallas_call(
        functools.partial(_hy_filter_kernel, seq=seq, tr=tr),
        grid=(n // tr, L),
        in_specs=[lanes, lanes,
                  lspec(LANES, H), lspec(1, H), lspec(1, H),
                  lspec(H, H), lspec(1, H), lspec(1, H),
                  lspec(H, 2 * HY_WIDTH),
                  pl.BlockSpec((1, HY_WIDTH), lambda j, l: (0, 0))],
        out_specs=[pl.BlockSpec((None, tr, HY_WIDTH), lambda j, l: (l, j, 0)),
                   pl.BlockSpec((L, 1, HY_WIDTH), lambda j, l: (0, 0, 0))],
        out_shape=[jax.ShapeDtypeStruct((L, n, HY_WIDTH), F32),
                   jax.ShapeDtypeStruct((L, 1, HY_WIDTH), F32)],
        scratch_shapes=[pltpu.VMEM((tr, LANES), F32)],
        compiler_params=_cparams(("arbitrary", "arbitrary")),
        name="hy_filter",
    )(jnp.asarray(bandv), jnp.asarray(phase), fw1p, vec(fb1), vec(freq1), fw2, vec(fb2), vec(freq2), fw3,
      jnp.asarray(deltas))


def _k1_rows(n1):
    k1 = n1 // 2 + 1
    return k1, -(-k1 // 8) * 8


def _bf_pair(m):
    m32 = jnp.asarray(np.asarray(m, np.float32))
    return _split(m32)


SUBLANES = 8


def _stage_a_matrix(n1_used, n1_total):
    k1n, k1p = _k1_rows(n1_total)
    k1 = np.arange(k1p)
    valid = (k1 < k1n)[:, None]
    ang = 2.0 * np.pi * np.outer(k1, np.arange(n1_used)) / n1_total
    eye = np.eye(SUBLANES)
    return np.concatenate([np.kron(np.cos(ang) * valid, eye), np.kron(-np.sin(ang) * valid, eye)], axis=0)


def _stage_a_rows(x_ref, a_scr, kah_ref, kal_ref, n1_used, k1p):
    def stage_a(g, carry):
        xg = jnp.concatenate(
            [x_ref[pl.ds(pl.multiple_of(n1 * FFT_N2 + g * SUBLANES, SUBLANES), SUBLANES), :]
             for n1 in range(n1_used)], axis=0)
        ag = _dot3(kah_ref[...], kal_ref[...], *_split(xg))
        gs = pl.ds(pl.multiple_of(g * SUBLANES, SUBLANES), SUBLANES)
        for part in range(2):
            for k1 in range(k1p):
                r = (part * k1p + k1) * SUBLANES
                a_scr[part, k1, gs, :] = ag[r:r + SUBLANES]
        return carry

    lax.fori_loop(0, FFT_N2 // SUBLANES, stage_a, 0, unroll=2)


def _stage_b_consts(n1_total):
    n = n1_total * FFT_N2
    k1n, k1p = _k1_rows(n1_total)
    ang2 = 2.0 * np.pi * np.outer(np.arange(FFT_N2), np.arange(FFT_N2)) / FFT_N2
    c2, s2 = np.cos(ang2), np.sin(ang2)
    mf = np.block([[c2, s2], [-s2, c2]])
    mi = np.block([[c2, -s2], [s2, c2]])
    angt = 2.0 * np.pi * np.outer(np.arange(k1p), np.arange(FFT_N2)) / n
    twc = jnp.asarray(np.cos(angt).astype(np.float32)).reshape(k1p, FFT_N2, 1)
    tws = jnp.asarray(np.sin(angt).astype(np.float32)).reshape(k1p, FFT_N2, 1)
    return _bf_pair(mf), _bf_pair(mi), twc, tws


def _fwd_b(ar, ai, c, s, mfh, mfl):
    tr_ = ar * c + ai * s
    ti_ = ai * c - ar * s
    xh, xl = _split(jnp.concatenate([tr_, ti_], axis=0))
    z = _dot3(mfh, mfl, xh, xl)
    return z[:FFT_N2], z[FFT_N2:]


def _hy_spectrum_kernel(k_ref, ssq_ref, tw_ref, kah_ref, kal_ref, mfh_ref, mfl_ref, kr_ref, ki_ref, a_scr,
                        *, n1_total, k1n, k1p):
    _stage_a_rows(k_ref, a_scr, kah_ref, kal_ref, n1_total, k1p)
    scale = lax.rsqrt(ssq_ref[...] + 1e-6) * (1.0 / (n1_total * FFT_N2))

    def stage_b(k1, carry):
        tw = tw_ref[k1]
        zr, zi = _fwd_b(a_scr[0, k1], a_scr[1, k1], tw[:, 0:1], tw[:, 1:2], mfh_ref[...], mfl_ref[...])
        kr_ref[k1] = zr * scale
        ki_ref[k1] = zi * scale
        return carry

    lax.fori_loop(0, k1n, stage_b, 0, unroll=3 if k1n % 3 == 0 else 1)
    for k1 in range(k1n, k1p):
        kr_ref[k1] = jnp.zeros(kr_ref.shape[1:], F32)
        ki_ref[k1] = jnp.zeros(ki_ref.shape[1:], F32)


def _hy_spectrum(kfilt, ssq, consts, n1_total):
    L, n, C = kfilt.shape
    k1n, k1p = _k1_rows(n1_total)
    (mfh, mfl), _, twc, tws = consts
    tw = jnp.concatenate([twc, tws], axis=-1)
    kah, kal = _bf_pair(_stage_a_matrix(n1_total, n1_total))
    full = lambda a: pl.BlockSpec(a.shape, lambda l, c: (0,) * a.ndim)
    ospec = pl.BlockSpec((None, k1p, FFT_N2, LANES), lambda l, c: (l, 0, 0, c))
    return pl.pallas_call(
        functools.partial(_hy_spectrum_kernel, n1_total=n1_total, k1n=k1n, k1p=k1p),
        grid=(L, C // LANES),
        in_specs=[pl.BlockSpec((None, n, LANES), lambda l, c: (l, 0, c)),
                  pl.BlockSpec((None, 1, LANES), lambda l, c: (l, 0, c)),
                  full(tw), full(kah), full(kal), full(mfh), full(mfl)],
        out_specs=[ospec, ospec],
        out_shape=[jax.ShapeDtypeStruct((L, k1p, FFT_N2, C), F32)] * 2,
        scratch_shapes=[pltpu.VMEM((2, k1p, FFT_N2, LANES), F32)],
        compiler_params=_cparams(("parallel", "parallel")),
        name="hy_spectrum",
    )(kfilt, ssq, tw, kah, kal, mfh, mfl)


def _hy_conv_kernel(vv_ref, x0_ref, kr_ref, ki_ref, tw_ref, kah_ref, kal_ref, kch_ref, kcl_ref,
                    mfh_ref, mfl_ref, mih_ref, mil_ref, bias_ref, o_ref, a_scr, *, n1u, k1n, k1p):
    groups = FFT_N2 // SUBLANES

    def rows(n1, g):
        return pl.ds(pl.multiple_of(n1 * FFT_N2 + g * SUBLANES, SUBLANES), SUBLANES)

    _stage_a_rows(vv_ref, a_scr, kah_ref, kal_ref, n1u, k1p)

    def stage_b(k1, carry):
        tw = tw_ref[k1]
        c = tw[:, 0:1]
        s = tw[:, 1:2]
        zr, zi = _fwd_b(a_scr[0, k1], a_scr[1, k1], c, s, mfh_ref[...], mfl_ref[...])
        kr = kr_ref[k1]
        ki = ki_ref[k1]
        pr = zr * kr - zi * ki
        pi = zr * ki + zi * kr
        y = _dot3(mih_ref[...], mil_ref[...], *_split(jnp.concatenate([pr, pi], axis=0)))
        yr = y[:FFT_N2]
        yi = y[FFT_N2:]
        a_scr[0, k1] = yr * c - yi * s
        a_scr[1, k1] = yi * c + yr * s
        return carry

    lax.fori_loop(0, k1n, stage_b, 0, unroll=11 if k1n % 11 == 0 else 1)

    def stage_c(g, carry):
        gs = pl.ds(pl.multiple_of(g * SUBLANES, SUBLANES), SUBLANES)
        bg = jnp.concatenate([a_scr[part, k1, gs, :] for part in range(2) for k1 in range(k1p)], axis=0)
        yg = _dot3(kch_ref[...], kcl_ref[...], *_split(bg))
        for n1 in range(n1u):
            r = rows(n1, g)
            o_ref[r, :] = (yg[n1 * SUBLANES:(n1 + 1) * SUBLANES] + vv_ref[r, :] * bias_ref[...]) * x0_ref[r, :]
        return carry

    lax.fori_loop(0, groups, stage_c, 0, unroll=2)


def _hy_conv(vv, x0c, kr4, ki4, layer, consts, bias_d, n1_total):
    B, S, C = vv.shape
    n1u = S // FFT_N2
    k1n, k1p = _k1_rows(n1_total)
    (mfh, mfl), (mih, mil), twc, tws = consts
    tw = jnp.concatenate([twc, tws], axis=-1)
    k1 = np.arange(k1p)
    eye = np.eye(SUBLANES)
    ka = _stage_a_matrix(n1u, n1_total)
    w = np.where((k1 == 0) | (k1 == n1_total // 2), 1.0, 2.0) * (k1 < k1n)
    ang_c = 2.0 * np.pi * np.outer(np.arange(n1u), k1) / n1_total
    kc = np.concatenate([np.kron(np.cos(ang_c) * w[None, :], eye), np.kron(-np.sin(ang_c) * w[None, :], eye)], axis=1)
    kah, kal = _bf_pair(ka)
    kch, kcl = _bf_pair(kc)
    nct = C // LANES
    xspec = pl.BlockSpec((None, S, LANES), lambda c, b: (b, 0, c))
    kspec = pl.BlockSpec((None, k1p, FFT_N2, LANES), lambda c, b: (layer, 0, 0, c))
    full = lambda a: pl.BlockSpec(a.shape, lambda c, b: (0,) * a.ndim)
    return pl.pallas_call(
        functools.partial(_hy_conv_kernel, n1u=n1u, k1n=k1n, k1p=k1p),
        grid=(nct, B),
        in_specs=[xspec, xspec, kspec, kspec, full(tw), full(kah), full(kal), full(kch), full(kcl),
                  full(mfh), full(mfl), full(mih), full(mil),
                  pl.BlockSpec((1, LANES), lambda c, b: (0, c))],
        out_specs=xspec,
        out_shape=jax.ShapeDtypeStruct((B, S, C), F32),
        scratch_shapes=[pltpu.VMEM((2, k1p, FFT_N2, LANES), F32)],
        compiler_params=_cparams(("parallel", "parallel")),
        name="hy_conv",
    )(vv, x0c, kr4, ki4, tw, kah, kal, kch, kcl, mfh, mfl, mih, mil, bias_d.reshape(1, C))


def _conv3(x, w_ref, b_ref):
    S = x.shape[0]
    row = lax.broadcasted_iota(jnp.int32, x.shape, 0)
    prev = jnp.where(row == 0, 0.0, pltpu.roll(x, 1, 0))
    nxt = jnp.where(row == S - 1, 0.0, pltpu.roll(x, S - 1, 0))
    return prev * w_ref[0:1, :] + x * w_ref[1:2, :] + nxt * w_ref[2:3, :] + b_ref[...]


def _hy_gate_kernel(x0_ref, x1_ref, v_ref, w0_ref, w1_ref, w2_ref, b0_ref, b1_ref, b2_ref,
                    vv_ref, x0c_ref):
    x0c_ref[...] = _conv3(x0_ref[...], w0_ref, b0_ref)
    vv_ref[...] = _conv3(v_ref[...], w2_ref, b2_ref) * _conv3(x1_ref[...], w1_ref, b1_ref)


def _hy_gate(u3, conv_w, conv_b):
    B, S, _ = u3.shape
    nct = HY_WIDTH // LANES
    conv_b2 = conv_b.reshape(1, 3 * HY_WIDTH)
    uspec = lambda part: pl.BlockSpec((None, S, LANES), lambda b, c: (b, 0, part * nct + c))
    wspec = lambda part: pl.BlockSpec((3, LANES), lambda b, c: (0, part * nct + c))
    bspec = lambda part: pl.BlockSpec((1, LANES), lambda b, c: (0, part * nct + c))
    ospec = pl.BlockSpec((None, S, LANES), lambda b, c: (b, 0, c))
    return pl.pallas_call(
        _hy_gate_kernel,
        grid=(B, nct),
        in_specs=[uspec(0), uspec(1), uspec(2), wspec(0), wspec(1), wspec(2),
                  bspec(0), bspec(1), bspec(2)],
        out_specs=[ospec, ospec],
        out_shape=[jax.ShapeDtypeStruct((B, S, HY_WIDTH), F32)] * 2,
        compiler_params=_cparams(("parallel", "parallel")),
        name="hy_gate",
    )(u3, u3, u3, conv_w, conv_w, conv_w, conv_b2, conv_b2, conv_b2)


def _attn_kernel(q_ref, k_ref, v_ref, lq1_ref, lk1_ref, lq2_ref, lk2_ref, g_ref, o_ref,
                 s_scr, p_scr, vx_scr, kt_scr, *, lam_init, tq):
    S = q_ref.shape[0]
    nblk = S // tq
    lam = (jnp.exp(jnp.sum(lq1_ref[...] * lk1_ref[...], keepdims=True))
           - jnp.exp(jnp.sum(lq2_ref[...] * lk2_ref[...], keepdims=True)) + lam_init)

    def blk(i):
        return pl.ds(pl.multiple_of(i * tq, tq), tq)

    kt = k_ref[...].T
    feat = lax.broadcasted_iota(jnp.int32, kt.shape, 0)
    kt_scr[0] = jnp.where(feat < ATT_HEAD_DIM, kt, jnp.zeros_like(kt))
    kt_scr[1] = jnp.where(feat >= ATT_HEAD_DIM, kt, jnp.zeros_like(kt))

    def scores(i, slot):
        q = q_ref[blk(i), :]
        for c in range(2):
            s_scr[slot, c] = jnp.dot(q, kt_scr[c], preferred_element_type=F32)

    vx_scr[:, :LANES] = v_ref[...]
    vx_scr[:, LANES:] = (lax.broadcasted_iota(jnp.int32, (S, LANES), 1) == 0).astype(BF16)

    def softmax(slot):
        for c in range(2):
            s = s_scr[slot, c]
            m = jnp.max(s, axis=-1, keepdims=True)
            p_scr[slot, c] = jnp.exp2(s - m).astype(BF16)

    def values(i, slot):
        oe1 = jnp.dot(p_scr[slot, 0], vx_scr[...], preferred_element_type=F32)
        oe2 = jnp.dot(p_scr[slot, 1], vx_scr[...], preferred_element_type=F32)
        o = oe1[:, :LANES] / oe1[:, LANES:LANES + 1] - oe2[:, :LANES] * (lam / oe2[:, LANES:LANES + 1])
        o = o * lax.rsqrt(jnp.mean(o * o, axis=-1, keepdims=True) + RMS_EPS) * g_ref[...]
        o_ref[blk(i), :] = o * (1.0 - lam_init)

    scores(0, 0)
    softmax(0)
    scores(1, 1)

    def pair(j, carry):
        t = 2 * j
        values(t - 2, 0)
        softmax(1)
        scores(t, 0)
        values(t - 1, 1)
        softmax(0)
        scores(t + 1, 1)
        return carry

    lax.fori_loop(1, nblk // 2, pair, 0)
    values(nblk - 2, 0)
    softmax(1)
    values(nblk - 1, 1)


def _rope_tables(seq):
    pos = np.arange(seq, dtype=np.float64)
    inv_freq = np.power(ROPE_THETA, -np.arange(0, ROPE_DIM, 2, dtype=np.float64) / ROPE_DIM)
    ang = pos[:, None] * inv_freq[None, :]
    half = ROPE_DIM // 2
    cos_f = np.ones((seq, LANES), np.float32)
    sin_a = np.zeros((seq, LANES), np.float32)
    sin_b = np.zeros((seq, LANES), np.float32)
    for base in range(0, LANES, ATT_HEAD_DIM):
        cos_f[:, base:base + half] = np.cos(ang)
        cos_f[:, base + half:base + ROPE_DIM] = np.cos(ang)
        sin_a[:, base:base + half] = -np.sin(ang)
        sin_b[:, base + half:base + ROPE_DIM] = np.sin(ang)
    return jnp.asarray(cos_f), jnp.asarray(sin_a), jnp.asarray(sin_b)


def _diff_attention(qkv3, lq1, lk1, lq2, lk2, subln_g, lam_init, tq=256):
    B, S, _ = qkv3.shape
    nb = ATT_WIDTH // LANES
    assert S % (2 * tq) == 0
    spec = lambda part: pl.BlockSpec((None, S, LANES), lambda b, h: (b, 0, part * nb + h))
    vec = pl.BlockSpec((1, ATT_HEAD_DIM), lambda b, h: (0, 0))
    r1 = lambda a: a.reshape(1, -1)
    return pl.pallas_call(
        functools.partial(_attn_kernel, lam_init=lam_init, tq=tq),
        grid=(B, ATT_HEADS),
        in_specs=[spec(0), spec(1), spec(2), vec, vec, vec, vec,
                  pl.BlockSpec((1, LANES), lambda b, h: (0, 0))],
        out_specs=pl.BlockSpec((None, S, LANES), lambda b, h: (b, 0, h)),
        out_shape=jax.ShapeDtypeStruct((B, S, ATT_WIDTH), F32),
        scratch_shapes=[pltpu.VMEM((2, 2, tq, S), F32), pltpu.VMEM((2, 2, tq, S), BF16),
                        pltpu.VMEM((S, 2 * LANES), BF16), pltpu.VMEM((2, LANES, S), BF16)],
        compiler_params=_cparams(("parallel", "parallel")),
        name="diff_attn",
    )(qkv3, qkv3, qkv3, r1(lq1), r1(lk1), r1(lq2), r1(lk2), r1(subln_g))


def _pool_kernel(u_ref, w_ref, b_ref, sc_ref, o_ref):
    ct = pl.program_id(1)
    x = u_ref[...]
    S = x.shape[0]
    row = lax.broadcasted_iota(jnp.int32, x.shape, 0)
    lane = lax.broadcasted_iota(jnp.int32, x.shape, 1)

    def shifted(d):
        if d == 0:
            return x
        r = pltpu.roll(x, (-d) % S, 0)
        return jnp.where((row + d >= 0) & (row + d < S), r, 0.0)

    sums = {}
    acc = shifted(-1) + x
    sums[2] = acc
    lo, hi = -1, 0
    for w in POOL_WINDOWS[1:]:
        for d in list(range(-(w // 2), lo)) + list(range(hi + 1, w // 2)):
            acc = acc + shifted(d)
        lo, hi = -(w // 2), w // 2 - 1
        sums[w] = acc
    grp = 2 * ct + (lane >= POOL_GROUP).astype(jnp.int32)
    win_sum = sums[POOL_WINDOWS[-1]]
    half = jnp.full(x.shape, POOL_WINDOWS[-1] // 2, jnp.int32)
    for g in range(len(POOL_WINDOWS) - 2, -1, -1):
        win_sum = jnp.where(grp == g, sums[POOL_WINDOWS[g]], win_sum)
        half = jnp.where(grp == g, POOL_WINDOWS[g] // 2, half)
    cnt = jnp.minimum(row + half - 1, S - 1) - jnp.maximum(row - half, 0) + 1
    y = win_sum / cnt.astype(F32) - x
    y = _dot3(*_split(y), *_split(w_ref[...])) + b_ref[...]
    o_ref[...] = y * sc_ref[...]


def _pool_mixer(u3, w, b, scale):
    B, S, _ = u3.shape
    nct = POOL_WIDTH // LANES
    gpt = LANES // POOL_GROUP
    wbd = jnp.zeros((nct, LANES, LANES), F32)
    for g in range(len(POOL_WINDOWS)):
        t, o = divmod(g, gpt)
        wbd = wbd.at[t, o * POOL_GROUP:(o + 1) * POOL_GROUP, o * POOL_GROUP:(o + 1) * POOL_GROUP].set(w[g])
    vspec = pl.BlockSpec((1, LANES), lambda bb, c: (0, c))
    return pl.pallas_call(
        _pool_kernel,
        grid=(B, nct),
        in_specs=[pl.BlockSpec((None, S, LANES), lambda bb, c: (bb, 0, c)),
                  pl.BlockSpec((None, LANES, LANES), lambda bb, c: (c, 0, 0)),
                  vspec, vspec],
        out_specs=pl.BlockSpec((None, S, LANES), lambda bb, c: (bb, 0, c)),
        out_shape=jax.ShapeDtypeStruct((B, S, POOL_WIDTH), F32),
        compiler_params=_cparams(("parallel", "parallel")),
        name="pool_mixer",
    )(u3, wbd, b.reshape(1, POOL_WIDTH), scale.reshape(1, POOL_WIDTH))


PAIRS_PER_GROUP = 6
N_CLASSES = N_GROUPS * PAIRS_PER_GROUP
ROUTE_W_LO, ROUTE_W_HI, ROUTE_CLS, ROUTE_RANK = 0, 1, 2, 3
TOKEN_ROWS = D_MODEL // LANES


def _store_token_major(ref, x, first_row, unit_rows):
    n = x.shape[0]
    for j in range(TOKEN_ROWS):
        ref[pl.ds(first_row + j, n, stride=unit_rows), :] = x[:, j * LANES:(j + 1) * LANES]


def _load_token_major(ref, n, first_row, unit_rows):
    return jnp.concatenate([ref[pl.ds(first_row + j, n, stride=unit_rows), :] for j in range(TOKEN_ROWS)],
                           axis=1)


def _class_experts(c):
    g, pidx = divmod(c, PAIRS_PER_GROUP)
    pairs = [(a, b) for a in range(EXPERTS_PER_GROUP) for b in range(a + 1, EXPERTS_PER_GROUP)]
    lo, hi = pairs[pidx]
    return g * EXPERTS_PER_GROUP + lo, g * EXPERTS_PER_GROUP + hi


def _outproj_kernel(h_ref, yh_ref, ya_ref, yp_ref, w_ref, g_ref, b_ref,
                    wgh_ref, wgl_ref, bg_ref, tri_ref, h1_ref, h1t_ref, route_ref, cnt_ref, carry_scr):
    @pl.when(pl.program_id(0) == 0)
    def _():
        carry_scr[...] = jnp.zeros_like(carry_scr)

    sub = tri_ref.shape[0]
    carry = carry_scr[...]
    for s in range(h_ref.shape[0] // sub):
        carry = _outproj_rows(slice(s * sub, (s + 1) * sub), s * sub * TOKEN_ROWS, carry,
                              h_ref, yh_ref, ya_ref, yp_ref, w_ref, g_ref, b_ref,
                              wgh_ref, wgl_ref, bg_ref, tri_ref, h1_ref, h1t_ref, route_ref)
    carry_scr[...] = carry
    cnt_ref[...] = carry


def _outproj_rows(rows, first_unit_row, carry, h_ref, yh_ref, ya_ref, yp_ref, w_ref,
                  g_ref, b_ref, wgh_ref, wgl_ref, bg_ref, tri_ref, h1_ref, h1t_ref, route_ref):
    d = functools.partial(jnp.dot, preferred_element_type=F32)
    o1, o2 = HY_WIDTH, HY_WIDTH + ATT_WIDTH
    mix = (d(yh_ref[rows, :].astype(BF16), w_ref[:o1, :]) + d(ya_ref[rows, :].astype(BF16), w_ref[o1:o2, :])
           + d(yp_ref[rows, :].astype(BF16), w_ref[o2:, :]))
    h1 = _ln_rows(DN_ALPHA * h_ref[rows, :] + mix, g_ref[...], b_ref[...])
    h1_ref[rows, :] = h1
    _store_token_major(h1t_ref, h1, first_unit_row, TOKEN_ROWS)

    logit = _dot3(*_split(h1), wgh_ref[...], wgl_ref[...]) + bg_ref[...]
    lni = lax.broadcasted_iota(jnp.int32, logit.shape, 1)
    ln = lni.astype(F32)
    grp = lax.shift_right_arithmetic(lni - GATE_FINE_LANE, 2).astype(F32)
    first = lambda mask: jnp.min(jnp.where(mask, ln, float(LANES)), axis=-1, keepdims=True)
    cmask = lni < N_GROUPS
    lc = jnp.where(cmask, logit, NEG_BIG)
    mc = jnp.max(lc, axis=-1, keepdims=True)
    gw = 1.0 / jnp.sum(jnp.where(cmask, jnp.exp(lc - mc), 0.0), axis=-1, keepdims=True)
    gi = first(cmask & (lc == mc))
    fmask = (lni >= GATE_FINE_LANE) & (lni < GATE_FINE_LANE + N_EXPERTS) & (grp == gi)
    lf = jnp.where(fmask, logit, NEG_BIG)
    m1 = jnp.max(lf, axis=-1, keepdims=True)
    i1 = first(fmask & (lf == m1))
    rest = fmask & (ln != i1)
    lf2 = jnp.where(rest, logit, NEG_BIG)
    m2 = jnp.max(lf2, axis=-1, keepdims=True)
    i2 = first(rest & (lf2 == m2))
    e2 = jnp.exp(m2 - m1)
    w1 = gw / (1.0 + e2)
    w2 = gw * e2 / (1.0 + e2)

    j1 = i1 - GATE_FINE_LANE - EXPERTS_PER_GROUP * gi
    j2 = i2 - GATE_FINE_LANE - EXPERTS_PER_GROUP * gi
    lo = jnp.minimum(j1, j2)
    hi = jnp.maximum(j1, j2)
    w_lo = jnp.where(j1 < j2, w1, w2)
    w_hi = jnp.where(j1 < j2, w2, w1)
    base = jnp.where(lo == 0.0, 0.0, jnp.where(lo == 1.0, 3.0, 5.0))
    cls = gi * PAIRS_PER_GROUP + base + hi - lo - 1.0
    onehot = ln == cls
    before = jnp.dot(tri_ref[...], onehot.astype(BF16), preferred_element_type=F32)
    rank = jnp.sum(jnp.where(onehot, before + carry, 0.0), axis=-1, keepdims=True)
    route_ref[rows, :] = jnp.where(lni == ROUTE_W_LO, w_lo,
                                   jnp.where(lni == ROUTE_W_HI, w_hi,
                                             jnp.where(lni == ROUTE_CLS, cls,
                                                       jnp.where(lni == ROUTE_RANK, rank, 0.0))))
    return carry + jnp.sum(onehot.astype(F32), axis=0, keepdims=True)


def _out_proj_ln_route(h2, yh, ya, yp, w_out_bf, layer, g, b, wgc, bgc, wgf, bgf, tm=512):
    T, D = h2.shape
    sub = tm
    tri = jnp.asarray(np.tril(np.ones((sub, sub), np.float32), -1)).astype(BF16)
    wg = jnp.zeros((D, LANES), F32).at[:, :N_GROUPS].set(wgc)
    wg = wg.at[:, GATE_FINE_LANE:GATE_FINE_LANE + N_EXPERTS].set(wgf)
    bg = jnp.zeros((1, LANES), F32).at[0, :N_GROUPS].set(bgc)
    bg = bg.at[0, GATE_FINE_LANE:GATE_FINE_LANE + N_EXPERTS].set(bgf)
    wgh, wgl = _split(wg)
    row = lambda c: pl.BlockSpec((tm, c), lambda i: (i, 0))
    full = lambda r, c: pl.BlockSpec((r, c), lambda i: (0, 0))
    return pl.pallas_call(
        _outproj_kernel,
        grid=(T // tm,),
        in_specs=[row(D), row(HY_WIDTH), row(ATT_WIDTH), row(POOL_WIDTH),
                  pl.BlockSpec((None, D, D), lambda i: (layer, 0, 0)),
                  full(1, D), full(1, D), full(D, LANES), full(D, LANES), full(1, LANES), full(sub, sub)],
        out_specs=[row(D), pl.BlockSpec((tm * TOKEN_ROWS, LANES), lambda i: (i, 0)), row(LANES), full(1, LANES)],
        out_shape=[jax.ShapeDtypeStruct((T, D), F32), jax.ShapeDtypeStruct((T * TOKEN_ROWS, LANES), F32),
                   jax.ShapeDtypeStruct((T, LANES), F32), jax.ShapeDtypeStruct((1, LANES), F32)],
        scratch_shapes=[pltpu.VMEM((1, LANES), F32)],
        compiler_params=_cparams(("arbitrary",)),
        name="out_proj_ln_route",
    )(h2, yh, ya, yp, w_out_bf, g.reshape(1, D), b.reshape(1, D), wgh, wgl, bg, tri)


MOE_TILE = 256
DMA_BATCH = 256


def _moe_rows_padded(T):
    return T + N_CLASSES * MOE_TILE


def _token_rows(ref, t, unit_rows):
    return ref.at[pl.ds(pl.multiple_of(t * unit_rows, unit_rows), unit_rows)]


DMA_GROUP = 16


def _start_tokens(n, slot_of, start_copy):
    def group(g, carry):
        first = g * DMA_GROUP
        slots = [slot_of(first + k) for k in range(DMA_GROUP)]
        for k in range(DMA_GROUP):
            start_copy(first + k, slots[k], k % 2)
        return carry

    lax.fori_loop(0, n // DMA_GROUP, group, 0)


def _wait_tokens(n, src_ref, dst_ref, unit_rows, sems):
    rows = pl.ds(0, (n // 2) * unit_rows)
    for sem in sems:
        pltpu.make_async_copy(src_ref.at[rows], dst_ref.at[rows], sem).wait()


def _dispatch_kernel(cls_ref, rank_ref, cnt_ref, h1t_ref, xs_ref, dest_ref, tlo_ref, thi_ref, tval_ref,
                     off_scr, zero_scr, sem, *, tm, n_tiles):
    i = pl.program_id(0)
    base = i * tm
    tile_rows = MOE_TILE * TOKEN_ROWS

    def fill_copy(tile):
        rows = pl.ds(pl.multiple_of(tile * tile_rows, tile_rows), tile_rows)
        return pltpu.make_async_copy(zero_scr, xs_ref.at[rows], sem.at[2])

    @pl.when(i == 0)
    def _():
        zero_scr[...] = jnp.zeros_like(zero_scr)
        off = jnp.int32(0)
        tile = jnp.int32(0)
        fills = []
        for c in range(N_CLASSES):
            n = cnt_ref[c]
            nt = lax.shift_right_logical(n + (MOE_TILE - 1), MOE_TILE.bit_length() - 1)
            off_scr[c] = off
            e_lo, e_hi = _class_experts(c)

            def mark(k, carry, tile=tile, e_lo=e_lo, e_hi=e_hi):
                tlo_ref[tile + k] = e_lo
                thi_ref[tile + k] = e_hi
                tval_ref[tile + k] = 1
                return carry

            lax.fori_loop(0, nt, mark, 0)
            fill = fill_copy(tile + nt - 1)
            fills.append((nt > 0, fill))

            @pl.when(nt > 0)
            def _(fill=fill):
                fill.start()

            off = off + nt * MOE_TILE
            tile = tile + nt

        def unused(k, carry):
            tlo_ref[k] = 0
            thi_ref[k] = 0
            tval_ref[k] = 0
            fill_copy(k).start()
            return carry

        lax.fori_loop(tile, n_tiles, unused, 0)
        for used, fill in fills:
            @pl.when(used)
            def _(fill=fill):
                fill.wait()

        def unused_wait(k, carry):
            fill_copy(k).wait()
            return carry

        lax.fori_loop(tile, n_tiles, unused_wait, 0)

    def slots_of_step(step):
        def slot(r, carry):
            t = step * tm + r
            dest_ref[t] = off_scr[cls_ref[t]] + rank_ref[t]
            return carry

        lax.fori_loop(0, tm, slot, 0, unroll=8)

    @pl.when(i == 0)
    def _():
        slots_of_step(0)

    def start_copy(r, d, prio):
        pltpu.make_async_copy(_token_rows(h1t_ref, r, TOKEN_ROWS), _token_rows(xs_ref, d, TOKEN_ROWS),
                              sem.at[prio]).start(priority=prio)

    _start_tokens(tm, lambda r: dest_ref[base + r], start_copy)

    @pl.when(i + 1 < pl.num_programs(0))
    def _():
        slots_of_step(i + 1)

    _wait_tokens(tm, h1t_ref, xs_ref, TOKEN_ROWS, [sem.at[0], sem.at[1]])


def _moe_dispatch(h1t, cls_i, rank_i, cnt_i, tm=512):
    T = h1t.shape[0] // TOKEN_ROWS
    slots = _moe_rows_padded(T)
    n_tiles = slots // MOE_TILE
    smem = pl.BlockSpec(memory_space=pltpu.SMEM)
    return pl.pallas_call(
        functools.partial(_dispatch_kernel, tm=tm, n_tiles=n_tiles),
        grid_spec=pltpu.PrefetchScalarGridSpec(
            num_scalar_prefetch=3,
            grid=(T // tm,),
            in_specs=[pl.BlockSpec((tm * TOKEN_ROWS, LANES), lambda i, *_: (i, 0))],
            out_specs=[pl.BlockSpec(memory_space=pl.ANY), smem, smem, smem, smem],
            scratch_shapes=[pltpu.SMEM((N_CLASSES,), jnp.int32),
                            pltpu.VMEM((MOE_TILE * TOKEN_ROWS, LANES), F32),
                            pltpu.SemaphoreType.DMA((3,))],
        ),
        out_shape=[jax.ShapeDtypeStruct((slots * TOKEN_ROWS, LANES), F32), jax.ShapeDtypeStruct((T,), jnp.int32),
                   jax.ShapeDtypeStruct((n_tiles,), jnp.int32), jax.ShapeDtypeStruct((n_tiles,), jnp.int32),
                   jax.ShapeDtypeStruct((n_tiles,), jnp.int32)],
        compiler_params=_cparams(("arbitrary",)),
        name="moe_dispatch",
    )(cls_i, rank_i, cnt_i, h1t)


def _experts_kernel(tlo_ref, thi_ref, tval_ref, xs_ref, w1l_ref, w3l_ref, w2l_ref, w1h_ref, w3h_ref, w2h_ref,
                    ys_ref):
    j = pl.program_id(0)
    d = functools.partial(jnp.dot, preferred_element_type=F32)

    @pl.when(tval_ref[j] == 1)
    def _():
        xb = _load_token_major(xs_ref, MOE_TILE, 0, TOKEN_ROWS).astype(BF16)

        def expert(w1_ref, w3_ref, w2_ref):
            a = d(xb, w1_ref[...])
            c = d(xb, w3_ref[...])
            return d((a * jax.nn.sigmoid(a) * c).astype(BF16), w2_ref[...])

        _store_token_major(ys_ref.at[0], expert(w1l_ref, w3l_ref, w2l_ref), 0, TOKEN_ROWS)
        _store_token_major(ys_ref.at[1], expert(w1h_ref, w3h_ref, w2h_ref), 0, TOKEN_ROWS)

    @pl.when(tval_ref[j] == 0)
    def _():
        ys_ref[...] = jnp.zeros_like(ys_ref)


def _moe_experts(xs, tlo, thi, tval, w1_bf, w3_bf, w2_bf, layer):
    slots = xs.shape[0] // TOKEN_ROWS
    D = D_MODEL
    wspec = lambda shape, which: pl.BlockSpec(
        (None, None) + shape, (lambda j, tlo, thi, tval: (layer, tlo[j], 0, 0)) if which == 0
        else (lambda j, tlo, thi, tval: (layer, thi[j], 0, 0)))
    up = (D, D_EXPERT)
    down = (D_EXPERT, D)
    return pl.pallas_call(
        _experts_kernel,
        grid_spec=pltpu.PrefetchScalarGridSpec(
            num_scalar_prefetch=3,
            grid=(slots // MOE_TILE,),
            in_specs=[pl.BlockSpec((MOE_TILE * TOKEN_ROWS, LANES), lambda j, *_: (j, 0)),
                      wspec(up, 0), wspec(up, 0), wspec(down, 0), wspec(up, 1), wspec(up, 1), wspec(down, 1)],
            out_specs=pl.BlockSpec((None, 2, MOE_TILE * TOKEN_ROWS, LANES), lambda j, *_: (j, 0, 0, 0)),
        ),
        out_shape=jax.ShapeDtypeStruct((slots // MOE_TILE, 2, MOE_TILE * TOKEN_ROWS, LANES), F32),
        compiler_params=_cparams(("arbitrary",)),
        name="moe_experts",
    )(tlo, thi, tval, xs, w1_bf, w3_bf, w2_bf, w1_bf, w3_bf, w2_bf)


def _combine_kernel(dest_ref, h1_ref, route_ref, ys_ref, p_ref, pwg_ref, pbg_ref, pwp_ref, g_ref, b_ref,
                    o_ref, y_scr, ym_scr, sem, *, tm):
    i = pl.program_id(0)
    d = functools.partial(jnp.dot, preferred_element_type=F32)
    tile_shift = MOE_TILE.bit_length() - 1

    def unit_rows(t):
        return pl.ds(pl.multiple_of(t * TOKEN_ROWS, TOKEN_ROWS), TOKEN_ROWS)

    def gather(step, buf):
        def start_copy(r, slot, prio):
            tile = lax.shift_right_logical(slot, tile_shift)
            src = ys_ref.at[tile, :, unit_rows(slot & (MOE_TILE - 1)), :]
            pltpu.make_async_copy(src, y_scr.at[buf, :, unit_rows(r), :],
                                  sem.at[2 * buf + prio]).start(priority=prio)
        _start_tokens(tm, lambda r: dest_ref[step * tm + r], start_copy)

    def wait_gathers(buf):
        rows = pl.ds(0, (tm // 2) * TOKEN_ROWS)
        for prio in range(2):
            pltpu.make_async_copy(ys_ref.at[0, :, rows, :], y_scr.at[buf, :, rows, :],
                                  sem.at[2 * buf + prio]).wait()

    @pl.when(i == 0)
    def _():
        gather(0, 0)

    for buf in range(2):
        @pl.when((i + 1 < pl.num_programs(0)) & ((i + 1) % 2 == buf))
        def _(buf=buf):
            gather(i + 1, buf)

    sub = tm // 2
    halves = [slice(s * sub, (s + 1) * sub) for s in range(2)]
    y_ple = []
    for rows in halves:
        z = d(h1_ref[rows, :].astype(BF16), pwg_ref[...]) + pbg_ref[...]
        y_ple.append(jax.nn.sigmoid(z) * d(p_ref[rows, :].astype(BF16), pwp_ref[...]))
    for buf in range(2):
        @pl.when(i % 2 == buf)
        def _(buf=buf):
            wait_gathers(buf)
            for s, rows in enumerate(halves):
                rec = route_ref[rows, :]
                first = s * sub * TOKEN_ROWS
                ym_scr[rows, :] = (rec[:, ROUTE_W_LO:ROUTE_W_LO + 1]
                                   * _load_token_major(y_scr.at[buf, 0], sub, first, TOKEN_ROWS)
                                   + rec[:, ROUTE_W_HI:ROUTE_W_HI + 1]
                                   * _load_token_major(y_scr.at[buf, 1], sub, first, TOKEN_ROWS))
    for s, rows in enumerate(halves):
        r_ = DN_ALPHA * h1_ref[rows, :] + ym_scr[rows, :] + y_ple[s]
        o_ref[rows, :] = _ln_rows(r_, g_ref[...], b_ref[...])


def _moe_combine_ple_ln(h1, route, ys, dest, p3, pwg_bf, pbg, pwp_bf, layer, g, b, tm=512):
    T, D = h1.shape
    row = lambda c: pl.BlockSpec((tm, c), lambda i, *_: (i, 0))
    full = lambda r, c: pl.BlockSpec((r, c), lambda i, *_: (0, 0))
    lay = lambda r, c: pl.BlockSpec((None, r, c), lambda i, *_: (layer, 0, 0))
    return pl.pallas_call(
        functools.partial(_combine_kernel, tm=tm),
        grid_spec=pltpu.PrefetchScalarGridSpec(
            num_scalar_prefetch=1,
            grid=(T // tm,),
            in_specs=[row(D), row(LANES), pl.BlockSpec(memory_space=pl.ANY),
                      pl.BlockSpec((None, tm, PLE_DIM), lambda i, *_: (layer, i, 0)),
                      lay(D, D), full(1, D), lay(PLE_DIM, D), full(1, D), full(1, D)],
            out_specs=row(D),
            scratch_shapes=[pltpu.VMEM((2, 2, tm * TOKEN_ROWS, LANES), F32), pltpu.VMEM((tm, D), F32),
                            pltpu.SemaphoreType.DMA((4,))],
        ),
        out_shape=jax.ShapeDtypeStruct((T, D), F32),
        compiler_params=_cparams(("arbitrary",)),
        name="moe_combine_ple_ln",
    )(dest, h1, route, ys, p3, pwg_bf, pbg.reshape(1, D), pwp_bf, g.reshape(1, D), b.reshape(1, D))


def _hyena_mixer(u3, kr4, ki4, layer, consts, conv_w, conv_b, bias_d, n1_total):
    B, S, _ = u3.shape
    vv, x0c = _hy_gate(u3, conv_w, conv_b)
    y = _hy_conv(vv, x0c, kr4, ki4, layer, consts, bias_d, n1_total)
    return y.reshape(B * S, HY_WIDTH)


def kernel(x, p, ln0_g, ln0_b, w_in, hy_conv_w, hy_conv_b, hy_fw1, hy_fb1, hy_freq1, hy_fw2, hy_fb2, hy_freq2, hy_fw3, hy_bias, att_lq1, att_lk1, att_lq2, att_lk2, att_subln_g, pool_w, pool_b, pool_scale, w_out, ln1_g, ln1_b, moe_wgc, moe_bgc, moe_wgf, moe_bgf, moe_w1, moe_w3, moe_w2, ple_wg, ple_bg, ple_wp, ln2_g, ln2_b):
    B, S, D = x.shape
    L = w_in.shape[0]
    T = B * S
    n1_total = 2 * S // FFT_N2
    C = HY_WIDTH

    kfilt, ssq = _hy_filter(S, hy_fw1, hy_fb1, hy_freq1, hy_fw2, hy_fb2, hy_freq2, hy_fw3)
    consts = _stage_b_consts(n1_total)
    kr_all, ki_all = _hy_spectrum(kfilt, ssq, consts, n1_total)
    tables = _rope_tables(S)

    w_in_bf, w_out_bf = w_in.astype(BF16), w_out.astype(BF16)
    w1_bf, w3_bf, w2_bf = moe_w1.astype(BF16), moe_w3.astype(BF16), moe_w2.astype(BF16)
    pwg_bf, pwp_bf = ple_wg.astype(BF16), ple_wp.astype(BF16)
    p3 = p.reshape(L, T, PLE_DIM)

    h = _layer_norm(x.reshape(T, D), ln0_g, ln0_b)
    for i in range(L):
        lam_init = 0.8 - 0.6 * math.exp(-0.3 * i)
        uh, qkv, up = _in_proj(h, w_in_bf, i, tables, S)
        y_hy = _hyena_mixer(uh.reshape(B, S, 3 * C), kr_all, ki_all, i, consts,
                            hy_conv_w[i], hy_conv_b[i], hy_bias[i], n1_total)
        y_att = _diff_attention(qkv.reshape(B, S, 3 * ATT_WIDTH), att_lq1[i], att_lk1[i], att_lq2[i],
                                att_lk2[i], att_subln_g[i], lam_init).reshape(T, ATT_WIDTH)
        y_pool = _pool_mixer(up.reshape(B, S, POOL_WIDTH), pool_w[i], pool_b[i],
                             pool_scale[i]).reshape(T, POOL_WIDTH)
        h1, h1t, route, counts = _out_proj_ln_route(h, y_hy, y_att, y_pool, w_out_bf, i, ln1_g[i],
                                                    ln1_b[i], moe_wgc[i], moe_bgc[i], moe_wgf[i], moe_bgf[i])
        cls_i = route[:, ROUTE_CLS].astype(jnp.int32)
        rank_i = route[:, ROUTE_RANK].astype(jnp.int32)
        xs, dest, tlo, thi, tval = _moe_dispatch(h1t, cls_i, rank_i, counts[0].astype(jnp.int32))
        ys = _moe_experts(xs, tlo, thi, tval, w1_bf, w3_bf, w2_bf, i)
        h = _moe_combine_ple_ln(h1, route, ys, dest, p3, pwg_bf, ple_bg[i], pwp_bf, i, ln2_g[i], ln2_b[i])
    return h.reshape(B, S, D)
```

```python
import functools
import math

import numpy as np
import jax
import jax.numpy as jnp
from jax import lax
from jax.experimental import pallas as pl
from jax.experimental.pallas import tpu as pltpu

F32 = jnp.float32
BF16 = jnp.bfloat16

D_MODEL = 1024
DEPTH = 4
HY_WIDTH = 256
ATT_WIDTH = 512
ATT_HEADS = 4
ATT_HEAD_DIM = 64
POOL_WINDOWS = (2, 4, 8, 16)
POOL_WIDTH = 256
POOL_GROUP = 64
IN_WIDTH = 3 * HY_WIDTH + 3 * ATT_WIDTH + POOL_WIDTH
ROPE_THETA = 500000.0
ROPE_DIM = ATT_HEAD_DIM // 4
HY_EMB = 33
HY_BANDS = (HY_EMB - 1) // 2
HY_FILTER_HIDDEN = 64
HY_DECAY_TARGET = 1e-2
HY_FAST_DECAY = 0.3
HY_SLOW_DECAY = 1.5
N_GROUPS = 4
EXPERTS_PER_GROUP = 4
N_EXPERTS = 16
D_EXPERT = 256
PLE_DIM = 256
LN_EPS = 1e-5
RMS_EPS = 1e-5
DN_ALPHA = (2 * DEPTH) ** 0.25

LANES = 128
FFT_N2 = 128
GATE_COARSE_LANE = 0
GATE_FINE_LANE = N_GROUPS
NEG_BIG = -1e30
LOG2E = 1.4426950408889634
VMEM_LIMIT = 56 * 1024 * 1024


def _cparams(sem):
    return pltpu.CompilerParams(dimension_semantics=sem, vmem_limit_bytes=VMEM_LIMIT)


def _split(x):
    hi = x.astype(BF16)
    lo = (x - hi.astype(F32)).astype(BF16)
    return hi, lo


def _dot3(ah, al, bh, bl):
    d = functools.partial(jnp.dot, preferred_element_type=F32)
    return d(ah, bh) + (d(ah, bl) + d(al, bh))


def _ln_rows(x, g, b):
    mu = jnp.mean(x, axis=-1, keepdims=True)
    xc = x - mu
    var = jnp.mean(xc * xc, axis=-1, keepdims=True)
    return xc * lax.rsqrt(var + LN_EPS) * g + b


def _ln_kernel(x_ref, g_ref, b_ref, o_ref):
    o_ref[...] = _ln_rows(x_ref[...], g_ref[...], b_ref[...])


def _layer_norm(x2, g, b, tm=512):
    T, D = x2.shape
    return pl.pallas_call(
        _ln_kernel,
        grid=(T // tm,),
        in_specs=[pl.BlockSpec((tm, D), lambda i: (i, 0)),
                  pl.BlockSpec((1, D), lambda i: (0, 0)),
                  pl.BlockSpec((1, D), lambda i: (0, 0))],
        out_specs=pl.BlockSpec((tm, D), lambda i: (i, 0)),
        out_shape=jax.ShapeDtypeStruct((T, D), F32),
        compiler_params=_cparams(("parallel",)),
        name="ln0",
    )(x2, g.reshape(1, D), b.reshape(1, D))


def _rope(x, cos_f, sin_a, sin_b):
    half = ROPE_DIM // 2
    return x * cos_f + pltpu.roll(x, LANES - half, 1) * sin_a + pltpu.roll(x, half, 1) * sin_b


def _inproj_kernel(h_ref, w_ref, cos_ref, sa_ref, sb_ref, uh_ref, qkv_ref, up_ref):
    hb = h_ref[...].astype(BF16)
    mm = lambda c0, c1: jnp.dot(hb, w_ref[:, c0:c1], preferred_element_type=F32)
    o_q = 3 * HY_WIDTH
    o_k = o_q + ATT_WIDTH
    o_v = o_k + ATT_WIDTH
    o_p = o_v + ATT_WIDTH
    uh_ref[...] = mm(0, o_q)
    up_ref[...] = mm(o_p, IN_WIDTH)
    cos_f, sin_a, sin_b = cos_ref[...], sa_ref[...], sb_ref[...]
    q = mm(o_q, o_k)
    k = mm(o_k, o_v)
    qscale = ATT_HEAD_DIM ** -0.5 * LOG2E
    for hd in range(ATT_HEADS):
        sl = slice(hd * LANES, (hd + 1) * LANES)
        qkv_ref[:, sl] = (_rope(q[:, sl], cos_f, sin_a, sin_b) * qscale).astype(BF16)
        qkv_ref[:, ATT_WIDTH + hd * LANES:ATT_WIDTH + (hd + 1) * LANES] = \
            _rope(k[:, sl], cos_f, sin_a, sin_b).astype(BF16)
    qkv_ref[:, 2 * ATT_WIDTH:] = mm(o_v, o_p).astype(BF16)


def _in_proj(h2, w_bf, layer, tables, seq, tm=512):
    T, D = h2.shape
    N = w_bf.shape[2]
    spt = seq // tm
    tspec = pl.BlockSpec((tm, LANES), lambda i: (i % spt, 0))
    row = lambda c: pl.BlockSpec((tm, c), lambda i: (i, 0))
    return pl.pallas_call(
        _inproj_kernel,
        grid=(T // tm,),
        in_specs=[row(D), pl.BlockSpec((None, D, N), lambda i: (layer, 0, 0)), tspec, tspec, tspec],
        out_specs=[row(3 * HY_WIDTH), row(3 * ATT_WIDTH), row(POOL_WIDTH)],
        out_shape=[jax.ShapeDtypeStruct((T, 3 * HY_WIDTH), F32),
                   jax.ShapeDtypeStruct((T, 3 * ATT_WIDTH), BF16),
                   jax.ShapeDtypeStruct((T, POOL_WIDTH), F32)],
        compiler_params=_cparams(("parallel",)),
        name="in_proj",
    )(h2, w_bf, *tables)


def _hy_filter_kernel(band_ref, phase_ref, fw1_ref, fb1_ref, fr1_ref, fw2_ref, fb2_ref, fr2_ref, fw3_ref,
                      dl_ref, k_ref, ssq_ref, z_scr, *, seq, tr):
    j = pl.program_id(0)
    l = pl.program_id(1)
    n = j * tr + lax.broadcasted_iota(jnp.int32, (tr, 1), 0)
    pos = jnp.where(n < seq, n, 2 * seq - n).astype(F32)
    t = pos * (1.0 / (seq - 1))

    @pl.when(l == 0)
    def _():
        wpos = (2.0 * math.pi / seq) * pos
        lane = lax.broadcasted_iota(jnp.int32, (tr, LANES), 1)
        z_scr[...] = jnp.where(lane == 0, t,
                               jnp.where(lane < HY_EMB, jnp.sin(wpos * band_ref[...] + phase_ref[...]), 0.0))

    @pl.when((l == 0) & (j == 0))
    def _():
        ssq_ref[...] = jnp.zeros_like(ssq_ref)

    z = z_scr[...]
    h1 = _dot3(*_split(z), *_split(fw1_ref[...])) + fb1_ref[...]
    h1 = jnp.sin(fr1_ref[...] * h1)
    h2 = _dot3(*_split(h1), *_split(fw2_ref[...])) + fb2_ref[...]
    h2 = jnp.sin(fr2_ref[...] * h2)
    filt = _dot3(*_split(h2), *_split(fw3_ref[...]))
    window = jnp.exp(-t * jnp.abs(dl_ref[...]))
    kk = jnp.where(n < seq, filt[:, :HY_WIDTH], filt[:, HY_WIDTH:]) * window
    kk = jnp.where(n == seq, 0.0, kk)
    k_ref[...] = kk
    ssq_ref[l] += jnp.sum(kk * kk, axis=0, keepdims=True)


def _hy_filter(seq, fw1, fb1, freq1, fw2, fb2, freq2, fw3, tr=1024):
    L = fw1.shape[0]
    n = 2 * seq
    H = HY_FILTER_HIDDEN
    bands = np.linspace(1e-4, HY_BANDS - 1, HY_BANDS)
    bandv = np.zeros((1, LANES), np.float32)
    bandv[0, 1:1 + HY_BANDS] = bands
    bandv[0, 1 + HY_BANDS:HY_EMB] = bands
    phase = np.zeros((1, LANES), np.float32)
    phase[0, 1:1 + HY_BANDS] = 0.5 * math.pi
    phase[0, 1 + HY_BANDS:HY_EMB] = math.pi
    fw1p = jnp.zeros((L, LANES, H), F32).at[:, :HY_EMB].set(fw1)
    max_decay = math.log(HY_DECAY_TARGET) / HY_FAST_DECAY
    min_decay = math.log(HY_DECAY_TARGET) / HY_SLOW_DECAY
    deltas = np.linspace(min_decay, max_decay, HY_WIDTH).astype(np.float32).reshape(1, HY_WIDTH)
    vec = lambda a: a.reshape(L, 1, a.shape[-1])
    lspec = lambda r, c: pl.BlockSpec((None, r, c), lambda j, l: (l, 0, 0))
    lanes = pl.BlockSpec((1, LANES), lambda j, l: (0, 0))
    return pl.pallas_call(
        functools.partial(_hy_filter_kernel, seq=seq, tr=tr),
        grid=(n // tr, L),
        in_specs=[lanes, lanes,
                  lspec(LANES, H), lspec(1, H), lspec(1, H),
                  lspec(H, H), lspec(1, H), lspec(1, H),
                  lspec(H, 2 * HY_WIDTH),
                  pl.BlockSpec((1, HY_WIDTH), lambda j, l: (0, 0))],
        out_specs=[pl.BlockSpec((None, tr, HY_WIDTH), lambda j, l: (l, j, 0)),
                   pl.BlockSpec((L, 1, HY_WIDTH), lambda j, l: (0, 0, 0))],
        out_shape=[jax.ShapeDtypeStruct((L, n, HY_WIDTH), F32),
                   jax.ShapeDtypeStruct((L, 1, HY_WIDTH), F32)],
        scratch_shapes=[pltpu.VMEM((tr, LANES), F32)],
        compiler_params=_cparams(("arbitrary", "arbitrary")),
        name="hy_filter",
    )(jnp.asarray(bandv), jnp.asarray(phase), fw1p, vec(fb1), vec(freq1), fw2, vec(fb2), vec(freq2), fw3,
      jnp.asarray(deltas))


def _k1_rows(n1):
    k1 = n1 // 2 + 1
    return k1, -(-k1 // 8) * 8


def _bf_pair(m):
    m32 = jnp.asarray(np.asarray(m, np.float32))
    return _split(m32)


SUBLANES = 8


def _stage_a_matrix(n1_used, n1_total):
    k1n, k1p = _k1_rows(n1_total)
    k1 = np.arange(k1p)
    valid = (k1 < k1n)[:, None]
    ang = 2.0 * np.pi * np.outer(k1, np.arange(n1_used)) / n1_total
    eye = np.eye(SUBLANES)
    return np.concatenate([np.kron(np.cos(ang) * valid, eye), np.kron(-np.sin(ang) * valid, eye)], axis=0)


def _stage_a_rows(x_ref, a_scr, kah_ref, kal_ref, n1_used, k1p):
    def stage_a(g, carry):
        xg = jnp.concatenate(
            [x_ref[pl.ds(pl.multiple_of(n1 * FFT_N2 + g * SUBLANES, SUBLANES), SUBLANES), :]
             for n1 in range(n1_used)], axis=0)
        ag = _dot3(kah_ref[...], kal_ref[...], *_split(xg))
        gs = pl.ds(pl.multiple_of(g * SUBLANES, SUBLANES), SUBLANES)
        for part in range(2):
            for k1 in range(k1p):
                r = (part * k1p + k1) * SUBLANES
                a_scr[part, k1, gs, :] = ag[r:r + SUBLANES]
        return carry

    lax.fori_loop(0, FFT_N2 // SUBLANES, stage_a, 0, unroll=2)


def _stage_b_consts(n1_total):
    n = n1_total * FFT_N2
    k1n, k1p = _k1_rows(n1_total)
    ang2 = 2.0 * np.pi * np.outer(np.arange(FFT_N2), np.arange(FFT_N2)) / FFT_N2
    c2, s2 = np.cos(ang2), np.sin(ang2)
    mf = np.block([[c2, s2], [-s2, c2]])
    mi = np.block([[c2, -s2], [s2, c2]])
    angt = 2.0 * np.pi * np.outer(np.arange(k1p), np.arange(FFT_N2)) / n
    twc = jnp.asarray(np.cos(angt).astype(np.float32)).reshape(k1p, FFT_N2, 1)
    tws = jnp.asarray(np.sin(angt).astype(np.float32)).reshape(k1p, FFT_N2, 1)
    return _bf_pair(mf), _bf_pair(mi), twc, tws


def _fwd_b(ar, ai, c, s, mfh, mfl):
    tr_ = ar * c + ai * s
    ti_ = ai * c - ar * s
    xh, xl = _split(jnp.concatenate([tr_, ti_], axis=0))
    z = _dot3(mfh, mfl, xh, xl)
    return z[:FFT_N2], z[FFT_N2:]


def _hy_spectrum_kernel(k_ref, ssq_ref, tw_ref, kah_ref, kal_ref, mfh_ref, mfl_ref, kr_ref, ki_ref, a_scr,
                        *, n1_total, k1n, k1p):
    _stage_a_rows(k_ref, a_scr, kah_ref, kal_ref, n1_total, k1p)
    scale = lax.rsqrt(ssq_ref[...] + 1e-6) * (1.0 / (n1_total * FFT_N2))

    def stage_b(k1, carry):
        tw = tw_ref[k1]
        zr, zi = _fwd_b(a_scr[0, k1], a_scr[1, k1], tw[:, 0:1], tw[:, 1:2], mfh_ref[...], mfl_ref[...])
        kr_ref[k1] = zr * scale
        ki_ref[k1] = zi * scale
        return carry

    lax.fori_loop(0, k1n, stage_b, 0, unroll=3 if k1n % 3 == 0 else 1)
    for k1 in range(k1n, k1p):
        kr_ref[k1] = jnp.zeros(kr_ref.shape[1:], F32)
        ki_ref[k1] = jnp.zeros(ki_ref.shape[1:], F32)


def _hy_spectrum(kfilt, ssq, consts, n1_total):
    L, n, C = kfilt.shape
    k1n, k1p = _k1_rows(n1_total)
    (mfh, mfl), _, twc, tws = consts
    tw = jnp.concatenate([twc, tws], axis=-1)
    kah, kal = _bf_pair(_stage_a_matrix(n1_total, n1_total))
    full = lambda a: pl.BlockSpec(a.shape, lambda l, c: (0,) * a.ndim)
    ospec = pl.BlockSpec((None, k1p, FFT_N2, LANES), lambda l, c: (l, 0, 0, c))
    return pl.pallas_call(
        functools.partial(_hy_spectrum_kernel, n1_total=n1_total, k1n=k1n, k1p=k1p),
        grid=(L, C // LANES),
        in_specs=[pl.BlockSpec((None, n, LANES), lambda l, c: (l, 0, c)),
                  pl.BlockSpec((None, 1, LANES), lambda l, c: (l, 0, c)),
                  full(tw), full(kah), full(kal), full(mfh), full(mfl)],
        out_specs=[ospec, ospec],
        out_shape=[jax.ShapeDtypeStruct((L, k1p, FFT_N2, C), F32)] * 2,
        scratch_shapes=[pltpu.VMEM((2, k1p, FFT_N2, LANES), F32)],
        compiler_params=_cparams(("parallel", "parallel")),
        name="hy_spectrum",
    )(kfilt, ssq, tw, kah, kal, mfh, mfl)


def _hy_conv_kernel(vv_ref, x0_ref, kr_ref, ki_ref, tw_ref, kah_ref, kal_ref, kch_ref, kcl_ref,
                    mfh_ref, mfl_ref, mih_ref, mil_ref, bias_ref, o_ref, a_scr, *, n1u, k1n, k1p):
    groups = FFT_N2 // SUBLANES

    def rows(n1, g):
        return pl.ds(pl.multiple_of(n1 * FFT_N2 + g * SUBLANES, SUBLANES), SUBLANES)

    _stage_a_rows(vv_ref, a_scr, kah_ref, kal_ref, n1u, k1p)

    def stage_b(k1, carry):
        tw = tw_ref[k1]
        c = tw[:, 0:1]
        s = tw[:, 1:2]
        zr, zi = _fwd_b(a_scr[0, k1], a_scr[1, k1], c, s, mfh_ref[...], mfl_ref[...])
        kr = kr_ref[k1]
        ki = ki_ref[k1]
        pr = zr * kr - zi * ki
        pi = zr * ki + zi * kr
        y = _dot3(mih_ref[...], mil_ref[...], *_split(jnp.concatenate([pr, pi], axis=0)))
        yr = y[:FFT_N2]
        yi = y[FFT_N2:]
        a_scr[0, k1] = yr * c - yi * s
        a_scr[1, k1] = yi * c + yr * s
        return carry

    lax.fori_loop(0, k1n, stage_b, 0, unroll=11 if k1n % 11 == 0 else 1)

    def stage_c(g, carry):
        gs = pl.ds(pl.multiple_of(g * SUBLANES, SUBLANES), SUBLANES)
        bg = jnp.concatenate([a_scr[part, k1, gs, :] for part in range(2) for k1 in range(k1p)], axis=0)
        yg = _dot3(kch_ref[...], kcl_ref[...], *_split(bg))
        for n1 in range(n1u):
            r = rows(n1, g)
            o_ref[r, :] = (yg[n1 * SUBLANES:(n1 + 1) * SUBLANES] + vv_ref[r, :] * bias_ref[...]) * x0_ref[r, :]
        return carry

    lax.fori_loop(0, groups, stage_c, 0, unroll=2)


def _hy_conv(vv, x0c, kr4, ki4, layer, consts, bias_d, n1_total):
    B, S, C = vv.shape
    n1u = S // FFT_N2
    k1n, k1p = _k1_rows(n1_total)
    (mfh, mfl), (mih, mil), twc, tws = consts
    tw = jnp.concatenate([twc, tws], axis=-1)
    k1 = np.arange(k1p)
    eye = np.eye(SUBLANES)
    ka = _stage_a_matrix(n1u, n1_total)
    w = np.where((k1 == 0) | (k1 == n1_total // 2), 1.0, 2.0) * (k1 < k1n)
    ang_c = 2.0 * np.pi * np.outer(np.arange(n1u), k1) / n1_total
    kc = np.concatenate([np.kron(np.cos(ang_c) * w[None, :], eye), np.kron(-np.sin(ang_c) * w[None, :], eye)], axis=1)
    kah, kal = _bf_pair(ka)
    kch, kcl = _bf_pair(kc)
    nct = C // LANES
    xspec = pl.BlockSpec((None, S, LANES), lambda c, b: (b, 0, c))
    kspec = pl.BlockSpec((None, k1p, FFT_N2, LANES), lambda c, b: (layer, 0, 0, c))
    full = lambda a: pl.BlockSpec(a.shape, lambda c, b: (0,) * a.ndim)
    return pl.pallas_call(
        functools.partial(_hy_conv_kernel, n1u=n1u, k1n=k1n, k1p=k1p),
        grid=(nct, B),
        in_specs=[xspec, xspec, kspec, kspec, full(tw), full(kah), full(kal), full(kch), full(kcl),
                  full(mfh), full(mfl), full(mih), full(mil),
                  pl.BlockSpec((1, LANES), lambda c, b: (0, c))],
        out_specs=xspec,
        out_shape=jax.ShapeDtypeStruct((B, S, C), F32),
        scratch_shapes=[pltpu.VMEM((2, k1p, FFT_N2, LANES), F32)],
        compiler_params=_cparams(("parallel", "parallel")),
        name="hy_conv",
    )(vv, x0c, kr4, ki4, tw, kah, kal, kch, kcl, mfh, mfl, mih, mil, bias_d.reshape(1, C))


def _conv3(x, w_ref, b_ref):
    S = x.shape[0]
    row = lax.broadcasted_iota(jnp.int32, x.shape, 0)
    prev = jnp.where(row == 0, 0.0, pltpu.roll(x, 1, 0))
    nxt = jnp.where(row == S - 1, 0.0, pltpu.roll(x, S - 1, 0))
    return prev * w_ref[0:1, :] + x * w_ref[1:2, :] + nxt * w_ref[2:3, :] + b_ref[...]


def _hy_gate_kernel(x0_ref, x1_ref, v_ref, w0_ref, w1_ref, w2_ref, b0_ref, b1_ref, b2_ref,
                    vv_ref, x0c_ref):
    x0c_ref[...] = _conv3(x0_ref[...], w0_ref, b0_ref)
    vv_ref[...] = _conv3(v_ref[...], w2_ref, b2_ref) * _conv3(x1_ref[...], w1_ref, b1_ref)


def _hy_gate(u3, conv_w, conv_b):
    B, S, _ = u3.shape
    nct = HY_WIDTH // LANES
    conv_b2 = conv_b.reshape(1, 3 * HY_WIDTH)
    uspec = lambda part: pl.BlockSpec((None, S, LANES), lambda b, c: (b, 0, part * nct + c))
    wspec = lambda part: pl.BlockSpec((3, LANES), lambda b, c: (0, part * nct + c))
    bspec = lambda part: pl.BlockSpec((1, LANES), lambda b, c: (0, part * nct + c))
    ospec = pl.BlockSpec((None, S, LANES), lambda b, c: (b, 0, c))
    return pl.pallas_call(
        _hy_gate_kernel,
        grid=(B, nct),
        in_specs=[uspec(0), uspec(1), uspec(2), wspec(0), wspec(1), wspec(2),
                  bspec(0), bspec(1), bspec(2)],
        out_specs=[ospec, ospec],
        out_shape=[jax.ShapeDtypeStruct((B, S, HY_WIDTH), F32)] * 2,
        compiler_params=_cparams(("parallel", "parallel")),
        name="hy_gate",
    )(u3, u3, u3, conv_w, conv_w, conv_w, conv_b2, conv_b2, conv_b2)


def _attn_kernel(q_ref, k_ref, v_ref, lq1_ref, lk1_ref, lq2_ref, lk2_ref, g_ref, o_ref,
                 s_scr, p_scr, vx_scr, kt_scr, *, lam_init, tq):
    S = q_ref.shape[0]
    nblk = S // tq
    lam = (jnp.exp(jnp.sum(lq1_ref[...] * lk1_ref[...], keepdims=True))
           - jnp.exp(jnp.sum(lq2_ref[...] * lk2_ref[...], keepdims=True)) + lam_init)

    def blk(i):
        return pl.ds(pl.multiple_of(i * tq, tq), tq)

    kt = k_ref[...].T
    feat = lax.broadcasted_iota(jnp.int32, kt.shape, 0)
    kt_scr[0] = jnp.where(feat < ATT_HEAD_DIM, kt, jnp.zeros_like(kt))
    kt_scr[1] = jnp.where(feat >= ATT_HEAD_DIM, kt, jnp.zeros_like(kt))

    def scores(i, slot):
        q = q_ref[blk(i), :]
        for c in range(2):
            s_scr[slot, c] = jnp.dot(q, kt_scr[c], preferred_element_type=F32)

    vx_scr[:, :LANES] = v_ref[...]
    vx_scr[:, LANES:] = (lax.broadcasted_iota(jnp.int32, (S, LANES), 1) == 0).astype(BF16)

    def softmax(slot):
        for c in range(2):
            s = s_scr[slot, c]
            m = jnp.max(s, axis=-1, keepdims=True)
            p_scr[slot, c] = jnp.exp2(s - m).astype(BF16)

    def values(i, slot):
        oe1 = jnp.dot(p_scr[slot, 0], vx_scr[...], preferred_element_type=F32)
        oe2 = jnp.dot(p_scr[slot, 1], vx_scr[...], preferred_element_type=F32)
        o = oe1[:, :LANES] / oe1[:, LANES:LANES + 1] - oe2[:, :LANES] * (lam / oe2[:, LANES:LANES + 1])
        o = o * lax.rsqrt(jnp.mean(o * o, axis=-1, keepdims=True) + RMS_EPS) * g_ref[...]
        o_ref[blk(i), :] = o * (1.0 - lam_init)

    scores(0, 0)
    softmax(0)
    scores(1, 1)

    def pair(j, carry):
        t = 2 * j
        values(t - 2, 0)
        softmax(1)
        scores(t, 0)
        values(t - 1, 1)
        softmax(0)
        scores(t + 1, 1)
        return carry

    lax.fori_loop(1, nblk // 2, pair, 0)
    values(nblk - 2, 0)
    softmax(1)
    values(nblk - 1, 1)


def _rope_tables(seq):
    pos = np.arange(seq, dtype=np.float64)
    inv_freq = np.power(ROPE_THETA, -np.arange(0, ROPE_DIM, 2, dtype=np.float64) / ROPE_DIM)
    ang = pos[:, None] * inv_freq[None, :]
    half = ROPE_DIM // 2
    cos_f = np.ones((seq, LANES), np.float32)
    sin_a = np.zeros((seq, LANES), np.float32)
    sin_b = np.zeros((seq, LANES), np.float32)
    for base in range(0, LANES, ATT_HEAD_DIM):
        cos_f[:, base:base + half] = np.cos(ang)
        cos_f[:, base + half:base + ROPE_DIM] = np.cos(ang)
        sin_a[:, base:base + half] = -np.sin(ang)
        sin_b[:, base + half:base + ROPE_DIM] = np.sin(ang)
    return jnp.asarray(cos_f), jnp.asarray(sin_a), jnp.asarray(sin_b)


def _diff_attention(qkv3, lq1, lk1, lq2, lk2, subln_g, lam_init, tq=256):
    B, S, _ = qkv3.shape
    nb = ATT_WIDTH // LANES
    assert S % (2 * tq) == 0
    spec = lambda part: pl.BlockSpec((None, S, LANES), lambda b, h: (b, 0, part * nb + h))
    vec = pl.BlockSpec((1, ATT_HEAD_DIM), lambda b, h: (0, 0))
    r1 = lambda a: a.reshape(1, -1)
    return pl.pallas_call(
        functools.partial(_attn_kernel, lam_init=lam_init, tq=tq),
        grid=(B, ATT_HEADS),
        in_specs=[spec(0), spec(1), spec(2), vec, vec, vec, vec,
                  pl.BlockSpec((1, LANES), lambda b, h: (0, 0))],
        out_specs=pl.BlockSpec((None, S, LANES), lambda b, h: (b, 0, h)),
        out_shape=jax.ShapeDtypeStruct((B, S, ATT_WIDTH), F32),
        scratch_shapes=[pltpu.VMEM((2, 2, tq, S), F32), pltpu.VMEM((2, 2, tq, S), BF16),
                        pltpu.VMEM((S, 2 * LANES), BF16), pltpu.VMEM((2, LANES, S), BF16)],
        compiler_params=_cparams(("parallel", "parallel")),
        name="diff_attn",
    )(qkv3, qkv3, qkv3, r1(lq1), r1(lk1), r1(lq2), r1(lk2), r1(subln_g))


def _pool_kernel(u_ref, w_ref, b_ref, sc_ref, o_ref):
    ct = pl.program_id(1)
    x = u_ref[...]
    S = x.shape[0]
    row = lax.broadcasted_iota(jnp.int32, x.shape, 0)
    lane = lax.broadcasted_iota(jnp.int32, x.shape, 1)

    def shifted(d):
        if d == 0:
            return x
        r = pltpu.roll(x, (-d) % S, 0)
        return jnp.where((row + d >= 0) & (row + d < S), r, 0.0)

    sums = {}
    acc = shifted(-1) + x
    sums[2] = acc
    lo, hi = -1, 0
    for w in POOL_WINDOWS[1:]:
        for d in list(range(-(w // 2), lo)) + list(range(hi + 1, w // 2)):
            acc = acc + shifted(d)
        lo, hi = -(w // 2), w // 2 - 1
        sums[w] = acc
    grp = 2 * ct + (lane >= POOL_GROUP).astype(jnp.int32)
    win_sum = sums[POOL_WINDOWS[-1]]
    half = jnp.full(x.shape, POOL_WINDOWS[-1] // 2, jnp.int32)
    for g in range(len(POOL_WINDOWS) - 2, -1, -1):
        win_sum = jnp.where(grp == g, sums[POOL_WINDOWS[g]], win_sum)
        half = jnp.where(grp == g, POOL_WINDOWS[g] // 2, half)
    cnt = jnp.minimum(row + half - 1, S - 1) - jnp.maximum(row - half, 0) + 1
    y = win_sum / cnt.astype(F32) - x
    y = _dot3(*_split(y), *_split(w_ref[...])) + b_ref[...]
    o_ref[...] = y * sc_ref[...]


def _pool_mixer(u3, w, b, scale):
    B, S, _ = u3.shape
    nct = POOL_WIDTH // LANES
    gpt = LANES // POOL_GROUP
    wbd = jnp.zeros((nct, LANES, LANES), F32)
    for g in range(len(POOL_WINDOWS)):
        t, o = divmod(g, gpt)
        wbd = wbd.at[t, o * POOL_GROUP:(o + 1) * POOL_GROUP, o * POOL_GROUP:(o + 1) * POOL_GROUP].set(w[g])
    vspec = pl.BlockSpec((1, LANES), lambda bb, c: (0, c))
    return pl.pallas_call(
        _pool_kernel,
        grid=(B, nct),
        in_specs=[pl.BlockSpec((None, S, LANES), lambda bb, c: (bb, 0, c)),
                  pl.BlockSpec((None, LANES, LANES), lambda bb, c: (c, 0, 0)),
                  vspec, vspec],
        out_specs=pl.BlockSpec((None, S, LANES), lambda bb, c: (bb, 0, c)),
        out_shape=jax.ShapeDtypeStruct((B, S, POOL_WIDTH), F32),
        compiler_params=_cparams(("parallel", "parallel")),
        name="pool_mixer",
    )(u3, wbd, b.reshape(1, POOL_WIDTH), scale.reshape(1, POOL_WIDTH))


PAIRS_PER_GROUP = 6
N_CLASSES = N_GROUPS * PAIRS_PER_GROUP
ROUTE_W_LO, ROUTE_W_HI, ROUTE_CLS, ROUTE_RANK = 0, 1, 2, 3
TOKEN_ROWS = D_MODEL // LANES


def _store_token_major(ref, x, first_row, unit_rows):
    n = x.shape[0]
    for j in range(TOKEN_ROWS):
        ref[pl.ds(first_row + j, n, stride=unit_rows), :] = x[:, j * LANES:(j + 1) * LANES]


def _load_token_major(ref, n, first_row, unit_rows):
    return jnp.concatenate([ref[pl.ds(first_row + j, n, stride=unit_rows), :] for j in range(TOKEN_ROWS)],
                           axis=1)


def _class_experts(c):
    g, pidx = divmod(c, PAIRS_PER_GROUP)
    pairs = [(a, b) for a in range(EXPERTS_PER_GROUP) for b in range(a + 1, EXPERTS_PER_GROUP)]
    lo, hi = pairs[pidx]
    return g * EXPERTS_PER_GROUP + lo, g * EXPERTS_PER_GROUP + hi


def _outproj_kernel(h_ref, yh_ref, ya_ref, yp_ref, w_ref, g_ref, b_ref,
                    wgh_ref, wgl_ref, bg_ref, tri_ref, h1_ref, h1t_ref, route_ref, cnt_ref, carry_scr):
    @pl.when(pl.program_id(0) == 0)
    def _():
        carry_scr[...] = jnp.zeros_like(carry_scr)

    sub = tri_ref.shape[0]
    carry = carry_scr[...]
    for s in range(h_ref.shape[0] // sub):
        carry = _outproj_rows(slice(s * sub, (s + 1) * sub), s * sub * TOKEN_ROWS, carry,
                              h_ref, yh_ref, ya_ref, yp_ref, w_ref, g_ref, b_ref,
                              wgh_ref, wgl_ref, bg_ref, tri_ref, h1_ref, h1t_ref, route_ref)
    carry_scr[...] = carry
    cnt_ref[...] = carry


def _outproj_rows(rows, first_unit_row, carry, h_ref, yh_ref, ya_ref, yp_ref, w_ref,
                  g_ref, b_ref, wgh_ref, wgl_ref, bg_ref, tri_ref, h1_ref, h1t_ref, route_ref):
    d = functools.partial(jnp.dot, preferred_element_type=F32)
    o1, o2 = HY_WIDTH, HY_WIDTH + ATT_WIDTH
    mix = (d(yh_ref[rows, :].astype(BF16), w_ref[:o1, :]) + d(ya_ref[rows, :].astype(BF16), w_ref[o1:o2, :])
           + d(yp_ref[rows, :].astype(BF16), w_ref[o2:, :]))
    h1 = _ln_rows(DN_ALPHA * h_ref[rows, :] + mix, g_ref[...], b_ref[...])
    h1_ref[rows, :] = h1
    _store_token_major(h1t_ref, h1, first_unit_row, TOKEN_ROWS)

    logit = _dot3(*_split(h1), wgh_ref[...], wgl_ref[...]) + bg_ref[...]
    lni = lax.broadcasted_iota(jnp.int32, logit.shape, 1)
    ln = lni.astype(F32)
    grp = lax.shift_right_arithmetic(lni - GATE_FINE_LANE, 2).astype(F32)
    first = lambda mask: jnp.min(jnp.where(mask, ln, float(LANES)), axis=-1, keepdims=True)
    cmask = lni < N_GROUPS
    lc = jnp.where(cmask, logit, NEG_BIG)
    mc = jnp.max(lc, axis=-1, keepdims=True)
    gw = 1.0 / jnp.sum(jnp.where(cmask, jnp.exp(lc - mc), 0.0), axis=-1, keepdims=True)
    gi = first(cmask & (lc == mc))
    fmask = (lni >= GATE_FINE_LANE) & (lni < GATE_FINE_LANE + N_EXPERTS) & (grp == gi)
    lf = jnp.where(fmask, logit, NEG_BIG)
    m1 = jnp.max(lf, axis=-1, keepdims=True)
    i1 = first(fmask & (lf == m1))
    rest = fmask & (ln != i1)
    lf2 = jnp.where(rest, logit, NEG_BIG)
    m2 = jnp.max(lf2, axis=-1, keepdims=True)
    i2 = first(rest & (lf2 == m2))
    e2 = jnp.exp(m2 - m1)
    w1 = gw / (1.0 + e2)
    w2 = gw * e2 / (1.0 + e2)

    j1 = i1 - GATE_FINE_LANE - EXPERTS_PER_GROUP * gi
    j2 = i2 - GATE_FINE_LANE - EXPERTS_PER_GROUP * gi
    lo = jnp.minimum(j1, j2)
    hi = jnp.maximum(j1, j2)
    w_lo = jnp.where(j1 < j2, w1, w2)
    w_hi = jnp.where(j1 < j2, w2, w1)
    base = jnp.where(lo == 0.0, 0.0, jnp.where(lo == 1.0, 3.0, 5.0))
    cls = gi * PAIRS_PER_GROUP + base + hi - lo - 1.0
    onehot = ln == cls
    before = jnp.dot(tri_ref[...], onehot.astype(BF16), preferred_element_type=F32)
    rank = jnp.sum(jnp.where(onehot, before + carry, 0.0), axis=-1, keepdims=True)
    route_ref[rows, :] = jnp.where(lni == ROUTE_W_LO, w_lo,
                                   jnp.where(lni == ROUTE_W_HI, w_hi,
                                             jnp.where(lni == ROUTE_CLS, cls,
                                                       jnp.where(lni == ROUTE_RANK, rank, 0.0))))
    return carry + jnp.sum(onehot.astype(F32), axis=0, keepdims=True)


def _out_proj_ln_route(h2, yh, ya, yp, w_out_bf, layer, g, b, wgc, bgc, wgf, bgf, tm=512):
    T, D = h2.shape
    sub = tm
    tri = jnp.asarray(np.tril(np.ones((sub, sub), np.float32), -1)).astype(BF16)
    wg = jnp.zeros((D, LANES), F32).at[:, :N_GROUPS].set(wgc)
    wg = wg.at[:, GATE_FINE_LANE:GATE_FINE_LANE + N_EXPERTS].set(wgf)
    bg = jnp.zeros((1, LANES), F32).at[0, :N_GROUPS].set(bgc)
    bg = bg.at[0, GATE_FINE_LANE:GATE_FINE_LANE + N_EXPERTS].set(bgf)
    wgh, wgl = _split(wg)
    row = lambda c: pl.BlockSpec((tm, c), lambda i: (i, 0))
    full = lambda r, c: pl.BlockSpec((r, c), lambda i: (0, 0))
    return pl.pallas_call(
        _outproj_kernel,
        grid=(T // tm,),
        in_specs=[row(D), row(HY_WIDTH), row(ATT_WIDTH), row(POOL_WIDTH),
                  pl.BlockSpec((None, D, D), lambda i: (layer, 0, 0)),
                  full(1, D), full(1, D), full(D, LANES), full(D, LANES), full(1, LANES), full(sub, sub)],
        out_specs=[row(D), pl.BlockSpec((tm * TOKEN_ROWS, LANES), lambda i: (i, 0)), row(LANES), full(1, LANES)],
        out_shape=[jax.ShapeDtypeStruct((T, D), F32), jax.ShapeDtypeStruct((T * TOKEN_ROWS, LANES), F32),
                   jax.ShapeDtypeStruct((T, LANES), F32), jax.ShapeDtypeStruct((1, LANES), F32)],
        scratch_shapes=[pltpu.VMEM((1, LANES), F32)],
        compiler_params=_cparams(("arbitrary",)),
        name="out_proj_ln_route",
    )(h2, yh, ya, yp, w_out_bf, g.reshape(1, D), b.reshape(1, D), wgh, wgl, bg, tri)


MOE_TILE = 512
DMA_BATCH = 256


def _moe_rows_padded(T):
    return T + N_CLASSES * MOE_TILE


def _token_rows(ref, t, unit_rows):
    return ref.at[pl.ds(pl.multiple_of(t * unit_rows, unit_rows), unit_rows)]


DMA_GROUP = 8


def _start_tokens(n, slot_of, start_copy):
    def group(g, carry):
        first = g * DMA_GROUP
        slots = [slot_of(first + k) for k in range(DMA_GROUP)]
        for k in range(DMA_GROUP):
            start_copy(first + k, slots[k], k % 2)
        return carry

    lax.fori_loop(0, n // DMA_GROUP, group, 0)


def _wait_tokens(n, src_ref, dst_ref, unit_rows, sems):
    rows = pl.ds(0, (n // 2) * unit_rows)
    for sem in sems:
        pltpu.make_async_copy(src_ref.at[rows], dst_ref.at[rows], sem).wait()


def _dispatch_kernel(cls_ref, rank_ref, cnt_ref, h1t_ref, xs_ref, dest_ref, tlo_ref, thi_ref, tval_ref,
                     off_scr, zero_scr, sem, *, tm, n_tiles):
    i = pl.program_id(0)
    base = i * tm
    tile_rows = MOE_TILE * TOKEN_ROWS

    def fill_copy(tile):
        rows = pl.ds(pl.multiple_of(tile * tile_rows, tile_rows), tile_rows)
        return pltpu.make_async_copy(zero_scr, xs_ref.at[rows], sem.at[2])

    @pl.when(i == 0)
    def _():
        zero_scr[...] = jnp.zeros_like(zero_scr)
        off = jnp.int32(0)
        tile = jnp.int32(0)
        fills = []
        for c in range(N_CLASSES):
            n = cnt_ref[c]
            nt = lax.shift_right_logical(n + (MOE_TILE - 1), MOE_TILE.bit_length() - 1)
            off_scr[c] = off
            e_lo, e_hi = _class_experts(c)

            def mark(k, carry, tile=tile, e_lo=e_lo, e_hi=e_hi):
                tlo_ref[tile + k] = e_lo
                thi_ref[tile + k] = e_hi
                tval_ref[tile + k] = 1
                return carry

            lax.fori_loop(0, nt, mark, 0)
            fill = fill_copy(tile + nt - 1)
            fills.append((nt > 0, fill))

            @pl.when(nt > 0)
            def _(fill=fill):
                fill.start()

            off = off + nt * MOE_TILE
            tile = tile + nt

        def unused(k, carry):
            tlo_ref[k] = 0
            thi_ref[k] = 0
            tval_ref[k] = 0
            fill_copy(k).start()
            return carry

        lax.fori_loop(tile, n_tiles, unused, 0)
        for used, fill in fills:
            @pl.when(used)
            def _(fill=fill):
                fill.wait()

        def unused_wait(k, carry):
            fill_copy(k).wait()
            return carry

        lax.fori_loop(tile, n_tiles, unused_wait, 0)

    def slots_of_step(step):
        def slot(r, carry):
            t = step * tm + r
            dest_ref[t] = off_scr[cls_ref[t]] + rank_ref[t]
            return carry

        lax.fori_loop(0, tm, slot, 0, unroll=8)

    @pl.when(i == 0)
    def _():
        slots_of_step(0)

    def start_copy(r, d, prio):
        pltpu.make_async_copy(_token_rows(h1t_ref, r, TOKEN_ROWS), _token_rows(xs_ref, d, TOKEN_ROWS),
                              sem.at[prio]).start(priority=prio)

    _start_tokens(tm, lambda r: dest_ref[base + r], start_copy)

    @pl.when(i + 1 < pl.num_programs(0))
    def _():
        slots_of_step(i + 1)

    _wait_tokens(tm, h1t_ref, xs_ref, TOKEN_ROWS, [sem.at[0], sem.at[1]])


def _moe_dispatch(h1t, cls_i, rank_i, cnt_i, tm=512):
    T = h1t.shape[0] // TOKEN_ROWS
    slots = _moe_rows_padded(T)
    n_tiles = slots // MOE_TILE
    smem = pl.BlockSpec(memory_space=pltpu.SMEM)
    return pl.pallas_call(
        functools.partial(_dispatch_kernel, tm=tm, n_tiles=n_tiles),
        grid_spec=pltpu.PrefetchScalarGridSpec(
            num_scalar_prefetch=3,
            grid=(T // tm,),
            in_specs=[pl.BlockSpec((tm * TOKEN_ROWS, LANES), lambda i, *_: (i, 0))],
            out_specs=[pl.BlockSpec(memory_space=pl.ANY), smem, smem, smem, smem],
            scratch_shapes=[pltpu.SMEM((N_CLASSES,), jnp.int32),
                            pltpu.VMEM((MOE_TILE * TOKEN_ROWS, LANES), F32),
                            pltpu.SemaphoreType.DMA((3,))],
        ),
        out_shape=[jax.ShapeDtypeStruct((slots * TOKEN_ROWS, LANES), F32), jax.ShapeDtypeStruct((T,), jnp.int32),
                   jax.ShapeDtypeStruct((n_tiles,), jnp.int32), jax.ShapeDtypeStruct((n_tiles,), jnp.int32),
                   jax.ShapeDtypeStruct((n_tiles,), jnp.int32)],
        compiler_params=_cparams(("arbitrary",)),
        name="moe_dispatch",
    )(cls_i, rank_i, cnt_i, h1t)


def _experts_kernel(tlo_ref, thi_ref, tval_ref, xs_ref, w1l_ref, w3l_ref, w2l_ref, w1h_ref, w3h_ref, w2h_ref,
                    ys_ref):
    j = pl.program_id(0)
    d = functools.partial(jnp.dot, preferred_element_type=F32)

    @pl.when(tval_ref[j] == 1)
    def _():
        xb = _load_token_major(xs_ref, MOE_TILE, 0, TOKEN_ROWS).astype(BF16)

        def expert(w1_ref, w3_ref, w2_ref):
            a = d(xb, w1_ref[...])
            c = d(xb, w3_ref[...])
            return d((a * jax.nn.sigmoid(a) * c).astype(BF16), w2_ref[...])

        _store_token_major(ys_ref.at[0], expert(w1l_ref, w3l_ref, w2l_ref), 0, TOKEN_ROWS)
        _store_token_major(ys_ref.at[1], expert(w1h_ref, w3h_ref, w2h_ref), 0, TOKEN_ROWS)

    @pl.when(tval_ref[j] == 0)
    def _():
        ys_ref[...] = jnp.zeros_like(ys_ref)


def _moe_experts(xs, tlo, thi, tval, w1_bf, w3_bf, w2_bf, layer):
    slots = xs.shape[0] // TOKEN_ROWS
    D = D_MODEL
    wspec = lambda shape, which: pl.BlockSpec(
        (None, None) + shape, (lambda j, tlo, thi, tval: (layer, tlo[j], 0, 0)) if which == 0
        else (lambda j, tlo, thi, tval: (layer, thi[j], 0, 0)))
    up = (D, D_EXPERT)
    down = (D_EXPERT, D)
    return pl.pallas_call(
        _experts_kernel,
        grid_spec=pltpu.PrefetchScalarGridSpec(
            num_scalar_prefetch=3,
            grid=(slots // MOE_TILE,),
            in_specs=[pl.BlockSpec((MOE_TILE * TOKEN_ROWS, LANES), lambda j, *_: (j, 0)),
                      wspec(up, 0), wspec(up, 0), wspec(down, 0), wspec(up, 1), wspec(up, 1), wspec(down, 1)],
            out_specs=pl.BlockSpec((None, 2, MOE_TILE * TOKEN_ROWS, LANES), lambda j, *_: (j, 0, 0, 0)),
        ),
        out_shape=jax.ShapeDtypeStruct((slots // MOE_TILE, 2, MOE_TILE * TOKEN_ROWS, LANES), F32),
        compiler_params=_cparams(("arbitrary",)),
        name="moe_experts",
    )(tlo, thi, tval, xs, w1_bf, w3_bf, w2_bf, w1_bf, w3_bf, w2_bf)


def _combine_kernel(dest_ref, h1_ref, route_ref, ys_ref, p_ref, pwg_ref, pbg_ref, pwp_ref, g_ref, b_ref,
                    o_ref, y_scr, ym_scr, sem, *, tm):
    i = pl.program_id(0)
    d = functools.partial(jnp.dot, preferred_element_type=F32)
    tile_shift = MOE_TILE.bit_length() - 1

    def unit_rows(t):
        return pl.ds(pl.multiple_of(t * TOKEN_ROWS, TOKEN_ROWS), TOKEN_ROWS)

    def gather(step, buf):
        def start_copy(r, slot, prio):
            tile = lax.shift_right_logical(slot, tile_shift)
            src = ys_ref.at[tile, :, unit_rows(slot & (MOE_TILE - 1)), :]
            pltpu.make_async_copy(src, y_scr.at[buf, :, unit_rows(r), :],
                                  sem.at[2 * buf + prio]).start(priority=prio)
        _start_tokens(tm, lambda r: dest_ref[step * tm + r], start_copy)

    def wait_gathers(buf):
        rows = pl.ds(0, (tm // 2) * TOKEN_ROWS)
        for prio in range(2):
            pltpu.make_async_copy(ys_ref.at[0, :, rows, :], y_scr.at[buf, :, rows, :],
                                  sem.at[2 * buf + prio]).wait()

    @pl.when(i == 0)
    def _():
        gather(0, 0)

    for buf in range(2):
        @pl.when((i + 1 < pl.num_programs(0)) & ((i + 1) % 2 == buf))
        def _(buf=buf):
            gather(i + 1, buf)

    sub = tm // 2
    halves = [slice(s * sub, (s + 1) * sub) for s in range(2)]
    y_ple = []
    for rows in halves:
        z = d(h1_ref[rows, :].astype(BF16), pwg_ref[...]) + pbg_ref[...]
        y_ple.append(jax.nn.sigmoid(z) * d(p_ref[rows, :].astype(BF16), pwp_ref[...]))
    for buf in range(2):
        @pl.when(i % 2 == buf)
        def _(buf=buf):
            wait_gathers(buf)
            for s, rows in enumerate(halves):
                rec = route_ref[rows, :]
                first = s * sub * TOKEN_ROWS
                ym_scr[rows, :] = (rec[:, ROUTE_W_LO:ROUTE_W_LO + 1]
                                   * _load_token_major(y_scr.at[buf, 0], sub, first, TOKEN_ROWS)
                                   + rec[:, ROUTE_W_HI:ROUTE_W_HI + 1]
                                   * _load_token_major(y_scr.at[buf, 1], sub, first, TOKEN_ROWS))
    for s, rows in enumerate(halves):
        r_ = DN_ALPHA * h1_ref[rows, :] + ym_scr[rows, :] + y_ple[s]
        o_ref[rows, :] = _ln_rows(r_, g_ref[...], b_ref[...])


def _moe_combine_ple_ln(h1, route, ys, dest, p3, pwg_bf, pbg, pwp_bf, layer, g, b, tm=512):
    T, D = h1.shape
    row = lambda c: pl.BlockSpec((tm, c), lambda i, *_: (i, 0))
    full = lambda r, c: pl.BlockSpec((r, c), lambda i, *_: (0, 0))
    lay = lambda r, c: pl.BlockSpec((None, r, c), lambda i, *_: (layer, 0, 0))
    return pl.pallas_call(
        functools.partial(_combine_kernel, tm=tm),
        grid_spec=pltpu.PrefetchScalarGridSpec(
            num_scalar_prefetch=1,
            grid=(T // tm,),
            in_specs=[row(D), row(LANES), pl.BlockSpec(memory_space=pl.ANY),
                      pl.BlockSpec((None, tm, PLE_DIM), lambda i, *_: (layer, i, 0)),
                      lay(D, D), full(1, D), lay(PLE_DIM, D), full(1, D), full(1, D)],
            out_specs=row(D),
            scratch_shapes=[pltpu.VMEM((2, 2, tm * TOKEN_ROWS, LANES), F32), pltpu.VMEM((tm, D), F32),
                            pltpu.SemaphoreType.DMA((4,))],
        ),
        out_shape=jax.ShapeDtypeStruct((T, D), F32),
        compiler_params=_cparams(("arbitrary",)),
        name="moe_combine_ple_ln",
    )(dest, h1, route, ys, p3, pwg_bf, pbg.reshape(1, D), pwp_bf, g.reshape(1, D), b.reshape(1, D))


def _hyena_mixer(u3, kr4, ki4, layer, consts, conv_w, conv_b, bias_d, n1_total):
    B, S, _ = u3.shape
    vv, x0c = _hy_gate(u3, conv_w, conv_b)
    y = _hy_conv(vv, x0c, kr4, ki4, layer, consts, bias_d, n1_total)
    return y.reshape(B * S, HY_WIDTH)


def kernel(x, p, ln0_g, ln0_b, w_in, hy_conv_w, hy_conv_b, hy_fw1, hy_fb1, hy_freq1, hy_fw2, hy_fb2, hy_freq2, hy_fw3, hy_bias, att_lq1, att_lk1, att_lq2, att_lk2, att_subln_g, pool_w, pool_b, pool_scale, w_out, ln1_g, ln1_b, moe_wgc, moe_bgc, moe_wgf, moe_bgf, moe_w1, moe_w3, moe_w2, ple_wg, ple_bg, ple_wp, ln2_g, ln2_b):
    B, S, D = x.shape
    L = w_in.shape[0]
    T = B * S
    n1_total = 2 * S // FFT_N2
    C = HY_WIDTH

    kfilt, ssq = _hy_filter(S, hy_fw1, hy_fb1, hy_freq1, hy_fw2, hy_fb2, hy_freq2, hy_fw3)
    consts = _stage_b_consts(n1_total)
    kr_all, ki_all = _hy_spectrum(kfilt, ssq, consts, n1_total)
    tables = _rope_tables(S)

    w_in_bf, w_out_bf = w_in.astype(BF16), w_out.astype(BF16)
    w1_bf, w3_bf, w2_bf = moe_w1.astype(BF16), moe_w3.astype(BF16), moe_w2.astype(BF16)
    pwg_bf, pwp_bf = ple_wg.astype(BF16), ple_wp.astype(BF16)
    p3 = p.reshape(L, T, PLE_DIM)

    h = _layer_norm(x.reshape(T, D), ln0_g, ln0_b)
    for i in range(L):
        lam_init = 0.8 - 0.6 * math.exp(-0.3 * i)
        uh, qkv, up = _in_proj(h, w_in_bf, i, tables, S)
        y_hy = _hyena_mixer(uh.reshape(B, S, 3 * C), kr_all, ki_all, i, consts,
                            hy_conv_w[i], hy_conv_b[i], hy_bias[i], n1_total)
        y_att = _diff_attention(qkv.reshape(B, S, 3 * ATT_WIDTH), att_lq1[i], att_lk1[i], att_lq2[i],
                                att_lk2[i], att_subln_g[i], lam_init).reshape(T, ATT_WIDTH)
        y_pool = _pool_mixer(up.reshape(B, S, POOL_WIDTH), pool_w[i], pool_b[i],
                             pool_scale[i]).reshape(T, POOL_WIDTH)
        h1, h1t, route, counts = _out_proj_ln_route(h, y_hy, y_att, y_pool, w_out_bf, i, ln1_g[i],
                                                    ln1_b[i], moe_wgc[i], moe_bgc[i], moe_wgf[i], moe_bgf[i])
        cls_i = route[:, ROUTE_CLS].astype(jnp.int32)
        rank_i = route[:, ROUTE_RANK].astype(jnp.int32)
        xs, dest, tlo, thi, tval = _moe_dispatch(h1t, cls_i, rank_i, counts[0].astype(jnp.int32))
        ys = _moe_experts(xs, tlo, thi, tval, w1_bf, w3_bf, w2_bf, i)
        h = _moe_combine_ple_ln(h1, route, ys, dest, p3, pwg_bf, ple_bg[i], pwp_bf, i, ln2_g[i], ln2_b[i])
    return h.reshape(B, S, D)
```

```python
import functools
import math

import numpy as np
import jax
import jax.numpy as jnp
from jax import lax
from jax.experimental import pallas as pl
from jax.experimental.pallas import tpu as pltpu

F32 = jnp.float32
BF16 = jnp.bfloat16

D_MODEL = 1024
DEPTH = 4
HY_WIDTH = 256
ATT_WIDTH = 512
ATT_HEADS = 4
ATT_HEAD_DIM = 64
POOL_WINDOWS = (2, 4, 8, 16)
POOL_WIDTH = 256
POOL_GROUP = 64
IN_WIDTH = 3 * HY_WIDTH + 3 * ATT_WIDTH + POOL_WIDTH
ROPE_THETA = 500000.0
ROPE_DIM = ATT_HEAD_DIM // 4
HY_EMB = 33
HY_BANDS = (HY_EMB - 1) // 2
HY_FILTER_HIDDEN = 64
HY_DECAY_TARGET = 1e-2
HY_FAST_DECAY = 0.3
HY_SLOW_DECAY = 1.5
N_GROUPS = 4
EXPERTS_PER_GROUP = 4
N_EXPERTS = 16
D_EXPERT = 256
PLE_DIM = 256
LN_EPS = 1e-5
RMS_EPS = 1e-5
DN_ALPHA = (2 * DEPTH) ** 0.25

LANES = 128
FFT_N2 = 128
GATE_COARSE_LANE = 0
GATE_FINE_LANE = N_GROUPS
NEG_BIG = -1e30
LOG2E = 1.4426950408889634
VMEM_LIMIT = 56 * 1024 * 1024


def _cparams(sem):
    return pltpu.CompilerParams(dimension_semantics=sem, vmem_limit_bytes=VMEM_LIMIT)


def _split(x):
    hi = x.astype(BF16)
    lo = (x - hi.astype(F32)).astype(BF16)
    return hi, lo


def _dot3(ah, al, bh, bl):
    d = functools.partial(jnp.dot, preferred_element_type=F32)
    return d(ah, bh) + (d(ah, bl) + d(al, bh))


def _ln_rows(x, g, b):
    mu = jnp.mean(x, axis=-1, keepdims=True)
    xc = x - mu
    var = jnp.mean(xc * xc, axis=-1, keepdims=True)
    return xc * lax.rsqrt(var + LN_EPS) * g + b


def _ln_kernel(x_ref, g_ref, b_ref, o_ref):
    o_ref[...] = _ln_rows(x_ref[...], g_ref[...], b_ref[...])


def _layer_norm(x2, g, b, tm=512):
    T, D = x2.shape
    return pl.pallas_call(
        _ln_kernel,
        grid=(T // tm,),
        in_specs=[pl.BlockSpec((tm, D), lambda i: (i, 0)),
                  pl.BlockSpec((1, D), lambda i: (0, 0)),
                  pl.BlockSpec((1, D), lambda i: (0, 0))],
        out_specs=pl.BlockSpec((tm, D), lambda i: (i, 0)),
        out_shape=jax.ShapeDtypeStruct((T, D), F32),
        compiler_params=_cparams(("parallel",)),
        name="ln0",
    )(x2, g.reshape(1, D), b.reshape(1, D))


def _rope(x, cos_f, sin_a, sin_b):
    half = ROPE_DIM // 2
    return x * cos_f + pltpu.roll(x, LANES - half, 1) * sin_a + pltpu.roll(x, half, 1) * sin_b


def _inproj_kernel(h_ref, w_ref, cos_ref, sa_ref, sb_ref, uh_ref, qkv_ref, up_ref):
    hb = h_ref[...].astype(BF16)
    mm = lambda c0, c1: jnp.dot(hb, w_ref[:, c0:c1], preferred_element_type=F32)
    o_q = 3 * HY_WIDTH
    o_k = o_q + ATT_WIDTH
    o_v = o_k + ATT_WIDTH
    o_p = o_v + ATT_WIDTH
    uh_ref[...] = mm(0, o_q)
    up_ref[...] = mm(o_p, IN_WIDTH)
    cos_f, sin_a, sin_b = cos_ref[...], sa_ref[...], sb_ref[...]
    q = mm(o_q, o_k)
    k = mm(o_k, o_v)
    qscale = ATT_HEAD_DIM ** -0.5 * LOG2E
    for hd in range(ATT_HEADS):
        sl = slice(hd * LANES, (hd + 1) * LANES)
        qkv_ref[:, sl] = (_rope(q[:, sl], cos_f, sin_a, sin_b) * qscale).astype(BF16)
        qkv_ref[:, ATT_WIDTH + hd * LANES:ATT_WIDTH + (hd + 1) * LANES] = \
            _rope(k[:, sl], cos_f, sin_a, sin_b).astype(BF16)
    qkv_ref[:, 2 * ATT_WIDTH:] = mm(o_v, o_p).astype(BF16)


def _in_proj(h2, w_bf, layer, tables, seq, tm=512):
    T, D = h2.shape
    N = w_bf.shape[2]
    spt = seq // tm
    tspec = pl.BlockSpec((tm, LANES), lambda i: (i % spt, 0))
    row = lambda c: pl.BlockSpec((tm, c), lambda i: (i, 0))
    return pl.pallas_call(
        _inproj_kernel,
        grid=(T // tm,),
        in_specs=[row(D), pl.BlockSpec((None, D, N), lambda i: (layer, 0, 0)), tspec, tspec, tspec],
        out_specs=[row(3 * HY_WIDTH), row(3 * ATT_WIDTH), row(POOL_WIDTH)],
        out_shape=[jax.ShapeDtypeStruct((T, 3 * HY_WIDTH), F32),
                   jax.ShapeDtypeStruct((T, 3 * ATT_WIDTH), BF16),
                   jax.ShapeDtypeStruct((T, POOL_WIDTH), F32)],
        compiler_params=_cparams(("parallel",)),
        name="in_proj",
    )(h2, w_bf, *tables)


def _hy_filter_kernel(band_ref, phase_ref, fw1_ref, fb1_ref, fr1_ref, fw2_ref, fb2_ref, fr2_ref, fw3_ref,
                      dl_ref, k_ref, ssq_ref, z_scr, *, seq, tr):
    j = pl.program_id(0)
    l = pl.program_id(1)
    n = j * tr + lax.broadcasted_iota(jnp.int32, (tr, 1), 0)
    pos = jnp.where(n < seq, n, 2 * seq - n).astype(F32)
    t = pos * (1.0 / (seq - 1))

    @pl.when(l == 0)
    def _():
        wpos = (2.0 * math.pi / seq) * pos
        lane = lax.broadcasted_iota(jnp.int32, (tr, LANES), 1)
        z_scr[...] = jnp.where(lane == 0, t,
                               jnp.where(lane < HY_EMB, jnp.sin(wpos * band_ref[...] + phase_ref[...]), 0.0))

    @pl.when((l == 0) & (j == 0))
    def _():
        ssq_ref[...] = jnp.zeros_like(ssq_ref)

    z = z_scr[...]
    h1 = _dot3(*_split(z), *_split(fw1_ref[...])) + fb1_ref[...]
    h1 = jnp.sin(fr1_ref[...] * h1)
    h2 = _dot3(*_split(h1), *_split(fw2_ref[...])) + fb2_ref[...]
    h2 = jnp.sin(fr2_ref[...] * h2)
    filt = _dot3(*_split(h2), *_split(fw3_ref[...]))
    window = jnp.exp(-t * jnp.abs(dl_ref[...]))
    kk = jnp.where(n < seq, filt[:, :HY_WIDTH], filt[:, HY_WIDTH:]) * window
    kk = jnp.where(n == seq, 0.0, kk)
    k_ref[...] = kk
    ssq_ref[l] += jnp.sum(kk * kk, axis=0, keepdims=True)


def _hy_filter(seq, fw1, fb1, freq1, fw2, fb2, freq2, fw3, tr=1024):
    L = fw1.shape[0]
    n = 2 * seq
    H = HY_FILTER_HIDDEN
    bands = np.linspace(1e-4, HY_BANDS - 1, HY_BANDS)
    bandv = np.zeros((1, LANES), np.float32)
    bandv[0, 1:1 + HY_BANDS] = bands
    bandv[0, 1 + HY_BANDS:HY_EMB] = bands
    phase = np.zeros((1, LANES), np.float32)
    phase[0, 1:1 + HY_BANDS] = 0.5 * math.pi
    phase[0, 1 + HY_BANDS:HY_EMB] = math.pi
    fw1p = jnp.zeros((L, LANES, H), F32).at[:, :HY_EMB].set(fw1)
    max_decay = math.log(HY_DECAY_TARGET) / HY_FAST_DECAY
    min_decay = math.log(HY_DECAY_TARGET) / HY_SLOW_DECAY
    deltas = np.linspace(min_decay, max_decay, HY_WIDTH).astype(np.float32).reshape(1, HY_WIDTH)
    vec = lambda a: a.reshape(L, 1, a.shape[-1])
    lspec = lambda r, c: pl.BlockSpec((None, r, c), lambda j, l: (l, 0, 0))
    lanes = pl.BlockSpec((1, LANES), lambda j, l: (0, 0))
    return pl.pallas_call(
        functools.partial(_hy_filter_kernel, seq=seq, tr=tr),
        grid=(n // tr, L),
        in_specs=[lanes, lanes,
                  lspec(LANES, H), lspec(1, H), lspec(1, H),
                  lspec(H, H), lspec(1, H), lspec(1, H),
                  lspec(H, 2 * HY_WIDTH),
                  pl.BlockSpec((1, HY_WIDTH), lambda j, l: (0, 0))],
        out_specs=[pl.BlockSpec((None, tr, HY_WIDTH), lambda j, l: (l, j, 0)),
                   pl.BlockSpec((L, 1, HY_WIDTH), lambda j, l: (0, 0, 0))],
        out_shape=[jax.ShapeDtypeStruct((L, n, HY_WIDTH), F32),
                   jax.ShapeDtypeStruct((L, 1, HY_WIDTH), F32)],
        scratch_shapes=[pltpu.VMEM((tr, LANES), F32)],
        compiler_params=_cparams(("arbitrary", "arbitrary")),
        name="hy_filter",
    )(jnp.asarray(bandv), jnp.asarray(phase), fw1p, vec(fb1), vec(freq1), fw2, vec(fb2), vec(freq2), fw3,
      jnp.asarray(deltas))


def _k1_rows(n1):
    k1 = n1 // 2 + 1
    return k1, -(-k1 // 8) * 8


def _bf_pair(m):
    m32 = jnp.asarray(np.asarray(m, np.float32))
    return _split(m32)


SUBLANES = 8


def _stage_a_matrix(n1_used, n1_total):
    k1n, k1p = _k1_rows(n1_total)
    k1 = np.arange(k1p)
    valid = (k1 < k1n)[:, None]
    ang = 2.0 * np.pi * np.outer(k1, np.arange(n1_used)) / n1_total
    eye = np.eye(SUBLANES)
    return np.concatenate([np.kron(np.cos(ang) * valid, eye), np.kron(-np.sin(ang) * valid, eye)], axis=0)


def _stage_a_rows(x_ref, a_scr, kah_ref, kal_ref, n1_used, k1p):
    def stage_a(g, carry):
        xg = jnp.concatenate(
            [x_ref[pl.ds(pl.multiple_of(n1 * FFT_N2 + g * SUBLANES, SUBLANES), SUBLANES), :]
             for n1 in range(n1_used)], axis=0)
        ag = _dot3(kah_ref[...], kal_ref[...], *_split(xg))
        gs = pl.ds(pl.multiple_of(g * SUBLANES, SUBLANES), SUBLANES)
        for part in range(2):
            for k1 in range(k1p):
                r = (part * k1p + k1) * SUBLANES
                a_scr[part, k1, gs, :] = ag[r:r + SUBLANES]
        return carry

    lax.fori_loop(0, FFT_N2 // SUBLANES, stage_a, 0, unroll=2)


def _stage_b_consts(n1_total):
    n = n1_total * FFT_N2
    k1n, k1p = _k1_rows(n1_total)
    ang2 = 2.0 * np.pi * np.outer(np.arange(FFT_N2), np.arange(FFT_N2)) / FFT_N2
    c2, s2 = np.cos(ang2), np.sin(ang2)
    mf = np.block([[c2, s2], [-s2, c2]])
    mi = np.block([[c2, -s2], [s2, c2]])
    angt = 2.0 * np.pi * np.outer(np.arange(k1p), np.arange(FFT_N2)) / n
    twc = jnp.asarray(np.cos(angt).astype(np.float32)).reshape(k1p, FFT_N2, 1)
    tws = jnp.asarray(np.sin(angt).astype(np.float32)).reshape(k1p, FFT_N2, 1)
    return _bf_pair(mf), _bf_pair(mi), twc, tws


def _fwd_b(ar, ai, c, s, mfh, mfl):
    tr_ = ar * c + ai * s
    ti_ = ai * c - ar * s
    xh, xl = _split(jnp.concatenate([tr_, ti_], axis=0))
    z = _dot3(mfh, mfl, xh, xl)
    return z[:FFT_N2], z[FFT_N2:]


def _hy_spectrum_kernel(k_ref, ssq_ref, tw_ref, kah_ref, kal_ref, mfh_ref, mfl_ref, kr_ref, ki_ref, a_scr,
                        *, n1_total, k1n, k1p):
    _stage_a_rows(k_ref, a_scr, kah_ref, kal_ref, n1_total, k1p)
    scale = lax.rsqrt(ssq_ref[...] + 1e-6) * (1.0 / (n1_total * FFT_N2))

    def stage_b(k1, carry):
        tw = tw_ref[k1]
        zr, zi = _fwd_b(a_scr[0, k1], a_scr[1, k1], tw[:, 0:1], tw[:, 1:2], mfh_ref[...], mfl_ref[...])
        kr_ref[k1] = zr * scale
        ki_ref[k1] = zi * scale
        return carry

    lax.fori_loop(0, k1n, stage_b, 0, unroll=3 if k1n % 3 == 0 else 1)
    for k1 in range(k1n, k1p):
        kr_ref[k1] = jnp.zeros(kr_ref.shape[1:], F32)
        ki_ref[k1] = jnp.zeros(ki_ref.shape[1:], F32)


def _hy_spectrum(kfilt, ssq, consts, n1_total):
    L, n, C = kfilt.shape
    k1n, k1p = _k1_rows(n1_total)
    (mfh, mfl), _, twc, tws = consts
    tw = jnp.concatenate([twc, tws], axis=-1)
    kah, kal = _bf_pair(_stage_a_matrix(n1_total, n1_total))
    full = lambda a: pl.BlockSpec(a.shape, lambda l, c: (0,) * a.ndim)
    ospec = pl.BlockSpec((None, k1p, FFT_N2, LANES), lambda l, c: (l, 0, 0, c))
    return pl.pallas_call(
        functools.partial(_hy_spectrum_kernel, n1_total=n1_total, k1n=k1n, k1p=k1p),
        grid=(L, C // LANES),
        in_specs=[pl.BlockSpec((None, n, LANES), lambda l, c: (l, 0, c)),
                  pl.BlockSpec((None, 1, LANES), lambda l, c: (l, 0, c)),
                  full(tw), full(kah), full(kal), full(mfh), full(mfl)],
        out_specs=[ospec, ospec],
        out_shape=[jax.ShapeDtypeStruct((L, k1p, FFT_N2, C), F32)] * 2,
        scratch_shapes=[pltpu.VMEM((2, k1p, FFT_N2, LANES), F32)],
        compiler_params=_cparams(("parallel", "parallel")),
        name="hy_spectrum",
    )(kfilt, ssq, tw, kah, kal, mfh, mfl)


def _hy_conv_kernel(vv_ref, x0_ref, kr_ref, ki_ref, tw_ref, kah_ref, kal_ref, kch_ref, kcl_ref,
                    mfh_ref, mfl_ref, mih_ref, mil_ref, bias_ref, o_ref, a_scr, *, n1u, k1n, k1p):
    groups = FFT_N2 // SUBLANES

    def rows(n1, g):
        return pl.ds(pl.multiple_of(n1 * FFT_N2 + g * SUBLANES, SUBLANES), SUBLANES)

    _stage_a_rows(vv_ref, a_scr, kah_ref, kal_ref, n1u, k1p)

    def stage_b(k1, carry):
        tw = tw_ref[k1]
        c = tw[:, 0:1]
        s = tw[:, 1:2]
        zr, zi = _fwd_b(a_scr[0, k1], a_scr[1, k1], c, s, mfh_ref[...], mfl_ref[...])
        kr = kr_ref[k1]
        ki = ki_ref[k1]
        pr = zr * kr - zi * ki
        pi = zr * ki + zi * kr
        y = _dot3(mih_ref[...], mil_ref[...], *_split(jnp.concatenate([pr, pi], axis=0)))
        yr = y[:FFT_N2]
        yi = y[FFT_N2:]
        a_scr[0, k1] = yr * c - yi * s
        a_scr[1, k1] = yi * c + yr * s
        return carry

    lax.fori_loop(0, k1n, stage_b, 0, unroll=11 if k1n % 11 == 0 else 1)

    def stage_c(g, carry):
        gs = pl.ds(pl.multiple_of(g * SUBLANES, SUBLANES), SUBLANES)
        bg = jnp.concatenate([a_scr[part, k1, gs, :] for part in range(2) for k1 in range(k1p)], axis=0)
        yg = _dot3(kch_ref[...], kcl_ref[...], *_split(bg))
        for n1 in range(n1u):
            r = rows(n1, g)
            o_ref[r, :] = (yg[n1 * SUBLANES:(n1 + 1) * SUBLANES] + vv_ref[r, :] * bias_ref[...]) * x0_ref[r, :]
        return carry

    lax.fori_loop(0, groups, stage_c, 0, unroll=2)


def _hy_conv(vv, x0c, kr4, ki4, layer, consts, bias_d, n1_total):
    B, S, C = vv.shape
    n1u = S // FFT_N2
    k1n, k1p = _k1_rows(n1_total)
    (mfh, mfl), (mih, mil), twc, tws = consts
    tw = jnp.concatenate([twc, tws], axis=-1)
    k1 = np.arange(k1p)
    eye = np.eye(SUBLANES)
    ka = _stage_a_matrix(n1u, n1_total)
    w = np.where((k1 == 0) | (k1 == n1_total // 2), 1.0, 2.0) * (k1 < k1n)
    ang_c = 2.0 * np.pi * np.outer(np.arange(n1u), k1) / n1_total
    kc = np.concatenate([np.kron(np.cos(ang_c) * w[None, :], eye), np.kron(-np.sin(ang_c) * w[None, :], eye)], axis=1)
    kah, kal = _bf_pair(ka)
    kch, kcl = _bf_pair(kc)
    nct = C // LANES
    xspec = pl.BlockSpec((None, S, LANES), lambda c, b: (b, 0, c))
    kspec = pl.BlockSpec((None, k1p, FFT_N2, LANES), lambda c, b: (layer, 0, 0, c))
    full = lambda a: pl.BlockSpec(a.shape, lambda c, b: (0,) * a.ndim)
    return pl.pallas_call(
        functools.partial(_hy_conv_kernel, n1u=n1u, k1n=k1n, k1p=k1p),
        grid=(nct, B),
        in_specs=[xspec, xspec, kspec, kspec, full(tw), full(kah), full(kal), full(kch), full(kcl),
                  full(mfh), full(mfl), full(mih), full(mil),
                  pl.BlockSpec((1, LANES), lambda c, b: (0, c))],
        out_specs=xspec,
        out_shape=jax.ShapeDtypeStruct((B, S, C), F32),
        scratch_shapes=[pltpu.VMEM((2, k1p, FFT_N2, LANES), F32)],
        compiler_params=_cparams(("parallel", "parallel")),
        name="hy_conv",
    )(vv, x0c, kr4, ki4, tw, kah, kal, kch, kcl, mfh, mfl, mih, mil, bias_d.reshape(1, C))


def _conv3(x, w_ref, b_ref):
    S = x.shape[0]
    row = lax.broadcasted_iota(jnp.int32, x.shape, 0)
    prev = jnp.where(row == 0, 0.0, pltpu.roll(x, 1, 0))
    nxt = jnp.where(row == S - 1, 0.0, pltpu.roll(x, S - 1, 0))
    return prev * w_ref[0:1, :] + x * w_ref[1:2, :] + nxt * w_ref[2:3, :] + b_ref[...]


def _hy_gate_kernel(x0_ref, x1_ref, v_ref, w0_ref, w1_ref, w2_ref, b0_ref, b1_ref, b2_ref,
                    vv_ref, x0c_ref):
    x0c_ref[...] = _conv3(x0_ref[...], w0_ref, b0_ref)
    vv_ref[...] = _conv3(v_ref[...], w2_ref, b2_ref) * _conv3(x1_ref[...], w1_ref, b1_ref)


def _hy_gate(u3, conv_w, conv_b):
    B, S, _ = u3.shape
    nct = HY_WIDTH // LANES
    conv_b2 = conv_b.reshape(1, 3 * HY_WIDTH)
    uspec = lambda part: pl.BlockSpec((None, S, LANES), lambda b, c: (b, 0, part * nct + c))
    wspec = lambda part: pl.BlockSpec((3, LANES), lambda b, c: (0, part * nct + c))
    bspec = lambda part: pl.BlockSpec((1, LANES), lambda b, c: (0, part * nct + c))
    ospec = pl.BlockSpec((None, S, LANES), lambda b, c: (b, 0, c))
    return pl.pallas_call(
        _hy_gate_kernel,
        grid=(B, nct),
        in_specs=[uspec(0), uspec(1), uspec(2), wspec(0), wspec(1), wspec(2),
                  bspec(0), bspec(1), bspec(2)],
        out_specs=[ospec, ospec],
        out_shape=[jax.ShapeDtypeStruct((B, S, HY_WIDTH), F32)] * 2,
        compiler_params=_cparams(("parallel", "parallel")),
        name="hy_gate",
    )(u3, u3, u3, conv_w, conv_w, conv_w, conv_b2, conv_b2, conv_b2)


def _attn_kernel(q_ref, k_ref, v_ref, lq1_ref, lk1_ref, lq2_ref, lk2_ref, g_ref, o_ref,
                 s_scr, p_scr, vx_scr, kt_scr, *, lam_init, tq):
    S = q_ref.shape[0]
    nblk = S // tq
    lam = (jnp.exp(jnp.sum(lq1_ref[...] * lk1_ref[...], keepdims=True))
           - jnp.exp(jnp.sum(lq2_ref[...] * lk2_ref[...], keepdims=True)) + lam_init)

    def blk(i):
        return pl.ds(pl.multiple_of(i * tq, tq), tq)

    kt = k_ref[...].T
    feat = lax.broadcasted_iota(jnp.int32, kt.shape, 0)
    kt_scr[0] = jnp.where(feat < ATT_HEAD_DIM, kt, jnp.zeros_like(kt))
    kt_scr[1] = jnp.where(feat >= ATT_HEAD_DIM, kt, jnp.zeros_like(kt))

    def scores(i, slot):
        q = q_ref[blk(i), :]
        for c in range(2):
            s_scr[slot, c] = jnp.dot(q, kt_scr[c], preferred_element_type=F32)

    vx_scr[:, :LANES] = v_ref[...]
    vx_scr[:, LANES:] = (lax.broadcasted_iota(jnp.int32, (S, LANES), 1) == 0).astype(BF16)

    def softmax(slot):
        for c in range(2):
            s = s_scr[slot, c]
            m = jnp.max(s, axis=-1, keepdims=True)
            p_scr[slot, c] = jnp.exp2(s - m).astype(BF16)

    def values(i, slot):
        oe1 = jnp.dot(p_scr[slot, 0], vx_scr[...], preferred_element_type=F32)
        oe2 = jnp.dot(p_scr[slot, 1], vx_scr[...], preferred_element_type=F32)
        o = oe1[:, :LANES] / oe1[:, LANES:LANES + 1] - oe2[:, :LANES] * (lam / oe2[:, LANES:LANES + 1])
        o = o * lax.rsqrt(jnp.mean(o * o, axis=-1, keepdims=True) + RMS_EPS) * g_ref[...]
        o_ref[blk(i), :] = o * (1.0 - lam_init)

    scores(0, 0)
    softmax(0)
    scores(1, 1)

    def pair(j, carry):
        t = 2 * j
        values(t - 2, 0)
        softmax(1)
        scores(t, 0)
        values(t - 1, 1)
        softmax(0)
        scores(t + 1, 1)
        return carry

    lax.fori_loop(1, nblk // 2, pair, 0)
    values(nblk - 2, 0)
    softmax(1)
    values(nblk - 1, 1)


def _rope_tables(seq):
    pos = np.arange(seq, dtype=np.float64)
    inv_freq = np.power(ROPE_THETA, -np.arange(0, ROPE_DIM, 2, dtype=np.float64) / ROPE_DIM)
    ang = pos[:, None] * inv_freq[None, :]
    half = ROPE_DIM // 2
    cos_f = np.ones((seq, LANES), np.float32)
    sin_a = np.zeros((seq, LANES), np.float32)
    sin_b = np.zeros((seq, LANES), np.float32)
    for base in range(0, LANES, ATT_HEAD_DIM):
        cos_f[:, base:base + half] = np.cos(ang)
        cos_f[:, base + half:base + ROPE_DIM] = np.cos(ang)
        sin_a[:, base:base + half] = -np.sin(ang)
        sin_b[:, base + half:base + ROPE_DIM] = np.sin(ang)
    return jnp.asarray(cos_f), jnp.asarray(sin_a), jnp.asarray(sin_b)


def _diff_attention(qkv3, lq1, lk1, lq2, lk2, subln_g, lam_init, tq=128):
    B, S, _ = qkv3.shape
    nb = ATT_WIDTH // LANES
    assert S % (2 * tq) == 0
    spec = lambda part: pl.BlockSpec((None, S, LANES), lambda b, h: (b, 0, part * nb + h))
    vec = pl.BlockSpec((1, ATT_HEAD_DIM), lambda b, h: (0, 0))
    r1 = lambda a: a.reshape(1, -1)
    return pl.pallas_call(
        functools.partial(_attn_kernel, lam_init=lam_init, tq=tq),
        grid=(B, ATT_HEADS),
        in_specs=[spec(0), spec(1), spec(2), vec, vec, vec, vec,
                  pl.BlockSpec((1, LANES), lambda b, h: (0, 0))],
        out_specs=pl.BlockSpec((None, S, LANES), lambda b, h: (b, 0, h)),
        out_shape=jax.ShapeDtypeStruct((B, S, ATT_WIDTH), F32),
        scratch_shapes=[pltpu.VMEM((2, 2, tq, S), F32), pltpu.VMEM((2, 2, tq, S), BF16),
                        pltpu.VMEM((S, 2 * LANES), BF16), pltpu.VMEM((2, LANES, S), BF16)],
        compiler_params=_cparams(("parallel", "parallel")),
        name="diff_attn",
    )(qkv3, qkv3, qkv3, r1(lq1), r1(lk1), r1(lq2), r1(lk2), r1(subln_g))


def _pool_kernel(u_ref, w_ref, b_ref, sc_ref, o_ref):
    ct = pl.program_id(1)
    x = u_ref[...]
    S = x.shape[0]
    row = lax.broadcasted_iota(jnp.int32, x.shape, 0)
    lane = lax.broadcasted_iota(jnp.int32, x.shape, 1)

    def shifted(d):
        if d == 0:
            return x
        r = pltpu.roll(x, (-d) % S, 0)
        return jnp.where((row + d >= 0) & (row + d < S), r, 0.0)

    sums = {}
    acc = shifted(-1) + x
    sums[2] = acc
    lo, hi = -1, 0
    for w in POOL_WINDOWS[1:]:
        for d in list(range(-(w // 2), lo)) + list(range(hi + 1, w // 2)):
            acc = acc + shifted(d)
        lo, hi = -(w // 2), w // 2 - 1
        sums[w] = acc
    grp = 2 * ct + (lane >= POOL_GROUP).astype(jnp.int32)
    win_sum = sums[POOL_WINDOWS[-1]]
    half = jnp.full(x.shape, POOL_WINDOWS[-1] // 2, jnp.int32)
    for g in range(len(POOL_WINDOWS) - 2, -1, -1):
        win_sum = jnp.where(grp == g, sums[POOL_WINDOWS[g]], win_sum)
        half = jnp.where(grp == g, POOL_WINDOWS[g] // 2, half)
    cnt = jnp.minimum(row + half - 1, S - 1) - jnp.maximum(row - half, 0) + 1
    y = win_sum / cnt.astype(F32) - x
    y = _dot3(*_split(y), *_split(w_ref[...])) + b_ref[...]
    o_ref[...] = y * sc_ref[...]


def _pool_mixer(u3, w, b, scale):
    B, S, _ = u3.shape
    nct = POOL_WIDTH // LANES
    gpt = LANES // POOL_GROUP
    wbd = jnp.zeros((nct, LANES, LANES), F32)
    for g in range(len(POOL_WINDOWS)):
        t, o = divmod(g, gpt)
        wbd = wbd.at[t, o * POOL_GROUP:(o + 1) * POOL_GROUP, o * POOL_GROUP:(o + 1) * POOL_GROUP].set(w[g])
    vspec = pl.BlockSpec((1, LANES), lambda bb, c: (0, c))
    return pl.pallas_call(
        _pool_kernel,
        grid=(B, nct),
        in_specs=[pl.BlockSpec((None, S, LANES), lambda bb, c: (bb, 0, c)),
                  pl.BlockSpec((None, LANES, LANES), lambda bb, c: (c, 0, 0)),
                  vspec, vspec],
        out_specs=pl.BlockSpec((None, S, LANES), lambda bb, c: (bb, 0, c)),
        out_shape=jax.ShapeDtypeStruct((B, S, POOL_WIDTH), F32),
        compiler_params=_cparams(("parallel", "parallel")),
        name="pool_mixer",
    )(u3, wbd, b.reshape(1, POOL_WIDTH), scale.reshape(1, POOL_WIDTH))


PAIRS_PER_GROUP = 6
N_CLASSES = N_GROUPS * PAIRS_PER_GROUP
ROUTE_W_LO, ROUTE_W_HI, ROUTE_CLS, ROUTE_RANK = 0, 1, 2, 3
TOKEN_ROWS = D_MODEL // LANES


def _store_token_major(ref, x, first_row, unit_rows):
    n = x.shape[0]
    for j in range(TOKEN_ROWS):
        ref[pl.ds(first_row + j, n, stride=unit_rows), :] = x[:, j * LANES:(j + 1) * LANES]


def _load_token_major(ref, n, first_row, unit_rows):
    return jnp.concatenate([ref[pl.ds(first_row + j, n, stride=unit_rows), :] for j in range(TOKEN_ROWS)],
                           axis=1)


def _class_experts(c):
    g, pidx = divmod(c, PAIRS_PER_GROUP)
    pairs = [(a, b) for a in range(EXPERTS_PER_GROUP) for b in range(a + 1, EXPERTS_PER_GROUP)]
    lo, hi = pairs[pidx]
    return g * EXPERTS_PER_GROUP + lo, g * EXPERTS_PER_GROUP + hi


def _outproj_kernel(h_ref, yh_ref, ya_ref, yp_ref, w_ref, g_ref, b_ref,
                    wgh_ref, wgl_ref, bg_ref, tri_ref, h1_ref, h1t_ref, route_ref, cnt_ref, carry_scr):
    @pl.when(pl.program_id(0) == 0)
    def _():
        carry_scr[...] = jnp.zeros_like(carry_scr)

    sub = tri_ref.shape[0]
    carry = carry_scr[...]
    for s in range(h_ref.shape[0] // sub):
        carry = _outproj_rows(slice(s * sub, (s + 1) * sub), s * sub * TOKEN_ROWS, carry,
                              h_ref, yh_ref, ya_ref, yp_ref, w_ref, g_ref, b_ref,
                              wgh_ref, wgl_ref, bg_ref, tri_ref, h1_ref, h1t_ref, route_ref)
    carry_scr[...] = carry
    cnt_ref[...] = carry


def _outproj_rows(rows, first_unit_row, carry, h_ref, yh_ref, ya_ref, yp_ref, w_ref,
                  g_ref, b_ref, wgh_ref, wgl_ref, bg_ref, tri_ref, h1_ref, h1t_ref, route_ref):
    d = functools.partial(jnp.dot, preferred_element_type=F32)
    o1, o2 = HY_WIDTH, HY_WIDTH + ATT_WIDTH
    mix = (d(yh_ref[rows, :].astype(BF16), w_ref[:o1, :]) + d(ya_ref[rows, :].astype(BF16), w_ref[o1:o2, :])
           + d(yp_ref[rows, :].astype(BF16), w_ref[o2:, :]))
    h1 = _ln_rows(DN_ALPHA * h_ref[rows, :] + mix, g_ref[...], b_ref[...])
    h1_ref[rows, :] = h1
    _store_token_major(h1t_ref, h1, first_unit_row, TOKEN_ROWS)

    logit = _dot3(*_split(h1), wgh_ref[...], wgl_ref[...]) + bg_ref[...]
    lni = lax.broadcasted_iota(jnp.int32, logit.shape, 1)
    ln = lni.astype(F32)
    grp = lax.shift_right_arithmetic(lni - GATE_FINE_LANE, 2).astype(F32)
    first = lambda mask: jnp.min(jnp.where(mask, ln, float(LANES)), axis=-1, keepdims=True)
    cmask = lni < N_GROUPS
    lc = jnp.where(cmask, logit, NEG_BIG)
    mc = jnp.max(lc, axis=-1, keepdims=True)
    gw = 1.0 / jnp.sum(jnp.where(cmask, jnp.exp(lc - mc), 0.0), axis=-1, keepdims=True)
    gi = first(cmask & (lc == mc))
    fmask = (lni >= GATE_FINE_LANE) & (lni < GATE_FINE_LANE + N_EXPERTS) & (grp == gi)
    lf = jnp.where(fmask, logit, NEG_BIG)
    m1 = jnp.max(lf, axis=-1, keepdims=True)
    i1 = first(fmask & (lf == m1))
    rest = fmask & (ln != i1)
    lf2 = jnp.where(rest, logit, NEG_BIG)
    m2 = jnp.max(lf2, axis=-1, keepdims=True)
    i2 = first(rest & (lf2 == m2))
    e2 = jnp.exp(m2 - m1)
    w1 = gw / (1.0 + e2)
    w2 = gw * e2 / (1.0 + e2)

    j1 = i1 - GATE_FINE_LANE - EXPERTS_PER_GROUP * gi
    j2 = i2 - GATE_FINE_LANE - EXPERTS_PER_GROUP * gi
    lo = jnp.minimum(j1, j2)
    hi = jnp.maximum(j1, j2)
    w_lo = jnp.where(j1 < j2, w1, w2)
    w_hi = jnp.where(j1 < j2, w2, w1)
    base = jnp.where(lo == 0.0, 0.0, jnp.where(lo == 1.0, 3.0, 5.0))
    cls = gi * PAIRS_PER_GROUP + base + hi - lo - 1.0
    onehot = ln == cls
    before = jnp.dot(tri_ref[...], onehot.astype(BF16), preferred_element_type=F32)
    rank = jnp.sum(jnp.where(onehot, before + carry, 0.0), axis=-1, keepdims=True)
    route_ref[rows, :] = jnp.where(lni == ROUTE_W_LO, w_lo,
                                   jnp.where(lni == ROUTE_W_HI, w_hi,
                                             jnp.where(lni == ROUTE_CLS, cls,
                                                       jnp.where(lni == ROUTE_RANK, rank, 0.0))))
    return carry + jnp.sum(onehot.astype(F32), axis=0, keepdims=True)


def _out_proj_ln_route(h2, yh, ya, yp, w_out_bf, layer, g, b, wgc, bgc, wgf, bgf, tm=512):
    T, D = h2.shape
    sub = tm
    tri = jnp.asarray(np.tril(np.ones((sub, sub), np.float32), -1)).astype(BF16)
    wg = jnp.zeros((D, LANES), F32).at[:, :N_GROUPS].set(wgc)
    wg = wg.at[:, GATE_FINE_LANE:GATE_FINE_LANE + N_EXPERTS].set(wgf)
    bg = jnp.zeros((1, LANES), F32).at[0, :N_GROUPS].set(bgc)
    bg = bg.at[0, GATE_FINE_LANE:GATE_FINE_LANE + N_EXPERTS].set(bgf)
    wgh, wgl = _split(wg)
    row = lambda c: pl.BlockSpec((tm, c), lambda i: (i, 0))
    full = lambda r, c: pl.BlockSpec((r, c), lambda i: (0, 0))
    return pl.pallas_call(
        _outproj_kernel,
        grid=(T // tm,),
        in_specs=[row(D), row(HY_WIDTH), row(ATT_WIDTH), row(POOL_WIDTH),
                  pl.BlockSpec((None, D, D), lambda i: (layer, 0, 0)),
                  full(1, D), full(1, D), full(D, LANES), full(D, LANES), full(1, LANES), full(sub, sub)],
        out_specs=[row(D), pl.BlockSpec((tm * TOKEN_ROWS, LANES), lambda i: (i, 0)), row(LANES), full(1, LANES)],
        out_shape=[jax.ShapeDtypeStruct((T, D), F32), jax.ShapeDtypeStruct((T * TOKEN_ROWS, LANES), F32),
                   jax.ShapeDtypeStruct((T, LANES), F32), jax.ShapeDtypeStruct((1, LANES), F32)],
        scratch_shapes=[pltpu.VMEM((1, LANES), F32)],
        compiler_params=_cparams(("arbitrary",)),
        name="out_proj_ln_route",
    )(h2, yh, ya, yp, w_out_bf, g.reshape(1, D), b.reshape(1, D), wgh, wgl, bg, tri)


MOE_TILE = 256
DMA_BATCH = 256


def _moe_rows_padded(T):
    return T + N_CLASSES * MOE_TILE


def _token_rows(ref, t, unit_rows):
    return ref.at[pl.ds(pl.multiple_of(t * unit_rows, unit_rows), unit_rows)]


DMA_GROUP = 8


def _start_tokens(n, slot_of, start_copy):
    def group(g, carry):
        first = g * DMA_GROUP
        slots = [slot_of(first + k) for k in range(DMA_GROUP)]
        for k in range(DMA_GROUP):
            start_copy(first + k, slots[k], k % 2)
        return carry

    lax.fori_loop(0, n // DMA_GROUP, group, 0)


def _wait_tokens(n, src_ref, dst_ref, unit_rows, sems):
    rows = pl.ds(0, (n // 2) * unit_rows)
    for sem in sems:
        pltpu.make_async_copy(src_ref.at[rows], dst_ref.at[rows], sem).wait()


def _dispatch_kernel(cls_ref, rank_ref, cnt_ref, h1t_ref, xs_ref, dest_ref, tlo_ref, thi_ref, tval_ref,
                     off_scr, zero_scr, sem, *, tm, n_tiles):
    i = pl.program_id(0)
    base = i * tm
    tile_rows = MOE_TILE * TOKEN_ROWS

    def fill_copy(tile):
        rows = pl.ds(pl.multiple_of(tile * tile_rows, tile_rows), tile_rows)
        return pltpu.make_async_copy(zero_scr, xs_ref.at[rows], sem.at[2])

    @pl.when(i == 0)
    def _():
        zero_scr[...] = jnp.zeros_like(zero_scr)
        off = jnp.int32(0)
        tile = jnp.int32(0)
        fills = []
        for c in range(N_CLASSES):
            n = cnt_ref[c]
            nt = lax.shift_right_logical(n + (MOE_TILE - 1), MOE_TILE.bit_length() - 1)
            off_scr[c] = off
            e_lo, e_hi = _class_experts(c)

            def mark(k, carry, tile=tile, e_lo=e_lo, e_hi=e_hi):
                tlo_ref[tile + k] = e_lo
                thi_ref[tile + k] = e_hi
                tval_ref[tile + k] = 1
                return carry

            lax.fori_loop(0, nt, mark, 0)
            fill = fill_copy(tile + nt - 1)
            fills.append((nt > 0, fill))

            @pl.when(nt > 0)
            def _(fill=fill):
                fill.start()

            off = off + nt * MOE_TILE
            tile = tile + nt

        def unused(k, carry):
            tlo_ref[k] = 0
            thi_ref[k] = 0
            tval_ref[k] = 0
            fill_copy(k).start()
            return carry

        lax.fori_loop(tile, n_tiles, unused, 0)
        for used, fill in fills:
            @pl.when(used)
            def _(fill=fill):
                fill.wait()

        def unused_wait(k, carry):
            fill_copy(k).wait()
            return carry

        lax.fori_loop(tile, n_tiles, unused_wait, 0)

    def slots_of_step(step):
        def slot(r, carry):
            t = step * tm + r
            dest_ref[t] = off_scr[cls_ref[t]] + rank_ref[t]
            return carry

        lax.fori_loop(0, tm, slot, 0, unroll=8)

    @pl.when(i == 0)
    def _():
        slots_of_step(0)

    def start_copy(r, d, prio):
        pltpu.make_async_copy(_token_rows(h1t_ref, r, TOKEN_ROWS), _token_rows(xs_ref, d, TOKEN_ROWS),
                              sem.at[prio]).start(priority=prio)

    _start_tokens(tm, lambda r: dest_ref[base + r], start_copy)

    @pl.when(i + 1 < pl.num_programs(0))
    def _():
        slots_of_step(i + 1)

    _wait_tokens(tm, h1t_ref, xs_ref, TOKEN_ROWS, [sem.at[0], sem.at[1]])


def _moe_dispatch(h1t, cls_i, rank_i, cnt_i, tm=512):
    T = h1t.shape[0] // TOKEN_ROWS
    slots = _moe_rows_padded(T)
    n_tiles = slots // MOE_TILE
    smem = pl.BlockSpec(memory_space=pltpu.SMEM)
    return pl.pallas_call(
        functools.partial(_dispatch_kernel, tm=tm, n_tiles=n_tiles),
        grid_spec=pltpu.PrefetchScalarGridSpec(
            num_scalar_prefetch=3,
            grid=(T // tm,),
            in_specs=[pl.BlockSpec((tm * TOKEN_ROWS, LANES), lambda i, *_: (i, 0))],
            out_specs=[pl.BlockSpec(memory_space=pl.ANY), smem, smem, smem, smem],
            scratch_shapes=[pltpu.SMEM((N_CLASSES,), jnp.int32),
                            pltpu.VMEM((MOE_TILE * TOKEN_ROWS, LANES), F32),
                            pltpu.SemaphoreType.DMA((3,))],
        ),
        out_shape=[jax.ShapeDtypeStruct((slots * TOKEN_ROWS, LANES), F32), jax.ShapeDtypeStruct((T,), jnp.int32),
                   jax.ShapeDtypeStruct((n_tiles,), jnp.int32), jax.ShapeDtypeStruct((n_tiles,), jnp.int32),
                   jax.ShapeDtypeStruct((n_tiles,), jnp.int32)],
        compiler_params=_cparams(("arbitrary",)),
        name="moe_dispatch",
    )(cls_i, rank_i, cnt_i, h1t)


def _experts_kernel(tlo_ref, thi_ref, tval_ref, xs_ref, w1l_ref, w3l_ref, w2l_ref, w1h_ref, w3h_ref, w2h_ref,
                    ys_ref):
    j = pl.program_id(0)
    d = functools.partial(jnp.dot, preferred_element_type=F32)

    @pl.when(tval_ref[j] == 1)
    def _():
        xb = _load_token_major(xs_ref, MOE_TILE, 0, TOKEN_ROWS).astype(BF16)

        def expert(w1_ref, w3_ref, w2_ref):
            a = d(xb, w1_ref[...])
            c = d(xb, w3_ref[...])
            return d((a * jax.nn.sigmoid(a) * c).astype(BF16), w2_ref[...])

        _store_token_major(ys_ref.at[0], expert(w1l_ref, w3l_ref, w2l_ref), 0, TOKEN_ROWS)
        _store_token_major(ys_ref.at[1], expert(w1h_ref, w3h_ref, w2h_ref), 0, TOKEN_ROWS)

    @pl.when(tval_ref[j] == 0)
    def _():
        ys_ref[...] = jnp.zeros_like(ys_ref)


def _moe_experts(xs, tlo, thi, tval, w1_bf, w3_bf, w2_bf, layer):
    slots = xs.shape[0] // TOKEN_ROWS
    D = D_MODEL
    wspec = lambda shape, which: pl.BlockSpec(
        (None, None) + shape, (lambda j, tlo, thi, tval: (layer, tlo[j], 0, 0)) if which == 0
        else (lambda j, tlo, thi, tval: (layer, thi[j], 0, 0)))
    up = (D, D_EXPERT)
    down = (D_EXPERT, D)
    return pl.pallas_call(
        _experts_kernel,
        grid_spec=pltpu.PrefetchScalarGridSpec(
            num_scalar_prefetch=3,
            grid=(slots // MOE_TILE,),
            in_specs=[pl.BlockSpec((MOE_TILE * TOKEN_ROWS, LANES), lambda j, *_: (j, 0)),
                      wspec(up, 0), wspec(up, 0), wspec(down, 0), wspec(up, 1), wspec(up, 1), wspec(down, 1)],
            out_specs=pl.BlockSpec((None, 2, MOE_TILE * TOKEN_ROWS, LANES), lambda j, *_: (j, 0, 0, 0)),
        ),
        out_shape=jax.ShapeDtypeStruct((slots // MOE_TILE, 2, MOE_TILE * TOKEN_ROWS, LANES), F32),
        compiler_params=_cparams(("arbitrary",)),
        name="moe_experts",
    )(tlo, thi, tval, xs, w1_bf, w3_bf, w2_bf, w1_bf, w3_bf, w2_bf)


def _combine_kernel(dest_ref, h1_ref, route_ref, ys_ref, p_ref, pwg_ref, pbg_ref, pwp_ref, g_ref, b_ref,
                    o_ref, y_scr, ym_scr, sem, *, tm):
    i = pl.program_id(0)
    d = functools.partial(jnp.dot, preferred_element_type=F32)
    tile_shift = MOE_TILE.bit_length() - 1

    def unit_rows(t):
        return pl.ds(pl.multiple_of(t * TOKEN_ROWS, TOKEN_ROWS), TOKEN_ROWS)

    def gather(step, buf):
        def start_copy(r, slot, prio):
            tile = lax.shift_right_logical(slot, tile_shift)
            src = ys_ref.at[tile, :, unit_rows(slot & (MOE_TILE - 1)), :]
            pltpu.make_async_copy(src, y_scr.at[buf, :, unit_rows(r), :],
                                  sem.at[2 * buf + prio]).start(priority=prio)
        _start_tokens(tm, lambda r: dest_ref[step * tm + r], start_copy)

    def wait_gathers(buf):
        rows = pl.ds(0, (tm // 2) * TOKEN_ROWS)
        for prio in range(2):
            pltpu.make_async_copy(ys_ref.at[0, :, rows, :], y_scr.at[buf, :, rows, :],
                                  sem.at[2 * buf + prio]).wait()

    @pl.when(i == 0)
    def _():
        gather(0, 0)

    for buf in range(2):
        @pl.when((i + 1 < pl.num_programs(0)) & ((i + 1) % 2 == buf))
        def _(buf=buf):
            gather(i + 1, buf)

    sub = tm // 2
    halves = [slice(s * sub, (s + 1) * sub) for s in range(2)]
    y_ple = []
    for rows in halves:
        z = d(h1_ref[rows, :].astype(BF16), pwg_ref[...]) + pbg_ref[...]
        y_ple.append(jax.nn.sigmoid(z) * d(p_ref[rows, :].astype(BF16), pwp_ref[...]))
    for buf in range(2):
        @pl.when(i % 2 == buf)
        def _(buf=buf):
            wait_gathers(buf)
            for s, rows in enumerate(halves):
                rec = route_ref[rows, :]
                first = s * sub * TOKEN_ROWS
                ym_scr[rows, :] = (rec[:, ROUTE_W_LO:ROUTE_W_LO + 1]
                                   * _load_token_major(y_scr.at[buf, 0], sub, first, TOKEN_ROWS)
                                   + rec[:, ROUTE_W_HI:ROUTE_W_HI + 1]
                                   * _load_token_major(y_scr.at[buf, 1], sub, first, TOKEN_ROWS))
    for s, rows in enumerate(halves):
        r_ = DN_ALPHA * h1_ref[rows, :] + ym_scr[rows, :] + y_ple[s]
        o_ref[rows, :] = _ln_rows(r_, g_ref[...], b_ref[...])


def _moe_combine_ple_ln(h1, route, ys, dest, p3, pwg_bf, pbg, pwp_bf, layer, g, b, tm=512):
    T, D = h1.shape
    row = lambda c: pl.BlockSpec((tm, c), lambda i, *_: (i, 0))
    full = lambda r, c: pl.BlockSpec((r, c), lambda i, *_: (0, 0))
    lay = lambda r, c: pl.BlockSpec((None, r, c), lambda i, *_: (layer, 0, 0))
    return pl.pallas_call(
        functools.partial(_combine_kernel, tm=tm),
        grid_spec=pltpu.PrefetchScalarGridSpec(
            num_scalar_prefetch=1,
            grid=(T // tm,),
            in_specs=[row(D), row(LANES), pl.BlockSpec(memory_space=pl.ANY),
                      pl.BlockSpec((None, tm, PLE_DIM), lambda i, *_: (layer, i, 0)),
                      lay(D, D), full(1, D), lay(PLE_DIM, D), full(1, D), full(1, D)],
            out_specs=row(D),
            scratch_shapes=[pltpu.VMEM((2, 2, tm * TOKEN_ROWS, LANES), F32), pltpu.VMEM((tm, D), F32),
                            pltpu.SemaphoreType.DMA((4,))],
        ),
        out_shape=jax.ShapeDtypeStruct((T, D), F32),
        compiler_params=_cparams(("arbitrary",)),
        name="moe_combine_ple_ln",
    )(dest, h1, route, ys, p3, pwg_bf, pbg.reshape(1, D), pwp_bf, g.reshape(1, D), b.reshape(1, D))


def _hyena_mixer(u3, kr4, ki4, layer, consts, conv_w, conv_b, bias_d, n1_total):
    B, S, _ = u3.shape
    vv, x0c = _hy_gate(u3, conv_w, conv_b)
    y = _hy_conv(vv, x0c, kr4, ki4, layer, consts, bias_d, n1_total)
    return y.reshape(B * S, HY_WIDTH)


def kernel(x, p, ln0_g, ln0_b, w_in, hy_conv_w, hy_conv_b, hy_fw1, hy_fb1, hy_freq1, hy_fw2, hy_fb2, hy_freq2, hy_fw3, hy_bias, att_lq1, att_lk1, att_lq2, att_lk2, att_subln_g, pool_w, pool_b, pool_scale, w_out, ln1_g, ln1_b, moe_wgc, moe_bgc, moe_wgf, moe_bgf, moe_w1, moe_w3, moe_w2, ple_wg, ple_bg, ple_wp, ln2_g, ln2_b):
    B, S, D = x.shape
    L = w_in.shape[0]
    T = B * S
    n1_total = 2 * S // FFT_N2
    C = HY_WIDTH

    kfilt, ssq = _hy_filter(S, hy_fw1, hy_fb1, hy_freq1, hy_fw2, hy_fb2, hy_freq2, hy_fw3)
    consts = _stage_b_consts(n1_total)
    kr_all, ki_all = _hy_spectrum(kfilt, ssq, consts, n1_total)
    tables = _rope_tables(S)

    w_in_bf, w_out_bf = w_in.astype(BF16), w_out.astype(BF16)
    w1_bf, w3_bf, w2_bf = moe_w1.astype(BF16), moe_w3.astype(BF16), moe_w2.astype(BF16)
    pwg_bf, pwp_bf = ple_wg.astype(BF16), ple_wp.astype(BF16)
    p3 = p.reshape(L, T, PLE_DIM)

    h = _layer_norm(x.reshape(T, D), ln0_g, ln0_b)
    for i in range(L):
        lam_init = 0.8 - 0.6 * math.exp(-0.3 * i)
        uh, qkv, up = _in_proj(h, w_in_bf, i, tables, S)
        y_hy = _hyena_mixer(uh.reshape(B, S, 3 * C), kr_all, ki_all, i, consts,
                            hy_conv_w[i], hy_conv_b[i], hy_bias[i], n1_total)
        y_att = _diff_attention(qkv.reshape(B, S, 3 * ATT_WIDTH), att_lq1[i], att_lk1[i], att_lq2[i],
                                att_lk2[i], att_subln_g[i], lam_init).reshape(T, ATT_WIDTH)
        y_pool = _pool_mixer(up.reshape(B, S, POOL_WIDTH), pool_w[i], pool_b[i],
                             pool_scale[i]).reshape(T, POOL_WIDTH)
        h1, h1t, route, counts = _out_proj_ln_route(h, y_hy, y_att, y_pool, w_out_bf, i, ln1_g[i],
                                                    ln1_b[i], moe_wgc[i], moe_bgc[i], moe_wgf[i], moe_bgf[i])
        cls_i = route[:, ROUTE_CLS].astype(jnp.int32)
        rank_i = route[:, ROUTE_RANK].astype(jnp.int32)
        xs, dest, tlo, thi, tval = _moe_dispatch(h1t, cls_i, rank_i, counts[0].astype(jnp.int32))
        ys = _moe_experts(xs, tlo, thi, tval, w1_bf, w3_bf, w2_bf, i)
        h = _moe_combine_ple_ln(h1, route, ys, dest, p3, pwg_bf, ple_bg[i], pwp_bf, i, ln2_g[i], ln2_b[i])
    return h.reshape(B, S, D)
```
